```python
import math, functools
import jax, jax.numpy as jnp
from jax import lax
import numpy as np

D_MODEL = 1024
BATCH = 8
SEQ = 2048
DEPTH = 1
DEC_BATCH = 128
DEC_SEQ = 1
PAST_LEN = 2048
PAGE_SIZE = 128

N_HEADS = 8
HEAD_DIM = 64
N_KV_HEADS = 2
HEADS_PER_GROUP = N_HEADS // N_KV_HEADS
ROT_DIM = HEAD_DIM // 4
ROPE_THETA = 500000.0
NSA_BLOCK = 64
NSA_TOP_N = 16
NSA_WINDOW = 512
PHI_HIDDEN = 2 * HEAD_DIM
NSA_Q_BLOCK = 32
WIN_Q_BLOCK = 128
D_CONV = 512
CONV_WIDTH = 3
D_FF = 2816
FFN_HALF = 0.5
NORM_EPS = 1e-6
NEG_INF = -1e30
Q_COLS = N_HEADS * HEAD_DIM
KV_COLS = 3 * 2 * N_KV_HEADS * HEAD_DIM
NSA_GATE_COLS = 3 * N_HEADS
CONV_COLS = 3 * D_CONV
MERGE_COLS = 2 * D_MODEL
IN_COLS = Q_COLS + KV_COLS + NSA_GATE_COLS + CONV_COLS + MERGE_COLS

kernel_name = 'hybrid_nsa_shortconv_macaron_step'


def rmsnorm(x, g):
    xf = x.astype(jnp.float32)
    y = xf * lax.rsqrt(jnp.mean(xf * xf, axis=-1, keepdims=True) + NORM_EPS)
    return y.astype(x.dtype) * g


def rope(x, pos):
    half = ROT_DIM // 2
    inv = jnp.power(ROPE_THETA, -jnp.arange(half, dtype=jnp.float32) * 2.0 / ROT_DIM)
    ang = pos.astype(jnp.float32)[:, None] * inv[None, :]
    ang = ang.reshape((1, pos.shape[0]) + (1,) * (x.ndim - 3) + (half,))
    cos = jnp.cos(ang).astype(x.dtype)
    sin = jnp.sin(ang).astype(x.dtype)
    x1 = x[..., :half]
    x2 = x[..., half:ROT_DIM]
    return jnp.concatenate([x1 * cos - x2 * sin, x2 * cos + x1 * sin, x[..., ROT_DIM:]], axis=-1)


def swiglu(a, w_gu, w_down):
    gu = a @ w_gu
    return (jax.nn.silu(gu[..., :D_FF]) * gu[..., D_FF:]) @ w_down


def project(a, pos, w_in):
    B, T, _ = a.shape
    z = a @ w_in
    o1 = Q_COLS
    o2 = o1 + KV_COLS
    o3 = o2 + NSA_GATE_COLS
    o4 = o3 + CONV_COLS
    q = rope(z[..., :o1].reshape(B, T, N_KV_HEADS, HEADS_PER_GROUP, HEAD_DIM), pos)
    kv = z[..., o1:o2].reshape(B, T, 3, 2, N_KV_HEADS, HEAD_DIM)
    kv = jnp.stack([rope(kv[:, :, :, 0], pos), kv[:, :, :, 1]], axis=3)
    nsa_g = jax.nn.sigmoid(z[..., o2:o3]).reshape(B, T, 3, N_KV_HEADS, HEADS_PER_GROUP)
    conv_h = z[..., o3:o3 + D_CONV]
    conv_b = z[..., o3 + D_CONV:o3 + 2 * D_CONV]
    conv_c = z[..., o3 + 2 * D_CONV:o4]
    merge_g = jax.nn.sigmoid(z[..., o4:]).reshape(B, T, 2, D_MODEL)
    return q, kv, nsa_g, conv_h, conv_b, conv_c, merge_g


def compress(xb, pe, w1, b1, w2):
    B, NB = xb.shape[:2]
    xb = xb.transpose(0, 1, 3, 2, 4) + pe
    h = jax.nn.silu(xb.reshape(B, NB, N_KV_HEADS, NSA_BLOCK * HEAD_DIM) @ w1 + b1)
    return h @ w2


def nsa_global(q, q_pos, cmp_kv, slc_kv, phi_pe, phi_w1, phi_b1, phi_w2):
    B, Tq = q.shape[:2]
    Tk = cmp_kv.shape[1]
    pad_k = (-Tk) % NSA_BLOCK
    kpad = ((0, 0), (0, pad_k), (0, 0), (0, 0), (0, 0))
    cmp_kv = jnp.pad(cmp_kv, kpad)
    slc_kv = jnp.pad(slc_kv, kpad)
    NB = (Tk + pad_k) // NSA_BLOCK
    cb = cmp_kv.reshape(B, NB, NSA_BLOCK, 2, N_KV_HEADS, HEAD_DIM)
    kc = compress(cb[:, :, :, 0], phi_pe[0], phi_w1[0], phi_b1[0], phi_w2[0])
    vc = compress(cb[:, :, :, 1], phi_pe[1], phi_w1[1], phi_b1[1], phi_w2[1])
    sb = slc_kv.reshape(B, NB, NSA_BLOCK, 2, N_KV_HEADS, HEAD_DIM).transpose(0, 4, 1, 2, 3, 5)
    n_sel = min(NSA_TOP_N, NB)
    qb = min(NSA_Q_BLOCK, Tq)
    nq = -(-Tq // qb)
    pad_q = nq * qb - Tq
    qs = jnp.pad(q, ((0, 0), (0, pad_q), (0, 0), (0, 0), (0, 0)))
    qs = qs.reshape(B, nq, qb, N_KV_HEADS, HEADS_PER_GROUP, HEAD_DIM).transpose(1, 0, 2, 3, 4, 5)
    ps = jnp.pad(q_pos, (0, pad_q), mode='edge').reshape(nq, qb)
    scale = HEAD_DIM ** -0.5
    blk = jnp.arange(NB, dtype=jnp.int32)
    offs = jnp.arange(NSA_BLOCK, dtype=jnp.int32)
    gather = jax.vmap(jax.vmap(lambda blocks, idx: blocks[idx]))

    def one_block(args):
        qq, pp = args
        s = jnp.einsum('btghd,bngd->bghtn', qq, kc).astype(jnp.float32) * scale
        cvalid = (blk[None, :] + 1) * NSA_BLOCK <= pp[:, None] + 1
        s = jnp.where(cvalid, s, NEG_INF)
        pc = jnp.where(cvalid, jax.nn.softmax(s, axis=-1), 0.0)
        o_cmp = jnp.einsum('bghtn,bngd->btghd', pc.astype(vc.dtype), vc)
        imp = pc.sum(axis=2)
        cur = (pp // NSA_BLOCK)[:, None]
        forced = (blk[None, :] == 0) | (blk[None, :] == cur) | (blk[None, :] == cur - 1)
        avail = blk[None, :] * NSA_BLOCK <= pp[:, None]
        imp = jnp.where(forced, HEADS_PER_GROUP + 1.0, imp)
        imp = jnp.where(avail, imp, -1.0)
        top_v, top_i = lax.top_k(imp, n_sel)
        g = gather(sb, top_i)
        s2 = jnp.einsum('btghd,bgtnld->bghtnl', qq, g[..., 0, :]).astype(jnp.float32) * scale
        kpos = top_i[..., None] * NSA_BLOCK + offs
        m2 = (top_v >= 0.0)[..., None] & (kpos <= pp[None, None, :, None, None])
        s2 = jnp.where(m2[:, :, None], s2, NEG_INF)
        p2 = jax.nn.softmax(s2.reshape(s2.shape[:4] + (n_sel * NSA_BLOCK,)), axis=-1).reshape(s2.shape)
        o_slc = jnp.einsum('bghtnl,bgtnld->btghd', p2.astype(g.dtype), g[..., 1, :])
        return o_cmp, o_slc

    o_cmp, o_slc = lax.map(one_block, (qs, ps))

    def unblock(o):
        o = o.transpose(1, 0, 2, 3, 4, 5).reshape(B, nq * qb, N_KV_HEADS, HEADS_PER_GROUP, HEAD_DIM)
        return o[:, :Tq]

    return unblock(o_cmp), unblock(o_slc)


def window_attn(qb, kb, vb, qpos, kpos):
    s = jnp.einsum('bntghd,bnkgd->bnghtk', qb, kb).astype(jnp.float32) * (HEAD_DIM ** -0.5)
    diff = qpos[:, :, None] - kpos[:, None, :]
    m = (diff >= 0) & (diff < NSA_WINDOW) & (kpos[:, None, :] >= 0)
    s = jnp.where(m[None, :, None, None], s, NEG_INF)
    p = jax.nn.softmax(s, axis=-1)
    return jnp.einsum('bnghtk,bnkgd->bntghd', p.astype(vb.dtype), vb)


def window_attn_banded(q, win_kv, pos):
    B, S = q.shape[:2]
    wb = WIN_Q_BLOCK if S % WIN_Q_BLOCK == 0 else S
    nb = S // wb
    wpad = jnp.pad(win_kv, ((0, 0), (NSA_WINDOW, 0), (0, 0), (0, 0), (0, 0)))
    idx = jnp.arange(nb, dtype=jnp.int32)[:, None] * wb + jnp.arange(NSA_WINDOW + wb, dtype=jnp.int32)[None, :]
    g = wpad[:, idx]
    kpos = idx - NSA_WINDOW
    o = window_attn(q.reshape(B, nb, wb, N_KV_HEADS, HEADS_PER_GROUP, HEAD_DIM),
                    g[:, :, :, 0], g[:, :, :, 1], pos.reshape(nb, wb), kpos)
    return o.reshape(B, S, N_KV_HEADS, HEADS_PER_GROUP, HEAD_DIM)


def token_mixer(a, pos, past, mix_w):
    w_in, phi_pe, phi_w1, phi_b1, phi_w2, w_attn_up, conv_w, w_conv_out, w_out = mix_w
    B, T, _ = a.shape
    q, kv, nsa_g, conv_h, conv_b, conv_c, merge_g = project(a, pos, w_in)
    cmp_new, slc_new, win_new = kv[:, :, 0], kv[:, :, 1], kv[:, :, 2]
    if past is None:
        cmp_full, slc_full = cmp_new, slc_new
        o_win = window_attn_banded(q, win_new, pos)
        win_keep = min(NSA_WINDOW, T)
        new_win = win_new[:, T - win_keep:]
        conv_buf = jnp.zeros((B, CONV_WIDTH - 1, D_CONV), a.dtype)
    else:
        cmp_past, slc_past, win_buf, conv_buf, past_len = past
        cmp_full = jnp.concatenate([cmp_past, cmp_new], axis=1)
        slc_full = jnp.concatenate([slc_past, slc_new], axis=1)
        nw = win_buf.shape[1]
        keys = jnp.concatenate([win_buf, win_new], axis=1)
        kpos = past_len - nw + jnp.arange(nw + T, dtype=jnp.int32)
        o_win = window_attn(q[:, None], keys[:, None, :, 0], keys[:, None, :, 1], pos[None], kpos[None])[:, 0]
        win_keep = min(NSA_WINDOW, past_len + T)
        new_win = keys[:, keys.shape[1] - win_keep:]
    o_cmp, o_slc = nsa_global(q, pos, cmp_full, slc_full, phi_pe, phi_w1, phi_b1, phi_w2)
    o_att = (nsa_g[:, :, 0, :, :, None] * o_cmp + nsa_g[:, :, 1, :, :, None] * o_slc
             + nsa_g[:, :, 2, :, :, None] * o_win)
    attn_out = o_att.reshape(B, T, Q_COLS) @ w_attn_up
    u = conv_c * conv_h
    u_ext = jnp.concatenate([conv_buf, u], axis=1)
    y = conv_w[0] * u_ext[:, 0:T]
    for j in range(1, CONV_WIDTH):
        y = y + conv_w[j] * u_ext[:, j:j + T]
    conv_out = (conv_b * y) @ w_conv_out
    new_conv = u_ext[:, u_ext.shape[1] - (CONV_WIDTH - 1):]
    merged = merge_g[:, :, 0] * attn_out + merge_g[:, :, 1] * conv_out
    return merged @ w_out, (cmp_new, slc_new, new_win, new_conv)


def decoder_layer(x, c, pos, past, mix_w, w_ada, b_ada, g_norm, w_ffn_gu, w_ffn_down):
    B, _, D = x.shape
    mod = (jax.nn.silu(c) @ w_ada + b_ada).reshape(B, 9, 1, D)
    a = rmsnorm(x, g_norm[0]) * (1.0 + mod[:, 1]) + mod[:, 0]
    x = x + FFN_HALF * mod[:, 2] * rmsnorm(swiglu(a, w_ffn_gu[0], w_ffn_down[0]), g_norm[1])
    a = rmsnorm(x, g_norm[2]) * (1.0 + mod[:, 4]) + mod[:, 3]
    m, states = token_mixer(a, pos, past, mix_w)
    x = x + mod[:, 5] * rmsnorm(m, g_norm[3])
    a = rmsnorm(x, g_norm[4]) * (1.0 + mod[:, 7]) + mod[:, 6]
    x = x + FFN_HALF * mod[:, 8] * rmsnorm(swiglu(a, w_ffn_gu[1], w_ffn_down[1]), g_norm[5])
    return x, states


def setup_inputs(seed: int = 0) -> dict:
    key = jax.random.key(seed)
    ks = jax.random.split(key, 24)
    n_pages = PAST_LEN // PAGE_SIZE
    n_phys = (DEC_BATCH * n_pages * 5) // 4
    win_keep = min(NSA_WINDOW, PAST_LEN)

    def nrm(k, shape, s):
        return jax.random.normal(k, shape, jnp.float32) * s

    page_table = jax.random.permutation(ks[8], n_phys)[:DEC_BATCH * n_pages].reshape(DEC_BATCH, n_pages).astype(jnp.int32)
    return {
        'x_prompt': nrm(ks[0], (BATCH, SEQ, D_MODEL), 1.0),
        'x_sample': nrm(ks[1], (DEC_BATCH, DEC_SEQ, D_MODEL), 1.0),
        'c_prompt': nrm(ks[2], (BATCH, D_MODEL), 1.0),
        'c_sample': nrm(ks[3], (DEC_BATCH, D_MODEL), 1.0),
        'cache_cmp_kv': nrm(ks[4], (DEPTH, n_phys, PAGE_SIZE, 2, N_KV_HEADS, HEAD_DIM), 1.0),
        'cache_slc_kv': nrm(ks[5], (DEPTH, n_phys, PAGE_SIZE, 2, N_KV_HEADS, HEAD_DIM), 1.0),
        'state_win_kv': nrm(ks[6], (DEPTH, DEC_BATCH, win_keep, 2, N_KV_HEADS, HEAD_DIM), 1.0),
        'state_conv': nrm(ks[7], (DEPTH, DEC_BATCH, CONV_WIDTH - 1, D_CONV), 1.0),
        'page_table': page_table,
        'w_ada': nrm(ks[9], (DEPTH, D_MODEL, 9 * D_MODEL), 0.5 * D_MODEL ** -0.5),
        'b_ada': nrm(ks[10], (DEPTH, 9 * D_MODEL), 0.02),
        'g_norm': 1.0 + nrm(ks[11], (DEPTH, 6, D_MODEL), 0.02),
        'w_ffn_gu': nrm(ks[12], (DEPTH, 2, D_MODEL, 2 * D_FF), D_MODEL ** -0.5),
        'w_ffn_down': nrm(ks[13], (DEPTH, 2, D_FF, D_MODEL), D_FF ** -0.5),
        'w_in': nrm(ks[14], (DEPTH, D_MODEL, IN_COLS), D_MODEL ** -0.5),
        'phi_pe': nrm(ks[15], (DEPTH, 2, NSA_BLOCK, HEAD_DIM), 0.02),
        'phi_w1': nrm(ks[16], (DEPTH, 2, NSA_BLOCK * HEAD_DIM, PHI_HIDDEN), (NSA_BLOCK * HEAD_DIM) ** -0.5),
        'phi_b1': nrm(ks[17], (DEPTH, 2, PHI_HIDDEN), 0.02),
        'phi_w2': nrm(ks[18], (DEPTH, 2, PHI_HIDDEN, HEAD_DIM), PHI_HIDDEN ** -0.5),
        'w_attn_up': nrm(ks[19], (DEPTH, Q_COLS, D_MODEL), Q_COLS ** -0.5),
        'conv_w': nrm(ks[20], (DEPTH, CONV_WIDTH, D_CONV), CONV_WIDTH ** -0.5),
        'w_conv_out': nrm(ks[21], (DEPTH, D_CONV, D_MODEL), D_CONV ** -0.5),
        'w_out': nrm(ks[22], (DEPTH, D_MODEL, D_MODEL), D_MODEL ** -0.5),
    }


def reference(x_prompt, x_sample, c_prompt, c_sample, cache_cmp_kv, cache_slc_kv, state_win_kv, state_conv,
              page_table, w_ada, b_ada, g_norm, w_ffn_gu, w_ffn_down, w_in, phi_pe, phi_w1, phi_b1, phi_w2,
              w_attn_up, conv_w, w_conv_out, w_out):
    n_dec = x_sample.shape[0]
    n_pages = page_table.shape[1]
    past_len = n_pages * PAGE_SIZE
    pos_p = jnp.arange(x_prompt.shape[1], dtype=jnp.int32)
    pos_s = past_len + jnp.arange(x_sample.shape[1], dtype=jnp.int32)
    h_p, h_s = x_prompt, x_sample
    st_p = [[], [], [], []]
    st_s = [[], [], [], []]
    for l in range(DEPTH):
        mix_w = (w_in[l], phi_pe[l], phi_w1[l], phi_b1[l], phi_w2[l], w_attn_up[l], conv_w[l], w_conv_out[l], w_out[l])
        h_p, sp = decoder_layer(h_p, c_prompt, pos_p, None, mix_w, w_ada[l], b_ada[l], g_norm[l], w_ffn_gu[l], w_ffn_down[l])
        cmp_past = cache_cmp_kv[l][page_table].reshape(n_dec, past_len, 2, N_KV_HEADS, HEAD_DIM)
        slc_past = cache_slc_kv[l][page_table].reshape(n_dec, past_len, 2, N_KV_HEADS, HEAD_DIM)
        past = (cmp_past, slc_past, state_win_kv[l], state_conv[l], past_len)
        h_s, ss = decoder_layer(h_s, c_sample, pos_s, past, mix_w, w_ada[l], b_ada[l], g_norm[l], w_ffn_gu[l], w_ffn_down[l])
        for i in range(4):
            st_p[i].append(sp[i])
            st_s[i].append(ss[i])
    cmp_kv_prompt = jnp.stack(st_p[0], axis=0)
    slc_kv_prompt = jnp.stack(st_p[1], axis=0)
    win_kv_prompt = jnp.stack(st_p[2], axis=0)
    conv_prompt = jnp.stack(st_p[3], axis=0)
    cmp_kv_sample = jnp.stack(st_s[0], axis=0)
    slc_kv_sample = jnp.stack(st_s[1], axis=0)
    win_kv_sample = jnp.stack(st_s[2], axis=0)
    conv_sample = jnp.stack(st_s[3], axis=0)
    return (h_p, h_s, cmp_kv_prompt, slc_kv_prompt, win_kv_prompt, conv_prompt,
            cmp_kv_sample, slc_kv_sample, win_kv_sample, conv_sample)
```

```python
import functools

import jax
import jax.numpy as jnp
from jax import lax
from jax.experimental import pallas as pl
from jax.experimental.pallas import tpu as pltpu

D_MODEL = 1024
N_HEADS = 8
HEAD_DIM = 64
N_KV_HEADS = 2
HEADS_PER_GROUP = N_HEADS // N_KV_HEADS
ROT_DIM = HEAD_DIM // 4
ROPE_THETA = 500000.0
NSA_BLOCK = 64
NSA_TOP_N = 16
NSA_WINDOW = 512
PHI_HIDDEN = 2 * HEAD_DIM
D_CONV = 512
CONV_WIDTH = 3
D_FF = 2816
FFN_HALF = 0.5
NORM_EPS = 1e-6
PAGE_SIZE = 128
Q_COLS = N_HEADS * HEAD_DIM
KV_BRANCH_COLS = 2 * N_KV_HEADS * HEAD_DIM
KV_COLS = 3 * KV_BRANCH_COLS
NSA_GATE_COLS = 3 * N_HEADS
CONV_COLS = 3 * D_CONV
MERGE_COLS = 2 * D_MODEL

LANES = 128
GATE_PAD = LANES
QKVG_COLS = Q_COLS + KV_COLS + GATE_PAD
MASK_INIT = -1e30
MASK_BIAS = -2e30
VMEM_LIMIT = 56 * 1024 * 1024
MIX_HALO = 16

BF16 = jnp.bfloat16
F32 = jnp.float32


def _dot(a, b):
    return jnp.dot(a, b, preferred_element_type=F32)


def _dot_nt(a, b):
    return lax.dot_general(a, b, (((1,), (1,)), ((), ())), preferred_element_type=F32)


def _rms(x, g):
    return x * lax.rsqrt(jnp.mean(x * x, axis=-1, keepdims=True) + NORM_EPS) * g


def _silu(x):
    return x * jax.nn.sigmoid(x)


def _params(*sem):
    return pltpu.CompilerParams(dimension_semantics=sem, vmem_limit_bytes=VMEM_LIMIT)


def _ada_kernel(c_ref, w_ref, b_ref, o_ref):
    c = _silu(c_ref[...]).astype(BF16)
    o_ref[...] = _dot(c, w_ref[...].astype(BF16)) + b_ref[...]


def _ada(c, w_ada, b_ada):
    rows = c.shape[0]
    n = w_ada.shape[1]
    tn = 9 * LANES
    return pl.pallas_call(
        _ada_kernel,
        grid=(n // tn,),
        in_specs=[
            pl.BlockSpec((rows, D_MODEL), lambda j: (0, 0)),
            pl.BlockSpec((D_MODEL, tn), lambda j: (0, j)),
            pl.BlockSpec((1, tn), lambda j: (0, j)),
        ],
        out_specs=pl.BlockSpec((rows, tn), lambda j: (0, j)),
        out_shape=jax.ShapeDtypeStruct((rows, n), F32),
        compiler_params=_params("parallel"),
        name="ada",
    )(c, w_ada, b_ada.reshape(1, n))


class _Mod:
    def __init__(self, mod, per_row, tiles_per_seq=1):
        self.per_row = per_row
        self.tiles_per_seq = tiles_per_seq
        self.arr = mod if per_row else mod.reshape(mod.shape[0] * 9, 1, D_MODEL)

    def spec(self, k):
        if self.per_row:
            return pl.BlockSpec((self.arr.shape[0], D_MODEL), lambda i, *_: (0, k))
        tps = self.tiles_per_seq
        return pl.BlockSpec((None, 1, D_MODEL), lambda i, *_: ((i // tps) * 9 + k, 0, 0))


def _gspec(k):
    return pl.BlockSpec((None, 1, D_MODEL), lambda i, *_: (k, 0, 0))


def _ffn_kernel(x_ref, shift_ref, scale_ref, gate_ref, gpre_ref, gpost_ref, wg_ref, wu_ref, wd_ref,
                o_ref, a_scr, acc_scr):
    j = pl.program_id(1)

    @pl.when(j == 0)
    def _():
        a = _rms(x_ref[...], gpre_ref[...]) * (1.0 + scale_ref[...]) + shift_ref[...]
        a_scr[...] = a.astype(BF16)
        acc_scr[...] = jnp.zeros_like(acc_scr)

    a = a_scr[...]
    h = _silu(_dot(a, wg_ref[...])) * _dot(a, wu_ref[...])
    acc_scr[...] += _dot(h.astype(BF16), wd_ref[...])

    @pl.when(j == pl.num_programs(1) - 1)
    def _():
        y = _rms(acc_scr[...], gpost_ref[...])
        o_ref[...] = x_ref[...] + FFN_HALF * gate_ref[...] * y


def _ffn(x, mod, g3, w_gu, w_down, sub, tm):
    m = x.shape[0]
    tf = D_FF // 2
    nf = D_FF // tf
    row = pl.BlockSpec((tm, D_MODEL), lambda i, j: (i, 0))
    return pl.pallas_call(
        _ffn_kernel,
        grid=(m // tm, nf),
        in_specs=[
            row, mod.spec(3 * sub), mod.spec(3 * sub + 1), mod.spec(3 * sub + 2),
            _gspec(2 * sub), _gspec(2 * sub + 1),
            pl.BlockSpec((D_MODEL, tf), lambda i, j: (0, j)),
            pl.BlockSpec((D_MODEL, tf), lambda i, j: (0, j + nf)),
            pl.BlockSpec((tf, D_MODEL), lambda i, j: (j, 0)),
        ],
        out_specs=row,
        out_shape=jax.ShapeDtypeStruct((m, D_MODEL), F32),
        scratch_shapes=[pltpu.VMEM((tm, D_MODEL), BF16), pltpu.VMEM((tm, D_MODEL), F32)],
        compiler_params=_params("parallel", "arbitrary"),
        name=f"ffn{sub}",
    )(x, mod.arr, mod.arr, mod.arr, g3, g3, w_gu, w_gu, w_down)


def _rope(x, cos, sin_lo, sin_hi):
    return x * cos + pltpu.roll(x, LANES - ROT_DIM // 2, 1) * sin_lo + pltpu.roll(x, ROT_DIM // 2, 1) * sin_hi


def _proj_kernel(x_ref, shift_ref, scale_ref, g_ref, cos_ref, slo_ref, shi_ref, w_ref,
                 q_ref, cmp_ref, slc_ref, win_ref, kvb_ref, gate_ref):
    a = (_rms(x_ref[...], g_ref[...]) * (1.0 + scale_ref[...]) + shift_ref[...]).astype(BF16)
    z = _dot(a, w_ref[...])
    cos, slo, shi = cos_ref[...], slo_ref[...], shi_ref[...]
    scale = HEAD_DIM ** -0.5
    for s in range(Q_COLS // LANES):
        q = _rope(z[:, s * LANES:(s + 1) * LANES], cos, slo, shi) * scale
        q_ref[:, s * LANES:(s + 1) * LANES] = q.astype(BF16)
    for br, out in enumerate((cmp_ref, slc_ref, win_ref)):
        c0 = Q_COLS + br * KV_BRANCH_COLS
        k = _rope(z[:, c0:c0 + LANES], cos, slo, shi)
        v = z[:, c0 + LANES:c0 + 2 * LANES]
        out[:, :LANES] = k
        out[:, LANES:] = v
        kvb_ref[:, br * KV_BRANCH_COLS:br * KV_BRANCH_COLS + LANES] = k.astype(BF16)
        kvb_ref[:, br * KV_BRANCH_COLS + LANES:(br + 1) * KV_BRANCH_COLS] = v.astype(BF16)
    gate_ref[...] = jax.nn.sigmoid(z[:, Q_COLS + KV_COLS:])


def _proj(x, mod, g3, tables, w_qkvg, tm):
    m = x.shape[0]
    t_rows = tables[0].shape[0]
    nt = t_rows // tm
    row = lambda n: pl.BlockSpec((tm, n), lambda i: (i, 0))
    tab = pl.BlockSpec((tm, LANES), lambda i: (i % nt, 0))
    return pl.pallas_call(
        _proj_kernel,
        grid=(m // tm,),
        in_specs=[row(D_MODEL), mod.spec(3), mod.spec(4), _gspec(2), tab, tab, tab,
                  pl.BlockSpec((D_MODEL, QKVG_COLS), lambda i: (0, 0))],
        out_specs=[row(Q_COLS), row(KV_BRANCH_COLS), row(KV_BRANCH_COLS), row(KV_BRANCH_COLS),
                   row(KV_COLS), row(GATE_PAD)],
        out_shape=[
            jax.ShapeDtypeStruct((m, Q_COLS), BF16),
            jax.ShapeDtypeStruct((m, KV_BRANCH_COLS), F32),
            jax.ShapeDtypeStruct((m, KV_BRANCH_COLS), F32),
            jax.ShapeDtypeStruct((m, KV_BRANCH_COLS), F32),
            jax.ShapeDtypeStruct((m, KV_COLS), BF16),
            jax.ShapeDtypeStruct((m, GATE_PAD), F32),
        ],
        compiler_params=_params("parallel"),
        name="proj",
    )(x, mod.arr, mod.arr, g3, *tables, w_qkvg)


def _rope_tables(pos):
    half = ROT_DIM // 2
    inv = jnp.power(ROPE_THETA, -jnp.arange(half, dtype=F32) * 2.0 / ROT_DIM)
    ang = pos.astype(F32)[:, None] * inv[None, :]
    cos, sin = jnp.cos(ang), jnp.sin(ang)
    n = pos.shape[0]
    ones = jnp.ones((n, HEAD_DIM - ROT_DIM), F32)
    zeros = jnp.zeros((n, HEAD_DIM - ROT_DIM), F32)
    zh = jnp.zeros((n, half), F32)
    c = jnp.concatenate([cos, cos, ones], axis=1)
    lo = jnp.concatenate([-sin, zh, zeros], axis=1)
    hi = jnp.concatenate([zh, sin, zeros], axis=1)
    return tuple(jnp.tile(t, (1, LANES // HEAD_DIM)) for t in (c, lo, hi))


def _compress_kernel(x_ref, pe_ref, w1_ref, b1_ref, w2_ref, o_ref, acc_scr):
    j = pl.program_id(0)
    nt = x_ref.shape[1] // KV_BRANCH_COLS

    @pl.when(j == 0)
    def _():
        acc_scr[...] = jnp.zeros_like(acc_scr)

    for kv in range(2):
        acc = acc_scr[kv]
        for t in range(0, nt, 2):
            xs = []
            for tt in (t, t + 1):
                c0 = tt * KV_BRANCH_COLS + kv * LANES
                xs.append((x_ref[:, c0:c0 + LANES] + pe_ref[kv, tt]).astype(BF16))
            acc = acc + _dot(jnp.concatenate(xs, axis=1), w1_ref[kv, t * LANES:(t + 2) * LANES, :])
        acc_scr[kv] = acc

    @pl.when(j == pl.num_programs(0) - 1)
    def _():
        for kv in range(2):
            h = _silu(acc_scr[kv] + b1_ref[kv])
            o_ref[kv] = _dot(h.astype(BF16), w2_ref[kv]).astype(o_ref.dtype)


def _compress_weights(phi_pe, phi_w1, phi_b1, phi_w2):
    eye = jnp.eye(N_KV_HEADS, dtype=F32)
    w1 = phi_w1.reshape(2, NSA_BLOCK, HEAD_DIM, PHI_HIDDEN)
    w1 = jnp.einsum("gh,ktdj->ktgdhj", eye, w1).reshape(2, NSA_BLOCK * LANES, N_KV_HEADS * PHI_HIDDEN)
    w2 = jnp.einsum("gh,kjd->kgjhd", eye, phi_w2).reshape(2, N_KV_HEADS * PHI_HIDDEN, LANES)
    pe = jnp.tile(phi_pe, (1, 1, N_KV_HEADS)).reshape(2, NSA_BLOCK, 1, LANES)
    b1 = jnp.tile(phi_b1, (1, N_KV_HEADS)).reshape(2, 1, N_KV_HEADS * PHI_HIDDEN)
    return pe, w1.astype(BF16), b1, w2.astype(BF16)


def _compress(x_blocks, cw, out_dtype=BF16):
    pe, w1, b1, w2 = cw
    rows = x_blocks.shape[0]
    tchunk = 8
    nchunk = NSA_BLOCK // tchunk
    hid = N_KV_HEADS * PHI_HIDDEN
    return pl.pallas_call(
        _compress_kernel,
        grid=(nchunk,),
        in_specs=[
            pl.BlockSpec((rows, tchunk * KV_BRANCH_COLS), lambda j: (0, j)),
            pl.BlockSpec((2, tchunk, 1, LANES), lambda j: (0, j, 0, 0)),
            pl.BlockSpec((2, tchunk * LANES, hid), lambda j: (0, j, 0)),
            pl.BlockSpec((2, 1, hid), lambda j: (0, 0, 0)),
            pl.BlockSpec((2, hid, LANES), lambda j: (0, 0, 0)),
        ],
        out_specs=pl.BlockSpec((2, rows, LANES), lambda j: (0, 0, 0)),
        out_shape=jax.ShapeDtypeStruct((2, rows, LANES), out_dtype),
        scratch_shapes=[pltpu.VMEM((2, rows, hid), F32)],
        compiler_params=_params("arbitrary"),
        name="compress",
    )(x_blocks, pe, w1, b1, w2)


def _gather_compress_kernel(pt_ref, *refs, n_pages, group):
    del pt_ref
    pages = refs[:n_pages]
    pe_ref, w1_ref, b1_ref, w2_ref, o_ref, xk_scr, xv_scr = refs[n_pages:]
    x_scr = (xk_scr, xv_scr)
    slot = pl.program_id(0) % group
    for k in range(n_pages):
        r0 = pl.multiple_of((slot * n_pages + k) * PAGE_SIZE, PAGE_SIZE)
        for kv in range(2):
            x_scr[kv][pl.ds(r0, PAGE_SIZE), :] = pages[k][:, kv * LANES:(kv + 1) * LANES]

    @pl.when(slot == group - 1)
    def _():
        rows = group * n_pages * PAGE_SIZE // NSA_BLOCK
        for kv in range(2):
            acc = jnp.zeros((rows, N_KV_HEADS * PHI_HIDDEN), F32)
            for t in range(0, NSA_BLOCK, 2):
                xs = []
                for tt in (t, t + 1):
                    x = x_scr[kv][pl.ds(tt, rows, stride=NSA_BLOCK), :]
                    xs.append((x + pe_ref[kv, tt]).astype(BF16))
                acc = acc + _dot(jnp.concatenate(xs, axis=1), w1_ref[kv, t * LANES:(t + 2) * LANES, :])
            h = _silu(acc + b1_ref[kv])
            o_ref[kv] = _dot(h.astype(BF16), w2_ref[kv])


def _page_specs(n_pages):
    return [pl.BlockSpec((None, PAGE_SIZE, KV_BRANCH_COLS), lambda s, pt, k=k: (pt[s, k], 0, 0))
            for k in range(n_pages)]


def _gather_compress(cache, page_table, cw):
    pe, w1, b1, w2 = cw
    n_dec, n_pages = page_table.shape
    group = 8
    rows = group * n_pages * PAGE_SIZE // NSA_BLOCK
    const = lambda a: pl.BlockSpec(a.shape, lambda s, pt: (0,) * a.ndim)
    return pl.pallas_call(
        functools.partial(_gather_compress_kernel, n_pages=n_pages, group=group),
        grid_spec=pltpu.PrefetchScalarGridSpec(
            num_scalar_prefetch=1,
            grid=(n_dec,),
            in_specs=_page_specs(n_pages) + [const(pe), const(w1), const(b1), const(w2)],
            out_specs=pl.BlockSpec((2, rows, LANES), lambda s, pt: (0, s // group, 0)),
            scratch_shapes=[pltpu.VMEM((group * n_pages * PAGE_SIZE, LANES), F32)] * 2,
        ),
        out_shape=jax.ShapeDtypeStruct((2, n_dec // group * rows, LANES), F32),
        compiler_params=_params("arbitrary"),
        name="gather_compress",
    )(page_table, *([cache] * n_pages), pe, w1, b1, w2)


def _group_queries(q, g):
    lane = lax.broadcasted_iota(jnp.int32, (1, LANES), 1)
    keep = (lane < HEAD_DIM) if g == 0 else (lane >= HEAD_DIM)
    zero = jnp.zeros((), q.dtype)
    return jnp.concatenate(
        [jnp.where(keep, q[:, j * LANES:(j + 1) * LANES], zero) for j in range(HEADS_PER_GROUP)], axis=0)


def _select_blocks(imp, nblk, n_sel):
    nb = imp.shape[1]
    rank = jnp.zeros(imp.shape, F32)
    for m in range(nb):
        col = imp[:, m:m + 1]
        beats = jnp.where(col > imp, 1.0, jnp.where(col == imp, jnp.where(m < nblk, 1.0, 0.0), 0.0))
        rank = rank + beats
    return jnp.where(rank < n_sel, jnp.where(imp >= 0.0, 1.0, 0.0), 0.0)


def _flash(qg, kv_ref, tile0, ntiles, tk, bias_fn):
    rows = qg.shape[0]
    t = rows // HEADS_PER_GROUP

    def body(j, carry):
        m, l, acc = carry
        ks = pl.multiple_of((tile0 + j) * tk, tk)
        k = kv_ref[pl.ds(ks, tk), :LANES]
        v = kv_ref[pl.ds(ks, tk), LANES:]
        s = _dot_nt(qg, k).reshape(HEADS_PER_GROUP, t, tk) + bias_fn(ks)[None]
        s = s.reshape(rows, tk)
        m_new = jnp.maximum(m, jnp.max(s, axis=-1, keepdims=True))
        alpha = jnp.exp(m - m_new)
        p = jnp.exp(s - m_new)
        l = alpha * l + jnp.sum(p, axis=-1, keepdims=True)
        acc = alpha * acc + _dot(p.astype(BF16), v)
        return m_new, l, acc

    init = (jnp.full((rows, 1), MASK_INIT, F32), jnp.zeros((rows, 1), F32), jnp.zeros((rows, LANES), F32))
    _, l, acc = lax.fori_loop(0, ntiles, body, init)
    return acc / l


def _attn_kernel(q_ref, gate_ref, kc_ref, vc_ref, slc_ref, win_ref, o_ref, *, tq, nb):
    i = pl.program_id(1)
    p0 = i * tq
    pos = p0 + lax.broadcasted_iota(jnp.int32, (tq, 1), 0)
    nblk = lax.broadcasted_iota(jnp.int32, (1, nb), 1)
    q = q_ref[...]
    gates = gate_ref[...]
    lane = lax.broadcasted_iota(jnp.int32, (1, LANES), 1)
    tk = tq
    outs = []
    for g in range(N_KV_HEADS):
        qg = _group_queries(q, g)

        cvalid = (nblk + 1) * NSA_BLOCK <= pos + 1
        sc = _dot_nt(qg, kc_ref[...]).reshape(HEADS_PER_GROUP, tq, nb)
        sc = jnp.where(cvalid[None], sc, MASK_INIT)
        e = jnp.exp(sc - jnp.max(sc, axis=-1, keepdims=True))
        pc = jnp.where(cvalid[None], e / jnp.sum(e, axis=-1, keepdims=True), 0.0)
        o_cmp = _dot(pc.reshape(HEADS_PER_GROUP * tq, nb).astype(BF16), vc_ref[...])

        imp = jnp.sum(pc, axis=0)
        cur = pos // NSA_BLOCK
        forced = HEADS_PER_GROUP + 1.0
        imp = jnp.where(nblk == 0, forced, jnp.where(nblk == cur, forced, jnp.where(nblk == cur - 1, forced, imp)))
        imp = jnp.where(nblk * NSA_BLOCK <= pos, imp, -1.0)
        sel_b = _select_blocks(imp, nblk, NSA_TOP_N).astype(BF16)

        def slc_bias(ks):
            kpos = ks + lax.broadcasted_iota(jnp.int32, (1, tk), 1)
            blk_of_key = jnp.where(
                lax.broadcasted_iota(jnp.int32, (nb, 1), 0) == kpos // NSA_BLOCK, 1.0, 0.0).astype(BF16)
            chosen = _dot(sel_b, blk_of_key)
            return jnp.where(kpos <= pos, jnp.where(chosen > 0.5, 0.0, MASK_BIAS), MASK_BIAS)

        o_slc = _flash(qg, slc_ref, 0, i + 1, tk, slc_bias)

        def win_bias(ks):
            diff = pos - (ks + lax.broadcasted_iota(jnp.int32, (1, tk), 1))
            return jnp.where(diff >= 0, jnp.where(diff < NSA_WINDOW, 0.0, MASK_BIAS), MASK_BIAS)

        wt0 = jnp.maximum(i - NSA_WINDOW // tk, 0)
        o_win = _flash(qg, win_ref, wt0, i + 1 - wt0, tk, win_bias)

        per_head = []
        for h in range(HEADS_PER_GROUP):
            head = g * HEADS_PER_GROUP + h
            rows = slice(h * tq, (h + 1) * tq)
            per_head.append(gates[:, head:head + 1] * o_cmp[rows]
                            + gates[:, N_HEADS + head:N_HEADS + head + 1] * o_slc[rows]
                            + gates[:, 2 * N_HEADS + head:2 * N_HEADS + head + 1] * o_win[rows])
        outs.append(per_head)
    for j in range(HEADS_PER_GROUP):
        o_ref[:, j * LANES:(j + 1) * LANES] = jnp.where(lane < HEAD_DIM, outs[0][j], outs[1][j]).astype(o_ref.dtype)


def _attn_prompt(q, gates, kc, vc, kvb, n_seq, seq, tq):
    m = q.shape[0]
    nb = seq // NSA_BLOCK
    nq = seq // tq
    row = lambda n: pl.BlockSpec((tq, n), lambda b, i: (b * nq + i, 0))
    cblk = pl.BlockSpec((nb, LANES), lambda b, i: (b, 0))
    return pl.pallas_call(
        functools.partial(_attn_kernel, tq=tq, nb=nb),
        grid=(n_seq, nq),
        in_specs=[row(Q_COLS), row(GATE_PAD), cblk, cblk,
                  pl.BlockSpec((seq, KV_BRANCH_COLS), lambda b, i: (b, 1)),
                  pl.BlockSpec((seq, KV_BRANCH_COLS), lambda b, i: (b, 2))],
        out_specs=row(Q_COLS),
        out_shape=jax.ShapeDtypeStruct((m, Q_COLS), BF16),
        compiler_params=_params("parallel", "arbitrary"),
        name="attn_prompt",
    )(q, gates, kc, vc, kvb, kvb)


def _softmax_pv(score_tiles, value_tiles):
    m = functools.reduce(jnp.maximum, [jnp.max(s, axis=-1, keepdims=True) for s in score_tiles])
    ps = [jnp.exp(s - m) for s in score_tiles]
    l = functools.reduce(jnp.add, [jnp.sum(p, axis=-1, keepdims=True) for p in ps])
    o = functools.reduce(jnp.add, [_dot(p.astype(BF16), v) for p, v in zip(ps, value_tiles)])
    return o / l


def _attn_decode_kernel(pt_ref, *refs, n_pages, pos, nbp):
    del pt_ref
    pages = refs[:n_pages]
    q_ref, kvb_ref, gate_ref, kvc_ref, win_ref, o_ref = refs[n_pages:]
    tk = PAGE_SIZE
    lane = lax.broadcasted_iota(jnp.int32, (1, LANES), 1)
    head = lax.broadcasted_iota(jnp.int32, (N_HEADS, 1), 0)
    low_group = head < HEADS_PER_GROUP
    low_lanes = lane < HEAD_DIM

    q32 = q_ref[...].astype(F32)
    qm = jnp.zeros((N_HEADS, LANES), F32)
    for j in range(HEADS_PER_GROUP):
        qm = jnp.where(head % HEADS_PER_GROUP == j, q32[:, j * LANES:(j + 1) * LANES], qm)
    qm = jnp.where(low_group, jnp.where(low_lanes, qm, 0.0), jnp.where(low_lanes, 0.0, qm)).astype(BF16)

    gate_row = gate_ref[...]
    gate_col = [jnp.sum(jnp.where(lane == head + br * N_HEADS, gate_row, 0.0), axis=-1, keepdims=True)
                for br in range(3)]
    kv_new = kvb_ref[...].astype(F32)
    first_row = lax.broadcasted_iota(jnp.int32, (tk, 1), 0) == 0

    def new_key_tile(c0):
        return jnp.where(first_row, kv_new[:, c0:c0 + LANES], 0.0).astype(BF16)

    nblk = lax.broadcasted_iota(jnp.int32, (1, nbp), 1)
    cvalid = (nblk + 1) * NSA_BLOCK <= pos + 1
    sc = jnp.where(cvalid, _dot_nt(qm, kvc_ref[0].astype(BF16)), MASK_INIT)
    e = jnp.exp(sc - jnp.max(sc, axis=-1, keepdims=True))
    pc = jnp.where(cvalid, e / jnp.sum(e, axis=-1, keepdims=True), 0.0)
    o_cmp = _dot(pc.astype(BF16), kvc_ref[1].astype(BF16))

    imp = jnp.where(low_group,
                    jnp.sum(jnp.where(low_group, pc, 0.0), axis=0, keepdims=True),
                    jnp.sum(jnp.where(low_group, 0.0, pc), axis=0, keepdims=True))
    cur = pos // NSA_BLOCK
    forced = HEADS_PER_GROUP + 1.0
    imp = jnp.where(nblk == 0, forced, jnp.where(nblk == cur, forced, jnp.where(nblk == cur - 1, forced, imp)))
    imp = jnp.where(nblk * NSA_BLOCK <= pos, imp, -1.0)
    sel = _select_blocks(imp, nblk, NSA_TOP_N)

    def sel_bias(n):
        return jnp.where(sel[:, n:n + 1] > 0.5, 0.0, MASK_BIAS)

    s_tiles, v_tiles = [], []
    blocks_per_page = tk // NSA_BLOCK
    for k in range(n_pages):
        page = pages[k]
        s = _dot_nt(qm, page[:, :LANES].astype(BF16))
        bias = sel_bias(k * blocks_per_page + blocks_per_page - 1)
        for b in range(blocks_per_page - 2, -1, -1):
            bias = jnp.where(lane < (b + 1) * NSA_BLOCK, sel_bias(k * blocks_per_page + b), bias)
        s_tiles.append(s + bias)
        v_tiles.append(page[:, LANES:].astype(BF16))
    s_new = _dot_nt(qm, new_key_tile(KV_BRANCH_COLS))
    s_tiles.append(s_new + jnp.where(lane == 0, sel_bias(n_pages * blocks_per_page), MASK_BIAS))
    v_tiles.append(new_key_tile(KV_BRANCH_COLS + LANES))
    o_slc = _softmax_pv(s_tiles, v_tiles)

    n_win = win_ref.shape[0]
    s_tiles, v_tiles = [], []
    for k in range(n_win // tk):
        diff = n_win - (k * tk + lane)
        bias = jnp.where(diff < NSA_WINDOW, 0.0, MASK_BIAS)
        s_tiles.append(_dot_nt(qm, win_ref[k * tk:(k + 1) * tk, :LANES].astype(BF16)) + bias)
        v_tiles.append(win_ref[k * tk:(k + 1) * tk, LANES:].astype(BF16))
    s_tiles.append(_dot_nt(qm, new_key_tile(2 * KV_BRANCH_COLS)) + jnp.where(lane == 0, 0.0, MASK_BIAS))
    v_tiles.append(new_key_tile(2 * KV_BRANCH_COLS + LANES))
    o_win = _softmax_pv(s_tiles, v_tiles)

    o = gate_col[0] * o_cmp + gate_col[1] * o_slc + gate_col[2] * o_win
    for j in range(HEADS_PER_GROUP):
        pair = jnp.where(low_lanes, o[j:j + 1], o[j + HEADS_PER_GROUP:j + HEADS_PER_GROUP + 1])
        o_ref[:, j * LANES:(j + 1) * LANES] = pair.astype(o_ref.dtype)


def _attn_decode(q, kvb, gates, kvc, cache_slc, win_state, page_table, pos):
    n_dec, n_pages = page_table.shape
    nbp = kvc.shape[2]
    n_win = win_state.shape[1]
    per_seq = lambda a: pl.BlockSpec((None, 1, a.shape[-1]), lambda s, pt: (s, 0, 0))
    q3, kvb3, g3 = (a.reshape(n_dec, 1, a.shape[-1]) for a in (q, kvb, gates))
    out = pl.pallas_call(
        functools.partial(_attn_decode_kernel, n_pages=n_pages, pos=pos, nbp=nbp),
        grid_spec=pltpu.PrefetchScalarGridSpec(
            num_scalar_prefetch=1,
            grid=(n_dec,),
            in_specs=_page_specs(n_pages) + [
                per_seq(q3), per_seq(kvb3), per_seq(g3),
                pl.BlockSpec((2, None, nbp, LANES), lambda s, pt: (0, s, 0, 0)),
                pl.BlockSpec((None, n_win, KV_BRANCH_COLS), lambda s, pt: (s, 0, 0)),
            ],
            out_specs=pl.BlockSpec((None, 1, Q_COLS), lambda s, pt: (s, 0, 0)),
        ),
        out_shape=jax.ShapeDtypeStruct((n_dec, 1, Q_COLS), BF16),
        compiler_params=_params("arbitrary"),
        name="attn_decode",
    )(page_table, *([cache_slc] * n_pages), q3, kvb3, g3, kvc, win_state)
    return out.reshape(n_dec, Q_COLS)


def _mix_kernel(*refs, tm, halo, tiles_per_seq):
    if halo:
        (x_ref, xh_ref, o_ref, shift_ref, scale_ref, gate_ref, gpre_ref, gpost_ref, cw_ref,
         wconv_ref, wmerge_ref, wup_ref, wco_ref, wout_ref, y_ref, ulast_ref, u_scr) = refs
        x = x_ref[...]
        xe = jnp.concatenate([xh_ref[...], x], axis=0)
    else:
        (x_ref, um1_ref, um2_ref, o_ref, shift_ref, scale_ref, gate_ref, gpre_ref, gpost_ref, cw_ref,
         wconv_ref, wmerge_ref, wup_ref, wco_ref, wout_ref, y_ref, ulast_ref) = refs
        x = x_ref[...]
        xe = x
    h0 = xe.shape[0] - tm
    a = (_rms(xe, gpre_ref[...]) * (1.0 + scale_ref[...]) + shift_ref[...]).astype(BF16)
    zc = _dot(a, wconv_ref[...])
    u = zc[:, 2 * D_CONV:] * zc[:, :D_CONV]
    cb = zc[h0:, D_CONV:2 * D_CONV]
    if halo:
        keep_halo = jnp.where(pl.program_id(0) % tiles_per_seq == 0, 0.0, 1.0)
        rows = lax.broadcasted_iota(jnp.int32, (xe.shape[0], 1), 0)
        u = jnp.where(rows < h0, u * keep_halo, u)
        u_scr[...] = u
        um1 = u_scr[h0 - 1:h0 - 1 + tm, :]
        um2 = u_scr[h0 - 2:h0 - 2 + tm, :]
        u0 = u[h0:]
        ulast_ref[...] = u[tm:]
    else:
        um1, um2, u0 = um1_ref[...], um2_ref[...], u
        ulast_ref[...] = u
    cw = cw_ref[...]
    y = cw[0:1] * um2 + cw[1:2] * um1 + cw[2:3] * u0
    conv_out = _dot((cb * y).astype(BF16), wco_ref[...])
    attn_out = _dot(o_ref[...], wup_ref[...])
    mg = jax.nn.sigmoid(_dot(a[h0:], wmerge_ref[...]))
    merged = mg[:, :D_MODEL] * attn_out + mg[:, D_MODEL:] * conv_out
    mixed = _dot(merged.astype(BF16), wout_ref[...])
    y_ref[...] = x + gate_ref[...] * _rms(mixed, gpost_ref[...])


def _mix(x, o_att, mod, g3, conv_w, weights, tm, tiles_per_seq=None, prev=None):
    m = x.shape[0]
    halo = prev is None
    h0 = MIX_HALO
    row = lambda n: pl.BlockSpec((tm, n), lambda i: (i, 0))
    full = lambda a: pl.BlockSpec(a.shape, lambda i: (0,) * a.ndim)
    common = [row(Q_COLS), mod.spec(3), mod.spec(4), mod.spec(5), _gspec(2), _gspec(3), full(conv_w)]
    common += [full(w) for w in weights]
    common_args = [o_att, mod.arr, mod.arr, mod.arr, g3, g3, conv_w, *weights]
    if halo:
        hb = tm // h0
        in_specs = [row(D_MODEL), pl.BlockSpec((h0, D_MODEL), lambda i: (jnp.maximum(i * hb - 1, 0), 0))] + common
        args = [x, x] + common_args
        scratch = [pltpu.VMEM((tm + h0, D_CONV), F32)]
        ulast = (jax.ShapeDtypeStruct((m // tm * h0, D_CONV), F32), pl.BlockSpec((h0, D_CONV), lambda i: (i, 0)))
    else:
        in_specs = [row(D_MODEL), row(D_CONV), row(D_CONV)] + common
        args = [x, prev[0], prev[1]] + common_args
        scratch = []
        ulast = (jax.ShapeDtypeStruct((m, D_CONV), F32), row(D_CONV))
    return pl.pallas_call(
        functools.partial(_mix_kernel, tm=tm, halo=halo, tiles_per_seq=tiles_per_seq),
        grid=(m // tm,),
        in_specs=in_specs,
        out_specs=[row(D_MODEL), ulast[1]],
        out_shape=[jax.ShapeDtypeStruct((m, D_MODEL), F32), ulast[0]],
        scratch_shapes=scratch,
        compiler_params=_params("parallel"),
        name="mix",
    )(*args)


def _head_pair_perm():
    order = []
    for j in range(HEADS_PER_GROUP):
        for g in range(N_KV_HEADS):
            head = g * HEADS_PER_GROUP + j
            order.extend(range(head * HEAD_DIM, (head + 1) * HEAD_DIM))
    return jnp.array(order, dtype=jnp.int32)


def _layer_weights(w_in, w_attn_up, w_conv_out, w_out):
    perm = _head_pair_perm()
    o1 = Q_COLS
    o2 = o1 + KV_COLS
    o3 = o2 + NSA_GATE_COLS
    o4 = o3 + CONV_COLS
    w_q = w_in[:, :o1][:, perm]
    w_g = jnp.pad(w_in[:, o2:o3], ((0, 0), (0, GATE_PAD - NSA_GATE_COLS)))
    w_qkvg = jnp.concatenate([w_q, w_in[:, o1:o2], w_g], axis=1).astype(BF16)
    mix_w = (w_in[:, o3:o4].astype(BF16), w_in[:, o4:].astype(BF16), w_attn_up[perm].astype(BF16),
             w_conv_out.astype(BF16), w_out.astype(BF16))
    return w_qkvg, mix_w


def kernel(x_prompt, x_sample, c_prompt, c_sample, cache_cmp_kv, cache_slc_kv, state_win_kv, state_conv,
           page_table, w_ada, b_ada, g_norm, w_ffn_gu, w_ffn_down, w_in, phi_pe, phi_w1, phi_b1, phi_w2,
           w_attn_up, conv_w, w_conv_out, w_out):
    n_seq, seq, _ = x_prompt.shape
    n_dec = x_sample.shape[0]
    assert w_ada.shape[0] == 1
    l = 0

    mod_all = _ada(jnp.concatenate([c_prompt, c_sample], axis=0), w_ada[l], b_ada[l])
    g3 = g_norm[l].reshape(6, 1, D_MODEL)
    w_gu = w_ffn_gu[l].astype(BF16)
    w_down = w_ffn_down[l].astype(BF16)
    w_qkvg, mix_w = _layer_weights(w_in[l], w_attn_up[l], w_conv_out[l], w_out[l])
    cw = _compress_weights(phi_pe[l], phi_w1[l], phi_b1[l], phi_w2[l])

    tm = 512
    tq = 256
    m = n_seq * seq
    mod_p = _Mod(mod_all[:n_seq], per_row=False, tiles_per_seq=seq // tm)
    x0 = x_prompt.reshape(m, D_MODEL)
    x1 = _ffn(x0, mod_p, g3, w_gu[0], w_down[0], 0, tm)
    tables = _rope_tables(jnp.arange(seq, dtype=jnp.int32))
    q, cmp_new, slc_new, win_new, kvb, gates = _proj(x1, mod_p, g3, tables, w_qkvg, tm)
    nb = seq // NSA_BLOCK
    kvc = _compress(cmp_new.reshape(n_seq * nb, NSA_BLOCK * KV_BRANCH_COLS), cw)
    o_att = _attn_prompt(q, gates, kvc[0], kvc[1], kvb, n_seq, seq, tq)
    x2, ulast = _mix(x1, o_att, mod_p, g3, conv_w[l], mix_w, tm, tiles_per_seq=seq // tm)
    y_prompt = _ffn(x2, mod_p, g3, w_gu[1], w_down[1], 2, tm).reshape(n_seq, seq, D_MODEL)

    kv_shape = (1, n_seq, seq, 2, N_KV_HEADS, HEAD_DIM)
    cmp_kv_prompt = cmp_new.reshape(kv_shape)
    slc_kv_prompt = slc_new.reshape(kv_shape)
    win_keep = min(NSA_WINDOW, seq)
    win_kv_prompt = win_new.reshape(kv_shape)[:, :, seq - win_keep:]
    conv_prompt = ulast.reshape(n_seq, seq // tm, MIX_HALO, D_CONV)[:, -1, MIX_HALO - (CONV_WIDTH - 1):][None]

    n_pages = page_table.shape[1]
    past_len = n_pages * PAGE_SIZE
    pos_s = past_len + jnp.arange(x_sample.shape[1], dtype=jnp.int32)
    assert x_sample.shape[1] == 1 and past_len % NSA_BLOCK == 0 and state_win_kv.shape[2] <= past_len
    mod_s = _Mod(mod_all[n_seq:], per_row=True)
    xs1 = _ffn(x_sample.reshape(n_dec, D_MODEL), mod_s, g3, w_gu[0], w_down[0], 0, n_dec)
    tables_s = _rope_tables(jnp.broadcast_to(pos_s, (n_dec,)))
    q_s, cmp_s, slc_s, win_s, kvb_s, gates_s = _proj(xs1, mod_s, g3, tables_s, w_qkvg, n_dec)
    nb_past = past_len // NSA_BLOCK
    nb_pad = -(-(nb_past + 1) // NSA_BLOCK) * NSA_BLOCK
    cache_cmp = cache_cmp_kv[l].reshape(-1, PAGE_SIZE, KV_BRANCH_COLS)
    cache_slc = cache_slc_kv[l].reshape(-1, PAGE_SIZE, KV_BRANCH_COLS)
    kvc_past = _gather_compress(cache_cmp, page_table, cw).reshape(2, n_dec, nb_past, LANES)
    new_block = jnp.pad(cmp_s[:, None, :], ((0, 0), (0, NSA_BLOCK - 1), (0, 0)))
    kvc_new = _compress(new_block.reshape(n_dec, NSA_BLOCK * KV_BRANCH_COLS), cw, F32)[:, :, None, :]
    kvc_s = jnp.concatenate(
        [kvc_past, kvc_new, jnp.zeros((2, n_dec, nb_pad - nb_past - 1, LANES), F32)], axis=2)
    win_state = state_win_kv[l].reshape(n_dec, -1, KV_BRANCH_COLS)
    o_att_s = _attn_decode(q_s, kvb_s, gates_s, kvc_s, cache_slc, win_state, page_table, past_len)
    conv_state = state_conv[l]
    xs2, u_s = _mix(xs1, o_att_s, mod_s, g3, conv_w[l], mix_w, n_dec,
                    prev=(conv_state[:, CONV_WIDTH - 2], conv_state[:, CONV_WIDTH - 3]))
    y_sample = _ffn(xs2, mod_s, g3, w_gu[1], w_down[1], 2, n_dec).reshape(x_sample.shape)

    kvs = (1, n_dec, 1, 2, N_KV_HEADS, HEAD_DIM)
    win_keep_s = min(NSA_WINDOW, past_len + 1)
    win_kv_sample = jnp.concatenate([win_state, win_s[:, None, :]], axis=1)[:, -win_keep_s:]
    win_kv_sample = win_kv_sample.reshape(1, n_dec, win_keep_s, 2, N_KV_HEADS, HEAD_DIM)
    conv_sample = jnp.concatenate([conv_state[:, 1:], u_s[:, None, :]], axis=1)[None]
    return (y_prompt, y_sample, cmp_kv_prompt, slc_kv_prompt, win_kv_prompt, conv_prompt,
            cmp_s.reshape(kvs), slc_s.reshape(kvs), win_kv_sample, conv_sample)
```

```python
import functools

import jax
import jax.numpy as jnp
from jax import lax
from jax.experimental import pallas as pl
from jax.experimental.pallas import tpu as pltpu

D_MODEL = 1024
N_HEADS = 8
HEAD_DIM = 64
N_KV_HEADS = 2
HEADS_PER_GROUP = N_HEADS // N_KV_HEADS
ROT_DIM = HEAD_DIM // 4
ROPE_THETA = 500000.0
NSA_BLOCK = 64
NSA_TOP_N = 16
NSA_WINDOW = 512
PHI_HIDDEN = 2 * HEAD_DIM
D_CONV = 512
CONV_WIDTH = 3
D_FF = 2816
FFN_HALF = 0.5
NORM_EPS = 1e-6
PAGE_SIZE = 128
Q_COLS = N_HEADS * HEAD_DIM
KV_BRANCH_COLS = 2 * N_KV_HEADS * HEAD_DIM
KV_COLS = 3 * KV_BRANCH_COLS
NSA_GATE_COLS = 3 * N_HEADS
CONV_COLS = 3 * D_CONV
MERGE_COLS = 2 * D_MODEL
GD_ROWS = N_KV_HEADS * HEAD_DIM

LANES = 128
GATE_PAD = LANES
QKVG_COLS = Q_COLS + KV_COLS + GATE_PAD
MASK_INIT = -1e30
MASK_BIAS = -2e30
VMEM_LIMIT = 56 * 1024 * 1024
MIX_HALO = 16

BF16 = jnp.bfloat16
F32 = jnp.float32


def _dot(a, b):
    return jnp.dot(a, b, preferred_element_type=F32)


def _dot_nt(a, b):
    return lax.dot_general(a, b, (((1,), (1,)), ((), ())), preferred_element_type=F32)


def _rms(x, g):
    return x * lax.rsqrt(jnp.mean(x * x, axis=-1, keepdims=True) + NORM_EPS) * g


def _silu(x):
    return x * jax.nn.sigmoid(x)


def _params(*sem):
    return pltpu.CompilerParams(dimension_semantics=sem, vmem_limit_bytes=VMEM_LIMIT)


def _ada_kernel(c_ref, w_ref, b_ref, o_ref):
    c = _silu(c_ref[...]).astype(BF16)
    o_ref[...] = _dot(c, w_ref[...].astype(BF16)) + b_ref[...]


def _ada(c, w_ada, b_ada):
    rows = c.shape[0]
    n = w_ada.shape[1]
    tn = 9 * LANES
    return pl.pallas_call(
        _ada_kernel,
        grid=(n // tn,),
        in_specs=[
            pl.BlockSpec((rows, D_MODEL), lambda j: (0, 0)),
            pl.BlockSpec((D_MODEL, tn), lambda j: (0, j)),
            pl.BlockSpec((1, tn), lambda j: (0, j)),
        ],
        out_specs=pl.BlockSpec((rows, tn), lambda j: (0, j)),
        out_shape=jax.ShapeDtypeStruct((rows, n), F32),
        compiler_params=_params("parallel"),
        name="ada",
    )(c, w_ada, b_ada.reshape(1, n))


class _Mod:
    def __init__(self, mod, per_row, tiles_per_seq=1):
        self.per_row = per_row
        self.tiles_per_seq = tiles_per_seq
        self.arr = mod if per_row else mod.reshape(mod.shape[0] * 9, 1, D_MODEL)

    def spec(self, k):
        if self.per_row:
            return pl.BlockSpec((self.arr.shape[0], D_MODEL), lambda i, *_: (0, k))
        tps = self.tiles_per_seq
        return pl.BlockSpec((None, 1, D_MODEL), lambda i, *_: ((i // tps) * 9 + k, 0, 0))


def _gspec(k):
    return pl.BlockSpec((None, 1, D_MODEL), lambda i, *_: (k, 0, 0))


def _ffn_kernel(x_ref, shift_ref, scale_ref, gate_ref, gpre_ref, gpost_ref, wg_ref, wu_ref, wd_ref,
                o_ref, a_scr, acc_scr):
    j = pl.program_id(1)

    @pl.when(j == 0)
    def _():
        a = _rms(x_ref[...], gpre_ref[...]) * (1.0 + scale_ref[...]) + shift_ref[...]
        a_scr[...] = a.astype(BF16)
        acc_scr[...] = jnp.zeros_like(acc_scr)

    a = a_scr[...]
    h = _silu(_dot(a, wg_ref[...])) * _dot(a, wu_ref[...])
    acc_scr[...] += _dot(h.astype(BF16), wd_ref[...])

    @pl.when(j == pl.num_programs(1) - 1)
    def _():
        y = _rms(acc_scr[...], gpost_ref[...])
        o_ref[...] = x_ref[...] + FFN_HALF * gate_ref[...] * y


def _ffn(x, mod, g3, w_gu, w_down, sub, tm):
    m = x.shape[0]
    tf = D_FF // 2
    nf = D_FF // tf
    row = pl.BlockSpec((tm, D_MODEL), lambda i, j: (i, 0))
    return pl.pallas_call(
        _ffn_kernel,
        grid=(m // tm, nf),
        in_specs=[
            row, mod.spec(3 * sub), mod.spec(3 * sub + 1), mod.spec(3 * sub + 2),
            _gspec(2 * sub), _gspec(2 * sub + 1),
            pl.BlockSpec((D_MODEL, tf), lambda i, j: (0, j)),
            pl.BlockSpec((D_MODEL, tf), lambda i, j: (0, j + nf)),
            pl.BlockSpec((tf, D_MODEL), lambda i, j: (j, 0)),
        ],
        out_specs=row,
        out_shape=jax.ShapeDtypeStruct((m, D_MODEL), F32),
        scratch_shapes=[pltpu.VMEM((tm, D_MODEL), BF16), pltpu.VMEM((tm, D_MODEL), F32)],
        compiler_params=_params("parallel", "arbitrary"),
        name=f"ffn{sub}",
    )(x, mod.arr, mod.arr, mod.arr, g3, g3, w_gu, w_gu, w_down)


def _rope(x, cos, sin_lo, sin_hi):
    return x * cos + pltpu.roll(x, LANES - ROT_DIM // 2, 1) * sin_lo + pltpu.roll(x, ROT_DIM // 2, 1) * sin_hi


def _proj_kernel(x_ref, shift_ref, scale_ref, g_ref, cos_ref, slo_ref, shi_ref, w_ref,
                 q_ref, cmp_ref, kvt_ref, kvb_ref, gate_ref):
    a = (_rms(x_ref[...], g_ref[...]) * (1.0 + scale_ref[...]) + shift_ref[...]).astype(BF16)
    z = _dot(a, w_ref[...])
    cos, slo, shi = cos_ref[...], slo_ref[...], shi_ref[...]
    scale = HEAD_DIM ** -0.5
    for s in range(Q_COLS // LANES):
        q = _rope(z[:, s * LANES:(s + 1) * LANES], cos, slo, shi) * scale
        q_ref[:, s * LANES:(s + 1) * LANES] = q.astype(BF16)
    for br in range(3):
        c0 = Q_COLS + br * KV_BRANCH_COLS
        k = _rope(z[:, c0:c0 + LANES], cos, slo, shi)
        v = z[:, c0 + LANES:c0 + 2 * LANES]
        kvt_ref[br, :LANES, :] = k.T
        kvt_ref[br, LANES:, :] = v.T
        if br == 0:
            cmp_ref[:, :LANES] = k
            cmp_ref[:, LANES:] = v
        kvb_ref[:, br * KV_BRANCH_COLS:br * KV_BRANCH_COLS + LANES] = k.astype(BF16)
        kvb_ref[:, br * KV_BRANCH_COLS + LANES:(br + 1) * KV_BRANCH_COLS] = v.astype(BF16)
    gate_ref[...] = jax.nn.sigmoid(z[:, Q_COLS + KV_COLS:])


def _proj(x, mod, g3, tables, w_qkvg, tm):
    m = x.shape[0]
    t_rows = tables[0].shape[0]
    nt = t_rows // tm
    n_seq = m // t_rows
    row = lambda n: pl.BlockSpec((tm, n), lambda i: (i, 0))
    tab = pl.BlockSpec((tm, LANES), lambda i: (i % nt, 0))
    return pl.pallas_call(
        _proj_kernel,
        grid=(m // tm,),
        in_specs=[row(D_MODEL), mod.spec(3), mod.spec(4), _gspec(2), tab, tab, tab,
                  pl.BlockSpec((D_MODEL, QKVG_COLS), lambda i: (0, 0))],
        out_specs=[row(Q_COLS), row(KV_BRANCH_COLS),
                   pl.BlockSpec((3, None, KV_BRANCH_COLS, tm), lambda i: (0, i // nt, 0, i % nt)),
                   row(KV_COLS), row(GATE_PAD)],
        out_shape=[
            jax.ShapeDtypeStruct((m, Q_COLS), BF16),
            jax.ShapeDtypeStruct((m, KV_BRANCH_COLS), F32),
            jax.ShapeDtypeStruct((3, n_seq, KV_BRANCH_COLS, t_rows), F32),
            jax.ShapeDtypeStruct((m, KV_COLS), BF16),
            jax.ShapeDtypeStruct((m, GATE_PAD), F32),
        ],
        compiler_params=_params("parallel"),
        name="proj",
    )(x, mod.arr, mod.arr, g3, *tables, w_qkvg)


def _rope_tables(pos):
    half = ROT_DIM // 2
    inv = jnp.power(ROPE_THETA, -jnp.arange(half, dtype=F32) * 2.0 / ROT_DIM)
    ang = pos.astype(F32)[:, None] * inv[None, :]
    cos, sin = jnp.cos(ang), jnp.sin(ang)
    n = pos.shape[0]
    ones = jnp.ones((n, HEAD_DIM - ROT_DIM), F32)
    zeros = jnp.zeros((n, HEAD_DIM - ROT_DIM), F32)
    zh = jnp.zeros((n, half), F32)
    c = jnp.concatenate([cos, cos, ones], axis=1)
    lo = jnp.concatenate([-sin, zh, zeros], axis=1)
    hi = jnp.concatenate([zh, sin, zeros], axis=1)
    return tuple(jnp.tile(t, (1, LANES // HEAD_DIM)) for t in (c, lo, hi))


def _compress_kernel(x_ref, pe_ref, w1_ref, b1_ref, w2_ref, o_ref, acc_scr):
    j = pl.program_id(0)
    nt = x_ref.shape[1] // KV_BRANCH_COLS

    @pl.when(j == 0)
    def _():
        acc_scr[...] = jnp.zeros_like(acc_scr)

    for kv in range(2):
        acc = acc_scr[kv]
        for t in range(0, nt, 2):
            xs = []
            for tt in (t, t + 1):
                c0 = tt * KV_BRANCH_COLS + kv * LANES
                xs.append((x_ref[:, c0:c0 + LANES] + pe_ref[kv, tt]).astype(BF16))
            acc = acc + _dot(jnp.concatenate(xs, axis=1), w1_ref[kv, t * LANES:(t + 2) * LANES, :])
        acc_scr[kv] = acc

    @pl.when(j == pl.num_programs(0) - 1)
    def _():
        for kv in range(2):
            h = _silu(acc_scr[kv] + b1_ref[kv])
            o_ref[kv] = _dot(h.astype(BF16), w2_ref[kv]).astype(o_ref.dtype)


def _compress_weights(phi_pe, phi_w1, phi_b1, phi_w2):
    eye = jnp.eye(N_KV_HEADS, dtype=F32)
    w1 = phi_w1.reshape(2, NSA_BLOCK, HEAD_DIM, PHI_HIDDEN)
    w1 = jnp.einsum("gh,ktdj->ktgdhj", eye, w1).reshape(2, NSA_BLOCK * LANES, N_KV_HEADS * PHI_HIDDEN)
    w2 = jnp.einsum("gh,kjd->kgjhd", eye, phi_w2).reshape(2, N_KV_HEADS * PHI_HIDDEN, LANES)
    pe = jnp.tile(phi_pe, (1, 1, N_KV_HEADS)).reshape(2, NSA_BLOCK, 1, LANES)
    b1 = jnp.tile(phi_b1, (1, N_KV_HEADS)).reshape(2, 1, N_KV_HEADS * PHI_HIDDEN)
    return pe, w1.astype(BF16), b1, w2.astype(BF16)


def _compress_weights_paged(phi_pe, phi_w1, phi_b1, phi_w2):
    blocks = PAGE_SIZE // NSA_BLOCK
    eye = jnp.eye(blocks, dtype=F32)
    w1 = phi_w1.reshape(2, NSA_BLOCK, HEAD_DIM, PHI_HIDDEN)
    w1 = jnp.einsum("bc,ktdj->kdbtcj", eye, w1).reshape(2, HEAD_DIM * PAGE_SIZE, blocks * PHI_HIDDEN)
    w2 = jnp.einsum("bc,kjd->kbjcd", eye, phi_w2).reshape(2, blocks * PHI_HIDDEN, blocks * HEAD_DIM)
    pe = jnp.tile(phi_pe.transpose(0, 2, 1), (1, 1, blocks)).reshape(2, HEAD_DIM, 1, PAGE_SIZE)
    b1 = jnp.tile(phi_b1, (1, blocks)).reshape(2, 1, blocks * PHI_HIDDEN)
    return pe, w1.astype(BF16), b1, w2.astype(BF16)


def _compress(x_blocks, cw, out_dtype=BF16):
    pe, w1, b1, w2 = cw
    rows = x_blocks.shape[0]
    tchunk = 8
    nchunk = NSA_BLOCK // tchunk
    hid = N_KV_HEADS * PHI_HIDDEN
    return pl.pallas_call(
        _compress_kernel,
        grid=(nchunk,),
        in_specs=[
            pl.BlockSpec((rows, tchunk * KV_BRANCH_COLS), lambda j: (0, j)),
            pl.BlockSpec((2, tchunk, 1, LANES), lambda j: (0, j, 0, 0)),
            pl.BlockSpec((2, tchunk * LANES, hid), lambda j: (0, j, 0)),
            pl.BlockSpec((2, 1, hid), lambda j: (0, 0, 0)),
            pl.BlockSpec((2, hid, LANES), lambda j: (0, 0, 0)),
        ],
        out_specs=pl.BlockSpec((2, rows, LANES), lambda j: (0, 0, 0)),
        out_shape=jax.ShapeDtypeStruct((2, rows, LANES), out_dtype),
        scratch_shapes=[pltpu.VMEM((2, rows, hid), F32)],
        compiler_params=_params("arbitrary"),
        name="compress",
    )(x_blocks, pe, w1, b1, w2)


def _gather_compress_kernel(pt_ref, *refs, n_pages, group):
    del pt_ref
    pages = refs[:n_pages]
    pe_ref, w1_ref, b1_ref, w2_ref, o_ref, xk_scr, xv_scr = refs[n_pages:]
    x_scr = (xk_scr, xv_scr)
    slot = pl.program_id(0) % group
    for k in range(n_pages):
        r0 = pl.multiple_of((slot * n_pages + k) * GD_ROWS, GD_ROWS)
        for kv in range(2):
            x_scr[kv][pl.ds(r0, GD_ROWS), :] = pages[k][kv * GD_ROWS:(kv + 1) * GD_ROWS, :]

    @pl.when(slot == group - 1)
    def _():
        rows = group * n_pages * N_KV_HEADS
        for kv in range(2):
            acc = jnp.zeros((rows, w1_ref.shape[2]), F32)
            for d in range(0, HEAD_DIM, 2):
                xs = []
                for dd in (d, d + 1):
                    x = x_scr[kv][pl.ds(dd, rows, stride=HEAD_DIM), :]
                    xs.append((x + pe_ref[kv, dd]).astype(BF16))
                acc = acc + _dot(jnp.concatenate(xs, axis=1), w1_ref[kv, d * PAGE_SIZE:(d + 2) * PAGE_SIZE, :])
            h = _silu(acc + b1_ref[kv])
            o_ref[kv] = _dot(h.astype(BF16), w2_ref[kv])


def _page_specs(n_pages):
    return [pl.BlockSpec((None, KV_BRANCH_COLS, PAGE_SIZE), lambda s, pt, k=k: (pt[s, k], 0, 0))
            for k in range(n_pages)]


def _gather_compress(cache_t, page_table, cwp):
    pe, w1, b1, w2 = cwp
    n_dec, n_pages = page_table.shape
    group = 8
    rows = group * n_pages * N_KV_HEADS
    blocks = PAGE_SIZE // NSA_BLOCK
    const = lambda a: pl.BlockSpec(a.shape, lambda s, pt: (0,) * a.ndim)
    out = pl.pallas_call(
        functools.partial(_gather_compress_kernel, n_pages=n_pages, group=group),
        grid_spec=pltpu.PrefetchScalarGridSpec(
            num_scalar_prefetch=1,
            grid=(n_dec,),
            in_specs=_page_specs(n_pages) + [const(pe), const(w1), const(b1), const(w2)],
            out_specs=pl.BlockSpec((2, rows, blocks * HEAD_DIM), lambda s, pt: (0, s // group, 0)),
            scratch_shapes=[pltpu.VMEM((group * n_pages * GD_ROWS, PAGE_SIZE), F32)] * 2,
        ),
        out_shape=jax.ShapeDtypeStruct((2, n_dec // group * rows, blocks * HEAD_DIM), F32),
        compiler_params=_params("arbitrary"),
        name="gather_compress",
    )(page_table, *([cache_t] * n_pages), pe, w1, b1, w2)
    out = out.reshape(2, n_dec, n_pages, N_KV_HEADS, blocks, HEAD_DIM).transpose(0, 1, 2, 4, 3, 5)
    return out.reshape(2, n_dec, n_pages * blocks, N_KV_HEADS * HEAD_DIM)


def _group_queries(q, g):
    lane = lax.broadcasted_iota(jnp.int32, (1, LANES), 1)
    keep = (lane < HEAD_DIM) if g == 0 else (lane >= HEAD_DIM)
    zero = jnp.zeros((), q.dtype)
    return jnp.concatenate(
        [jnp.where(keep, q[:, j * LANES:(j + 1) * LANES], zero) for j in range(HEADS_PER_GROUP)], axis=0)


def _select_blocks(imp, nblk, n_sel):
    nb = imp.shape[1]
    rank = jnp.zeros(imp.shape, F32)
    for m in range(nb):
        col = imp[:, m:m + 1]
        beats = jnp.where(col > imp, 1.0, jnp.where(col == imp, jnp.where(m < nblk, 1.0, 0.0), 0.0))
        rank = rank + beats
    return jnp.where(rank < n_sel, jnp.where(imp >= 0.0, 1.0, 0.0), 0.0)


def _flash(qg, kv_ref, tile0, ntiles, tk, bias_fn):
    rows = qg.shape[0]
    t = rows // HEADS_PER_GROUP

    def body(j, carry):
        m, l, acc = carry
        ks = pl.multiple_of((tile0 + j) * tk, tk)
        k = kv_ref[pl.ds(ks, tk), :LANES]
        v = kv_ref[pl.ds(ks, tk), LANES:]
        s = _dot_nt(qg, k).reshape(HEADS_PER_GROUP, t, tk) + bias_fn(ks)[None]
        s = s.reshape(rows, tk)
        m_new = jnp.maximum(m, jnp.max(s, axis=-1, keepdims=True))
        alpha = jnp.exp(m - m_new)
        p = jnp.exp(s - m_new)
        l = alpha * l + jnp.sum(p, axis=-1, keepdims=True)
        acc = alpha * acc + _dot(p.astype(BF16), v)
        return m_new, l, acc

    init = (jnp.full((rows, 1), MASK_INIT, F32), jnp.zeros((rows, 1), F32), jnp.zeros((rows, LANES), F32))
    _, l, acc = lax.fori_loop(0, ntiles, body, init)
    return acc / l


def _attn_kernel(q_ref, gate_ref, kc_ref, vc_ref, slc_ref, win_ref, o_ref, *, tq, nb):
    i = pl.program_id(1)
    p0 = i * tq
    pos = p0 + lax.broadcasted_iota(jnp.int32, (tq, 1), 0)
    nblk = lax.broadcasted_iota(jnp.int32, (1, nb), 1)
    q = q_ref[...]
    gates = gate_ref[...]
    lane = lax.broadcasted_iota(jnp.int32, (1, LANES), 1)
    tk = tq
    outs = []
    for g in range(N_KV_HEADS):
        qg = _group_queries(q, g)

        cvalid = (nblk + 1) * NSA_BLOCK <= pos + 1
        sc = _dot_nt(qg, kc_ref[...]).reshape(HEADS_PER_GROUP, tq, nb)
        sc = jnp.where(cvalid[None], sc, MASK_INIT)
        e = jnp.exp(sc - jnp.max(sc, axis=-1, keepdims=True))
        pc = jnp.where(cvalid[None], e / jnp.sum(e, axis=-1, keepdims=True), 0.0)
        o_cmp = _dot(pc.reshape(HEADS_PER_GROUP * tq, nb).astype(BF16), vc_ref[...])

        imp = jnp.sum(pc, axis=0)
        cur = pos // NSA_BLOCK
        forced = HEADS_PER_GROUP + 1.0
        imp = jnp.where(nblk == 0, forced, jnp.where(nblk == cur, forced, jnp.where(nblk == cur - 1, forced, imp)))
        imp = jnp.where(nblk * NSA_BLOCK <= pos, imp, -1.0)
        sel_b = _select_blocks(imp, nblk, NSA_TOP_N).astype(BF16)

        def slc_bias(ks):
            kpos = ks + lax.broadcasted_iota(jnp.int32, (1, tk), 1)
            blk_of_key = jnp.where(
                lax.broadcasted_iota(jnp.int32, (nb, 1), 0) == kpos // NSA_BLOCK, 1.0, 0.0).astype(BF16)
            chosen = _dot(sel_b, blk_of_key)
            return jnp.where(kpos <= pos, jnp.where(chosen > 0.5, 0.0, MASK_BIAS), MASK_BIAS)

        o_slc = _flash(qg, slc_ref, 0, i + 1, tk, slc_bias)

        def win_bias(ks):
            diff = pos - (ks + lax.broadcasted_iota(jnp.int32, (1, tk), 1))
            return jnp.where(diff >= 0, jnp.where(diff < NSA_WINDOW, 0.0, MASK_BIAS), MASK_BIAS)

        wt0 = jnp.maximum(i - NSA_WINDOW // tk, 0)
        o_win = _flash(qg, win_ref, wt0, i + 1 - wt0, tk, win_bias)

        per_head = []
        for h in range(HEADS_PER_GROUP):
            head = g * HEADS_PER_GROUP + h
            rows = slice(h * tq, (h + 1) * tq)
            per_head.append(gates[:, head:head + 1] * o_cmp[rows]
                            + gates[:, N_HEADS + head:N_HEADS + head + 1] * o_slc[rows]
                            + gates[:, 2 * N_HEADS + head:2 * N_HEADS + head + 1] * o_win[rows])
        outs.append(per_head)
    for j in range(HEADS_PER_GROUP):
        o_ref[:, j * LANES:(j + 1) * LANES] = jnp.where(lane < HEAD_DIM, outs[0][j], outs[1][j]).astype(o_ref.dtype)


def _attn_prompt(q, gates, kc, vc, kvb, n_seq, seq, tq):
    m = q.shape[0]
    nb = seq // NSA_BLOCK
    nq = seq // tq
    row = lambda n: pl.BlockSpec((tq, n), lambda b, i: (b * nq + i, 0))
    cblk = pl.BlockSpec((nb, LANES), lambda b, i: (b, 0))
    return pl.pallas_call(
        functools.partial(_attn_kernel, tq=tq, nb=nb),
        grid=(n_seq, nq),
        in_specs=[row(Q_COLS), row(GATE_PAD), cblk, cblk,
                  pl.BlockSpec((seq, KV_BRANCH_COLS), lambda b, i: (b, 1)),
                  pl.BlockSpec((seq, KV_BRANCH_COLS), lambda b, i: (b, 2))],
        out_specs=row(Q_COLS),
        out_shape=jax.ShapeDtypeStruct((m, Q_COLS), BF16),
        compiler_params=_params("parallel", "arbitrary"),
        name="attn_prompt",
    )(q, gates, kc, vc, kvb, kvb)


def _softmax_pv(score_tiles, value_tiles):
    m = functools.reduce(jnp.maximum, [jnp.max(s, axis=-1, keepdims=True) for s in score_tiles])
    ps = [jnp.exp(s - m) for s in score_tiles]
    l = functools.reduce(jnp.add, [jnp.sum(p, axis=-1, keepdims=True) for p in ps])
    o = functools.reduce(jnp.add, [(_dot_nt if fm else _dot)(p.astype(BF16), v)
                                   for p, (v, fm) in zip(ps, value_tiles)])
    return o / l


def _attn_decode_kernel(pt_ref, *refs, n_pages, pos, nbp):
    del pt_ref
    pages = refs[:n_pages]
    q_ref, kvb_ref, gate_ref, kvc_ref, win_ref, o_ref = refs[n_pages:]
    tk = PAGE_SIZE
    lane = lax.broadcasted_iota(jnp.int32, (1, LANES), 1)
    head = lax.broadcasted_iota(jnp.int32, (N_HEADS, 1), 0)
    low_group = head < HEADS_PER_GROUP
    low_lanes = lane < HEAD_DIM

    q32 = q_ref[...].astype(F32)
    qm = jnp.zeros((N_HEADS, LANES), F32)
    for j in range(HEADS_PER_GROUP):
        qm = jnp.where(head % HEADS_PER_GROUP == j, q32[:, j * LANES:(j + 1) * LANES], qm)
    qm = jnp.where(low_group, jnp.where(low_lanes, qm, 0.0), jnp.where(low_lanes, 0.0, qm)).astype(BF16)

    gate_row = gate_ref[...]
    gate_col = [jnp.sum(jnp.where(lane == head + br * N_HEADS, gate_row, 0.0), axis=-1, keepdims=True)
                for br in range(3)]
    kv_new = kvb_ref[...].astype(F32)
    first_row = lax.broadcasted_iota(jnp.int32, (tk, 1), 0) == 0

    def new_key_tile(c0):
        return jnp.where(first_row, kv_new[:, c0:c0 + LANES], 0.0).astype(BF16)

    nblk = lax.broadcasted_iota(jnp.int32, (1, nbp), 1)
    cvalid = (nblk + 1) * NSA_BLOCK <= pos + 1
    sc = jnp.where(cvalid, _dot_nt(qm, kvc_ref[0].astype(BF16)), MASK_INIT)
    e = jnp.exp(sc - jnp.max(sc, axis=-1, keepdims=True))
    pc = jnp.where(cvalid, e / jnp.sum(e, axis=-1, keepdims=True), 0.0)
    o_cmp = _dot(pc.astype(BF16), kvc_ref[1].astype(BF16))

    imp = jnp.where(low_group,
                    jnp.sum(jnp.where(low_group, pc, 0.0), axis=0, keepdims=True),
                    jnp.sum(jnp.where(low_group, 0.0, pc), axis=0, keepdims=True))
    cur = pos // NSA_BLOCK
    forced = HEADS_PER_GROUP + 1.0
    imp = jnp.where(nblk == 0, forced, jnp.where(nblk == cur, forced, jnp.where(nblk == cur - 1, forced, imp)))
    imp = jnp.where(nblk * NSA_BLOCK <= pos, imp, -1.0)
    sel = _select_blocks(imp, nblk, NSA_TOP_N)

    def sel_bias(n):
        return jnp.where(sel[:, n:n + 1] > 0.5, 0.0, MASK_BIAS)

    s_tiles, v_tiles = [], []
    blocks_per_page = tk // NSA_BLOCK
    for k in range(n_pages):
        page = pages[k]
        s = _dot(qm, page[:GD_ROWS, :].astype(BF16))
        bias = sel_bias(k * blocks_per_page + blocks_per_page - 1)
        for b in range(blocks_per_page - 2, -1, -1):
            bias = jnp.where(lane < (b + 1) * NSA_BLOCK, sel_bias(k * blocks_per_page + b), bias)
        s_tiles.append(s + bias)
        v_tiles.append((page[GD_ROWS:, :].astype(BF16), True))
    s_new = _dot_nt(qm, new_key_tile(KV_BRANCH_COLS))
    s_tiles.append(s_new + jnp.where(lane == 0, sel_bias(n_pages * blocks_per_page), MASK_BIAS))
    v_tiles.append((new_key_tile(KV_BRANCH_COLS + LANES), False))
    o_slc = _softmax_pv(s_tiles, v_tiles)

    n_win = win_ref.shape[1]
    s_tiles, v_tiles = [], []
    for k in range(n_win // tk):
        diff = n_win - (k * tk + lane)
        bias = jnp.where(diff < NSA_WINDOW, 0.0, MASK_BIAS)
        s_tiles.append(_dot(qm, win_ref[:GD_ROWS, k * tk:(k + 1) * tk].astype(BF16)) + bias)
        v_tiles.append((win_ref[GD_ROWS:, k * tk:(k + 1) * tk].astype(BF16), True))
    s_tiles.append(_dot_nt(qm, new_key_tile(2 * KV_BRANCH_COLS)) + jnp.where(lane == 0, 0.0, MASK_BIAS))
    v_tiles.append((new_key_tile(2 * KV_BRANCH_COLS + LANES), False))
    o_win = _softmax_pv(s_tiles, v_tiles)

    o = gate_col[0] * o_cmp + gate_col[1] * o_slc + gate_col[2] * o_win
    for j in range(HEADS_PER_GROUP):
        pair = jnp.where(low_lanes, o[j:j + 1], o[j + HEADS_PER_GROUP:j + HEADS_PER_GROUP + 1])
        o_ref[:, j * LANES:(j + 1) * LANES] = pair.astype(o_ref.dtype)


def _attn_decode(q, kvb, gates, kvc, cache_slc, win_state, page_table, pos):
    n_dec, n_pages = page_table.shape
    nbp = kvc.shape[2]
    n_win = win_state.shape[2]
    per_seq = lambda a: pl.BlockSpec((None, 1, a.shape[-1]), lambda s, pt: (s, 0, 0))
    q3, kvb3, g3 = (a.reshape(n_dec, 1, a.shape[-1]) for a in (q, kvb, gates))
    out = pl.pallas_call(
        functools.partial(_attn_decode_kernel, n_pages=n_pages, pos=pos, nbp=nbp),
        grid_spec=pltpu.PrefetchScalarGridSpec(
            num_scalar_prefetch=1,
            grid=(n_dec,),
            in_specs=_page_specs(n_pages) + [
                per_seq(q3), per_seq(kvb3), per_seq(g3),
                pl.BlockSpec((2, None, nbp, LANES), lambda s, pt: (0, s, 0, 0)),
                pl.BlockSpec((None, KV_BRANCH_COLS, n_win), lambda s, pt: (s, 0, 0)),
            ],
            out_specs=pl.BlockSpec((None, 1, Q_COLS), lambda s, pt: (s, 0, 0)),
        ),
        out_shape=jax.ShapeDtypeStruct((n_dec, 1, Q_COLS), BF16),
        compiler_params=_params("arbitrary"),
        name="attn_decode",
    )(page_table, *([cache_slc] * n_pages), q3, kvb3, g3, kvc, win_state)
    return out.reshape(n_dec, Q_COLS)


def _mix_kernel(*refs, tm, halo, tiles_per_seq):
    if halo:
        (x_ref, xh_ref, o_ref, shift_ref, scale_ref, gate_ref, gpre_ref, gpost_ref, cw_ref,
         wconv_ref, wmerge_ref, wup_ref, wco_ref, wout_ref, y_ref, ulast_ref, u_scr) = refs
        x = x_ref[...]
        xe = jnp.concatenate([xh_ref[...], x], axis=0)
    else:
        (x_ref, um1_ref, um2_ref, o_ref, shift_ref, scale_ref, gate_ref, gpre_ref, gpost_ref, cw_ref,
         wconv_ref, wmerge_ref, wup_ref, wco_ref, wout_ref, y_ref, ulast_ref) = refs
        x = x_ref[...]
        xe = x
    h0 = xe.shape[0] - tm
    a = (_rms(xe, gpre_ref[...]) * (1.0 + scale_ref[...]) + shift_ref[...]).astype(BF16)
    zc = _dot(a, wconv_ref[...])
    u = zc[:, 2 * D_CONV:] * zc[:, :D_CONV]
    cb = zc[h0:, D_CONV:2 * D_CONV]
    if halo:
        keep_halo = jnp.where(pl.program_id(0) % tiles_per_seq == 0, 0.0, 1.0)
        rows = lax.broadcasted_iota(jnp.int32, (xe.shape[0], 1), 0)
        u = jnp.where(rows < h0, u * keep_halo, u)
        u_scr[...] = u
        um1 = u_scr[h0 - 1:h0 - 1 + tm, :]
        um2 = u_scr[h0 - 2:h0 - 2 + tm, :]
        u0 = u[h0:]
        ulast_ref[...] = u[tm:]
    else:
        um1, um2, u0 = um1_ref[...], um2_ref[...], u
        ulast_ref[...] = u
    cw = cw_ref[...]
    y = cw[0:1] * um2 + cw[1:2] * um1 + cw[2:3] * u0
    conv_out = _dot((cb * y).astype(BF16), wco_ref[...])
    attn_out = _dot(o_ref[...], wup_ref[...])
    mg = jax.nn.sigmoid(_dot(a[h0:], wmerge_ref[...]))
    merged = mg[:, :D_MODEL] * attn_out + mg[:, D_MODEL:] * conv_out
    mixed = _dot(merged.astype(BF16), wout_ref[...])
    y_ref[...] = x + gate_ref[...] * _rms(mixed, gpost_ref[...])


def _mix(x, o_att, mod, g3, conv_w, weights, tm, tiles_per_seq=None, prev=None):
    m = x.shape[0]
    halo = prev is None
    h0 = MIX_HALO
    row = lambda n: pl.BlockSpec((tm, n), lambda i: (i, 0))
    full = lambda a: pl.BlockSpec(a.shape, lambda i: (0,) * a.ndim)
    common = [row(Q_COLS), mod.spec(3), mod.spec(4), mod.spec(5), _gspec(2), _gspec(3), full(conv_w)]
    common += [full(w) for w in weights]
    common_args = [o_att, mod.arr, mod.arr, mod.arr, g3, g3, conv_w, *weights]
    if halo:
        hb = tm // h0
        in_specs = [row(D_MODEL), pl.BlockSpec((h0, D_MODEL), lambda i: (jnp.maximum(i * hb - 1, 0), 0))] + common
        args = [x, x] + common_args
        scratch = [pltpu.VMEM((tm + h0, D_CONV), F32)]
        ulast = (jax.ShapeDtypeStruct((m // tm * h0, D_CONV), F32), pl.BlockSpec((h0, D_CONV), lambda i: (i, 0)))
    else:
        in_specs = [row(D_MODEL), row(D_CONV), row(D_CONV)] + common
        args = [x, prev[0], prev[1]] + common_args
        scratch = []
        ulast = (jax.ShapeDtypeStruct((m, D_CONV), F32), row(D_CONV))
    return pl.pallas_call(
        functools.partial(_mix_kernel, tm=tm, halo=halo, tiles_per_seq=tiles_per_seq),
        grid=(m // tm,),
        in_specs=in_specs,
        out_specs=[row(D_MODEL), ulast[1]],
        out_shape=[jax.ShapeDtypeStruct((m, D_MODEL), F32), ulast[0]],
        scratch_shapes=scratch,
        compiler_params=_params("parallel"),
        name="mix",
    )(*args)


def _head_pair_perm():
    order = []
    for j in range(HEADS_PER_GROUP):
        for g in range(N_KV_HEADS):
            head = g * HEADS_PER_GROUP + j
            order.extend(range(head * HEAD_DIM, (head + 1) * HEAD_DIM))
    return jnp.array(order, dtype=jnp.int32)


def _layer_weights(w_in, w_attn_up, w_conv_out, w_out):
    perm = _head_pair_perm()
    o1 = Q_COLS
    o2 = o1 + KV_COLS
    o3 = o2 + NSA_GATE_COLS
    o4 = o3 + CONV_COLS
    w_q = w_in[:, :o1][:, perm]
    w_g = jnp.pad(w_in[:, o2:o3], ((0, 0), (0, GATE_PAD - NSA_GATE_COLS)))
    w_qkvg = jnp.concatenate([w_q, w_in[:, o1:o2], w_g], axis=1).astype(BF16)
    mix_w = (w_in[:, o3:o4].astype(BF16), w_in[:, o4:].astype(BF16), w_attn_up[perm].astype(BF16),
             w_conv_out.astype(BF16), w_out.astype(BF16))
    return w_qkvg, mix_w


def kernel(x_prompt, x_sample, c_prompt, c_sample, cache_cmp_kv, cache_slc_kv, state_win_kv, state_conv,
           page_table, w_ada, b_ada, g_norm, w_ffn_gu, w_ffn_down, w_in, phi_pe, phi_w1, phi_b1, phi_w2,
           w_attn_up, conv_w, w_conv_out, w_out):
    n_seq, seq, _ = x_prompt.shape
    n_dec = x_sample.shape[0]
    assert w_ada.shape[0] == 1
    l = 0

    mod_all = _ada(jnp.concatenate([c_prompt, c_sample], axis=0), w_ada[l], b_ada[l])
    g3 = g_norm[l].reshape(6, 1, D_MODEL)
    w_gu = w_ffn_gu[l].astype(BF16)
    w_down = w_ffn_down[l].astype(BF16)
    w_qkvg, mix_w = _layer_weights(w_in[l], w_attn_up[l], w_conv_out[l], w_out[l])
    cw = _compress_weights(phi_pe[l], phi_w1[l], phi_b1[l], phi_w2[l])

    tm = 512
    tq = 256
    m = n_seq * seq
    mod_p = _Mod(mod_all[:n_seq], per_row=False, tiles_per_seq=seq // tm)
    x0 = x_prompt.reshape(m, D_MODEL)
    x1 = _ffn(x0, mod_p, g3, w_gu[0], w_down[0], 0, tm)
    tables = _rope_tables(jnp.arange(seq, dtype=jnp.int32))
    q, cmp_new, kvt, kvb, gates = _proj(x1, mod_p, g3, tables, w_qkvg, tm)
    nb = seq // NSA_BLOCK
    kvc = _compress(cmp_new.reshape(n_seq * nb, NSA_BLOCK * KV_BRANCH_COLS), cw)
    o_att = _attn_prompt(q, gates, kvc[0], kvc[1], kvb, n_seq, seq, tq)
    x2, ulast = _mix(x1, o_att, mod_p, g3, conv_w[l], mix_w, tm, tiles_per_seq=seq // tm)
    y_prompt = _ffn(x2, mod_p, g3, w_gu[1], w_down[1], 2, tm).reshape(n_seq, seq, D_MODEL)

    def token_major(t):
        n, _, tt = t.shape
        return t.reshape(n, 2, N_KV_HEADS, HEAD_DIM, tt).transpose(0, 4, 1, 2, 3)[None]

    def feature_major(a):
        n, tt = a.shape[:2]
        return a.transpose(0, 2, 3, 4, 1).reshape(n, KV_BRANCH_COLS, tt)

    cmp_kv_prompt = token_major(kvt[0])
    slc_kv_prompt = token_major(kvt[1])
    win_keep = min(NSA_WINDOW, seq)
    win_kv_prompt = token_major(kvt[2][:, :, seq - win_keep:])
    conv_prompt = ulast.reshape(n_seq, seq // tm, MIX_HALO, D_CONV)[:, -1, MIX_HALO - (CONV_WIDTH - 1):][None]

    n_pages = page_table.shape[1]
    past_len = n_pages * PAGE_SIZE
    pos_s = past_len + jnp.arange(x_sample.shape[1], dtype=jnp.int32)
    assert x_sample.shape[1] == 1 and past_len % NSA_BLOCK == 0 and state_win_kv.shape[2] <= past_len
    mod_s = _Mod(mod_all[n_seq:], per_row=True)
    xs1 = _ffn(x_sample.reshape(n_dec, D_MODEL), mod_s, g3, w_gu[0], w_down[0], 0, n_dec)
    tables_s = _rope_tables(jnp.broadcast_to(pos_s, (n_dec,)))
    q_s, cmp_s, kvt_s, kvb_s, gates_s = _proj(xs1, mod_s, g3, tables_s, w_qkvg, n_dec)
    nb_past = past_len // NSA_BLOCK
    nb_pad = -(-(nb_past + 1) // NSA_BLOCK) * NSA_BLOCK
    cwp = _compress_weights_paged(phi_pe[l], phi_w1[l], phi_b1[l], phi_w2[l])
    kvc_past = _gather_compress(feature_major(cache_cmp_kv[l]), page_table, cwp)
    new_block = jnp.pad(cmp_s[:, None, :], ((0, 0), (0, NSA_BLOCK - 1), (0, 0)))
    kvc_new = _compress(new_block.reshape(n_dec, NSA_BLOCK * KV_BRANCH_COLS), cw, F32)[:, :, None, :]
    kvc_s = jnp.concatenate(
        [kvc_past, kvc_new, jnp.zeros((2, n_dec, nb_pad - nb_past - 1, LANES), F32)], axis=2)
    win_state = feature_major(state_win_kv[l])
    o_att_s = _attn_decode(q_s, kvb_s, gates_s, kvc_s, feature_major(cache_slc_kv[l]), win_state, page_table,
                           past_len)
    conv_state = state_conv[l]
    xs2, u_s = _mix(xs1, o_att_s, mod_s, g3, conv_w[l], mix_w, n_dec,
                    prev=(conv_state[:, CONV_WIDTH - 2], conv_state[:, CONV_WIDTH - 3]))
    y_sample = _ffn(xs2, mod_s, g3, w_gu[1], w_down[1], 2, n_dec).reshape(x_sample.shape)

    kvs = (1, n_dec, 1, 2, N_KV_HEADS, HEAD_DIM)
    win_keep_s = min(NSA_WINDOW, past_len + 1)
    win_new_s = kvt_s[2, 0].T[:, :, None]
    win_kv_sample = token_major(jnp.concatenate([win_state, win_new_s], axis=2)[:, :, -win_keep_s:])
    conv_sample = jnp.concatenate([conv_state[:, 1:], u_s[:, None, :]], axis=1)[None]
    return (y_prompt, y_sample, cmp_kv_prompt, slc_kv_prompt, win_kv_prompt, conv_prompt,
            token_major(kvt_s[0]).reshape(kvs), token_major(kvt_s[1]).reshape(kvs), win_kv_sample, conv_sample)
```

```python
import functools

import jax
import jax.numpy as jnp
from jax import lax
from jax.experimental import pallas as pl
from jax.experimental.pallas import tpu as pltpu

D_MODEL = 1024
N_HEADS = 8
HEAD_DIM = 64
N_KV_HEADS = 2
HEADS_PER_GROUP = N_HEADS // N_KV_HEADS
ROT_DIM = HEAD_DIM // 4
ROPE_THETA = 500000.0
NSA_BLOCK = 64
NSA_TOP_N = 16
NSA_WINDOW = 512
PHI_HIDDEN = 2 * HEAD_DIM
D_CONV = 512
CONV_WIDTH = 3
D_FF = 2816
FFN_HALF = 0.5
NORM_EPS = 1e-6
PAGE_SIZE = 128
Q_COLS = N_HEADS * HEAD_DIM
KV_BRANCH_COLS = 2 * N_KV_HEADS * HEAD_DIM
KV_COLS = 3 * KV_BRANCH_COLS
NSA_GATE_COLS = 3 * N_HEADS
CONV_COLS = 3 * D_CONV
MERGE_COLS = 2 * D_MODEL
GD_ROWS = N_KV_HEADS * HEAD_DIM

LANES = 128
GATE_PAD = LANES
QKVG_COLS = Q_COLS + KV_COLS + GATE_PAD
MASK_INIT = -1e30
MASK_BIAS = -2e30
VMEM_LIMIT = 56 * 1024 * 1024
MIX_HALO = 16

BF16 = jnp.bfloat16
F32 = jnp.float32


def _dot(a, b):
    return jnp.dot(a, b, preferred_element_type=F32)


def _dot_tn(a, b):
    return lax.dot_general(a, b, (((0,), (0,)), ((), ())), preferred_element_type=F32)


def _dot_nt(a, b):
    return lax.dot_general(a, b, (((1,), (1,)), ((), ())), preferred_element_type=F32)


def _rms(x, g):
    return x * lax.rsqrt(jnp.mean(x * x, axis=-1, keepdims=True) + NORM_EPS) * g


def _silu(x):
    return x * jax.nn.sigmoid(x)


def _params(*sem):
    return pltpu.CompilerParams(dimension_semantics=sem, vmem_limit_bytes=VMEM_LIMIT)


def _ada_kernel(c_ref, w_ref, b_ref, o_ref):
    c = _silu(c_ref[...]).astype(BF16)
    o_ref[...] = _dot(c, w_ref[...].astype(BF16)) + b_ref[...]


def _ada(c, w_ada, b_ada):
    rows = c.shape[0]
    n = w_ada.shape[1]
    tn = 9 * LANES
    return pl.pallas_call(
        _ada_kernel,
        grid=(n // tn,),
        in_specs=[
            pl.BlockSpec((rows, D_MODEL), lambda j: (0, 0)),
            pl.BlockSpec((D_MODEL, tn), lambda j: (0, j)),
            pl.BlockSpec((1, tn), lambda j: (0, j)),
        ],
        out_specs=pl.BlockSpec((rows, tn), lambda j: (0, j)),
        out_shape=jax.ShapeDtypeStruct((rows, n), F32),
        compiler_params=_params("parallel"),
        name="ada",
    )(c, w_ada, b_ada.reshape(1, n))


class _Mod:
    def __init__(self, mod, per_row, tiles_per_seq=1):
        self.per_row = per_row
        self.tiles_per_seq = tiles_per_seq
        self.arr = mod if per_row else mod.reshape(mod.shape[0] * 9, 1, D_MODEL)

    def spec(self, k):
        if self.per_row:
            return pl.BlockSpec((self.arr.shape[0], D_MODEL), lambda i, *_: (0, k))
        tps = self.tiles_per_seq
        return pl.BlockSpec((None, 1, D_MODEL), lambda i, *_: ((i // tps) * 9 + k, 0, 0))


def _gspec(k):
    return pl.BlockSpec((None, 1, D_MODEL), lambda i, *_: (k, 0, 0))


def _ffn_kernel(x_ref, shift_ref, scale_ref, gate_ref, gpre_ref, gpost_ref, wg_ref, wu_ref, wd_ref,
                o_ref, a_scr, acc_scr):
    j = pl.program_id(1)

    @pl.when(j == 0)
    def _():
        a = _rms(x_ref[...], gpre_ref[...]) * (1.0 + scale_ref[...]) + shift_ref[...]
        a_scr[...] = a.astype(BF16)
        acc_scr[...] = jnp.zeros_like(acc_scr)

    a = a_scr[...]
    h = _silu(_dot(a, wg_ref[...])) * _dot(a, wu_ref[...])
    acc_scr[...] += _dot(h.astype(BF16), wd_ref[...])

    @pl.when(j == pl.num_programs(1) - 1)
    def _():
        y = _rms(acc_scr[...], gpost_ref[...])
        o_ref[...] = x_ref[...] + FFN_HALF * gate_ref[...] * y


def _ffn(x, mod, g3, w_gu, w_down, sub, tm):
    m = x.shape[0]
    tf = D_FF // 2
    nf = D_FF // tf
    row = pl.BlockSpec((tm, D_MODEL), lambda i, j: (i, 0))
    return pl.pallas_call(
        _ffn_kernel,
        grid=(m // tm, nf),
        in_specs=[
            row, mod.spec(3 * sub), mod.spec(3 * sub + 1), mod.spec(3 * sub + 2),
            _gspec(2 * sub), _gspec(2 * sub + 1),
            pl.BlockSpec((D_MODEL, tf), lambda i, j: (0, j)),
            pl.BlockSpec((D_MODEL, tf), lambda i, j: (0, j + nf)),
            pl.BlockSpec((tf, D_MODEL), lambda i, j: (j, 0)),
        ],
        out_specs=row,
        out_shape=jax.ShapeDtypeStruct((m, D_MODEL), F32),
        scratch_shapes=[pltpu.VMEM((tm, D_MODEL), BF16), pltpu.VMEM((tm, D_MODEL), F32)],
        compiler_params=_params("parallel", "arbitrary"),
        name=f"ffn{sub}",
    )(x, mod.arr, mod.arr, mod.arr, g3, g3, w_gu, w_gu, w_down)


def _rope(x, cos, sin_lo, sin_hi):
    return x * cos + pltpu.roll(x, LANES - ROT_DIM // 2, 1) * sin_lo + pltpu.roll(x, ROT_DIM // 2, 1) * sin_hi


def _proj_kernel(x_ref, shift_ref, scale_ref, g_ref, cos_ref, slo_ref, shi_ref, w_ref,
                 q_ref, qt_ref, cmp_ref, kvt_ref, kvtb_ref, kvb_ref, gate_ref, gatet_ref):
    a = (_rms(x_ref[...], g_ref[...]) * (1.0 + scale_ref[...]) + shift_ref[...]).astype(BF16)
    z = _dot(a, w_ref[...])
    cos, slo, shi = cos_ref[...], slo_ref[...], shi_ref[...]
    scale = HEAD_DIM ** -0.5
    for s in range(Q_COLS // LANES):
        q = _rope(z[:, s * LANES:(s + 1) * LANES], cos, slo, shi) * scale
        q_ref[:, s * LANES:(s + 1) * LANES] = q.astype(BF16)
        qt_ref[s * LANES:(s + 1) * LANES, :] = q.T.astype(BF16)
    for br in range(3):
        c0 = Q_COLS + br * KV_BRANCH_COLS
        k = _rope(z[:, c0:c0 + LANES], cos, slo, shi)
        v = z[:, c0 + LANES:c0 + 2 * LANES]
        for r0, x in ((0, k), (LANES, v)):
            xt = x.T
            kvt_ref[br, r0:r0 + LANES, :] = xt
            kvtb_ref[br, r0:r0 + LANES, :] = xt.astype(BF16)
        if br == 0:
            cmp_ref[:, :LANES] = k
            cmp_ref[:, LANES:] = v
        kvb_ref[:, br * KV_BRANCH_COLS:br * KV_BRANCH_COLS + LANES] = k.astype(BF16)
        kvb_ref[:, br * KV_BRANCH_COLS + LANES:(br + 1) * KV_BRANCH_COLS] = v.astype(BF16)
    gate = jax.nn.sigmoid(z[:, Q_COLS + KV_COLS:])
    gate_ref[...] = gate
    gatet_ref[...] = gate.T


def _proj(x, mod, g3, tables, w_qkvg, tm):
    m = x.shape[0]
    t_rows = tables[0].shape[0]
    nt = t_rows // tm
    n_seq = m // t_rows
    row = lambda n: pl.BlockSpec((tm, n), lambda i: (i, 0))
    tab = pl.BlockSpec((tm, LANES), lambda i: (i % nt, 0))
    col = lambda n: pl.BlockSpec((n, tm), lambda i: (0, i))
    kvt_spec = pl.BlockSpec((3, None, KV_BRANCH_COLS, tm), lambda i: (0, i // nt, 0, i % nt))
    return pl.pallas_call(
        _proj_kernel,
        grid=(m // tm,),
        in_specs=[row(D_MODEL), mod.spec(3), mod.spec(4), _gspec(2), tab, tab, tab,
                  pl.BlockSpec((D_MODEL, QKVG_COLS), lambda i: (0, 0))],
        out_specs=[row(Q_COLS), col(Q_COLS), row(KV_BRANCH_COLS), kvt_spec, kvt_spec,
                   row(KV_COLS), row(GATE_PAD), col(GATE_PAD)],
        out_shape=[
            jax.ShapeDtypeStruct((m, Q_COLS), BF16),
            jax.ShapeDtypeStruct((Q_COLS, m), BF16),
            jax.ShapeDtypeStruct((m, KV_BRANCH_COLS), F32),
            jax.ShapeDtypeStruct((3, n_seq, KV_BRANCH_COLS, t_rows), F32),
            jax.ShapeDtypeStruct((3, n_seq, KV_BRANCH_COLS, t_rows), BF16),
            jax.ShapeDtypeStruct((m, KV_COLS), BF16),
            jax.ShapeDtypeStruct((m, GATE_PAD), F32),
            jax.ShapeDtypeStruct((GATE_PAD, m), F32),
        ],
        compiler_params=_params("parallel"),
        name="proj",
    )(x, mod.arr, mod.arr, g3, *tables, w_qkvg)


def _rope_tables(pos):
    half = ROT_DIM // 2
    inv = jnp.power(ROPE_THETA, -jnp.arange(half, dtype=F32) * 2.0 / ROT_DIM)
    ang = pos.astype(F32)[:, None] * inv[None, :]
    cos, sin = jnp.cos(ang), jnp.sin(ang)
    n = pos.shape[0]
    ones = jnp.ones((n, HEAD_DIM - ROT_DIM), F32)
    zeros = jnp.zeros((n, HEAD_DIM - ROT_DIM), F32)
    zh = jnp.zeros((n, half), F32)
    c = jnp.concatenate([cos, cos, ones], axis=1)
    lo = jnp.concatenate([-sin, zh, zeros], axis=1)
    hi = jnp.concatenate([zh, sin, zeros], axis=1)
    return tuple(jnp.tile(t, (1, LANES // HEAD_DIM)) for t in (c, lo, hi))


def _compress_kernel(x_ref, pe_ref, w1_ref, b1_ref, w2_ref, o_ref, *rest):
    vt_ref = rest[0] if len(rest) == 2 else None
    acc_scr = rest[-1]
    j = pl.program_id(0)
    nt = x_ref.shape[1] // KV_BRANCH_COLS

    @pl.when(j == 0)
    def _():
        acc_scr[...] = jnp.zeros_like(acc_scr)

    for kv in range(2):
        acc = acc_scr[kv]
        for t in range(0, nt, 2):
            xs = []
            for tt in (t, t + 1):
                c0 = tt * KV_BRANCH_COLS + kv * LANES
                xs.append((x_ref[:, c0:c0 + LANES] + pe_ref[kv, tt]).astype(BF16))
            acc = acc + _dot(jnp.concatenate(xs, axis=1), w1_ref[kv, t * LANES:(t + 2) * LANES, :])
        acc_scr[kv] = acc

    @pl.when(j == pl.num_programs(0) - 1)
    def _():
        for kv in range(2):
            h = _silu(acc_scr[kv] + b1_ref[kv])
            c = _dot(h.astype(BF16), w2_ref[kv])
            o_ref[kv] = c.astype(o_ref.dtype)
            if kv == 1 and vt_ref is not None:
                ct = c.T.astype(vt_ref.dtype)
                nb = vt_ref.shape[2]
                for b in range(vt_ref.shape[0]):
                    vt_ref[b] = ct[:, b * nb:(b + 1) * nb]


def _compress_weights(phi_pe, phi_w1, phi_b1, phi_w2):
    eye = jnp.eye(N_KV_HEADS, dtype=F32)
    w1 = phi_w1.reshape(2, NSA_BLOCK, HEAD_DIM, PHI_HIDDEN)
    w1 = jnp.einsum("gh,ktdj->ktgdhj", eye, w1).reshape(2, NSA_BLOCK * LANES, N_KV_HEADS * PHI_HIDDEN)
    w2 = jnp.einsum("gh,kjd->kgjhd", eye, phi_w2).reshape(2, N_KV_HEADS * PHI_HIDDEN, LANES)
    pe = jnp.tile(phi_pe, (1, 1, N_KV_HEADS)).reshape(2, NSA_BLOCK, 1, LANES)
    b1 = jnp.tile(phi_b1, (1, N_KV_HEADS)).reshape(2, 1, N_KV_HEADS * PHI_HIDDEN)
    return pe, w1.astype(BF16), b1, w2.astype(BF16)


def _compress_weights_paged(phi_pe, phi_w1, phi_b1, phi_w2):
    blocks = PAGE_SIZE // NSA_BLOCK
    eye = jnp.eye(blocks, dtype=F32)
    w1 = phi_w1.reshape(2, NSA_BLOCK, HEAD_DIM, PHI_HIDDEN)
    w1 = jnp.einsum("bc,ktdj->kdbtcj", eye, w1).reshape(2, HEAD_DIM * PAGE_SIZE, blocks * PHI_HIDDEN)
    w2 = jnp.einsum("bc,kjd->kbjcd", eye, phi_w2).reshape(2, blocks * PHI_HIDDEN, blocks * HEAD_DIM)
    pe = jnp.tile(phi_pe.transpose(0, 2, 1), (1, 1, blocks)).reshape(2, HEAD_DIM, 1, PAGE_SIZE)
    b1 = jnp.tile(phi_b1, (1, blocks)).reshape(2, 1, blocks * PHI_HIDDEN)
    return pe, w1.astype(BF16), b1, w2.astype(BF16)


def _compress(x_blocks, cw, out_dtype=BF16, blocks_per_seq=None):
    pe, w1, b1, w2 = cw
    rows = x_blocks.shape[0]
    out_specs = [pl.BlockSpec((2, rows, LANES), lambda j: (0, 0, 0))]
    out_shape = [jax.ShapeDtypeStruct((2, rows, LANES), out_dtype)]
    if blocks_per_seq is not None:
        vt_shape = (rows // blocks_per_seq, LANES, blocks_per_seq)
        out_specs.append(pl.BlockSpec(vt_shape, lambda j: (0, 0, 0)))
        out_shape.append(jax.ShapeDtypeStruct(vt_shape, out_dtype))
    tchunk = 8
    nchunk = NSA_BLOCK // tchunk
    hid = N_KV_HEADS * PHI_HIDDEN
    return pl.pallas_call(
        _compress_kernel,
        grid=(nchunk,),
        in_specs=[
            pl.BlockSpec((rows, tchunk * KV_BRANCH_COLS), lambda j: (0, j)),
            pl.BlockSpec((2, tchunk, 1, LANES), lambda j: (0, j, 0, 0)),
            pl.BlockSpec((2, tchunk * LANES, hid), lambda j: (0, j, 0)),
            pl.BlockSpec((2, 1, hid), lambda j: (0, 0, 0)),
            pl.BlockSpec((2, hid, LANES), lambda j: (0, 0, 0)),
        ],
        out_specs=out_specs,
        out_shape=out_shape,
        scratch_shapes=[pltpu.VMEM((2, rows, hid), F32)],
        compiler_params=_params("arbitrary"),
        name="compress",
    )(x_blocks, pe, w1, b1, w2)


def _gather_compress_kernel(pt_ref, *refs, n_pages, group):
    del pt_ref
    pages = refs[:n_pages]
    pe_ref, w1_ref, b1_ref, w2_ref, o_ref, xk_scr, xv_scr = refs[n_pages:]
    x_scr = (xk_scr, xv_scr)
    slot = pl.program_id(0) % group
    for k in range(n_pages):
        r0 = pl.multiple_of((slot * n_pages + k) * GD_ROWS, GD_ROWS)
        for kv in range(2):
            x_scr[kv][pl.ds(r0, GD_ROWS), :] = pages[k][kv * GD_ROWS:(kv + 1) * GD_ROWS, :]

    @pl.when(slot == group - 1)
    def _():
        rows = group * n_pages * N_KV_HEADS
        for kv in range(2):
            acc = jnp.zeros((rows, w1_ref.shape[2]), F32)
            for d in range(0, HEAD_DIM, 2):
                xs = []
                for dd in (d, d + 1):
                    x = x_scr[kv][pl.ds(dd, rows, stride=HEAD_DIM), :]
                    xs.append((x + pe_ref[kv, dd]).astype(BF16))
                acc = acc + _dot(jnp.concatenate(xs, axis=1), w1_ref[kv, d * PAGE_SIZE:(d + 2) * PAGE_SIZE, :])
            h = _silu(acc + b1_ref[kv])
            o_ref[kv] = _dot(h.astype(BF16), w2_ref[kv])


def _page_specs(n_pages):
    return [pl.BlockSpec((None, KV_BRANCH_COLS, PAGE_SIZE), lambda s, pt, k=k: (pt[s, k], 0, 0))
            for k in range(n_pages)]


def _gather_compress(cache_t, page_table, cwp):
    pe, w1, b1, w2 = cwp
    n_dec, n_pages = page_table.shape
    group = 8
    rows = group * n_pages * N_KV_HEADS
    blocks = PAGE_SIZE // NSA_BLOCK
    const = lambda a: pl.BlockSpec(a.shape, lambda s, pt: (0,) * a.ndim)
    out = pl.pallas_call(
        functools.partial(_gather_compress_kernel, n_pages=n_pages, group=group),
        grid_spec=pltpu.PrefetchScalarGridSpec(
            num_scalar_prefetch=1,
            grid=(n_dec,),
            in_specs=_page_specs(n_pages) + [const(pe), const(w1), const(b1), const(w2)],
            out_specs=pl.BlockSpec((2, rows, blocks * HEAD_DIM), lambda s, pt: (0, s // group, 0)),
            scratch_shapes=[pltpu.VMEM((group * n_pages * GD_ROWS, PAGE_SIZE), F32)] * 2,
        ),
        out_shape=jax.ShapeDtypeStruct((2, n_dec // group * rows, blocks * HEAD_DIM), F32),
        compiler_params=_params("arbitrary"),
        name="gather_compress",
    )(page_table, *([cache_t] * n_pages), pe, w1, b1, w2)
    out = out.reshape(2, n_dec, n_pages, N_KV_HEADS, blocks, HEAD_DIM).transpose(0, 1, 2, 4, 3, 5)
    return out.reshape(2, n_dec, n_pages * blocks, N_KV_HEADS * HEAD_DIM)


def _group_queries_t(qt, g):
    t = qt.shape[1]
    zeros = jnp.zeros((HEAD_DIM, t), qt.dtype)
    cols = []
    for j in range(HEADS_PER_GROUP):
        qh = qt[j * LANES + g * HEAD_DIM:j * LANES + (g + 1) * HEAD_DIM, :]
        cols.append(jnp.concatenate([qh, zeros] if g == 0 else [zeros, qh], axis=0))
    return jnp.concatenate(cols, axis=1)


def _tile_heads(x):
    return jnp.concatenate([x] * HEADS_PER_GROUP, axis=1)


def _select_blocks_t(imp, nblk, n_sel):
    nb = imp.shape[0]
    rank = jnp.zeros(imp.shape, F32)
    for m in range(nb):
        row = imp[m:m + 1, :]
        beats = jnp.where(row > imp, 1.0, jnp.where(row == imp, jnp.where(m < nblk, 1.0, 0.0), 0.0))
        rank = rank + beats
    return jnp.where(rank < n_sel, jnp.where(imp >= 0.0, 1.0, 0.0), 0.0)


def _select_blocks(imp, nblk, n_sel):
    nb = imp.shape[1]
    rank = jnp.zeros(imp.shape, F32)
    for m in range(nb):
        col = imp[:, m:m + 1]
        beats = jnp.where(col > imp, 1.0, jnp.where(col == imp, jnp.where(m < nblk, 1.0, 0.0), 0.0))
        rank = rank + beats
    return jnp.where(rank < n_sel, jnp.where(imp >= 0.0, 1.0, 0.0), 0.0)


def _flash_t(qg, k_ref, vt_ref, g, tile0, ntiles, tk, bias_fn):
    cols = qg.shape[1]
    v0 = GD_ROWS + g * HEAD_DIM

    def body(j, carry):
        m, l, acc = carry
        ks = pl.multiple_of((tile0 + j) * tk, tk)
        s = _dot(k_ref[pl.ds(ks, tk), :GD_ROWS], qg) + _tile_heads(bias_fn(ks))
        m_new = jnp.maximum(m, jnp.max(s, axis=0, keepdims=True))
        alpha = jnp.exp(m - m_new)
        p = jnp.exp(s - m_new)
        l = alpha * l + jnp.sum(p, axis=0, keepdims=True)
        acc = alpha * acc + _dot(vt_ref[v0:v0 + HEAD_DIM, pl.ds(ks, tk)], p.astype(BF16))
        return m_new, l, acc

    init = (jnp.full((1, cols), MASK_INIT, F32), jnp.zeros((1, cols), F32), jnp.zeros((HEAD_DIM, cols), F32))
    _, l, acc = lax.fori_loop(0, ntiles, body, init)
    return acc / l


def _attn_kernel(qt_ref, gatet_ref, kc_ref, vct_ref, slc_k_ref, slc_vt_ref, win_k_ref, win_vt_ref, o_ref,
                 *, tq, nb):
    i = pl.program_id(1)
    p0 = i * tq
    pos = p0 + lax.broadcasted_iota(jnp.int32, (1, tq), 1)
    nblk = lax.broadcasted_iota(jnp.int32, (nb, 1), 0)
    qt = qt_ref[...]
    gates = gatet_ref[...]
    tk = tq
    key_off = lax.broadcasted_iota(jnp.int32, (tk, 1), 0)
    blk_lane = lax.broadcasted_iota(jnp.int32, (1, nb), 1)
    for g in range(N_KV_HEADS):
        qg = _group_queries_t(qt, g)

        cvalid = (nblk + 1) * NSA_BLOCK <= _tile_heads(pos) + 1
        sc = jnp.where(cvalid, _dot(kc_ref[...], qg), MASK_INIT)
        e = jnp.exp(sc - jnp.max(sc, axis=0, keepdims=True))
        pc = jnp.where(cvalid, e / jnp.sum(e, axis=0, keepdims=True), 0.0)
        o_cmp = _dot(vct_ref[g * HEAD_DIM:(g + 1) * HEAD_DIM, :], pc.astype(BF16))

        imp = functools.reduce(jnp.add, [pc[:, j * tq:(j + 1) * tq] for j in range(HEADS_PER_GROUP)])
        cur = pos // NSA_BLOCK
        forced = HEADS_PER_GROUP + 1.0
        imp = jnp.where(nblk == 0, forced, jnp.where(nblk == cur, forced, jnp.where(nblk == cur - 1, forced, imp)))
        imp = jnp.where(nblk * NSA_BLOCK <= pos, imp, -1.0)
        sel_b = _select_blocks_t(imp, nblk, NSA_TOP_N).astype(BF16)

        def slc_bias(ks):
            kpos = ks + key_off
            blk_of_key = jnp.where(blk_lane == kpos // NSA_BLOCK, 1.0, 0.0).astype(BF16)
            chosen = _dot(blk_of_key, sel_b)
            return jnp.where(kpos <= pos, jnp.where(chosen > 0.5, 0.0, MASK_BIAS), MASK_BIAS)

        o_slc = _flash_t(qg, slc_k_ref, slc_vt_ref, g, 0, i + 1, tk, slc_bias)

        def win_bias(ks):
            diff = pos - (ks + key_off)
            return jnp.where(diff >= 0, jnp.where(diff < NSA_WINDOW, 0.0, MASK_BIAS), MASK_BIAS)

        wt0 = jnp.maximum(i - NSA_WINDOW // tk, 0)
        o_win = _flash_t(qg, win_k_ref, win_vt_ref, g, wt0, i + 1 - wt0, tk, win_bias)

        for j in range(HEADS_PER_GROUP):
            head = g * HEADS_PER_GROUP + j
            cols = slice(j * tq, (j + 1) * tq)
            o = (gates[head:head + 1] * o_cmp[:, cols]
                 + gates[N_HEADS + head:N_HEADS + head + 1] * o_slc[:, cols]
                 + gates[2 * N_HEADS + head:2 * N_HEADS + head + 1] * o_win[:, cols])
            r0 = j * LANES + g * HEAD_DIM
            o_ref[r0:r0 + HEAD_DIM, :] = o.astype(o_ref.dtype)


def _attn_prompt(qt, gatest, kc, vct, kvb, kvtb, n_seq, seq, tq):
    m = qt.shape[1]
    nb = seq // NSA_BLOCK
    nq = seq // tq
    col = lambda n: pl.BlockSpec((n, tq), lambda b, i: (0, b * nq + i))
    return pl.pallas_call(
        functools.partial(_attn_kernel, tq=tq, nb=nb),
        grid=(n_seq, nq),
        in_specs=[col(Q_COLS), col(4 * N_HEADS),
                  pl.BlockSpec((nb, LANES), lambda b, i: (b, 0)),
                  pl.BlockSpec((None, LANES, nb), lambda b, i: (b, 0, 0)),
                  pl.BlockSpec((seq, KV_BRANCH_COLS), lambda b, i: (b, 1)),
                  pl.BlockSpec((None, None, KV_BRANCH_COLS, seq), lambda b, i: (1, b, 0, 0)),
                  pl.BlockSpec((seq, KV_BRANCH_COLS), lambda b, i: (b, 2)),
                  pl.BlockSpec((None, None, KV_BRANCH_COLS, seq), lambda b, i: (2, b, 0, 0))],
        out_specs=col(Q_COLS),
        out_shape=jax.ShapeDtypeStruct((Q_COLS, m), BF16),
        compiler_params=_params("parallel", "arbitrary"),
        name="attn_prompt",
    )(qt, gatest, kc, vct, kvb, kvtb, kvb, kvtb)


def _softmax_pv(score_tiles, value_tiles):
    m = functools.reduce(jnp.maximum, [jnp.max(s, axis=-1, keepdims=True) for s in score_tiles])
    ps = [jnp.exp(s - m) for s in score_tiles]
    l = functools.reduce(jnp.add, [jnp.sum(p, axis=-1, keepdims=True) for p in ps])
    o = functools.reduce(jnp.add, [(_dot_nt if fm else _dot)(p.astype(BF16), v)
                                   for p, (v, fm) in zip(ps, value_tiles)])
    return o / l


def _attn_decode_kernel(pt_ref, *refs, n_pages, pos, nbp):
    del pt_ref
    pages = refs[:n_pages]
    q_ref, kvb_ref, gate_ref, kvc_ref, win_ref, o_ref = refs[n_pages:]
    tk = PAGE_SIZE
    lane = lax.broadcasted_iota(jnp.int32, (1, LANES), 1)
    head = lax.broadcasted_iota(jnp.int32, (N_HEADS, 1), 0)
    low_group = head < HEADS_PER_GROUP
    low_lanes = lane < HEAD_DIM

    q32 = q_ref[...].astype(F32)
    qm = jnp.zeros((N_HEADS, LANES), F32)
    for j in range(HEADS_PER_GROUP):
        qm = jnp.where(head % HEADS_PER_GROUP == j, q32[:, j * LANES:(j + 1) * LANES], qm)
    qm = jnp.where(low_group, jnp.where(low_lanes, qm, 0.0), jnp.where(low_lanes, 0.0, qm)).astype(BF16)

    gate_row = gate_ref[...]
    gate_col = [jnp.sum(jnp.where(lane == head + br * N_HEADS, gate_row, 0.0), axis=-1, keepdims=True)
                for br in range(3)]
    kv_new = kvb_ref[...].astype(F32)
    first_row = lax.broadcasted_iota(jnp.int32, (tk, 1), 0) == 0

    def new_key_tile(c0):
        return jnp.where(first_row, kv_new[:, c0:c0 + LANES], 0.0).astype(BF16)

    nblk = lax.broadcasted_iota(jnp.int32, (1, nbp), 1)
    cvalid = (nblk + 1) * NSA_BLOCK <= pos + 1
    sc = jnp.where(cvalid, _dot_nt(qm, kvc_ref[0].astype(BF16)), MASK_INIT)
    e = jnp.exp(sc - jnp.max(sc, axis=-1, keepdims=True))
    pc = jnp.where(cvalid, e / jnp.sum(e, axis=-1, keepdims=True), 0.0)
    o_cmp = _dot(pc.astype(BF16), kvc_ref[1].astype(BF16))

    imp = jnp.where(low_group,
                    jnp.sum(jnp.where(low_group, pc, 0.0), axis=0, keepdims=True),
                    jnp.sum(jnp.where(low_group, 0.0, pc), axis=0, keepdims=True))
    cur = pos // NSA_BLOCK
    forced = HEADS_PER_GROUP + 1.0
    imp = jnp.where(nblk == 0, forced, jnp.where(nblk == cur, forced, jnp.where(nblk == cur - 1, forced, imp)))
    imp = jnp.where(nblk * NSA_BLOCK <= pos, imp, -1.0)
    sel = _select_blocks(imp, nblk, NSA_TOP_N)

    def sel_bias(n):
        return jnp.where(sel[:, n:n + 1] > 0.5, 0.0, MASK_BIAS)

    s_tiles, v_tiles = [], []
    blocks_per_page = tk // NSA_BLOCK
    for k in range(n_pages):
        page = pages[k]
        s = _dot(qm, page[:GD_ROWS, :].astype(BF16))
        bias = sel_bias(k * blocks_per_page + blocks_per_page - 1)
        for b in range(blocks_per_page - 2, -1, -1):
            bias = jnp.where(lane < (b + 1) * NSA_BLOCK, sel_bias(k * blocks_per_page + b), bias)
        s_tiles.append(s + bias)
        v_tiles.append((page[GD_ROWS:, :].astype(BF16), True))
    s_new = _dot_nt(qm, new_key_tile(KV_BRANCH_COLS))
    s_tiles.append(s_new + jnp.where(lane == 0, sel_bias(n_pages * blocks_per_page), MASK_BIAS))
    v_tiles.append((new_key_tile(KV_BRANCH_COLS + LANES), False))
    o_slc = _softmax_pv(s_tiles, v_tiles)

    n_win = win_ref.shape[1]
    s_tiles, v_tiles = [], []
    for k in range(n_win // tk):
        diff = n_win - (k * tk + lane)
        bias = jnp.where(diff < NSA_WINDOW, 0.0, MASK_BIAS)
        s_tiles.append(_dot(qm, win_ref[:GD_ROWS, k * tk:(k + 1) * tk].astype(BF16)) + bias)
        v_tiles.append((win_ref[GD_ROWS:, k * tk:(k + 1) * tk].astype(BF16), True))
    s_tiles.append(_dot_nt(qm, new_key_tile(2 * KV_BRANCH_COLS)) + jnp.where(lane == 0, 0.0, MASK_BIAS))
    v_tiles.append((new_key_tile(2 * KV_BRANCH_COLS + LANES), False))
    o_win = _softmax_pv(s_tiles, v_tiles)

    o = gate_col[0] * o_cmp + gate_col[1] * o_slc + gate_col[2] * o_win
    for j in range(HEADS_PER_GROUP):
        pair = jnp.where(low_lanes, o[j:j + 1], o[j + HEADS_PER_GROUP:j + HEADS_PER_GROUP + 1])
        o_ref[:, j * LANES:(j + 1) * LANES] = pair.astype(o_ref.dtype)


def _attn_decode(q, kvb, gates, kvc, cache_slc, win_state, page_table, pos):
    n_dec, n_pages = page_table.shape
    nbp = kvc.shape[2]
    n_win = win_state.shape[2]
    per_seq = lambda a: pl.BlockSpec((None, 1, a.shape[-1]), lambda s, pt: (s, 0, 0))
    q3, kvb3, g3 = (a.reshape(n_dec, 1, a.shape[-1]) for a in (q, kvb, gates))
    out = pl.pallas_call(
        functools.partial(_attn_decode_kernel, n_pages=n_pages, pos=pos, nbp=nbp),
        grid_spec=pltpu.PrefetchScalarGridSpec(
            num_scalar_prefetch=1,
            grid=(n_dec,),
            in_specs=_page_specs(n_pages) + [
                per_seq(q3), per_seq(kvb3), per_seq(g3),
                pl.BlockSpec((2, None, nbp, LANES), lambda s, pt: (0, s, 0, 0)),
                pl.BlockSpec((None, KV_BRANCH_COLS, n_win), lambda s, pt: (s, 0, 0)),
            ],
            out_specs=pl.BlockSpec((None, 1, Q_COLS), lambda s, pt: (s, 0, 0)),
        ),
        out_shape=jax.ShapeDtypeStruct((n_dec, 1, Q_COLS), BF16),
        compiler_params=_params("arbitrary"),
        name="attn_decode",
    )(page_table, *([cache_slc] * n_pages), q3, kvb3, g3, kvc, win_state)
    return out.reshape(n_dec, Q_COLS)


def _mix_kernel(*refs, tm, halo, tiles_per_seq):
    if halo:
        (x_ref, xh_ref, o_ref, shift_ref, scale_ref, gate_ref, gpre_ref, gpost_ref, cw_ref,
         wconv_ref, wmerge_ref, wup_ref, wco_ref, wout_ref, y_ref, ulast_ref, u_scr) = refs
        x = x_ref[...]
        xe = jnp.concatenate([xh_ref[...], x], axis=0)
    else:
        (x_ref, um1_ref, um2_ref, o_ref, shift_ref, scale_ref, gate_ref, gpre_ref, gpost_ref, cw_ref,
         wconv_ref, wmerge_ref, wup_ref, wco_ref, wout_ref, y_ref, ulast_ref) = refs
        x = x_ref[...]
        xe = x
    h0 = xe.shape[0] - tm
    a = (_rms(xe, gpre_ref[...]) * (1.0 + scale_ref[...]) + shift_ref[...]).astype(BF16)
    zc = _dot(a, wconv_ref[...])
    u = zc[:, 2 * D_CONV:] * zc[:, :D_CONV]
    cb = zc[h0:, D_CONV:2 * D_CONV]
    if halo:
        keep_halo = jnp.where(pl.program_id(0) % tiles_per_seq == 0, 0.0, 1.0)
        rows = lax.broadcasted_iota(jnp.int32, (xe.shape[0], 1), 0)
        u = jnp.where(rows < h0, u * keep_halo, u)
        u_scr[...] = u
        um1 = u_scr[h0 - 1:h0 - 1 + tm, :]
        um2 = u_scr[h0 - 2:h0 - 2 + tm, :]
        u0 = u[h0:]
        ulast_ref[...] = u[tm:]
    else:
        um1, um2, u0 = um1_ref[...], um2_ref[...], u
        ulast_ref[...] = u
    cw = cw_ref[...]
    y = cw[0:1] * um2 + cw[1:2] * um1 + cw[2:3] * u0
    conv_out = _dot((cb * y).astype(BF16), wco_ref[...])
    attn_out = _dot_tn(o_ref[...], wup_ref[...])
    mg = jax.nn.sigmoid(_dot(a[h0:], wmerge_ref[...]))
    merged = mg[:, :D_MODEL] * attn_out + mg[:, D_MODEL:] * conv_out
    mixed = _dot(merged.astype(BF16), wout_ref[...])
    y_ref[...] = x + gate_ref[...] * _rms(mixed, gpost_ref[...])


def _mix(x, o_att, mod, g3, conv_w, weights, tm, tiles_per_seq=None, prev=None):
    m = x.shape[0]
    halo = prev is None
    h0 = MIX_HALO
    row = lambda n: pl.BlockSpec((tm, n), lambda i: (i, 0))
    full = lambda a: pl.BlockSpec(a.shape, lambda i: (0,) * a.ndim)
    att = pl.BlockSpec((Q_COLS, tm), lambda i: (0, i))
    common = [att, mod.spec(3), mod.spec(4), mod.spec(5), _gspec(2), _gspec(3), full(conv_w)]
    common += [full(w) for w in weights]
    common_args = [o_att, mod.arr, mod.arr, mod.arr, g3, g3, conv_w, *weights]
    if halo:
        hb = tm // h0
        in_specs = [row(D_MODEL), pl.BlockSpec((h0, D_MODEL), lambda i: (jnp.maximum(i * hb - 1, 0), 0))] + common
        args = [x, x] + common_args
        scratch = [pltpu.VMEM((tm + h0, D_CONV), F32)]
        ulast = (jax.ShapeDtypeStruct((m // tm * h0, D_CONV), F32), pl.BlockSpec((h0, D_CONV), lambda i: (i, 0)))
    else:
        in_specs = [row(D_MODEL), row(D_CONV), row(D_CONV)] + common
        args = [x, prev[0], prev[1]] + common_args
        scratch = []
        ulast = (jax.ShapeDtypeStruct((m, D_CONV), F32), row(D_CONV))
    return pl.pallas_call(
        functools.partial(_mix_kernel, tm=tm, halo=halo, tiles_per_seq=tiles_per_seq),
        grid=(m // tm,),
        in_specs=in_specs,
        out_specs=[row(D_MODEL), ulast[1]],
        out_shape=[jax.ShapeDtypeStruct((m, D_MODEL), F32), ulast[0]],
        scratch_shapes=scratch,
        compiler_params=_params("parallel"),
        name="mix",
    )(*args)


def _head_pair_perm():
    order = []
    for j in range(HEADS_PER_GROUP):
        for g in range(N_KV_HEADS):
            head = g * HEADS_PER_GROUP + j
            order.extend(range(head * HEAD_DIM, (head + 1) * HEAD_DIM))
    return jnp.array(order, dtype=jnp.int32)


def _layer_weights(w_in, w_attn_up, w_conv_out, w_out):
    perm = _head_pair_perm()
    o1 = Q_COLS
    o2 = o1 + KV_COLS
    o3 = o2 + NSA_GATE_COLS
    o4 = o3 + CONV_COLS
    w_q = w_in[:, :o1][:, perm]
    w_g = jnp.pad(w_in[:, o2:o3], ((0, 0), (0, GATE_PAD - NSA_GATE_COLS)))
    w_qkvg = jnp.concatenate([w_q, w_in[:, o1:o2], w_g], axis=1).astype(BF16)
    mix_w = (w_in[:, o3:o4].astype(BF16), w_in[:, o4:].astype(BF16), w_attn_up[perm].astype(BF16),
             w_conv_out.astype(BF16), w_out.astype(BF16))
    return w_qkvg, mix_w


def kernel(x_prompt, x_sample, c_prompt, c_sample, cache_cmp_kv, cache_slc_kv, state_win_kv, state_conv,
           page_table, w_ada, b_ada, g_norm, w_ffn_gu, w_ffn_down, w_in, phi_pe, phi_w1, phi_b1, phi_w2,
           w_attn_up, conv_w, w_conv_out, w_out):
    n_seq, seq, _ = x_prompt.shape
    n_dec = x_sample.shape[0]
    assert w_ada.shape[0] == 1
    l = 0

    mod_all = _ada(jnp.concatenate([c_prompt, c_sample], axis=0), w_ada[l], b_ada[l])
    g3 = g_norm[l].reshape(6, 1, D_MODEL)
    w_gu = w_ffn_gu[l].astype(BF16)
    w_down = w_ffn_down[l].astype(BF16)
    w_qkvg, mix_w = _layer_weights(w_in[l], w_attn_up[l], w_conv_out[l], w_out[l])
    cw = _compress_weights(phi_pe[l], phi_w1[l], phi_b1[l], phi_w2[l])

    tm = 512
    tq = 256
    m = n_seq * seq
    mod_p = _Mod(mod_all[:n_seq], per_row=False, tiles_per_seq=seq // tm)
    x0 = x_prompt.reshape(m, D_MODEL)
    x1 = _ffn(x0, mod_p, g3, w_gu[0], w_down[0], 0, tm)
    tables = _rope_tables(jnp.arange(seq, dtype=jnp.int32))
    _, qt, cmp_new, kvt, kvtb, kvb, _, gatest = _proj(x1, mod_p, g3, tables, w_qkvg, tm)
    nb = seq // NSA_BLOCK
    kvc, vct = _compress(cmp_new.reshape(n_seq * nb, NSA_BLOCK * KV_BRANCH_COLS), cw, blocks_per_seq=nb)
    o_att = _attn_prompt(qt, gatest, kvc[0], vct, kvb, kvtb, n_seq, seq, tq)
    x2, ulast = _mix(x1, o_att, mod_p, g3, conv_w[l], mix_w, tm, tiles_per_seq=seq // tm)
    y_prompt = _ffn(x2, mod_p, g3, w_gu[1], w_down[1], 2, tm).reshape(n_seq, seq, D_MODEL)

    def token_major(t):
        n, _, tt = t.shape
        return t.reshape(n, 2, N_KV_HEADS, HEAD_DIM, tt).transpose(0, 4, 1, 2, 3)[None]

    def feature_major(a):
        n, tt = a.shape[:2]
        return a.transpose(0, 2, 3, 4, 1).reshape(n, KV_BRANCH_COLS, tt)

    cmp_kv_prompt = token_major(kvt[0])
    slc_kv_prompt = token_major(kvt[1])
    win_keep = min(NSA_WINDOW, seq)
    win_kv_prompt = token_major(kvt[2][:, :, seq - win_keep:])
    conv_prompt = ulast.reshape(n_seq, seq // tm, MIX_HALO, D_CONV)[:, -1, MIX_HALO - (CONV_WIDTH - 1):][None]

    n_pages = page_table.shape[1]
    past_len = n_pages * PAGE_SIZE
    pos_s = past_len + jnp.arange(x_sample.shape[1], dtype=jnp.int32)
    assert x_sample.shape[1] == 1 and past_len % NSA_BLOCK == 0 and state_win_kv.shape[2] <= past_len
    mod_s = _Mod(mod_all[n_seq:], per_row=True)
    xs1 = _ffn(x_sample.reshape(n_dec, D_MODEL), mod_s, g3, w_gu[0], w_down[0], 0, n_dec)
    tables_s = _rope_tables(jnp.broadcast_to(pos_s, (n_dec,)))
    q_s, _, cmp_s, kvt_s, _, kvb_s, gates_s, _ = _proj(xs1, mod_s, g3, tables_s, w_qkvg, n_dec)
    nb_past = past_len // NSA_BLOCK
    nb_pad = -(-(nb_past + 1) // NSA_BLOCK) * NSA_BLOCK
    cwp = _compress_weights_paged(phi_pe[l], phi_w1[l], phi_b1[l], phi_w2[l])
    kvc_past = _gather_compress(feature_major(cache_cmp_kv[l]), page_table, cwp)
    new_block = jnp.pad(cmp_s[:, None, :], ((0, 0), (0, NSA_BLOCK - 1), (0, 0)))
    kvc_new = _compress(new_block.reshape(n_dec, NSA_BLOCK * KV_BRANCH_COLS), cw, F32)[0][:, :, None, :]
    kvc_s = jnp.concatenate(
        [kvc_past, kvc_new, jnp.zeros((2, n_dec, nb_pad - nb_past - 1, LANES), F32)], axis=2)
    win_state = feature_major(state_win_kv[l])
    o_att_s = _attn_decode(q_s, kvb_s, gates_s, kvc_s, feature_major(cache_slc_kv[l]), win_state, page_table,
                           past_len)
    conv_state = state_conv[l]
    xs2, u_s = _mix(xs1, o_att_s.T, mod_s, g3, conv_w[l], mix_w, n_dec,
                    prev=(conv_state[:, CONV_WIDTH - 2], conv_state[:, CONV_WIDTH - 3]))
    y_sample = _ffn(xs2, mod_s, g3, w_gu[1], w_down[1], 2, n_dec).reshape(x_sample.shape)

    kvs = (1, n_dec, 1, 2, N_KV_HEADS, HEAD_DIM)
    win_keep_s = min(NSA_WINDOW, past_len + 1)
    win_new_s = kvt_s[2, 0].T[:, :, None]
    win_kv_sample = token_major(jnp.concatenate([win_state, win_new_s], axis=2)[:, :, -win_keep_s:])
    conv_sample = jnp.concatenate([conv_state[:, 1:], u_s[:, None, :]], axis=1)[None]
    return (y_prompt, y_sample, cmp_kv_prompt, slc_kv_prompt, win_kv_prompt, conv_prompt,
            token_major(kvt_s[0]).reshape(kvs), token_major(kvt_s[1]).reshape(kvs), win_kv_sample, conv_sample)
```

```python
import functools

import jax
import jax.numpy as jnp
from jax import lax
from jax.experimental import pallas as pl
from jax.experimental.pallas import tpu as pltpu

D_MODEL = 1024
N_HEADS = 8
HEAD_DIM = 64
N_KV_HEADS = 2
HEADS_PER_GROUP = N_HEADS // N_KV_HEADS
ROT_DIM = HEAD_DIM // 4
ROPE_THETA = 500000.0
NSA_BLOCK = 64
NSA_TOP_N = 16
NSA_WINDOW = 512
PHI_HIDDEN = 2 * HEAD_DIM
D_CONV = 512
CONV_WIDTH = 3
D_FF = 2816
FFN_HALF = 0.5
NORM_EPS = 1e-6
PAGE_SIZE = 128
Q_COLS = N_HEADS * HEAD_DIM
KV_BRANCH_COLS = 2 * N_KV_HEADS * HEAD_DIM
KV_COLS = 3 * KV_BRANCH_COLS
NSA_GATE_COLS = 3 * N_HEADS
CONV_COLS = 3 * D_CONV
MERGE_COLS = 2 * D_MODEL
GD_ROWS = N_KV_HEADS * HEAD_DIM

LANES = 128
GATE_PAD = LANES
QKVG_COLS = Q_COLS + KV_COLS + GATE_PAD
LOG2_E = 1.4426950408889634
MASK_INIT = -1e30
MASK_BIAS = -2e30
VMEM_LIMIT = 56 * 1024 * 1024
SLAB_PITCH = HEAD_DIM + 8
MIX_HALO = 16

BF16 = jnp.bfloat16
F32 = jnp.float32


def _dot(a, b):
    return jnp.dot(a, b, preferred_element_type=F32)


def _dot_tn(a, b):
    return lax.dot_general(a, b, (((0,), (0,)), ((), ())), preferred_element_type=F32)


def _dot_nt(a, b):
    return lax.dot_general(a, b, (((1,), (1,)), ((), ())), preferred_element_type=F32)


def _rms(x, g):
    return x * lax.rsqrt(jnp.mean(x * x, axis=-1, keepdims=True) + NORM_EPS) * g


def _silu(x):
    return x * jax.nn.sigmoid(x)


def _params(*sem):
    return pltpu.CompilerParams(dimension_semantics=sem, vmem_limit_bytes=VMEM_LIMIT)


def _ada_kernel(c_ref, w_ref, b_ref, o_ref):
    c = _silu(c_ref[...]).astype(BF16)
    o_ref[...] = _dot(c, w_ref[...].astype(BF16)) + b_ref[...]


def _ada(c, w_ada, b_ada):
    rows = c.shape[0]
    n = w_ada.shape[1]
    tn = 9 * LANES
    return pl.pallas_call(
        _ada_kernel,
        grid=(n // tn,),
        in_specs=[
            pl.BlockSpec((rows, D_MODEL), lambda j: (0, 0)),
            pl.BlockSpec((D_MODEL, tn), lambda j: (0, j)),
            pl.BlockSpec((1, tn), lambda j: (0, j)),
        ],
        out_specs=pl.BlockSpec((rows, tn), lambda j: (0, j)),
        out_shape=jax.ShapeDtypeStruct((rows, n), F32),
        compiler_params=_params("parallel"),
        name="ada",
    )(c, w_ada, b_ada.reshape(1, n))


class _Mod:
    def __init__(self, mod, per_row, tiles_per_seq=1):
        self.per_row = per_row
        self.tiles_per_seq = tiles_per_seq
        self.arr = mod if per_row else mod.reshape(mod.shape[0] * 9, 1, D_MODEL)

    def spec(self, k):
        if self.per_row:
            return pl.BlockSpec((self.arr.shape[0], D_MODEL), lambda i, *_: (0, k))
        tps = self.tiles_per_seq
        return pl.BlockSpec((None, 1, D_MODEL), lambda i, *_: ((i // tps) * 9 + k, 0, 0))


def _gspec(k):
    return pl.BlockSpec((None, 1, D_MODEL), lambda i, *_: (k, 0, 0))


def _ffn_kernel(x_ref, shift_ref, scale_ref, gate_ref, gpre_ref, gpost_ref, wg_ref, wu_ref, wd_ref,
                o_ref, a_scr, acc_scr):
    j = pl.program_id(1)

    @pl.when(j == 0)
    def _():
        a = _rms(x_ref[...], gpre_ref[...]) * (1.0 + scale_ref[...]) + shift_ref[...]
        a_scr[...] = a.astype(BF16)
        acc_scr[...] = jnp.zeros_like(acc_scr)

    a = a_scr[...]
    h = _silu(_dot(a, wg_ref[...])) * _dot(a, wu_ref[...])
    acc_scr[...] += _dot(h.astype(BF16), wd_ref[...])

    @pl.when(j == pl.num_programs(1) - 1)
    def _():
        y = _rms(acc_scr[...], gpost_ref[...])
        o_ref[...] = x_ref[...] + FFN_HALF * gate_ref[...] * y


def _ffn(x, mod, g3, w_gu, w_down, sub, tm):
    m = x.shape[0]
    tf = D_FF // 2
    nf = D_FF // tf
    row = pl.BlockSpec((tm, D_MODEL), lambda i, j: (i, 0))
    return pl.pallas_call(
        _ffn_kernel,
        grid=(m // tm, nf),
        in_specs=[
            row, mod.spec(3 * sub), mod.spec(3 * sub + 1), mod.spec(3 * sub + 2),
            _gspec(2 * sub), _gspec(2 * sub + 1),
            pl.BlockSpec((D_MODEL, tf), lambda i, j: (0, j)),
            pl.BlockSpec((D_MODEL, tf), lambda i, j: (0, j + nf)),
            pl.BlockSpec((tf, D_MODEL), lambda i, j: (j, 0)),
        ],
        out_specs=row,
        out_shape=jax.ShapeDtypeStruct((m, D_MODEL), F32),
        scratch_shapes=[pltpu.VMEM((tm, D_MODEL), BF16), pltpu.VMEM((tm, D_MODEL), F32)],
        compiler_params=_params("parallel", "arbitrary"),
        name=f"ffn{sub}",
    )(x, mod.arr, mod.arr, mod.arr, g3, g3, w_gu, w_gu, w_down)


def _rope(x, cos, sin_lo, sin_hi):
    return x * cos + pltpu.roll(x, LANES - ROT_DIM // 2, 1) * sin_lo + pltpu.roll(x, ROT_DIM // 2, 1) * sin_hi


def _proj_kernel(x_ref, shift_ref, scale_ref, g_ref, cos_ref, slo_ref, shi_ref, w_ref,
                 q_ref, qt_ref, cmp_ref, kvt_ref, kvtb_ref, kvb_ref, gate_ref, gatet_ref):
    a = (_rms(x_ref[...], g_ref[...]) * (1.0 + scale_ref[...]) + shift_ref[...]).astype(BF16)
    z = _dot(a, w_ref[...])
    cos, slo, shi = cos_ref[...], slo_ref[...], shi_ref[...]
    scale = HEAD_DIM ** -0.5 * LOG2_E
    for s in range(Q_COLS // LANES):
        q = _rope(z[:, s * LANES:(s + 1) * LANES], cos, slo, shi) * scale
        q_ref[:, s * LANES:(s + 1) * LANES] = q.astype(BF16)
        qt_ref[s * LANES:(s + 1) * LANES, :] = q.T.astype(BF16)
    for br in range(3):
        c0 = Q_COLS + br * KV_BRANCH_COLS
        k = _rope(z[:, c0:c0 + LANES], cos, slo, shi)
        v = z[:, c0 + LANES:c0 + 2 * LANES]
        for r0, x in ((0, k), (LANES, v)):
            xt = x.T
            kvt_ref[br, r0:r0 + LANES, :] = xt
            kvtb_ref[br, r0:r0 + LANES, :] = xt.astype(BF16)
        if br == 0:
            cmp_ref[:, :LANES] = k
            cmp_ref[:, LANES:] = v
        kvb_ref[:, br * KV_BRANCH_COLS:br * KV_BRANCH_COLS + LANES] = k.astype(BF16)
        kvb_ref[:, br * KV_BRANCH_COLS + LANES:(br + 1) * KV_BRANCH_COLS] = v.astype(BF16)
    gate = jax.nn.sigmoid(z[:, Q_COLS + KV_COLS:])
    gate_ref[...] = gate
    gatet_ref[...] = gate.T


def _proj(x, mod, g3, tables, w_qkvg, tm):
    m = x.shape[0]
    t_rows = tables[0].shape[0]
    nt = t_rows // tm
    n_seq = m // t_rows
    row = lambda n: pl.BlockSpec((tm, n), lambda i: (i, 0))
    tab = pl.BlockSpec((tm, LANES), lambda i: (i % nt, 0))
    col = lambda n: pl.BlockSpec((n, tm), lambda i: (0, i))
    kvt_spec = pl.BlockSpec((3, None, KV_BRANCH_COLS, tm), lambda i: (0, i // nt, 0, i % nt))
    return pl.pallas_call(
        _proj_kernel,
        grid=(m // tm,),
        in_specs=[row(D_MODEL), mod.spec(3), mod.spec(4), _gspec(2), tab, tab, tab,
                  pl.BlockSpec((D_MODEL, QKVG_COLS), lambda i: (0, 0))],
        out_specs=[row(Q_COLS), col(Q_COLS), row(KV_BRANCH_COLS), kvt_spec, kvt_spec,
                   row(KV_COLS), row(GATE_PAD), col(GATE_PAD)],
        out_shape=[
            jax.ShapeDtypeStruct((m, Q_COLS), BF16),
            jax.ShapeDtypeStruct((Q_COLS, m), BF16),
            jax.ShapeDtypeStruct((m, KV_BRANCH_COLS), F32),
            jax.ShapeDtypeStruct((3, n_seq, KV_BRANCH_COLS, t_rows), F32),
            jax.ShapeDtypeStruct((3, n_seq, KV_BRANCH_COLS, t_rows), BF16),
            jax.ShapeDtypeStruct((m, KV_COLS), BF16),
            jax.ShapeDtypeStruct((m, GATE_PAD), F32),
            jax.ShapeDtypeStruct((GATE_PAD, m), F32),
        ],
        compiler_params=_params("parallel"),
        name="proj",
    )(x, mod.arr, mod.arr, g3, *tables, w_qkvg)


def _rope_tables(pos):
    half = ROT_DIM // 2
    inv = jnp.power(ROPE_THETA, -jnp.arange(half, dtype=F32) * 2.0 / ROT_DIM)
    ang = pos.astype(F32)[:, None] * inv[None, :]
    cos, sin = jnp.cos(ang), jnp.sin(ang)
    n = pos.shape[0]
    ones = jnp.ones((n, HEAD_DIM - ROT_DIM), F32)
    zeros = jnp.zeros((n, HEAD_DIM - ROT_DIM), F32)
    zh = jnp.zeros((n, half), F32)
    c = jnp.concatenate([cos, cos, ones], axis=1)
    lo = jnp.concatenate([-sin, zh, zeros], axis=1)
    hi = jnp.concatenate([zh, sin, zeros], axis=1)
    return tuple(jnp.tile(t, (1, LANES // HEAD_DIM)) for t in (c, lo, hi))


def _compress_kernel(x_ref, pe_ref, w1_ref, b1_ref, w2_ref, o_ref, *rest):
    vt_ref = rest[0] if len(rest) == 2 else None
    acc_scr = rest[-1]
    j = pl.program_id(0)
    nt = x_ref.shape[1] // KV_BRANCH_COLS

    @pl.when(j == 0)
    def _():
        acc_scr[...] = jnp.zeros_like(acc_scr)

    for kv in range(2):
        acc = acc_scr[kv]
        for t in range(0, nt, 2):
            xs = []
            for tt in (t, t + 1):
                c0 = tt * KV_BRANCH_COLS + kv * LANES
                xs.append((x_ref[:, c0:c0 + LANES] + pe_ref[kv, tt]).astype(BF16))
            acc = acc + _dot(jnp.concatenate(xs, axis=1), w1_ref[kv, t * LANES:(t + 2) * LANES, :])
        acc_scr[kv] = acc

    @pl.when(j == pl.num_programs(0) - 1)
    def _():
        for kv in range(2):
            h = _silu(acc_scr[kv] + b1_ref[kv])
            c = _dot(h.astype(BF16), w2_ref[kv])
            o_ref[kv] = c.astype(o_ref.dtype)
            if kv == 1 and vt_ref is not None:
                ct = c.T.astype(vt_ref.dtype)
                nb = vt_ref.shape[2]
                for b in range(vt_ref.shape[0]):
                    vt_ref[b] = ct[:, b * nb:(b + 1) * nb]


def _compress_weights(phi_pe, phi_w1, phi_b1, phi_w2):
    eye = jnp.eye(N_KV_HEADS, dtype=F32)
    w1 = phi_w1.reshape(2, NSA_BLOCK, HEAD_DIM, PHI_HIDDEN)
    w1 = jnp.einsum("gh,ktdj->ktgdhj", eye, w1).reshape(2, NSA_BLOCK * LANES, N_KV_HEADS * PHI_HIDDEN)
    w2 = jnp.einsum("gh,kjd->kgjhd", eye, phi_w2).reshape(2, N_KV_HEADS * PHI_HIDDEN, LANES)
    pe = jnp.tile(phi_pe, (1, 1, N_KV_HEADS)).reshape(2, NSA_BLOCK, 1, LANES)
    b1 = jnp.tile(phi_b1, (1, N_KV_HEADS)).reshape(2, 1, N_KV_HEADS * PHI_HIDDEN)
    return pe, w1.astype(BF16), b1, w2.astype(BF16)


def _compress_weights_paged(phi_pe, phi_w1, phi_b1, phi_w2):
    blocks = PAGE_SIZE // NSA_BLOCK
    eye = jnp.eye(blocks, dtype=F32)
    w1 = phi_w1.reshape(2, NSA_BLOCK, HEAD_DIM, PHI_HIDDEN)
    w1 = jnp.einsum("bc,ktdj->kdbtcj", eye, w1).reshape(2, HEAD_DIM * PAGE_SIZE, blocks * PHI_HIDDEN)
    w2 = jnp.einsum("bc,kjd->kbjcd", eye, phi_w2).reshape(2, blocks * PHI_HIDDEN, blocks * HEAD_DIM)
    pe = jnp.tile(phi_pe.transpose(0, 2, 1), (1, 1, blocks)).reshape(2, HEAD_DIM, 1, PAGE_SIZE)
    b1 = jnp.tile(phi_b1, (1, blocks)).reshape(2, 1, blocks * PHI_HIDDEN)
    return pe, w1.astype(BF16), b1, w2.astype(BF16)


def _compress(x_blocks, cw, out_dtype=BF16, blocks_per_seq=None):
    pe, w1, b1, w2 = cw
    rows = x_blocks.shape[0]
    out_specs = [pl.BlockSpec((2, rows, LANES), lambda j: (0, 0, 0))]
    out_shape = [jax.ShapeDtypeStruct((2, rows, LANES), out_dtype)]
    if blocks_per_seq is not None:
        vt_shape = (rows // blocks_per_seq, LANES, blocks_per_seq)
        out_specs.append(pl.BlockSpec(vt_shape, lambda j: (0, 0, 0)))
        out_shape.append(jax.ShapeDtypeStruct(vt_shape, out_dtype))
    tchunk = 8
    nchunk = NSA_BLOCK // tchunk
    hid = N_KV_HEADS * PHI_HIDDEN
    return pl.pallas_call(
        _compress_kernel,
        grid=(nchunk,),
        in_specs=[
            pl.BlockSpec((rows, tchunk * KV_BRANCH_COLS), lambda j: (0, j)),
            pl.BlockSpec((2, tchunk, 1, LANES), lambda j: (0, j, 0, 0)),
            pl.BlockSpec((2, tchunk * LANES, hid), lambda j: (0, j, 0)),
            pl.BlockSpec((2, 1, hid), lambda j: (0, 0, 0)),
            pl.BlockSpec((2, hid, LANES), lambda j: (0, 0, 0)),
        ],
        out_specs=out_specs,
        out_shape=out_shape,
        scratch_shapes=[pltpu.VMEM((2, rows, hid), F32)],
        compiler_params=_params("arbitrary"),
        name="compress",
    )(x_blocks, pe, w1, b1, w2)


def _gather_compress_kernel(pt_ref, *refs, n_pages, group):
    del pt_ref
    pages = refs[:n_pages]
    pe_ref, w1_ref, b1_ref, w2_ref, o_ref, xk_scr, xv_scr = refs[n_pages:]
    x_scr = (xk_scr, xv_scr)
    slot = pl.program_id(0) % group
    for k in range(n_pages):
        for kv in range(2):
            for g in range(N_KV_HEADS):
                r0 = pl.multiple_of(((slot * n_pages + k) * N_KV_HEADS + g) * SLAB_PITCH, SLAB_PITCH)
                s0 = kv * GD_ROWS + g * HEAD_DIM
                x_scr[kv][pl.ds(r0, HEAD_DIM), :] = pages[k][s0:s0 + HEAD_DIM, :]

    @pl.when(slot == group - 1)
    def _():
        rows = group * n_pages * N_KV_HEADS
        for kv in range(2):
            acc = jnp.zeros((rows, w1_ref.shape[2]), F32)
            for d in range(0, HEAD_DIM, 2):
                xs = []
                for dd in (d, d + 1):
                    x = x_scr[kv][pl.ds(dd, rows, stride=SLAB_PITCH), :]
                    xs.append((x + pe_ref[kv, dd]).astype(BF16))
                acc = acc + _dot(jnp.concatenate(xs, axis=1), w1_ref[kv, d * PAGE_SIZE:(d + 2) * PAGE_SIZE, :])
            h = _silu(acc + b1_ref[kv])
            o_ref[kv] = _dot(h.astype(BF16), w2_ref[kv])


def _page_specs(n_pages):
    return [pl.BlockSpec((None, KV_BRANCH_COLS, PAGE_SIZE), lambda s, pt, k=k: (pt[s, k], 0, 0))
            for k in range(n_pages)]


def _gather_compress(cache_t, page_table, cwp):
    pe, w1, b1, w2 = cwp
    n_dec, n_pages = page_table.shape
    group = 8
    rows = group * n_pages * N_KV_HEADS
    blocks = PAGE_SIZE // NSA_BLOCK
    const = lambda a: pl.BlockSpec(a.shape, lambda s, pt: (0,) * a.ndim)
    out = pl.pallas_call(
        functools.partial(_gather_compress_kernel, n_pages=n_pages, group=group),
        grid_spec=pltpu.PrefetchScalarGridSpec(
            num_scalar_prefetch=1,
            grid=(n_dec,),
            in_specs=_page_specs(n_pages) + [const(pe), const(w1), const(b1), const(w2)],
            out_specs=pl.BlockSpec((2, rows, blocks * HEAD_DIM), lambda s, pt: (0, s // group, 0)),
            scratch_shapes=[pltpu.VMEM((group * n_pages * N_KV_HEADS * SLAB_PITCH, PAGE_SIZE), F32)] * 2,
        ),
        out_shape=jax.ShapeDtypeStruct((2, n_dec // group * rows, blocks * HEAD_DIM), F32),
        compiler_params=_params("arbitrary"),
        name="gather_compress",
    )(page_table, *([cache_t] * n_pages), pe, w1, b1, w2)
    out = out.reshape(2, n_dec, n_pages, N_KV_HEADS, blocks, HEAD_DIM).transpose(0, 1, 2, 4, 3, 5)
    return out.reshape(2, n_dec, n_pages * blocks, N_KV_HEADS * HEAD_DIM)


def _group_queries_t(qt, g):
    t = qt.shape[1]
    zeros = jnp.zeros((HEAD_DIM, t), qt.dtype)
    cols = []
    for j in range(HEADS_PER_GROUP):
        qh = qt[j * LANES + g * HEAD_DIM:j * LANES + (g + 1) * HEAD_DIM, :]
        cols.append(jnp.concatenate([qh, zeros] if g == 0 else [zeros, qh], axis=0))
    return jnp.concatenate(cols, axis=1)


def _tile_heads(x):
    return jnp.concatenate([x] * HEADS_PER_GROUP, axis=1)


def _select_blocks_t(imp, nblk, n_sel):
    nb = imp.shape[0]
    rank = jnp.zeros(imp.shape, F32)
    for m in range(nb):
        row = imp[m:m + 1, :]
        beats = jnp.where(row > imp, 1.0, jnp.where(row == imp, jnp.where(m < nblk, 1.0, 0.0), 0.0))
        rank = rank + beats
    return jnp.where(rank < n_sel, jnp.where(imp >= 0.0, 1.0, 0.0), 0.0)


def _select_blocks(imp, nblk, n_sel):
    nb = imp.shape[1]
    rank = jnp.zeros(imp.shape, F32)
    for m in range(nb):
        col = imp[:, m:m + 1]
        beats = jnp.where(col > imp, 1.0, jnp.where(col == imp, jnp.where(m < nblk, 1.0, 0.0), 0.0))
        rank = rank + beats
    return jnp.where(rank < n_sel, jnp.where(imp >= 0.0, 1.0, 0.0), 0.0)


def _attn_tile(carry, k, q, vt_ref, ks, tk, bias):
    m, l, acc = carry
    s = _dot(k, q)
    if bias is not None:
        s = s + bias
    m_new = jnp.maximum(m, jnp.max(s, axis=0, keepdims=True))
    alpha = jnp.exp2(m - m_new)
    p = jnp.exp2(s - m_new)
    l = alpha * l + jnp.sum(p, axis=0, keepdims=True)
    pb = p.astype(BF16)
    half = pb.shape[1] // N_KV_HEADS
    pv = [_dot(vt_ref[GD_ROWS + g * HEAD_DIM:GD_ROWS + (g + 1) * HEAD_DIM, pl.ds(ks, tk)],
               pb[:, g * half:(g + 1) * half]) for g in range(N_KV_HEADS)]
    return m_new, l, alpha * acc + jnp.concatenate(pv, axis=1)


def _maybe(cond, fn, carry):
    return lax.fori_loop(0, jnp.where(cond, 1, 0), lambda _, c: fn(c), carry)


def _attn_kernel(qt_ref, gatet_ref, kc_ref, vct_ref, slc_k_ref, slc_vt_ref, win_k_ref, win_vt_ref, o_ref,
                 *, tq, nb):
    i = pl.program_id(1)
    p0 = i * tq
    tk = tq
    hq = HEADS_PER_GROUP * tq
    n_win_tiles = NSA_WINDOW // tk
    pos = p0 + lax.broadcasted_iota(jnp.int32, (1, tq), 1)
    nblk = lax.broadcasted_iota(jnp.int32, (nb, 1), 0)
    key_off = lax.broadcasted_iota(jnp.int32, (tk, 1), 0)
    qt = qt_ref[...]
    q_all = jnp.concatenate([_group_queries_t(qt, g) for g in range(N_KV_HEADS)], axis=1)
    tile_all = lambda x: jnp.concatenate([x] * N_HEADS, axis=1)

    cvalid = (nblk + 1) * NSA_BLOCK <= tile_all(pos) + 1
    sc = jnp.where(cvalid, _dot(kc_ref[...], q_all), MASK_INIT)
    e = jnp.exp2(sc - jnp.max(sc, axis=0, keepdims=True))
    pc = jnp.where(cvalid, e / jnp.sum(e, axis=0, keepdims=True), 0.0)
    pcb = pc.astype(BF16)
    o_cmp = jnp.concatenate([_dot(vct_ref[g * HEAD_DIM:(g + 1) * HEAD_DIM, :], pcb[:, g * hq:(g + 1) * hq])
                             for g in range(N_KV_HEADS)], axis=1)

    cur = pos // NSA_BLOCK
    forced = HEADS_PER_GROUP + 1.0
    sel_rows = []
    for g in range(N_KV_HEADS):
        imp = functools.reduce(jnp.add, [pc[:, g * hq + j * tq:g * hq + (j + 1) * tq]
                                         for j in range(HEADS_PER_GROUP)])
        imp = jnp.where(nblk == 0, forced, jnp.where(nblk == cur, forced, jnp.where(nblk == cur - 1, forced, imp)))
        imp = jnp.where(nblk * NSA_BLOCK <= pos, imp, -1.0)
        sel = _select_blocks_t(imp, nblk, NSA_TOP_N)
        sel_rows.append(_tile_heads(jnp.where(sel > 0.5, 0.0, MASK_BIAS)))
    sel_rows = jnp.concatenate(sel_rows, axis=1)
    sel_rows = jnp.concatenate([sel_rows, jnp.zeros((LANES - nb, sel_rows.shape[1]), F32)], axis=0)
    q_sel = jnp.concatenate([q_all, sel_rows.astype(BF16)], axis=0)
    blk_lane = lax.broadcasted_iota(jnp.int32, (1, LANES), 1)

    def slc_keys(ks):
        onehot = jnp.where(blk_lane == (ks + key_off) // NSA_BLOCK, 1.0, 0.0).astype(BF16)
        return jnp.concatenate([slc_k_ref[pl.ds(ks, tk), :GD_ROWS], onehot], axis=1)

    cols = q_all.shape[1]
    init = (jnp.full((1, cols), MASK_INIT, F32), jnp.zeros((1, cols), F32), jnp.zeros((HEAD_DIM, cols), F32))
    causal = tile_all(jnp.where(p0 + key_off <= pos, 0.0, MASK_BIAS))
    k_diag = pl.multiple_of(p0, tk)

    def slc_body(j, c):
        ks = pl.multiple_of(j * tk, tk)
        return _attn_tile(c, slc_keys(ks), q_sel, slc_vt_ref, ks, tk, None)

    c = lax.fori_loop(0, i, slc_body, init)
    _, l, acc = _attn_tile(c, slc_keys(k_diag), q_sel, slc_vt_ref, k_diag, tk, causal)
    o_slc = acc / l

    def win_tile(c, ks, bias):
        return _attn_tile(c, win_k_ref[pl.ds(ks, tk), :GD_ROWS], q_all, win_vt_ref, ks, tk, bias)

    def edge_tile(c):
        ks = pl.multiple_of((i - n_win_tiles) * tk, tk)
        bias = tile_all(jnp.where(pos - (ks + key_off) < NSA_WINDOW, 0.0, MASK_BIAS))
        return win_tile(c, ks, bias)

    c = _maybe(i >= n_win_tiles, edge_tile, init)
    c = lax.fori_loop(jnp.maximum(i - n_win_tiles + 1, 0), i,
                      lambda j, c: win_tile(c, pl.multiple_of(j * tk, tk), None), c)
    _, l, acc = win_tile(c, k_diag, causal)
    o_win = acc / l

    gates = gatet_ref[...]
    for g in range(N_KV_HEADS):
        for j in range(HEADS_PER_GROUP):
            head = g * HEADS_PER_GROUP + j
            cs = slice(g * hq + j * tq, g * hq + (j + 1) * tq)
            o = (gates[head:head + 1] * o_cmp[:, cs]
                 + gates[N_HEADS + head:N_HEADS + head + 1] * o_slc[:, cs]
                 + gates[2 * N_HEADS + head:2 * N_HEADS + head + 1] * o_win[:, cs])
            r0 = j * LANES + g * HEAD_DIM
            o_ref[r0:r0 + HEAD_DIM, :] = o.astype(o_ref.dtype)


def _attn_prompt(qt, gatest, kc, vct, kvb, kvtb, n_seq, seq, tq):
    m = qt.shape[1]
    nb = seq // NSA_BLOCK
    nq = seq // tq
    assert seq % tq == 0 and NSA_WINDOW % tq == 0 and tq % NSA_BLOCK == 0 and nb <= LANES
    col = lambda n: pl.BlockSpec((n, tq), lambda b, i: (0, b * nq + i))
    return pl.pallas_call(
        functools.partial(_attn_kernel, tq=tq, nb=nb),
        grid=(n_seq, nq),
        in_specs=[col(Q_COLS), col(4 * N_HEADS),
                  pl.BlockSpec((nb, LANES), lambda b, i: (b, 0)),
                  pl.BlockSpec((None, LANES, nb), lambda b, i: (b, 0, 0)),
                  pl.BlockSpec((seq, KV_BRANCH_COLS), lambda b, i: (b, 1)),
                  pl.BlockSpec((None, None, KV_BRANCH_COLS, seq), lambda b, i: (1, b, 0, 0)),
                  pl.BlockSpec((seq, KV_BRANCH_COLS), lambda b, i: (b, 2)),
                  pl.BlockSpec((None, None, KV_BRANCH_COLS, seq), lambda b, i: (2, b, 0, 0))],
        out_specs=col(Q_COLS),
        out_shape=jax.ShapeDtypeStruct((Q_COLS, m), BF16),
        compiler_params=_params("parallel", "arbitrary"),
        name="attn_prompt",
    )(qt, gatest, kc, vct, kvb, kvtb, kvb, kvtb)


def _softmax_pv(score_tiles, value_tiles):
    m = functools.reduce(jnp.maximum, [jnp.max(s, axis=-1, keepdims=True) for s in score_tiles])
    ps = [jnp.exp2(s - m) for s in score_tiles]
    l = functools.reduce(jnp.add, [jnp.sum(p, axis=-1, keepdims=True) for p in ps])
    o = functools.reduce(jnp.add, [(_dot_nt if fm else _dot)(p.astype(BF16), v)
                                   for p, (v, fm) in zip(ps, value_tiles)])
    return o / l


def _attn_decode_kernel(pt_ref, *refs, n_pages, pos, nbp):
    del pt_ref
    pages = refs[:n_pages]
    q_ref, kvb_ref, gate_ref, kvc_ref, win_ref, o_ref = refs[n_pages:]
    tk = PAGE_SIZE
    lane = lax.broadcasted_iota(jnp.int32, (1, LANES), 1)
    head = lax.broadcasted_iota(jnp.int32, (N_HEADS, 1), 0)
    low_group = head < HEADS_PER_GROUP
    low_lanes = lane < HEAD_DIM

    q32 = q_ref[...].astype(F32)
    qm = jnp.zeros((N_HEADS, LANES), F32)
    for j in range(HEADS_PER_GROUP):
        qm = jnp.where(head % HEADS_PER_GROUP == j, q32[:, j * LANES:(j + 1) * LANES], qm)
    qm = jnp.where(low_group, jnp.where(low_lanes, qm, 0.0), jnp.where(low_lanes, 0.0, qm)).astype(BF16)

    gate_row = gate_ref[...]
    gate_col = [jnp.sum(jnp.where(lane == head + br * N_HEADS, gate_row, 0.0), axis=-1, keepdims=True)
                for br in range(3)]
    kv_new = kvb_ref[...].astype(F32)
    first_row = lax.broadcasted_iota(jnp.int32, (tk, 1), 0) == 0

    def new_key_tile(c0):
        return jnp.where(first_row, kv_new[:, c0:c0 + LANES], 0.0).astype(BF16)

    nblk = lax.broadcasted_iota(jnp.int32, (1, nbp), 1)
    cvalid = (nblk + 1) * NSA_BLOCK <= pos + 1
    sc = jnp.where(cvalid, _dot_nt(qm, kvc_ref[0].astype(BF16)), MASK_INIT)
    e = jnp.exp2(sc - jnp.max(sc, axis=-1, keepdims=True))
    pc = jnp.where(cvalid, e / jnp.sum(e, axis=-1, keepdims=True), 0.0)
    o_cmp = _dot(pc.astype(BF16), kvc_ref[1].astype(BF16))

    imp = jnp.where(low_group,
                    jnp.sum(jnp.where(low_group, pc, 0.0), axis=0, keepdims=True),
                    jnp.sum(jnp.where(low_group, 0.0, pc), axis=0, keepdims=True))
    cur = pos // NSA_BLOCK
    forced = HEADS_PER_GROUP + 1.0
    imp = jnp.where(nblk == 0, forced, jnp.where(nblk == cur, forced, jnp.where(nblk == cur - 1, forced, imp)))
    imp = jnp.where(nblk * NSA_BLOCK <= pos, imp, -1.0)
    sel = _select_blocks(imp, nblk, NSA_TOP_N)

    def sel_bias(n):
        return jnp.where(sel[:, n:n + 1] > 0.5, 0.0, MASK_BIAS)

    s_tiles, v_tiles = [], []
    blocks_per_page = tk // NSA_BLOCK
    for k in range(n_pages):
        page = pages[k]
        s = _dot(qm, page[:GD_ROWS, :].astype(BF16))
        bias = sel_bias(k * blocks_per_page + blocks_per_page - 1)
        for b in range(blocks_per_page - 2, -1, -1):
            bias = jnp.where(lane < (b + 1) * NSA_BLOCK, sel_bias(k * blocks_per_page + b), bias)
        s_tiles.append(s + bias)
        v_tiles.append((page[GD_ROWS:, :].astype(BF16), True))
    s_new = _dot_nt(qm, new_key_tile(KV_BRANCH_COLS))
    s_tiles.append(s_new + jnp.where(lane == 0, sel_bias(n_pages * blocks_per_page), MASK_BIAS))
    v_tiles.append((new_key_tile(KV_BRANCH_COLS + LANES), False))
    o_slc = _softmax_pv(s_tiles, v_tiles)

    n_win = win_ref.shape[1]
    s_tiles, v_tiles = [], []
    for k in range(n_win // tk):
        diff = n_win - (k * tk + lane)
        bias = jnp.where(diff < NSA_WINDOW, 0.0, MASK_BIAS)
        s_tiles.append(_dot(qm, win_ref[:GD_ROWS, k * tk:(k + 1) * tk].astype(BF16)) + bias)
        v_tiles.append((win_ref[GD_ROWS:, k * tk:(k + 1) * tk].astype(BF16), True))
    s_tiles.append(_dot_nt(qm, new_key_tile(2 * KV_BRANCH_COLS)) + jnp.where(lane == 0, 0.0, MASK_BIAS))
    v_tiles.append((new_key_tile(2 * KV_BRANCH_COLS + LANES), False))
    o_win = _softmax_pv(s_tiles, v_tiles)

    o = gate_col[0] * o_cmp + gate_col[1] * o_slc + gate_col[2] * o_win
    for j in range(HEADS_PER_GROUP):
        pair = jnp.where(low_lanes, o[j:j + 1], o[j + HEADS_PER_GROUP:j + HEADS_PER_GROUP + 1])
        o_ref[:, j * LANES:(j + 1) * LANES] = pair.astype(o_ref.dtype)


def _attn_decode(q, kvb, gates, kvc, cache_slc, win_state, page_table, pos):
    n_dec, n_pages = page_table.shape
    nbp = kvc.shape[2]
    n_win = win_state.shape[2]
    per_seq = lambda a: pl.BlockSpec((None, 1, a.shape[-1]), lambda s, pt: (s, 0, 0))
    q3, kvb3, g3 = (a.reshape(n_dec, 1, a.shape[-1]) for a in (q, kvb, gates))
    out = pl.pallas_call(
        functools.partial(_attn_decode_kernel, n_pages=n_pages, pos=pos, nbp=nbp),
        grid_spec=pltpu.PrefetchScalarGridSpec(
            num_scalar_prefetch=1,
            grid=(n_dec,),
            in_specs=_page_specs(n_pages) + [
                per_seq(q3), per_seq(kvb3), per_seq(g3),
                pl.BlockSpec((2, None, nbp, LANES), lambda s, pt: (0, s, 0, 0)),
                pl.BlockSpec((None, KV_BRANCH_COLS, n_win), lambda s, pt: (s, 0, 0)),
            ],
            out_specs=pl.BlockSpec((None, 1, Q_COLS), lambda s, pt: (s, 0, 0)),
        ),
        out_shape=jax.ShapeDtypeStruct((n_dec, 1, Q_COLS), BF16),
        compiler_params=_params("arbitrary"),
        name="attn_decode",
    )(page_table, *([cache_slc] * n_pages), q3, kvb3, g3, kvc, win_state)
    return out.reshape(n_dec, Q_COLS)


def _mix_kernel(*refs, tm, halo, tiles_per_seq):
    if halo:
        (x_ref, xh_ref, o_ref, shift_ref, scale_ref, gate_ref, gpre_ref, gpost_ref, cw_ref,
         wconv_ref, wmerge_ref, wup_ref, wco_ref, wout_ref, y_ref, ulast_ref, u_scr) = refs
        x = x_ref[...]
        xe = jnp.concatenate([xh_ref[...], x], axis=0)
    else:
        (x_ref, um1_ref, um2_ref, o_ref, shift_ref, scale_ref, gate_ref, gpre_ref, gpost_ref, cw_ref,
         wconv_ref, wmerge_ref, wup_ref, wco_ref, wout_ref, y_ref, ulast_ref) = refs
        x = x_ref[...]
        xe = x
    h0 = xe.shape[0] - tm
    a = (_rms(xe, gpre_ref[...]) * (1.0 + scale_ref[...]) + shift_ref[...]).astype(BF16)
    zc = _dot(a, wconv_ref[...])
    u = zc[:, 2 * D_CONV:] * zc[:, :D_CONV]
    cb = zc[h0:, D_CONV:2 * D_CONV]
    if halo:
        keep_halo = jnp.where(pl.program_id(0) % tiles_per_seq == 0, 0.0, 1.0)
        rows = lax.broadcasted_iota(jnp.int32, (xe.shape[0], 1), 0)
        u = jnp.where(rows < h0, u * keep_halo, u)
        u_scr[...] = u
        um1 = u_scr[h0 - 1:h0 - 1 + tm, :]
        um2 = u_scr[h0 - 2:h0 - 2 + tm, :]
        u0 = u[h0:]
        ulast_ref[...] = u[tm:]
    else:
        um1, um2, u0 = um1_ref[...], um2_ref[...], u
        ulast_ref[...] = u
    cw = cw_ref[...]
    y = cw[0:1] * um2 + cw[1:2] * um1 + cw[2:3] * u0
    conv_out = _dot((cb * y).astype(BF16), wco_ref[...])
    attn_out = _dot_tn(o_ref[...], wup_ref[...])
    mg = jax.nn.sigmoid(_dot(a[h0:], wmerge_ref[...]))
    merged = mg[:, :D_MODEL] * attn_out + mg[:, D_MODEL:] * conv_out
    mixed = _dot(merged.astype(BF16), wout_ref[...])
    y_ref[...] = x + gate_ref[...] * _rms(mixed, gpost_ref[...])


def _mix(x, o_att, mod, g3, conv_w, weights, tm, tiles_per_seq=None, prev=None):
    m = x.shape[0]
    halo = prev is None
    h0 = MIX_HALO
    row = lambda n: pl.BlockSpec((tm, n), lambda i: (i, 0))
    full = lambda a: pl.BlockSpec(a.shape, lambda i: (0,) * a.ndim)
    att = pl.BlockSpec((Q_COLS, tm), lambda i: (0, i))
    common = [att, mod.spec(3), mod.spec(4), mod.spec(5), _gspec(2), _gspec(3), full(conv_w)]
    common += [full(w) for w in weights]
    common_args = [o_att, mod.arr, mod.arr, mod.arr, g3, g3, conv_w, *weights]
    if halo:
        hb = tm // h0
        in_specs = [row(D_MODEL), pl.BlockSpec((h0, D_MODEL), lambda i: (jnp.maximum(i * hb - 1, 0), 0))] + common
        args = [x, x] + common_args
        scratch = [pltpu.VMEM((tm + h0, D_CONV), F32)]
        ulast = (jax.ShapeDtypeStruct((m // tm * h0, D_CONV), F32), pl.BlockSpec((h0, D_CONV), lambda i: (i, 0)))
    else:
        in_specs = [row(D_MODEL), row(D_CONV), row(D_CONV)] + common
        args = [x, prev[0], prev[1]] + common_args
        scratch = []
        ulast = (jax.ShapeDtypeStruct((m, D_CONV), F32), row(D_CONV))
    return pl.pallas_call(
        functools.partial(_mix_kernel, tm=tm, halo=halo, tiles_per_seq=tiles_per_seq),
        grid=(m // tm,),
        in_specs=in_specs,
        out_specs=[row(D_MODEL), ulast[1]],
        out_shape=[jax.ShapeDtypeStruct((m, D_MODEL), F32), ulast[0]],
        scratch_shapes=scratch,
        compiler_params=_params("parallel"),
        name="mix",
    )(*args)


def _head_pair_perm():
    order = []
    for j in range(HEADS_PER_GROUP):
        for g in range(N_KV_HEADS):
            head = g * HEADS_PER_GROUP + j
            order.extend(range(head * HEAD_DIM, (head + 1) * HEAD_DIM))
    return jnp.array(order, dtype=jnp.int32)


def _layer_weights(w_in, w_attn_up, w_conv_out, w_out):
    perm = _head_pair_perm()
    o1 = Q_COLS
    o2 = o1 + KV_COLS
    o3 = o2 + NSA_GATE_COLS
    o4 = o3 + CONV_COLS
    w_q = w_in[:, :o1][:, perm]
    w_g = jnp.pad(w_in[:, o2:o3], ((0, 0), (0, GATE_PAD - NSA_GATE_COLS)))
    w_qkvg = jnp.concatenate([w_q, w_in[:, o1:o2], w_g], axis=1).astype(BF16)
    mix_w = (w_in[:, o3:o4].astype(BF16), w_in[:, o4:].astype(BF16), w_attn_up[perm].astype(BF16),
             w_conv_out.astype(BF16), w_out.astype(BF16))
    return w_qkvg, mix_w


def kernel(x_prompt, x_sample, c_prompt, c_sample, cache_cmp_kv, cache_slc_kv, state_win_kv, state_conv,
           page_table, w_ada, b_ada, g_norm, w_ffn_gu, w_ffn_down, w_in, phi_pe, phi_w1, phi_b1, phi_w2,
           w_attn_up, conv_w, w_conv_out, w_out):
    n_seq, seq, _ = x_prompt.shape
    n_dec = x_sample.shape[0]
    assert w_ada.shape[0] == 1
    l = 0

    mod_all = _ada(jnp.concatenate([c_prompt, c_sample], axis=0), w_ada[l], b_ada[l])
    g3 = g_norm[l].reshape(6, 1, D_MODEL)
    w_gu = w_ffn_gu[l].astype(BF16)
    w_down = w_ffn_down[l].astype(BF16)
    w_qkvg, mix_w = _layer_weights(w_in[l], w_attn_up[l], w_conv_out[l], w_out[l])
    cw = _compress_weights(phi_pe[l], phi_w1[l], phi_b1[l], phi_w2[l])

    tm = 512
    tq = 256
    m = n_seq * seq
    mod_p = _Mod(mod_all[:n_seq], per_row=False, tiles_per_seq=seq // tm)
    x0 = x_prompt.reshape(m, D_MODEL)
    x1 = _ffn(x0, mod_p, g3, w_gu[0], w_down[0], 0, tm)
    tables = _rope_tables(jnp.arange(seq, dtype=jnp.int32))
    _, qt, cmp_new, kvt, kvtb, kvb, _, gatest = _proj(x1, mod_p, g3, tables, w_qkvg, tm)
    nb = seq // NSA_BLOCK
    kvc, vct = _compress(cmp_new.reshape(n_seq * nb, NSA_BLOCK * KV_BRANCH_COLS), cw, blocks_per_seq=nb)
    o_att = _attn_prompt(qt, gatest, kvc[0], vct, kvb, kvtb, n_seq, seq, tq)
    x2, ulast = _mix(x1, o_att, mod_p, g3, conv_w[l], mix_w, tm, tiles_per_seq=seq // tm)
    y_prompt = _ffn(x2, mod_p, g3, w_gu[1], w_down[1], 2, tm).reshape(n_seq, seq, D_MODEL)

    def token_major(t):
        n, _, tt = t.shape
        return t.reshape(n, 2, N_KV_HEADS, HEAD_DIM, tt).transpose(0, 4, 1, 2, 3)[None]

    def feature_major(a):
        n, tt = a.shape[:2]
        return a.transpose(0, 2, 3, 4, 1).reshape(n, KV_BRANCH_COLS, tt)

    cmp_kv_prompt = token_major(kvt[0])
    slc_kv_prompt = token_major(kvt[1])
    win_keep = min(NSA_WINDOW, seq)
    win_kv_prompt = token_major(kvt[2][:, :, seq - win_keep:])
    conv_prompt = ulast.reshape(n_seq, seq // tm, MIX_HALO, D_CONV)[:, -1, MIX_HALO - (CONV_WIDTH - 1):][None]

    n_pages = page_table.shape[1]
    past_len = n_pages * PAGE_SIZE
    pos_s = past_len + jnp.arange(x_sample.shape[1], dtype=jnp.int32)
    assert x_sample.shape[1] == 1 and past_len % NSA_BLOCK == 0 and state_win_kv.shape[2] <= past_len
    mod_s = _Mod(mod_all[n_seq:], per_row=True)
    xs1 = _ffn(x_sample.reshape(n_dec, D_MODEL), mod_s, g3, w_gu[0], w_down[0], 0, n_dec)
    tables_s = _rope_tables(jnp.broadcast_to(pos_s, (n_dec,)))
    q_s, _, cmp_s, kvt_s, _, kvb_s, gates_s, _ = _proj(xs1, mod_s, g3, tables_s, w_qkvg, n_dec)
    nb_past = past_len // NSA_BLOCK
    nb_pad = -(-(nb_past + 1) // NSA_BLOCK) * NSA_BLOCK
    cwp = _compress_weights_paged(phi_pe[l], phi_w1[l], phi_b1[l], phi_w2[l])
    kvc_past = _gather_compress(feature_major(cache_cmp_kv[l]), page_table, cwp)
    new_block = jnp.pad(cmp_s[:, None, :], ((0, 0), (0, NSA_BLOCK - 1), (0, 0)))
    kvc_new = _compress(new_block.reshape(n_dec, NSA_BLOCK * KV_BRANCH_COLS), cw, F32)[0][:, :, None, :]
    kvc_s = jnp.concatenate(
        [kvc_past, kvc_new, jnp.zeros((2, n_dec, nb_pad - nb_past - 1, LANES), F32)], axis=2)
    win_state = feature_major(state_win_kv[l])
    o_att_s = _attn_decode(q_s, kvb_s, gates_s, kvc_s, feature_major(cache_slc_kv[l]), win_state, page_table,
                           past_len)
    conv_state = state_conv[l]
    xs2, u_s = _mix(xs1, o_att_s.T, mod_s, g3, conv_w[l], mix_w, n_dec,
                    prev=(conv_state[:, CONV_WIDTH - 2], conv_state[:, CONV_WIDTH - 3]))
    y_sample = _ffn(xs2, mod_s, g3, w_gu[1], w_down[1], 2, n_dec).reshape(x_sample.shape)

    kvs = (1, n_dec, 1, 2, N_KV_HEADS, HEAD_DIM)
    win_keep_s = min(NSA_WINDOW, past_len + 1)
    win_new_s = kvt_s[2, 0].T[:, :, None]
    win_kv_sample = token_major(jnp.concatenate([win_state, win_new_s], axis=2)[:, :, -win_keep_s:])
    conv_sample = jnp.concatenate([conv_state[:, 1:], u_s[:, None, :]], axis=1)[None]
    return (y_prompt, y_sample, cmp_kv_prompt, slc_kv_prompt, win_kv_prompt, conv_prompt,
            token_major(kvt_s[0]).reshape(kvs), token_major(kvt_s[1]).reshape(kvs), win_kv_sample, conv_sample)
```

```python
import functools

import jax
import jax.numpy as jnp
from jax import lax
from jax.experimental import pallas as pl
from jax.experimental.pallas import tpu as pltpu

D_MODEL = 1024
N_HEADS = 8
HEAD_DIM = 64
N_KV_HEADS = 2
HEADS_PER_GROUP = N_HEADS // N_KV_HEADS
ROT_DIM = HEAD_DIM // 4
ROPE_THETA = 500000.0
NSA_BLOCK = 64
NSA_TOP_N = 16
NSA_WINDOW = 512
PHI_HIDDEN = 2 * HEAD_DIM
D_CONV = 512
CONV_WIDTH = 3
D_FF = 2816
FFN_HALF = 0.5
NORM_EPS = 1e-6
PAGE_SIZE = 128
Q_COLS = N_HEADS * HEAD_DIM
KV_BRANCH_COLS = 2 * N_KV_HEADS * HEAD_DIM
KV_COLS = 3 * KV_BRANCH_COLS
NSA_GATE_COLS = 3 * N_HEADS
CONV_COLS = 3 * D_CONV
MERGE_COLS = 2 * D_MODEL
GD_ROWS = N_KV_HEADS * HEAD_DIM

LANES = 128
GATE_PAD = LANES
QKVG_COLS = Q_COLS + KV_COLS + GATE_PAD
LOG2_E = 1.4426950408889634
MASK_INIT = -1e30
MASK_BIAS = -2e30
VMEM_LIMIT = 56 * 1024 * 1024
SLAB_PITCH = HEAD_DIM + 8
DECODE_SEQS_PER_STEP = 4
MIX_HALO = 16

BF16 = jnp.bfloat16
F32 = jnp.float32


def _dot(a, b):
    return jnp.dot(a, b, preferred_element_type=F32)


def _dot_tn(a, b):
    return lax.dot_general(a, b, (((0,), (0,)), ((), ())), preferred_element_type=F32)


def _dot_nt(a, b):
    return lax.dot_general(a, b, (((1,), (1,)), ((), ())), preferred_element_type=F32)


def _rms(x, g):
    return x * lax.rsqrt(jnp.mean(x * x, axis=-1, keepdims=True) + NORM_EPS) * g


def _silu(x):
    return x * jax.nn.sigmoid(x)


def _params(*sem):
    return pltpu.CompilerParams(dimension_semantics=sem, vmem_limit_bytes=VMEM_LIMIT)


def _ada_kernel(c_ref, w_ref, b_ref, o_ref):
    c = _silu(c_ref[...]).astype(BF16)
    o_ref[...] = _dot(c, w_ref[...].astype(BF16)) + b_ref[...]


def _ada(c, w_ada, b_ada):
    rows = c.shape[0]
    n = w_ada.shape[1]
    tn = 9 * LANES
    return pl.pallas_call(
        _ada_kernel,
        grid=(n // tn,),
        in_specs=[
            pl.BlockSpec((rows, D_MODEL), lambda j: (0, 0)),
            pl.BlockSpec((D_MODEL, tn), lambda j: (0, j)),
            pl.BlockSpec((1, tn), lambda j: (0, j)),
        ],
        out_specs=pl.BlockSpec((rows, tn), lambda j: (0, j)),
        out_shape=jax.ShapeDtypeStruct((rows, n), F32),
        compiler_params=_params("parallel"),
        name="ada",
    )(c, w_ada, b_ada.reshape(1, n))


class _Mod:
    def __init__(self, mod, per_row, tiles_per_seq=1):
        self.per_row = per_row
        self.tiles_per_seq = tiles_per_seq
        self.arr = mod if per_row else mod.reshape(mod.shape[0] * 9, 1, D_MODEL)

    def spec(self, k):
        if self.per_row:
            return pl.BlockSpec((self.arr.shape[0], D_MODEL), lambda i, *_: (0, k))
        tps = self.tiles_per_seq
        return pl.BlockSpec((None, 1, D_MODEL), lambda i, *_: ((i // tps) * 9 + k, 0, 0))


def _gspec(k):
    return pl.BlockSpec((None, 1, D_MODEL), lambda i, *_: (k, 0, 0))


def _ffn_kernel(x_ref, shift_ref, scale_ref, gate_ref, gpre_ref, gpost_ref, wg_ref, wu_ref, wd_ref,
                o_ref, a_scr, acc_scr):
    j = pl.program_id(1)

    @pl.when(j == 0)
    def _():
        a = _rms(x_ref[...], gpre_ref[...]) * (1.0 + scale_ref[...]) + shift_ref[...]
        a_scr[...] = a.astype(BF16)
        acc_scr[...] = jnp.zeros_like(acc_scr)

    a = a_scr[...]
    h = _silu(_dot(a, wg_ref[...])) * _dot(a, wu_ref[...])
    acc_scr[...] += _dot(h.astype(BF16), wd_ref[...])

    @pl.when(j == pl.num_programs(1) - 1)
    def _():
        y = _rms(acc_scr[...], gpost_ref[...])
        o_ref[...] = x_ref[...] + FFN_HALF * gate_ref[...] * y


def _ffn(x, mod, g3, w_gu, w_down, sub, tm):
    m = x.shape[0]
    tf = D_FF // 2
    nf = D_FF // tf
    row = pl.BlockSpec((tm, D_MODEL), lambda i, j: (i, 0))
    return pl.pallas_call(
        _ffn_kernel,
        grid=(m // tm, nf),
        in_specs=[
            row, mod.spec(3 * sub), mod.spec(3 * sub + 1), mod.spec(3 * sub + 2),
            _gspec(2 * sub), _gspec(2 * sub + 1),
            pl.BlockSpec((D_MODEL, tf), lambda i, j: (0, j)),
            pl.BlockSpec((D_MODEL, tf), lambda i, j: (0, j + nf)),
            pl.BlockSpec((tf, D_MODEL), lambda i, j: (j, 0)),
        ],
        out_specs=row,
        out_shape=jax.ShapeDtypeStruct((m, D_MODEL), F32),
        scratch_shapes=[pltpu.VMEM((tm, D_MODEL), BF16), pltpu.VMEM((tm, D_MODEL), F32)],
        compiler_params=_params("parallel", "arbitrary"),
        name=f"ffn{sub}",
    )(x, mod.arr, mod.arr, mod.arr, g3, g3, w_gu, w_gu, w_down)


def _rope(x, cos, sin_lo, sin_hi):
    return x * cos + pltpu.roll(x, LANES - ROT_DIM // 2, 1) * sin_lo + pltpu.roll(x, ROT_DIM // 2, 1) * sin_hi


def _proj_kernel(x_ref, shift_ref, scale_ref, g_ref, cos_ref, slo_ref, shi_ref, w_ref,
                 q_ref, qt_ref, cmp_ref, kvt_ref, kvtb_ref, kvb_ref, gate_ref, gatet_ref):
    a = (_rms(x_ref[...], g_ref[...]) * (1.0 + scale_ref[...]) + shift_ref[...]).astype(BF16)
    z = _dot(a, w_ref[...])
    cos, slo, shi = cos_ref[...], slo_ref[...], shi_ref[...]
    scale = HEAD_DIM ** -0.5 * LOG2_E
    for s in range(Q_COLS // LANES):
        q = _rope(z[:, s * LANES:(s + 1) * LANES], cos, slo, shi) * scale
        q_ref[:, s * LANES:(s + 1) * LANES] = q.astype(BF16)
        qt_ref[s * LANES:(s + 1) * LANES, :] = q.T.astype(BF16)
    for br in range(3):
        c0 = Q_COLS + br * KV_BRANCH_COLS
        k = _rope(z[:, c0:c0 + LANES], cos, slo, shi)
        v = z[:, c0 + LANES:c0 + 2 * LANES]
        for r0, x in ((0, k), (LANES, v)):
            xt = x.T
            kvt_ref[br, r0:r0 + LANES, :] = xt
            kvtb_ref[br, r0:r0 + LANES, :] = xt.astype(BF16)
            if br == 0:
                for p in range(cmp_ref.shape[0]):
                    cmp_ref[p, r0:r0 + LANES, :] = xt[:, p * PAGE_SIZE:(p + 1) * PAGE_SIZE]
        kvb_ref[:, br * KV_BRANCH_COLS:br * KV_BRANCH_COLS + LANES] = k.astype(BF16)
        kvb_ref[:, br * KV_BRANCH_COLS + LANES:(br + 1) * KV_BRANCH_COLS] = v.astype(BF16)
    gate = jax.nn.sigmoid(z[:, Q_COLS + KV_COLS:])
    gate_ref[...] = gate
    gatet_ref[...] = gate.T


def _proj(x, mod, g3, tables, w_qkvg, tm):
    assert tm % PAGE_SIZE == 0
    m = x.shape[0]
    t_rows = tables[0].shape[0]
    nt = t_rows // tm
    n_seq = m // t_rows
    row = lambda n: pl.BlockSpec((tm, n), lambda i: (i, 0))
    tab = pl.BlockSpec((tm, LANES), lambda i: (i % nt, 0))
    col = lambda n: pl.BlockSpec((n, tm), lambda i: (0, i))
    kvt_spec = pl.BlockSpec((3, None, KV_BRANCH_COLS, tm), lambda i: (0, i // nt, 0, i % nt))
    return pl.pallas_call(
        _proj_kernel,
        grid=(m // tm,),
        in_specs=[row(D_MODEL), mod.spec(3), mod.spec(4), _gspec(2), tab, tab, tab,
                  pl.BlockSpec((D_MODEL, QKVG_COLS), lambda i: (0, 0))],
        out_specs=[row(Q_COLS), col(Q_COLS),
                   pl.BlockSpec((tm // PAGE_SIZE, KV_BRANCH_COLS, PAGE_SIZE), lambda i: (i, 0, 0)),
                   kvt_spec, kvt_spec, row(KV_COLS), row(GATE_PAD), col(GATE_PAD)],
        out_shape=[
            jax.ShapeDtypeStruct((m, Q_COLS), BF16),
            jax.ShapeDtypeStruct((Q_COLS, m), BF16),
            jax.ShapeDtypeStruct((m // PAGE_SIZE, KV_BRANCH_COLS, PAGE_SIZE), F32),
            jax.ShapeDtypeStruct((3, n_seq, KV_BRANCH_COLS, t_rows), F32),
            jax.ShapeDtypeStruct((3, n_seq, KV_BRANCH_COLS, t_rows), BF16),
            jax.ShapeDtypeStruct((m, KV_COLS), BF16),
            jax.ShapeDtypeStruct((m, GATE_PAD), F32),
            jax.ShapeDtypeStruct((GATE_PAD, m), F32),
        ],
        compiler_params=_params("parallel"),
        name="proj",
    )(x, mod.arr, mod.arr, g3, *tables, w_qkvg)


def _rope_tables(pos):
    half = ROT_DIM // 2
    inv = jnp.power(ROPE_THETA, -jnp.arange(half, dtype=F32) * 2.0 / ROT_DIM)
    ang = pos.astype(F32)[:, None] * inv[None, :]
    cos, sin = jnp.cos(ang), jnp.sin(ang)
    n = pos.shape[0]
    ones = jnp.ones((n, HEAD_DIM - ROT_DIM), F32)
    zeros = jnp.zeros((n, HEAD_DIM - ROT_DIM), F32)
    zh = jnp.zeros((n, half), F32)
    c = jnp.concatenate([cos, cos, ones], axis=1)
    lo = jnp.concatenate([-sin, zh, zeros], axis=1)
    hi = jnp.concatenate([zh, sin, zeros], axis=1)
    return tuple(jnp.tile(t, (1, LANES // HEAD_DIM)) for t in (c, lo, hi))


def _compress_weights_paged(phi_pe, phi_w1, phi_b1, phi_w2):
    blocks = PAGE_SIZE // NSA_BLOCK
    eye = jnp.eye(blocks, dtype=F32)
    w1 = phi_w1.reshape(2, NSA_BLOCK, HEAD_DIM, PHI_HIDDEN)
    w1 = jnp.einsum("bc,ktdj->kdbtcj", eye, w1).reshape(2, HEAD_DIM * PAGE_SIZE, blocks * PHI_HIDDEN)
    w2 = jnp.einsum("bc,kjd->kbjcd", eye, phi_w2).reshape(2, blocks * PHI_HIDDEN, blocks * HEAD_DIM)
    pe = jnp.tile(phi_pe.transpose(0, 2, 1), (1, 1, blocks)).reshape(2, HEAD_DIM, 1, PAGE_SIZE)
    b1 = jnp.tile(phi_b1, (1, blocks)).reshape(2, 1, blocks * PHI_HIDDEN)
    return pe, w1.astype(BF16), b1, w2.astype(BF16)


def _gather_compress_kernel(pt_ref, *refs, n_pages, group, contiguous):
    del pt_ref
    if contiguous:
        pages = [refs[0].at[k] for k in range(n_pages)]
        refs = refs[1:]
    else:
        pages, refs = refs[:n_pages], refs[n_pages:]
    pe_ref, w1_ref, b1_ref, w2_ref, o_ref, xk_scr, xv_scr = refs
    x_scr = (xk_scr, xv_scr)
    slot = pl.program_id(0) % group
    for k in range(n_pages):
        for kv in range(2):
            for g in range(N_KV_HEADS):
                r0 = pl.multiple_of(((slot * n_pages + k) * N_KV_HEADS + g) * SLAB_PITCH, SLAB_PITCH)
                s0 = kv * GD_ROWS + g * HEAD_DIM
                x_scr[kv][pl.ds(r0, HEAD_DIM), :] = pages[k][s0:s0 + HEAD_DIM, :]

    @pl.when(slot == group - 1)
    def _():
        rows = group * n_pages * N_KV_HEADS
        for kv in range(2):
            acc = jnp.zeros((rows, w1_ref.shape[2]), F32)
            for d in range(0, HEAD_DIM, 2):
                xs = []
                for dd in (d, d + 1):
                    x = x_scr[kv][pl.ds(dd, rows, stride=SLAB_PITCH), :]
                    xs.append((x + pe_ref[kv, dd]).astype(BF16))
                acc = acc + _dot(jnp.concatenate(xs, axis=1), w1_ref[kv, d * PAGE_SIZE:(d + 2) * PAGE_SIZE, :])
            h = _silu(acc + b1_ref[kv])
            o_ref[kv] = _dot(h.astype(BF16), w2_ref[kv])


def _page_specs(n_pages, seqs=1):
    return [pl.BlockSpec((None, KV_BRANCH_COLS, PAGE_SIZE), lambda s, pt, r=r, k=k: (pt[s * seqs + r, k], 0, 0))
            for r in range(seqs) for k in range(n_pages)]


def _gather_compress(cache_t, page_table, cwp, pages_per_seq=None):
    pe, w1, b1, w2 = cwp
    contiguous = page_table is None
    if contiguous:
        n_pages = pages_per_seq
        n_dec = cache_t.shape[0] // n_pages
        page_table = jnp.zeros((1, 1), jnp.int32)
        page_specs = [pl.BlockSpec((n_pages, KV_BRANCH_COLS, PAGE_SIZE), lambda s, pt: (s, 0, 0))]
        page_args = [cache_t]
    else:
        n_dec, n_pages = page_table.shape
        page_specs = _page_specs(n_pages)
        page_args = [cache_t] * n_pages
    group = 8
    assert n_dec % group == 0
    rows = group * n_pages * N_KV_HEADS
    blocks = PAGE_SIZE // NSA_BLOCK
    const = lambda a: pl.BlockSpec(a.shape, lambda s, pt: (0,) * a.ndim)
    out = pl.pallas_call(
        functools.partial(_gather_compress_kernel, n_pages=n_pages, group=group, contiguous=contiguous),
        grid_spec=pltpu.PrefetchScalarGridSpec(
            num_scalar_prefetch=1,
            grid=(n_dec,),
            in_specs=page_specs + [const(pe), const(w1), const(b1), const(w2)],
            out_specs=pl.BlockSpec((2, rows, blocks * HEAD_DIM), lambda s, pt: (0, s // group, 0)),
            scratch_shapes=[pltpu.VMEM((group * n_pages * N_KV_HEADS * SLAB_PITCH, PAGE_SIZE), F32)] * 2,
        ),
        out_shape=jax.ShapeDtypeStruct((2, n_dec // group * rows, blocks * HEAD_DIM), F32),
        compiler_params=_params("arbitrary"),
        name="gather_compress",
    )(page_table, *page_args, pe, w1, b1, w2)
    out = out.reshape(2, n_dec, n_pages, N_KV_HEADS, blocks, HEAD_DIM).transpose(0, 1, 2, 4, 3, 5)
    return out.reshape(2, n_dec, n_pages * blocks, N_KV_HEADS * HEAD_DIM)


def _group_queries_t(qt, g):
    t = qt.shape[1]
    zeros = jnp.zeros((HEAD_DIM, t), qt.dtype)
    cols = []
    for j in range(HEADS_PER_GROUP):
        qh = qt[j * LANES + g * HEAD_DIM:j * LANES + (g + 1) * HEAD_DIM, :]
        cols.append(jnp.concatenate([qh, zeros] if g == 0 else [zeros, qh], axis=0))
    return jnp.concatenate(cols, axis=1)


def _tile_heads(x):
    return jnp.concatenate([x] * HEADS_PER_GROUP, axis=1)


def _select_blocks_t(imp, nblk, n_sel):
    nb = imp.shape[0]
    rank = jnp.zeros(imp.shape, F32)
    for m in range(nb):
        row = imp[m:m + 1, :]
        beats = jnp.where(row > imp, 1.0, jnp.where(row == imp, jnp.where(m < nblk, 1.0, 0.0), 0.0))
        rank = rank + beats
    return jnp.where(rank < n_sel, jnp.where(imp >= 0.0, 1.0, 0.0), 0.0)


def _select_blocks(imp, nblk, n_sel):
    nb = imp.shape[1]
    rank = jnp.zeros(imp.shape, F32)
    for m in range(nb):
        col = imp[:, m:m + 1]
        beats = jnp.where(col > imp, 1.0, jnp.where(col == imp, jnp.where(m < nblk, 1.0, 0.0), 0.0))
        rank = rank + beats
    return jnp.where(rank < n_sel, jnp.where(imp >= 0.0, 1.0, 0.0), 0.0)


def _attn_tile(carry, k, q, vt_ref, ks, tk, bias):
    m, l, acc = carry
    s = _dot(k, q)
    if bias is not None:
        s = s + bias
    m_new = jnp.maximum(m, jnp.max(s, axis=0, keepdims=True))
    alpha = jnp.exp2(m - m_new)
    p = jnp.exp2(s - m_new)
    l = alpha * l + jnp.sum(p, axis=0, keepdims=True)
    pb = p.astype(BF16)
    half = pb.shape[1] // N_KV_HEADS
    pv = [_dot(vt_ref[GD_ROWS + g * HEAD_DIM:GD_ROWS + (g + 1) * HEAD_DIM, pl.ds(ks, tk)],
               pb[:, g * half:(g + 1) * half]) for g in range(N_KV_HEADS)]
    return m_new, l, alpha * acc + jnp.concatenate(pv, axis=1)


def _maybe(cond, fn, carry):
    return lax.fori_loop(0, jnp.where(cond, 1, 0), lambda _, c: fn(c), carry)


def _attn_kernel(qt_ref, gatet_ref, kc_ref, vct_ref, slc_k_ref, slc_vt_ref, win_k_ref, win_vt_ref, o_ref,
                 *, tq, nb):
    i = pl.program_id(1)
    p0 = i * tq
    tk = tq
    hq = HEADS_PER_GROUP * tq
    n_win_tiles = NSA_WINDOW // tk
    pos = p0 + lax.broadcasted_iota(jnp.int32, (1, tq), 1)
    nblk = lax.broadcasted_iota(jnp.int32, (nb, 1), 0)
    key_off = lax.broadcasted_iota(jnp.int32, (tk, 1), 0)
    qt = qt_ref[...]
    q_all = jnp.concatenate([_group_queries_t(qt, g) for g in range(N_KV_HEADS)], axis=1)
    tile_all = lambda x: jnp.concatenate([x] * N_HEADS, axis=1)

    cvalid = (nblk + 1) * NSA_BLOCK <= tile_all(pos) + 1
    sc = jnp.where(cvalid, _dot(kc_ref[...], q_all), MASK_INIT)
    e = jnp.exp2(sc - jnp.max(sc, axis=0, keepdims=True))
    pc = jnp.where(cvalid, e / jnp.sum(e, axis=0, keepdims=True), 0.0)
    pcb = pc.astype(BF16)
    o_cmp = jnp.concatenate([_dot(vct_ref[g * HEAD_DIM:(g + 1) * HEAD_DIM, :], pcb[:, g * hq:(g + 1) * hq])
                             for g in range(N_KV_HEADS)], axis=1)

    cur = pos // NSA_BLOCK
    forced = HEADS_PER_GROUP + 1.0
    sel_rows = []
    for g in range(N_KV_HEADS):
        imp = functools.reduce(jnp.add, [pc[:, g * hq + j * tq:g * hq + (j + 1) * tq]
                                         for j in range(HEADS_PER_GROUP)])
        imp = jnp.where(nblk == 0, forced, jnp.where(nblk == cur, forced, jnp.where(nblk == cur - 1, forced, imp)))
        imp = jnp.where(nblk * NSA_BLOCK <= pos, imp, -1.0)
        sel = _select_blocks_t(imp, nblk, NSA_TOP_N)
        sel_rows.append(_tile_heads(jnp.where(sel > 0.5, 0.0, MASK_BIAS)))
    sel_rows = jnp.concatenate(sel_rows, axis=1)
    sel_rows = jnp.concatenate([sel_rows, jnp.zeros((LANES - nb, sel_rows.shape[1]), F32)], axis=0)
    q_sel = jnp.concatenate([q_all, sel_rows.astype(BF16)], axis=0)
    blk_lane = lax.broadcasted_iota(jnp.int32, (1, LANES), 1)

    def slc_keys(ks):
        onehot = jnp.where(blk_lane == (ks + key_off) // NSA_BLOCK, 1.0, 0.0).astype(BF16)
        return jnp.concatenate([slc_k_ref[pl.ds(ks, tk), :GD_ROWS], onehot], axis=1)

    cols = q_all.shape[1]
    init = (jnp.full((1, cols), MASK_INIT, F32), jnp.zeros((1, cols), F32), jnp.zeros((HEAD_DIM, cols), F32))
    causal = tile_all(jnp.where(p0 + key_off <= pos, 0.0, MASK_BIAS))
    k_diag = pl.multiple_of(p0, tk)

    def slc_body(j, c):
        ks = pl.multiple_of(j * tk, tk)
        return _attn_tile(c, slc_keys(ks), q_sel, slc_vt_ref, ks, tk, None)

    c = lax.fori_loop(0, i, slc_body, init)
    _, l, acc = _attn_tile(c, slc_keys(k_diag), q_sel, slc_vt_ref, k_diag, tk, causal)
    o_slc = acc / l

    def win_tile(c, ks, bias):
        return _attn_tile(c, win_k_ref[pl.ds(ks, tk), :GD_ROWS], q_all, win_vt_ref, ks, tk, bias)

    def edge_tile(c):
        ks = pl.multiple_of((i - n_win_tiles) * tk, tk)
        bias = tile_all(jnp.where(pos - (ks + key_off) < NSA_WINDOW, 0.0, MASK_BIAS))
        return win_tile(c, ks, bias)

    c = _maybe(i >= n_win_tiles, edge_tile, init)
    c = lax.fori_loop(jnp.maximum(i - n_win_tiles + 1, 0), i,
                      lambda j, c: win_tile(c, pl.multiple_of(j * tk, tk), None), c)
    _, l, acc = win_tile(c, k_diag, causal)
    o_win = acc / l

    gates = gatet_ref[...]
    for g in range(N_KV_HEADS):
        for j in range(HEADS_PER_GROUP):
            head = g * HEADS_PER_GROUP + j
            cs = slice(g * hq + j * tq, g * hq + (j + 1) * tq)
            o = (gates[head:head + 1] * o_cmp[:, cs]
                 + gates[N_HEADS + head:N_HEADS + head + 1] * o_slc[:, cs]
                 + gates[2 * N_HEADS + head:2 * N_HEADS + head + 1] * o_win[:, cs])
            r0 = j * LANES + g * HEAD_DIM
            o_ref[r0:r0 + HEAD_DIM, :] = o.astype(o_ref.dtype)


def _attn_prompt(qt, gatest, kc, vct, kvb, kvtb, n_seq, seq, tq):
    m = qt.shape[1]
    nb = seq // NSA_BLOCK
    nq = seq // tq
    assert seq % tq == 0 and NSA_WINDOW % tq == 0 and tq % NSA_BLOCK == 0 and nb <= LANES
    col = lambda n: pl.BlockSpec((n, tq), lambda b, i: (0, b * nq + i))
    return pl.pallas_call(
        functools.partial(_attn_kernel, tq=tq, nb=nb),
        grid=(n_seq, nq),
        in_specs=[col(Q_COLS), col(4 * N_HEADS),
                  pl.BlockSpec((nb, LANES), lambda b, i: (b, 0)),
                  pl.BlockSpec((None, LANES, nb), lambda b, i: (b, 0, 0)),
                  pl.BlockSpec((seq, KV_BRANCH_COLS), lambda b, i: (b, 1)),
                  pl.BlockSpec((None, None, KV_BRANCH_COLS, seq), lambda b, i: (1, b, 0, 0)),
                  pl.BlockSpec((seq, KV_BRANCH_COLS), lambda b, i: (b, 2)),
                  pl.BlockSpec((None, None, KV_BRANCH_COLS, seq), lambda b, i: (2, b, 0, 0))],
        out_specs=col(Q_COLS),
        out_shape=jax.ShapeDtypeStruct((Q_COLS, m), BF16),
        compiler_params=_params("parallel", "arbitrary"),
        name="attn_prompt",
    )(qt, gatest, kc, vct, kvb, kvtb, kvb, kvtb)


def _softmax_pv(score_tiles, value_tiles):
    m = functools.reduce(jnp.maximum, [jnp.max(s, axis=-1, keepdims=True) for s in score_tiles])
    ps = [jnp.exp2(s - m) for s in score_tiles]
    l = functools.reduce(jnp.add, [jnp.sum(p, axis=-1, keepdims=True) for p in ps])
    o = functools.reduce(jnp.add, [(_dot_nt if fm else _dot)(p.astype(BF16), v)
                                   for p, (v, fm) in zip(ps, value_tiles)])
    return o / l


def _attn_decode_kernel(pt_ref, *refs, n_pages, seqs, pos, nbp):
    del pt_ref
    pages = refs[:seqs * n_pages]
    q_ref, kvb_ref, gate_ref, kvc_ref, win_ref, o_ref = refs[seqs * n_pages:]
    for r in range(seqs):
        _attn_decode_one(pages[r * n_pages:(r + 1) * n_pages], q_ref.at[r], kvb_ref.at[r], gate_ref.at[r],
                         kvc_ref.at[:, r], win_ref.at[r], o_ref.at[r], pos=pos, nbp=nbp)


def _attn_decode_one(pages, q_ref, kvb_ref, gate_ref, kvc_ref, win_ref, o_ref, *, pos, nbp):
    n_pages = len(pages)
    tk = PAGE_SIZE
    lane = lax.broadcasted_iota(jnp.int32, (1, LANES), 1)
    head = lax.broadcasted_iota(jnp.int32, (N_HEADS, 1), 0)
    low_group = head < HEADS_PER_GROUP
    low_lanes = lane < HEAD_DIM

    q32 = q_ref[...].astype(F32)
    qm = jnp.zeros((N_HEADS, LANES), F32)
    for j in range(HEADS_PER_GROUP):
        qm = jnp.where(head % HEADS_PER_GROUP == j, q32[:, j * LANES:(j + 1) * LANES], qm)
    qm = jnp.where(low_group, jnp.where(low_lanes, qm, 0.0), jnp.where(low_lanes, 0.0, qm)).astype(BF16)

    gate_row = gate_ref[...]
    gate_col = [jnp.sum(jnp.where(lane == head + br * N_HEADS, gate_row, 0.0), axis=-1, keepdims=True)
                for br in range(3)]
    kv_new = kvb_ref[...].astype(F32)
    first_row = lax.broadcasted_iota(jnp.int32, (tk, 1), 0) == 0

    def new_key_tile(c0):
        return jnp.where(first_row, kv_new[:, c0:c0 + LANES], 0.0).astype(BF16)

    nblk = lax.broadcasted_iota(jnp.int32, (1, nbp), 1)
    cvalid = (nblk + 1) * NSA_BLOCK <= pos + 1
    sc = jnp.where(cvalid, _dot_nt(qm, kvc_ref[0].astype(BF16)), MASK_INIT)
    e = jnp.exp2(sc - jnp.max(sc, axis=-1, keepdims=True))
    pc = jnp.where(cvalid, e / jnp.sum(e, axis=-1, keepdims=True), 0.0)
    o_cmp = _dot(pc.astype(BF16), kvc_ref[1].astype(BF16))

    imp = jnp.where(low_group,
                    jnp.sum(jnp.where(low_group, pc, 0.0), axis=0, keepdims=True),
                    jnp.sum(jnp.where(low_group, 0.0, pc), axis=0, keepdims=True))
    cur = pos // NSA_BLOCK
    forced = HEADS_PER_GROUP + 1.0
    imp = jnp.where(nblk == 0, forced, jnp.where(nblk == cur, forced, jnp.where(nblk == cur - 1, forced, imp)))
    imp = jnp.where(nblk * NSA_BLOCK <= pos, imp, -1.0)
    sel = _select_blocks(imp, nblk, NSA_TOP_N)

    def sel_bias(n):
        return jnp.where(sel[:, n:n + 1] > 0.5, 0.0, MASK_BIAS)

    s_tiles, v_tiles = [], []
    blocks_per_page = tk // NSA_BLOCK
    for k in range(n_pages):
        page = pages[k]
        s = _dot(qm, page[:GD_ROWS, :].astype(BF16))
        bias = sel_bias(k * blocks_per_page + blocks_per_page - 1)
        for b in range(blocks_per_page - 2, -1, -1):
            bias = jnp.where(lane < (b + 1) * NSA_BLOCK, sel_bias(k * blocks_per_page + b), bias)
        s_tiles.append(s + bias)
        v_tiles.append((page[GD_ROWS:, :].astype(BF16), True))
    s_new = _dot_nt(qm, new_key_tile(KV_BRANCH_COLS))
    s_tiles.append(s_new + jnp.where(lane == 0, sel_bias(n_pages * blocks_per_page), MASK_BIAS))
    v_tiles.append((new_key_tile(KV_BRANCH_COLS + LANES), False))
    o_slc = _softmax_pv(s_tiles, v_tiles)

    n_win = win_ref.shape[1]
    s_tiles, v_tiles = [], []
    for k in range(n_win // tk):
        diff = n_win - (k * tk + lane)
        bias = jnp.where(diff < NSA_WINDOW, 0.0, MASK_BIAS)
        s_tiles.append(_dot(qm, win_ref[:GD_ROWS, k * tk:(k + 1) * tk].astype(BF16)) + bias)
        v_tiles.append((win_ref[GD_ROWS:, k * tk:(k + 1) * tk].astype(BF16), True))
    s_tiles.append(_dot_nt(qm, new_key_tile(2 * KV_BRANCH_COLS)) + jnp.where(lane == 0, 0.0, MASK_BIAS))
    v_tiles.append((new_key_tile(2 * KV_BRANCH_COLS + LANES), False))
    o_win = _softmax_pv(s_tiles, v_tiles)

    o = gate_col[0] * o_cmp + gate_col[1] * o_slc + gate_col[2] * o_win
    for j in range(HEADS_PER_GROUP):
        pair = jnp.where(low_lanes, o[j:j + 1], o[j + HEADS_PER_GROUP:j + HEADS_PER_GROUP + 1])
        o_ref[:, j * LANES:(j + 1) * LANES] = pair.astype(o_ref.dtype)


def _attn_decode(q, kvb, gates, kvc, cache_slc, win_state, page_table, pos):
    n_dec, n_pages = page_table.shape
    nbp = kvc.shape[2]
    n_win = win_state.shape[2]
    seqs = DECODE_SEQS_PER_STEP
    per_seq = lambda a: pl.BlockSpec((seqs, 1, a.shape[-1]), lambda s, pt: (s, 0, 0))
    q3, kvb3, g3 = (a.reshape(n_dec, 1, a.shape[-1]) for a in (q, kvb, gates))
    out = pl.pallas_call(
        functools.partial(_attn_decode_kernel, n_pages=n_pages, seqs=seqs, pos=pos, nbp=nbp),
        grid_spec=pltpu.PrefetchScalarGridSpec(
            num_scalar_prefetch=1,
            grid=(n_dec // seqs,),
            in_specs=_page_specs(n_pages, seqs) + [
                per_seq(q3), per_seq(kvb3), per_seq(g3),
                pl.BlockSpec((2, seqs, nbp, LANES), lambda s, pt: (0, s, 0, 0)),
                pl.BlockSpec((seqs, KV_BRANCH_COLS, n_win), lambda s, pt: (s, 0, 0)),
            ],
            out_specs=pl.BlockSpec((seqs, 1, Q_COLS), lambda s, pt: (s, 0, 0)),
        ),
        out_shape=jax.ShapeDtypeStruct((n_dec, 1, Q_COLS), BF16),
        compiler_params=_params("arbitrary"),
        name="attn_decode",
    )(page_table, *([cache_slc] * (seqs * n_pages)), q3, kvb3, g3, kvc, win_state)
    return out.reshape(n_dec, Q_COLS)


def _mix_kernel(*refs, tm, halo, tiles_per_seq):
    if halo:
        (x_ref, xh_ref, o_ref, shift_ref, scale_ref, gate_ref, gpre_ref, gpost_ref, cw_ref,
         wconv_ref, wmerge_ref, wup_ref, wco_ref, wout_ref, y_ref, ulast_ref, u_scr) = refs
        x = x_ref[...]
        xe = jnp.concatenate([xh_ref[...], x], axis=0)
    else:
        (x_ref, um1_ref, um2_ref, o_ref, shift_ref, scale_ref, gate_ref, gpre_ref, gpost_ref, cw_ref,
         wconv_ref, wmerge_ref, wup_ref, wco_ref, wout_ref, y_ref, ulast_ref) = refs
        x = x_ref[...]
        xe = x
    h0 = xe.shape[0] - tm
    a = (_rms(xe, gpre_ref[...]) * (1.0 + scale_ref[...]) + shift_ref[...]).astype(BF16)
    zc = _dot(a, wconv_ref[...])
    u = zc[:, 2 * D_CONV:] * zc[:, :D_CONV]
    cb = zc[h0:, D_CONV:2 * D_CONV]
    if halo:
        keep_halo = jnp.where(pl.program_id(0) % tiles_per_seq == 0, 0.0, 1.0)
        rows = lax.broadcasted_iota(jnp.int32, (xe.shape[0], 1), 0)
        u = jnp.where(rows < h0, u * keep_halo, u)
        u_scr[...] = u
        um1 = u_scr[h0 - 1:h0 - 1 + tm, :]
        um2 = u_scr[h0 - 2:h0 - 2 + tm, :]
        u0 = u[h0:]
        ulast_ref[...] = u[tm:]
    else:
        um1, um2, u0 = um1_ref[...], um2_ref[...], u
        ulast_ref[...] = u
    cw = cw_ref[...]
    y = cw[0:1] * um2 + cw[1:2] * um1 + cw[2:3] * u0
    conv_out = _dot((cb * y).astype(BF16), wco_ref[...])
    attn_out = _dot_tn(o_ref[...], wup_ref[...])
    mg = jax.nn.sigmoid(_dot(a[h0:], wmerge_ref[...]))
    merged = mg[:, :D_MODEL] * attn_out + mg[:, D_MODEL:] * conv_out
    mixed = _dot(merged.astype(BF16), wout_ref[...])
    y_ref[...] = x + gate_ref[...] * _rms(mixed, gpost_ref[...])


def _mix(x, o_att, mod, g3, conv_w, weights, tm, tiles_per_seq=None, prev=None):
    m = x.shape[0]
    halo = prev is None
    h0 = MIX_HALO
    row = lambda n: pl.BlockSpec((tm, n), lambda i: (i, 0))
    full = lambda a: pl.BlockSpec(a.shape, lambda i: (0,) * a.ndim)
    att = pl.BlockSpec((Q_COLS, tm), lambda i: (0, i))
    common = [att, mod.spec(3), mod.spec(4), mod.spec(5), _gspec(2), _gspec(3), full(conv_w)]
    common += [full(w) for w in weights]
    common_args = [o_att, mod.arr, mod.arr, mod.arr, g3, g3, conv_w, *weights]
    if halo:
        hb = tm // h0
        in_specs = [row(D_MODEL), pl.BlockSpec((h0, D_MODEL), lambda i: (jnp.maximum(i * hb - 1, 0), 0))] + common
        args = [x, x] + common_args
        scratch = [pltpu.VMEM((tm + h0, D_CONV), F32)]
        ulast = (jax.ShapeDtypeStruct((m // tm * h0, D_CONV), F32), pl.BlockSpec((h0, D_CONV), lambda i: (i, 0)))
    else:
        in_specs = [row(D_MODEL), row(D_CONV), row(D_CONV)] + common
        args = [x, prev[0], prev[1]] + common_args
        scratch = []
        ulast = (jax.ShapeDtypeStruct((m, D_CONV), F32), row(D_CONV))
    return pl.pallas_call(
        functools.partial(_mix_kernel, tm=tm, halo=halo, tiles_per_seq=tiles_per_seq),
        grid=(m // tm,),
        in_specs=in_specs,
        out_specs=[row(D_MODEL), ulast[1]],
        out_shape=[jax.ShapeDtypeStruct((m, D_MODEL), F32), ulast[0]],
        scratch_shapes=scratch,
        compiler_params=_params("parallel"),
        name="mix",
    )(*args)


def _head_pair_perm():
    order = []
    for j in range(HEADS_PER_GROUP):
        for g in range(N_KV_HEADS):
            head = g * HEADS_PER_GROUP + j
            order.extend(range(head * HEAD_DIM, (head + 1) * HEAD_DIM))
    return jnp.array(order, dtype=jnp.int32)


def _layer_weights(w_in, w_attn_up, w_conv_out, w_out):
    perm = _head_pair_perm()
    o1 = Q_COLS
    o2 = o1 + KV_COLS
    o3 = o2 + NSA_GATE_COLS
    o4 = o3 + CONV_COLS
    w_q = w_in[:, :o1][:, perm]
    w_g = jnp.pad(w_in[:, o2:o3], ((0, 0), (0, GATE_PAD - NSA_GATE_COLS)))
    w_qkvg = jnp.concatenate([w_q, w_in[:, o1:o2], w_g], axis=1).astype(BF16)
    mix_w = (w_in[:, o3:o4].astype(BF16), w_in[:, o4:].astype(BF16), w_attn_up[perm].astype(BF16),
             w_conv_out.astype(BF16), w_out.astype(BF16))
    return w_qkvg, mix_w


def kernel(x_prompt, x_sample, c_prompt, c_sample, cache_cmp_kv, cache_slc_kv, state_win_kv, state_conv,
           page_table, w_ada, b_ada, g_norm, w_ffn_gu, w_ffn_down, w_in, phi_pe, phi_w1, phi_b1, phi_w2,
           w_attn_up, conv_w, w_conv_out, w_out):
    n_seq, seq, _ = x_prompt.shape
    n_dec = x_sample.shape[0]
    assert w_ada.shape[0] == 1
    l = 0

    mod_all = _ada(jnp.concatenate([c_prompt, c_sample], axis=0), w_ada[l], b_ada[l])
    g3 = g_norm[l].reshape(6, 1, D_MODEL)
    w_gu = w_ffn_gu[l].astype(BF16)
    w_down = w_ffn_down[l].astype(BF16)
    w_qkvg, mix_w = _layer_weights(w_in[l], w_attn_up[l], w_conv_out[l], w_out[l])
    cwp = _compress_weights_paged(phi_pe[l], phi_w1[l], phi_b1[l], phi_w2[l])

    tm = 512
    tq = 256
    m = n_seq * seq
    mod_p = _Mod(mod_all[:n_seq], per_row=False, tiles_per_seq=seq // tm)
    x0 = x_prompt.reshape(m, D_MODEL)
    x1 = _ffn(x0, mod_p, g3, w_gu[0], w_down[0], 0, tm)
    tables = _rope_tables(jnp.arange(seq, dtype=jnp.int32))
    _, qt, cmp_pages, kvt, kvtb, kvb, _, gatest = _proj(x1, mod_p, g3, tables, w_qkvg, tm)
    nb = seq // NSA_BLOCK
    kvc = _gather_compress(cmp_pages, None, cwp, pages_per_seq=seq // PAGE_SIZE)
    kc = kvc[0].reshape(n_seq * nb, LANES).astype(BF16)
    vct = kvc[1].transpose(0, 2, 1).astype(BF16)
    o_att = _attn_prompt(qt, gatest, kc, vct, kvb, kvtb, n_seq, seq, tq)
    x2, ulast = _mix(x1, o_att, mod_p, g3, conv_w[l], mix_w, tm, tiles_per_seq=seq // tm)
    y_prompt = _ffn(x2, mod_p, g3, w_gu[1], w_down[1], 2, tm).reshape(n_seq, seq, D_MODEL)

    def token_major(t):
        n, _, tt = t.shape
        return t.reshape(n, 2, N_KV_HEADS, HEAD_DIM, tt).transpose(0, 4, 1, 2, 3)[None]

    def feature_major(a):
        n, tt = a.shape[:2]
        return a.transpose(0, 2, 3, 4, 1).reshape(n, KV_BRANCH_COLS, tt)

    cmp_kv_prompt = token_major(kvt[0])
    slc_kv_prompt = token_major(kvt[1])
    win_keep = min(NSA_WINDOW, seq)
    win_kv_prompt = token_major(kvt[2][:, :, seq - win_keep:])
    conv_prompt = ulast.reshape(n_seq, seq // tm, MIX_HALO, D_CONV)[:, -1, MIX_HALO - (CONV_WIDTH - 1):][None]

    n_pages = page_table.shape[1]
    past_len = n_pages * PAGE_SIZE
    pos_s = past_len + jnp.arange(x_sample.shape[1], dtype=jnp.int32)
    assert x_sample.shape[1] == 1 and past_len % NSA_BLOCK == 0 and state_win_kv.shape[2] <= past_len
    mod_s = _Mod(mod_all[n_seq:], per_row=True)
    xs1 = _ffn(x_sample.reshape(n_dec, D_MODEL), mod_s, g3, w_gu[0], w_down[0], 0, n_dec)
    tables_s = _rope_tables(jnp.broadcast_to(pos_s, (n_dec,)))
    q_s, _, _, kvt_s, _, kvb_s, gates_s, _ = _proj(xs1, mod_s, g3, tables_s, w_qkvg, n_dec)
    nb_past = past_len // NSA_BLOCK
    nb_pad = -(-(nb_past + 1) // NSA_BLOCK) * NSA_BLOCK
    kvc_past = _gather_compress(feature_major(cache_cmp_kv[l]), page_table, cwp)
    new_pages = jnp.pad(kvt_s[0, 0].T[:, :, None], ((0, 0), (0, 0), (0, PAGE_SIZE - 1)))
    kvc_new = _gather_compress(new_pages, None, cwp, pages_per_seq=n_pages).reshape(2, n_dec, -1, LANES)[:, :, :1]
    kvc_s = jnp.concatenate(
        [kvc_past, kvc_new, jnp.zeros((2, n_dec, nb_pad - nb_past - 1, LANES), F32)], axis=2)
    win_state = feature_major(state_win_kv[l])
    o_att_s = _attn_decode(q_s, kvb_s, gates_s, kvc_s, feature_major(cache_slc_kv[l]), win_state, page_table,
                           past_len)
    conv_state = state_conv[l]
    xs2, u_s = _mix(xs1, o_att_s.T, mod_s, g3, conv_w[l], mix_w, n_dec,
                    prev=(conv_state[:, CONV_WIDTH - 2], conv_state[:, CONV_WIDTH - 3]))
    y_sample = _ffn(xs2, mod_s, g3, w_gu[1], w_down[1], 2, n_dec).reshape(x_sample.shape)

    kvs = (1, n_dec, 1, 2, N_KV_HEADS, HEAD_DIM)
    win_keep_s = min(NSA_WINDOW, past_len + 1)
    win_new_s = kvt_s[2, 0].T[:, :, None]
    win_kv_sample = token_major(jnp.concatenate([win_state, win_new_s], axis=2)[:, :, -win_keep_s:])
    conv_sample = jnp.concatenate([conv_state[:, 1:], u_s[:, None, :]], axis=1)[None]
    return (y_prompt, y_sample, cmp_kv_prompt, slc_kv_prompt, win_kv_prompt, conv_prompt,
            token_major(kvt_s[0]).reshape(kvs), token_major(kvt_s[1]).reshape(kvs), win_kv_sample, conv_sample)
```

```python
import functools

import jax
import jax.numpy as jnp
from jax import lax
from jax.experimental import pallas as pl
from jax.experimental.pallas import tpu as pltpu

D_MODEL = 1024
N_HEADS = 8
HEAD_DIM = 64
N_KV_HEADS = 2
HEADS_PER_GROUP = N_HEADS // N_KV_HEADS
ROT_DIM = HEAD_DIM // 4
ROPE_THETA = 500000.0
NSA_BLOCK = 64
NSA_TOP_N = 16
NSA_WINDOW = 512
PHI_HIDDEN = 2 * HEAD_DIM
D_CONV = 512
CONV_WIDTH = 3
D_FF = 2816
FFN_HALF = 0.5
NORM_EPS = 1e-6
PAGE_SIZE = 128
Q_COLS = N_HEADS * HEAD_DIM
KV_BRANCH_COLS = 2 * N_KV_HEADS * HEAD_DIM
KV_COLS = 3 * KV_BRANCH_COLS
NSA_GATE_COLS = 3 * N_HEADS
CONV_COLS = 3 * D_CONV
MERGE_COLS = 2 * D_MODEL
GD_ROWS = N_KV_HEADS * HEAD_DIM

LANES = 128
GATE_PAD = LANES
QKVG_COLS = Q_COLS + KV_COLS + GATE_PAD
LOG2_E = 1.4426950408889634
MASK_INIT = -1e30
MASK_BIAS = -2e30
VMEM_LIMIT = 56 * 1024 * 1024
SLAB_PITCH = HEAD_DIM + 8
DECODE_SEQS_PER_STEP = 4
MIX_HALO = 16

BF16 = jnp.bfloat16
F32 = jnp.float32


def _dot(a, b):
    return jnp.dot(a, b, preferred_element_type=F32)


def _dot_tn(a, b):
    return lax.dot_general(a, b, (((0,), (0,)), ((), ())), preferred_element_type=F32)


def _dot_nt(a, b):
    return lax.dot_general(a, b, (((1,), (1,)), ((), ())), preferred_element_type=F32)


def _rms(x, g):
    return x * lax.rsqrt(jnp.mean(x * x, axis=-1, keepdims=True) + NORM_EPS) * g


def _silu(x):
    return x * jax.nn.sigmoid(x)


def _params(*sem):
    return pltpu.CompilerParams(dimension_semantics=sem, vmem_limit_bytes=VMEM_LIMIT)


def _ada_kernel(c_ref, w_ref, b_ref, o_ref):
    c = _silu(c_ref[...]).astype(BF16)
    o_ref[...] = _dot(c, w_ref[...].astype(BF16)) + b_ref[...]


def _ada(c, w_ada, b_ada):
    rows = c.shape[0]
    n = w_ada.shape[1]
    tn = 9 * LANES
    return pl.pallas_call(
        _ada_kernel,
        grid=(n // tn,),
        in_specs=[
            pl.BlockSpec((rows, D_MODEL), lambda j: (0, 0)),
            pl.BlockSpec((D_MODEL, tn), lambda j: (0, j)),
            pl.BlockSpec((1, tn), lambda j: (0, j)),
        ],
        out_specs=pl.BlockSpec((rows, tn), lambda j: (0, j)),
        out_shape=jax.ShapeDtypeStruct((rows, n), F32),
        compiler_params=_params("parallel"),
        name="ada",
    )(c, w_ada, b_ada.reshape(1, n))


class _Mod:
    def __init__(self, mod, per_row, tiles_per_seq=1):
        self.per_row = per_row
        self.tiles_per_seq = tiles_per_seq
        self.arr = mod if per_row else mod.reshape(mod.shape[0] * 9, 1, D_MODEL)

    def spec(self, k):
        if self.per_row:
            return pl.BlockSpec((self.arr.shape[0], D_MODEL), lambda i, *_: (0, k))
        tps = self.tiles_per_seq
        return pl.BlockSpec((None, 1, D_MODEL), lambda i, *_: ((i // tps) * 9 + k, 0, 0))


def _gspec(k):
    return pl.BlockSpec((None, 1, D_MODEL), lambda i, *_: (k, 0, 0))


def _ffn_kernel(x_ref, shift_ref, scale_ref, gate_ref, gpre_ref, gpost_ref, wg_ref, wu_ref, wd_ref,
                o_ref, a_scr, acc_scr):
    j = pl.program_id(1)

    @pl.when(j == 0)
    def _():
        a = _rms(x_ref[...], gpre_ref[...]) * (1.0 + scale_ref[...]) + shift_ref[...]
        a_scr[...] = a.astype(BF16)
        acc_scr[...] = jnp.zeros_like(acc_scr)

    a = a_scr[...]
    h = _silu(_dot(a, wg_ref[...])) * _dot(a, wu_ref[...])
    acc_scr[...] += _dot(h.astype(BF16), wd_ref[...])

    @pl.when(j == pl.num_programs(1) - 1)
    def _():
        y = _rms(acc_scr[...], gpost_ref[...])
        o_ref[...] = x_ref[...] + FFN_HALF * gate_ref[...] * y


def _ffn(x, mod, g3, w_gu, w_down, sub, tm):
    m = x.shape[0]
    tf = D_FF // 2
    nf = D_FF // tf
    row = pl.BlockSpec((tm, D_MODEL), lambda i, j: (i, 0))
    return pl.pallas_call(
        _ffn_kernel,
        grid=(m // tm, nf),
        in_specs=[
            row, mod.spec(3 * sub), mod.spec(3 * sub + 1), mod.spec(3 * sub + 2),
            _gspec(2 * sub), _gspec(2 * sub + 1),
            pl.BlockSpec((D_MODEL, tf), lambda i, j: (0, j)),
            pl.BlockSpec((D_MODEL, tf), lambda i, j: (0, j + nf)),
            pl.BlockSpec((tf, D_MODEL), lambda i, j: (j, 0)),
        ],
        out_specs=row,
        out_shape=jax.ShapeDtypeStruct((m, D_MODEL), F32),
        scratch_shapes=[pltpu.VMEM((tm, D_MODEL), BF16), pltpu.VMEM((tm, D_MODEL), F32)],
        compiler_params=_params("parallel", "arbitrary"),
        name=f"ffn{sub}",
    )(x, mod.arr, mod.arr, mod.arr, g3, g3, w_gu, w_gu, w_down)


def _rope(x, cos, sin_lo, sin_hi):
    return x * cos + pltpu.roll(x, LANES - ROT_DIM // 2, 1) * sin_lo + pltpu.roll(x, ROT_DIM // 2, 1) * sin_hi


def _proj_kernel(x_ref, shift_ref, scale_ref, g_ref, cos_ref, slo_ref, shi_ref, w_ref,
                 q_ref, qt_ref, cmp_ref, cmpt_ref, slct_ref, wint_ref, kvtb_ref, kvb_ref, gate_ref, gatet_ref):
    a = (_rms(x_ref[...], g_ref[...]) * (1.0 + scale_ref[...]) + shift_ref[...]).astype(BF16)
    z = _dot(a, w_ref[...])
    cos, slo, shi = cos_ref[...], slo_ref[...], shi_ref[...]
    scale = HEAD_DIM ** -0.5 * LOG2_E
    for s in range(Q_COLS // LANES):
        q = _rope(z[:, s * LANES:(s + 1) * LANES], cos, slo, shi) * scale
        q_ref[:, s * LANES:(s + 1) * LANES] = q.astype(BF16)
        qt_ref[s * LANES:(s + 1) * LANES, :] = q.T.astype(BF16)
    kvt_refs = (cmpt_ref, slct_ref, wint_ref)
    for br in range(3):
        c0 = Q_COLS + br * KV_BRANCH_COLS
        k = _rope(z[:, c0:c0 + LANES], cos, slo, shi)
        v = z[:, c0 + LANES:c0 + 2 * LANES]
        for r0, x in ((0, k), (LANES, v)):
            xt = x.T
            kvt_refs[br][r0:r0 + LANES, :] = xt
            kvtb_ref[br, r0:r0 + LANES, :] = xt.astype(BF16)
            if br == 0:
                for p in range(cmp_ref.shape[0]):
                    cmp_ref[p, r0:r0 + LANES, :] = xt[:, p * PAGE_SIZE:(p + 1) * PAGE_SIZE]
        kvb_ref[:, br * KV_BRANCH_COLS:br * KV_BRANCH_COLS + LANES] = k.astype(BF16)
        kvb_ref[:, br * KV_BRANCH_COLS + LANES:(br + 1) * KV_BRANCH_COLS] = v.astype(BF16)
    gate = jax.nn.sigmoid(z[:, Q_COLS + KV_COLS:])
    gate_ref[...] = gate
    gatet_ref[...] = gate.T


def _proj(x, mod, g3, tables, w_qkvg, tm):
    assert tm % PAGE_SIZE == 0
    m = x.shape[0]
    t_rows = tables[0].shape[0]
    nt = t_rows // tm
    n_seq = m // t_rows
    row = lambda n: pl.BlockSpec((tm, n), lambda i: (i, 0))
    tab = pl.BlockSpec((tm, LANES), lambda i: (i % nt, 0))
    col = lambda n: pl.BlockSpec((n, tm), lambda i: (0, i))
    kvt_spec = pl.BlockSpec((None, KV_BRANCH_COLS, tm), lambda i: (i // nt, 0, i % nt))
    kvtb_spec = pl.BlockSpec((3, None, KV_BRANCH_COLS, tm), lambda i: (0, i // nt, 0, i % nt))
    return pl.pallas_call(
        _proj_kernel,
        grid=(m // tm,),
        in_specs=[row(D_MODEL), mod.spec(3), mod.spec(4), _gspec(2), tab, tab, tab,
                  pl.BlockSpec((D_MODEL, QKVG_COLS), lambda i: (0, 0))],
        out_specs=[row(Q_COLS), col(Q_COLS),
                   pl.BlockSpec((tm // PAGE_SIZE, KV_BRANCH_COLS, PAGE_SIZE), lambda i: (i, 0, 0)),
                   kvt_spec, kvt_spec, kvt_spec, kvtb_spec, row(KV_COLS), row(GATE_PAD), col(GATE_PAD)],
        out_shape=[
            jax.ShapeDtypeStruct((m, Q_COLS), BF16),
            jax.ShapeDtypeStruct((Q_COLS, m), BF16),
            jax.ShapeDtypeStruct((m // PAGE_SIZE, KV_BRANCH_COLS, PAGE_SIZE), F32),
            *[jax.ShapeDtypeStruct((n_seq, KV_BRANCH_COLS, t_rows), F32)] * 3,
            jax.ShapeDtypeStruct((3, n_seq, KV_BRANCH_COLS, t_rows), BF16),
            jax.ShapeDtypeStruct((m, KV_COLS), BF16),
            jax.ShapeDtypeStruct((m, GATE_PAD), F32),
            jax.ShapeDtypeStruct((GATE_PAD, m), F32),
        ],
        compiler_params=_params("parallel"),
        name="proj",
    )(x, mod.arr, mod.arr, g3, *tables, w_qkvg)


def _rope_tables(pos):
    half = ROT_DIM // 2
    inv = jnp.power(ROPE_THETA, -jnp.arange(half, dtype=F32) * 2.0 / ROT_DIM)
    ang = pos.astype(F32)[:, None] * inv[None, :]
    cos, sin = jnp.cos(ang), jnp.sin(ang)
    n = pos.shape[0]
    ones = jnp.ones((n, HEAD_DIM - ROT_DIM), F32)
    zeros = jnp.zeros((n, HEAD_DIM - ROT_DIM), F32)
    zh = jnp.zeros((n, half), F32)
    c = jnp.concatenate([cos, cos, ones], axis=1)
    lo = jnp.concatenate([-sin, zh, zeros], axis=1)
    hi = jnp.concatenate([zh, sin, zeros], axis=1)
    return tuple(jnp.tile(t, (1, LANES // HEAD_DIM)) for t in (c, lo, hi))


def _compress_weights_paged(phi_pe, phi_w1, phi_b1, phi_w2):
    blocks = PAGE_SIZE // NSA_BLOCK
    eye = jnp.eye(blocks, dtype=F32)
    w1 = phi_w1.reshape(2, NSA_BLOCK, HEAD_DIM, PHI_HIDDEN)
    w1 = jnp.einsum("bc,ktdj->kdbtcj", eye, w1).reshape(2, HEAD_DIM * PAGE_SIZE, blocks * PHI_HIDDEN)
    w2 = jnp.einsum("bc,kjd->kbjcd", eye, phi_w2).reshape(2, blocks * PHI_HIDDEN, blocks * HEAD_DIM)
    pe = jnp.tile(phi_pe.transpose(0, 2, 1), (1, 1, blocks)).reshape(2, HEAD_DIM, 1, PAGE_SIZE)
    b1 = jnp.tile(phi_b1, (1, blocks)).reshape(2, 1, blocks * PHI_HIDDEN)
    return pe, w1.astype(BF16), b1, w2.astype(BF16)


def _gather_compress_kernel(pt_ref, *refs, n_pages, group, contiguous):
    del pt_ref
    if contiguous:
        pages = [refs[0].at[k] for k in range(n_pages)]
        refs = refs[1:]
    else:
        pages, refs = refs[:n_pages], refs[n_pages:]
    pe_ref, w1_ref, b1_ref, w2_ref, o_ref, xk_scr, xv_scr, c_scr = refs
    x_scr = (xk_scr, xv_scr)
    slot = pl.program_id(0) % group
    for k in range(n_pages):
        for kv in range(2):
            for g in range(N_KV_HEADS):
                r0 = pl.multiple_of(((slot * n_pages + k) * N_KV_HEADS + g) * SLAB_PITCH, SLAB_PITCH)
                s0 = kv * GD_ROWS + g * HEAD_DIM
                x_scr[kv][pl.ds(r0, HEAD_DIM), :] = pages[k][s0:s0 + HEAD_DIM, :]

    @pl.when(slot == group - 1)
    def _():
        rows = group * n_pages * N_KV_HEADS
        for kv in range(2):
            acc = jnp.zeros((rows, w1_ref.shape[2]), F32)
            for d in range(0, HEAD_DIM, 2):
                xs = []
                for dd in (d, d + 1):
                    x = x_scr[kv][pl.ds(dd, rows, stride=SLAB_PITCH), :]
                    xs.append((x + pe_ref[kv, dd]).astype(BF16))
                acc = acc + _dot(jnp.concatenate(xs, axis=1), w1_ref[kv, d * PAGE_SIZE:(d + 2) * PAGE_SIZE, :])
            h = _silu(acc + b1_ref[kv])
            c_scr[...] = _dot(h.astype(BF16), w2_ref[kv])
            g0 = c_scr[pl.ds(0, rows // 2, stride=2), :]
            g1 = c_scr[pl.ds(1, rows // 2, stride=2), :]
            low = lax.broadcasted_iota(jnp.int32, (1, LANES), 1) < HEAD_DIM
            o_ref.at[kv][pl.ds(0, rows // 2, stride=2), :] = jnp.where(low, g0, pltpu.roll(g1, HEAD_DIM, 1))
            o_ref.at[kv][pl.ds(1, rows // 2, stride=2), :] = jnp.where(low, pltpu.roll(g0, HEAD_DIM, 1), g1)


def _page_specs(n_pages, seqs=1):
    return [pl.BlockSpec((None, KV_BRANCH_COLS, PAGE_SIZE), lambda s, pt, r=r, k=k: (pt[s * seqs + r, k], 0, 0))
            for r in range(seqs) for k in range(n_pages)]


def _gather_compress(cache_t, page_table, cwp, pages_per_seq=None):
    pe, w1, b1, w2 = cwp
    contiguous = page_table is None
    if contiguous:
        n_pages = pages_per_seq
        n_dec = cache_t.shape[0] // n_pages
        page_table = jnp.zeros((1, 1), jnp.int32)
        page_specs = [pl.BlockSpec((n_pages, KV_BRANCH_COLS, PAGE_SIZE), lambda s, pt: (s, 0, 0))]
        page_args = [cache_t]
    else:
        n_dec, n_pages = page_table.shape
        page_specs = _page_specs(n_pages)
        page_args = [cache_t] * n_pages
    group = 8
    assert n_dec % group == 0
    rows = group * n_pages * N_KV_HEADS
    blocks = PAGE_SIZE // NSA_BLOCK
    assert blocks == 2 and N_KV_HEADS == 2
    const = lambda a: pl.BlockSpec(a.shape, lambda s, pt: (0,) * a.ndim)
    out = pl.pallas_call(
        functools.partial(_gather_compress_kernel, n_pages=n_pages, group=group, contiguous=contiguous),
        grid_spec=pltpu.PrefetchScalarGridSpec(
            num_scalar_prefetch=1,
            grid=(n_dec,),
            in_specs=page_specs + [const(pe), const(w1), const(b1), const(w2)],
            out_specs=pl.BlockSpec((2, rows, blocks * HEAD_DIM), lambda s, pt: (0, s // group, 0)),
            scratch_shapes=[pltpu.VMEM((group * n_pages * N_KV_HEADS * SLAB_PITCH, PAGE_SIZE), F32)] * 2
            + [pltpu.VMEM((rows, LANES), F32)],
        ),
        out_shape=jax.ShapeDtypeStruct((2, n_dec // group * rows, blocks * HEAD_DIM), F32),
        compiler_params=_params("arbitrary"),
        name="gather_compress",
    )(page_table, *page_args, pe, w1, b1, w2)
    return out.reshape(2, n_dec, n_pages * blocks, N_KV_HEADS * HEAD_DIM)


def _group_queries_t(qt, g):
    t = qt.shape[1]
    zeros = jnp.zeros((HEAD_DIM, t), qt.dtype)
    cols = []
    for j in range(HEADS_PER_GROUP):
        qh = qt[j * LANES + g * HEAD_DIM:j * LANES + (g + 1) * HEAD_DIM, :]
        cols.append(jnp.concatenate([qh, zeros] if g == 0 else [zeros, qh], axis=0))
    return jnp.concatenate(cols, axis=1)


def _tile_heads(x):
    return jnp.concatenate([x] * HEADS_PER_GROUP, axis=1)


def _select_blocks_t(imp, nblk, n_sel):
    nb = imp.shape[0]
    rank = jnp.zeros(imp.shape, F32)
    for m in range(nb):
        row = imp[m:m + 1, :]
        beats = jnp.where(row > imp, 1.0, jnp.where(row == imp, jnp.where(m < nblk, 1.0, 0.0), 0.0))
        rank = rank + beats
    return jnp.where(rank < n_sel, jnp.where(imp >= 0.0, 1.0, 0.0), 0.0)


def _select_blocks(imp, nblk, n_sel):
    nb = imp.shape[1]
    rank = jnp.zeros(imp.shape, F32)
    for m in range(nb):
        col = imp[:, m:m + 1]
        beats = jnp.where(col > imp, 1.0, jnp.where(col == imp, jnp.where(m < nblk, 1.0, 0.0), 0.0))
        rank = rank + beats
    return jnp.where(rank < n_sel, jnp.where(imp >= 0.0, 1.0, 0.0), 0.0)


def _attn_tile(carry, k, q, vt_ref, ks, tk, bias):
    m, l, acc = carry
    s = _dot(k, q)
    if bias is not None:
        s = s + bias
    m_new = jnp.maximum(m, jnp.max(s, axis=0, keepdims=True))
    alpha = jnp.exp2(m - m_new)
    p = jnp.exp2(s - m_new)
    l = alpha * l + jnp.sum(p, axis=0, keepdims=True)
    pb = p.astype(BF16)
    half = pb.shape[1] // N_KV_HEADS
    pv = [_dot(vt_ref[GD_ROWS + g * HEAD_DIM:GD_ROWS + (g + 1) * HEAD_DIM, pl.ds(ks, tk)],
               pb[:, g * half:(g + 1) * half]) for g in range(N_KV_HEADS)]
    return m_new, l, alpha * acc + jnp.concatenate(pv, axis=1)


def _maybe(cond, fn, carry):
    return lax.fori_loop(0, jnp.where(cond, 1, 0), lambda _, c: fn(c), carry)


def _attn_kernel(qt_ref, gatet_ref, kc_ref, vct_ref, slc_k_ref, slc_vt_ref, win_k_ref, win_vt_ref, o_ref,
                 *, tq, nb):
    i = pl.program_id(1)
    p0 = i * tq
    tk = tq
    hq = HEADS_PER_GROUP * tq
    n_win_tiles = NSA_WINDOW // tk
    pos = p0 + lax.broadcasted_iota(jnp.int32, (1, tq), 1)
    nblk = lax.broadcasted_iota(jnp.int32, (nb, 1), 0)
    key_off = lax.broadcasted_iota(jnp.int32, (tk, 1), 0)
    qt = qt_ref[...]
    q_all = jnp.concatenate([_group_queries_t(qt, g) for g in range(N_KV_HEADS)], axis=1)
    tile_all = lambda x: jnp.concatenate([x] * N_HEADS, axis=1)

    cvalid = (nblk + 1) * NSA_BLOCK <= tile_all(pos) + 1
    sc = jnp.where(cvalid, _dot(kc_ref[...], q_all), MASK_INIT)
    e = jnp.exp2(sc - jnp.max(sc, axis=0, keepdims=True))
    pc = jnp.where(cvalid, e / jnp.sum(e, axis=0, keepdims=True), 0.0)
    pcb = pc.astype(BF16)
    o_cmp = jnp.concatenate([_dot(vct_ref[g * HEAD_DIM:(g + 1) * HEAD_DIM, :], pcb[:, g * hq:(g + 1) * hq])
                             for g in range(N_KV_HEADS)], axis=1)

    cur = pos // NSA_BLOCK
    forced = HEADS_PER_GROUP + 1.0
    sel_rows = []
    for g in range(N_KV_HEADS):
        imp = functools.reduce(jnp.add, [pc[:, g * hq + j * tq:g * hq + (j + 1) * tq]
                                         for j in range(HEADS_PER_GROUP)])
        imp = jnp.where(nblk == 0, forced, jnp.where(nblk == cur, forced, jnp.where(nblk == cur - 1, forced, imp)))
        imp = jnp.where(nblk * NSA_BLOCK <= pos, imp, -1.0)
        sel = _select_blocks_t(imp, nblk, NSA_TOP_N)
        sel_rows.append(_tile_heads(jnp.where(sel > 0.5, 0.0, MASK_BIAS)))
    sel_rows = jnp.concatenate(sel_rows, axis=1)
    sel_rows = jnp.concatenate([sel_rows, jnp.zeros((LANES - nb, sel_rows.shape[1]), F32)], axis=0)
    q_sel = jnp.concatenate([q_all, sel_rows.astype(BF16)], axis=0)
    blk_lane = lax.broadcasted_iota(jnp.int32, (1, LANES), 1)

    def slc_keys(ks):
        onehot = jnp.where(blk_lane == (ks + key_off) // NSA_BLOCK, 1.0, 0.0).astype(BF16)
        return jnp.concatenate([slc_k_ref[pl.ds(ks, tk), :GD_ROWS], onehot], axis=1)

    cols = q_all.shape[1]
    init = (jnp.full((1, cols), MASK_INIT, F32), jnp.zeros((1, cols), F32), jnp.zeros((HEAD_DIM, cols), F32))
    causal = tile_all(jnp.where(p0 + key_off <= pos, 0.0, MASK_BIAS))
    k_diag = pl.multiple_of(p0, tk)

    def slc_body(j, c):
        ks = pl.multiple_of(j * tk, tk)
        return _attn_tile(c, slc_keys(ks), q_sel, slc_vt_ref, ks, tk, None)

    c = lax.fori_loop(0, i, slc_body, init)
    _, l, acc = _attn_tile(c, slc_keys(k_diag), q_sel, slc_vt_ref, k_diag, tk, causal)
    o_slc = acc / l

    def win_tile(c, ks, bias):
        return _attn_tile(c, win_k_ref[pl.ds(ks, tk), :GD_ROWS], q_all, win_vt_ref, ks, tk, bias)

    def edge_tile(c):
        ks = pl.multiple_of((i - n_win_tiles) * tk, tk)
        bias = tile_all(jnp.where(pos - (ks + key_off) < NSA_WINDOW, 0.0, MASK_BIAS))
        return win_tile(c, ks, bias)

    c = _maybe(i >= n_win_tiles, edge_tile, init)
    c = lax.fori_loop(jnp.maximum(i - n_win_tiles + 1, 0), i,
                      lambda j, c: win_tile(c, pl.multiple_of(j * tk, tk), None), c)
    _, l, acc = win_tile(c, k_diag, causal)
    o_win = acc / l

    gates = gatet_ref[...]
    for g in range(N_KV_HEADS):
        for j in range(HEADS_PER_GROUP):
            head = g * HEADS_PER_GROUP + j
            cs = slice(g * hq + j * tq, g * hq + (j + 1) * tq)
            o = (gates[head:head + 1] * o_cmp[:, cs]
                 + gates[N_HEADS + head:N_HEADS + head + 1] * o_slc[:, cs]
                 + gates[2 * N_HEADS + head:2 * N_HEADS + head + 1] * o_win[:, cs])
            r0 = j * LANES + g * HEAD_DIM
            o_ref[r0:r0 + HEAD_DIM, :] = o.astype(o_ref.dtype)


def _attn_prompt(qt, gatest, kc, vct, kvb, kvtb, n_seq, seq, tq):
    m = qt.shape[1]
    nb = seq // NSA_BLOCK
    nq = seq // tq
    assert seq % tq == 0 and NSA_WINDOW % tq == 0 and tq % NSA_BLOCK == 0 and nb <= LANES
    col = lambda n: pl.BlockSpec((n, tq), lambda b, i: (0, b * nq + i))
    return pl.pallas_call(
        functools.partial(_attn_kernel, tq=tq, nb=nb),
        grid=(n_seq, nq),
        in_specs=[col(Q_COLS), col(4 * N_HEADS),
                  pl.BlockSpec((nb, LANES), lambda b, i: (b, 0)),
                  pl.BlockSpec((None, LANES, nb), lambda b, i: (b, 0, 0)),
                  pl.BlockSpec((seq, KV_BRANCH_COLS), lambda b, i: (b, 1)),
                  pl.BlockSpec((None, None, KV_BRANCH_COLS, seq), lambda b, i: (1, b, 0, 0)),
                  pl.BlockSpec((seq, KV_BRANCH_COLS), lambda b, i: (b, 2)),
                  pl.BlockSpec((None, None, KV_BRANCH_COLS, seq), lambda b, i: (2, b, 0, 0))],
        out_specs=col(Q_COLS),
        out_shape=jax.ShapeDtypeStruct((Q_COLS, m), BF16),
        compiler_params=_params("parallel", "arbitrary"),
        name="attn_prompt",
    )(qt, gatest, kc, vct, kvb, kvtb, kvb, kvtb)


def _softmax_pv(score_tiles, value_tiles):
    m = functools.reduce(jnp.maximum, [jnp.max(s, axis=-1, keepdims=True) for s in score_tiles])
    ps = [jnp.exp2(s - m) for s in score_tiles]
    l = functools.reduce(jnp.add, [jnp.sum(p, axis=-1, keepdims=True) for p in ps])
    o = functools.reduce(jnp.add, [(_dot_nt if fm else _dot)(p.astype(BF16), v)
                                   for p, (v, fm) in zip(ps, value_tiles)])
    return o / l


def _attn_decode_kernel(pt_ref, *refs, n_pages, seqs, pos, nbp):
    del pt_ref
    pages = refs[:seqs * n_pages]
    q_ref, kvb_ref, gate_ref, kvc_ref, win_ref, wnew_ref, o_ref, wnext_ref = refs[seqs * n_pages:]
    for r in range(seqs):
        _attn_decode_one(pages[r * n_pages:(r + 1) * n_pages], q_ref.at[r], kvb_ref.at[r], gate_ref.at[r],
                         kvc_ref.at[:, r], win_ref.at[r], wnew_ref.at[r], o_ref.at[r], wnext_ref.at[r],
                         pos=pos, nbp=nbp)


def _attn_decode_one(pages, q_ref, kvb_ref, gate_ref, kvc_ref, win_ref, wnew_ref, o_ref, wnext_ref, *, pos, nbp):
    n_pages = len(pages)
    tk = PAGE_SIZE
    lane = lax.broadcasted_iota(jnp.int32, (1, LANES), 1)
    head = lax.broadcasted_iota(jnp.int32, (N_HEADS, 1), 0)
    low_group = head < HEADS_PER_GROUP
    low_lanes = lane < HEAD_DIM

    q32 = q_ref[...].astype(F32)
    qm = jnp.zeros((N_HEADS, LANES), F32)
    for j in range(HEADS_PER_GROUP):
        qm = jnp.where(head % HEADS_PER_GROUP == j, q32[:, j * LANES:(j + 1) * LANES], qm)
    qm = jnp.where(low_group, jnp.where(low_lanes, qm, 0.0), jnp.where(low_lanes, 0.0, qm)).astype(BF16)

    gate_row = gate_ref[...]
    gate_col = [jnp.sum(jnp.where(lane == head + br * N_HEADS, gate_row, 0.0), axis=-1, keepdims=True)
                for br in range(3)]
    kv_new = kvb_ref[...].astype(F32)
    first_row = lax.broadcasted_iota(jnp.int32, (tk, 1), 0) == 0

    def new_key_tile(c0):
        return jnp.where(first_row, kv_new[:, c0:c0 + LANES], 0.0).astype(BF16)

    nblk = lax.broadcasted_iota(jnp.int32, (1, nbp), 1)
    cvalid = (nblk + 1) * NSA_BLOCK <= pos + 1
    sc = jnp.where(cvalid, _dot_nt(qm, kvc_ref[0].astype(BF16)), MASK_INIT)
    e = jnp.exp2(sc - jnp.max(sc, axis=-1, keepdims=True))
    pc = jnp.where(cvalid, e / jnp.sum(e, axis=-1, keepdims=True), 0.0)
    o_cmp = _dot(pc.astype(BF16), kvc_ref[1].astype(BF16))

    imp = jnp.where(low_group,
                    jnp.sum(jnp.where(low_group, pc, 0.0), axis=0, keepdims=True),
                    jnp.sum(jnp.where(low_group, 0.0, pc), axis=0, keepdims=True))
    cur = pos // NSA_BLOCK
    forced = HEADS_PER_GROUP + 1.0
    imp = jnp.where(nblk == 0, forced, jnp.where(nblk == cur, forced, jnp.where(nblk == cur - 1, forced, imp)))
    imp = jnp.where(nblk * NSA_BLOCK <= pos, imp, -1.0)
    sel = _select_blocks(imp, nblk, NSA_TOP_N)

    def sel_bias(n):
        return jnp.where(sel[:, n:n + 1] > 0.5, 0.0, MASK_BIAS)

    s_tiles, v_tiles = [], []
    blocks_per_page = tk // NSA_BLOCK
    for k in range(n_pages):
        page = pages[k]
        s = _dot(qm, page[:GD_ROWS, :].astype(BF16))
        bias = sel_bias(k * blocks_per_page + blocks_per_page - 1)
        for b in range(blocks_per_page - 2, -1, -1):
            bias = jnp.where(lane < (b + 1) * NSA_BLOCK, sel_bias(k * blocks_per_page + b), bias)
        s_tiles.append(s + bias)
        v_tiles.append((page[GD_ROWS:, :].astype(BF16), True))
    s_new = _dot_nt(qm, new_key_tile(KV_BRANCH_COLS))
    s_tiles.append(s_new + jnp.where(lane == 0, sel_bias(n_pages * blocks_per_page), MASK_BIAS))
    v_tiles.append((new_key_tile(KV_BRANCH_COLS + LANES), False))
    o_slc = _softmax_pv(s_tiles, v_tiles)

    n_win = win_ref.shape[1]
    s_tiles, v_tiles = [], []
    for k in range(n_win // tk):
        diff = n_win - (k * tk + lane)
        bias = jnp.where(diff < NSA_WINDOW, 0.0, MASK_BIAS)
        s_tiles.append(_dot(qm, win_ref[:GD_ROWS, k * tk:(k + 1) * tk].astype(BF16)) + bias)
        v_tiles.append((win_ref[GD_ROWS:, k * tk:(k + 1) * tk].astype(BF16), True))
    s_tiles.append(_dot_nt(qm, new_key_tile(2 * KV_BRANCH_COLS)) + jnp.where(lane == 0, 0.0, MASK_BIAS))
    v_tiles.append((new_key_tile(2 * KV_BRANCH_COLS + LANES), False))
    o_win = _softmax_pv(s_tiles, v_tiles)

    o = gate_col[0] * o_cmp + gate_col[1] * o_slc + gate_col[2] * o_win
    for j in range(HEADS_PER_GROUP):
        pair = jnp.where(low_lanes, o[j:j + 1], o[j + HEADS_PER_GROUP:j + HEADS_PER_GROUP + 1])
        o_ref[:, j * LANES:(j + 1) * LANES] = pair.astype(o_ref.dtype)

    last = lax.broadcasted_iota(jnp.int32, (1, n_win), 1) == n_win - 1
    wnext_ref[...] = jnp.where(last, wnew_ref[...], pltpu.roll(win_ref[...], n_win - 1, 1))


def _attn_decode(q, kvb, gates, kvc, cache_slc, win_state, win_new, page_table, pos):
    n_dec, n_pages = page_table.shape
    nbp = kvc.shape[2]
    n_win = win_state.shape[2]
    seqs = DECODE_SEQS_PER_STEP
    per_seq = lambda a: pl.BlockSpec((seqs, 1, a.shape[-1]), lambda s, pt: (s, 0, 0))
    win_spec = pl.BlockSpec((seqs, KV_BRANCH_COLS, n_win), lambda s, pt: (s, 0, 0))
    q3, kvb3, g3 = (a.reshape(n_dec, 1, a.shape[-1]) for a in (q, kvb, gates))
    out, win_next = pl.pallas_call(
        functools.partial(_attn_decode_kernel, n_pages=n_pages, seqs=seqs, pos=pos, nbp=nbp),
        grid_spec=pltpu.PrefetchScalarGridSpec(
            num_scalar_prefetch=1,
            grid=(n_dec // seqs,),
            in_specs=_page_specs(n_pages, seqs) + [
                per_seq(q3), per_seq(kvb3), per_seq(g3),
                pl.BlockSpec((2, seqs, nbp, LANES), lambda s, pt: (0, s, 0, 0)),
                win_spec, pl.BlockSpec((seqs, KV_BRANCH_COLS, 1), lambda s, pt: (s, 0, 0)),
            ],
            out_specs=[pl.BlockSpec((seqs, 1, Q_COLS), lambda s, pt: (s, 0, 0)), win_spec],
        ),
        out_shape=[jax.ShapeDtypeStruct((n_dec, 1, Q_COLS), BF16), jax.ShapeDtypeStruct(win_state.shape, F32)],
        compiler_params=_params("arbitrary"),
        name="attn_decode",
    )(page_table, *([cache_slc] * (seqs * n_pages)), q3, kvb3, g3, kvc, win_state, win_new)
    return out.reshape(n_dec, Q_COLS), win_next


def _mix_kernel(*refs, tm, halo, tiles_per_seq):
    if halo:
        (x_ref, xh_ref, o_ref, shift_ref, scale_ref, gate_ref, gpre_ref, gpost_ref, cw_ref,
         wconv_ref, wmerge_ref, wup_ref, wco_ref, wout_ref, y_ref, ulast_ref, u_scr) = refs
        x = x_ref[...]
        xe = jnp.concatenate([xh_ref[...], x], axis=0)
    else:
        (x_ref, um1_ref, um2_ref, o_ref, shift_ref, scale_ref, gate_ref, gpre_ref, gpost_ref, cw_ref,
         wconv_ref, wmerge_ref, wup_ref, wco_ref, wout_ref, y_ref, ulast_ref) = refs
        x = x_ref[...]
        xe = x
    h0 = xe.shape[0] - tm
    a = (_rms(xe, gpre_ref[...]) * (1.0 + scale_ref[...]) + shift_ref[...]).astype(BF16)
    zc = _dot(a, wconv_ref[...])
    u = zc[:, 2 * D_CONV:] * zc[:, :D_CONV]
    cb = zc[h0:, D_CONV:2 * D_CONV]
    if halo:
        keep_halo = jnp.where(pl.program_id(0) % tiles_per_seq == 0, 0.0, 1.0)
        rows = lax.broadcasted_iota(jnp.int32, (xe.shape[0], 1), 0)
        u = jnp.where(rows < h0, u * keep_halo, u)
        u_scr[...] = u
        um1 = u_scr[h0 - 1:h0 - 1 + tm, :]
        um2 = u_scr[h0 - 2:h0 - 2 + tm, :]
        u0 = u[h0:]
        ulast_ref[...] = u[tm:]
    else:
        um1, um2, u0 = um1_ref[...], um2_ref[...], u
        ulast_ref[...] = u
    cw = cw_ref[...]
    y = cw[0:1] * um2 + cw[1:2] * um1 + cw[2:3] * u0
    conv_out = _dot((cb * y).astype(BF16), wco_ref[...])
    attn_out = _dot_tn(o_ref[...], wup_ref[...])
    mg = jax.nn.sigmoid(_dot(a[h0:], wmerge_ref[...]))
    merged = mg[:, :D_MODEL] * attn_out + mg[:, D_MODEL:] * conv_out
    mixed = _dot(merged.astype(BF16), wout_ref[...])
    y_ref[...] = x + gate_ref[...] * _rms(mixed, gpost_ref[...])


def _mix(x, o_att, mod, g3, conv_w, weights, tm, tiles_per_seq=None, prev=None):
    m = x.shape[0]
    halo = prev is None
    h0 = MIX_HALO
    row = lambda n: pl.BlockSpec((tm, n), lambda i: (i, 0))
    full = lambda a: pl.BlockSpec(a.shape, lambda i: (0,) * a.ndim)
    att = pl.BlockSpec((Q_COLS, tm), lambda i: (0, i))
    common = [att, mod.spec(3), mod.spec(4), mod.spec(5), _gspec(2), _gspec(3), full(conv_w)]
    common += [full(w) for w in weights]
    common_args = [o_att, mod.arr, mod.arr, mod.arr, g3, g3, conv_w, *weights]
    if halo:
        hb = tm // h0
        in_specs = [row(D_MODEL), pl.BlockSpec((h0, D_MODEL), lambda i: (jnp.maximum(i * hb - 1, 0), 0))] + common
        args = [x, x] + common_args
        scratch = [pltpu.VMEM((tm + h0, D_CONV), F32)]
        ulast = (jax.ShapeDtypeStruct((m // tm * h0, D_CONV), F32), pl.BlockSpec((h0, D_CONV), lambda i: (i, 0)))
    else:
        in_specs = [row(D_MODEL), row(D_CONV), row(D_CONV)] + common
        args = [x, prev[0], prev[1]] + common_args
        scratch = []
        ulast = (jax.ShapeDtypeStruct((m, D_CONV), F32), row(D_CONV))
    return pl.pallas_call(
        functools.partial(_mix_kernel, tm=tm, halo=halo, tiles_per_seq=tiles_per_seq),
        grid=(m // tm,),
        in_specs=in_specs,
        out_specs=[row(D_MODEL), ulast[1]],
        out_shape=[jax.ShapeDtypeStruct((m, D_MODEL), F32), ulast[0]],
        scratch_shapes=scratch,
        compiler_params=_params("parallel"),
        name="mix",
    )(*args)


def _head_pair_perm():
    order = []
    for j in range(HEADS_PER_GROUP):
        for g in range(N_KV_HEADS):
            head = g * HEADS_PER_GROUP + j
            order.extend(range(head * HEAD_DIM, (head + 1) * HEAD_DIM))
    return jnp.array(order, dtype=jnp.int32)


def _layer_weights(w_in, w_attn_up, w_conv_out, w_out):
    perm = _head_pair_perm()
    o1 = Q_COLS
    o2 = o1 + KV_COLS
    o3 = o2 + NSA_GATE_COLS
    o4 = o3 + CONV_COLS
    w_q = w_in[:, :o1][:, perm]
    w_g = jnp.pad(w_in[:, o2:o3], ((0, 0), (0, GATE_PAD - NSA_GATE_COLS)))
    w_qkvg = jnp.concatenate([w_q, w_in[:, o1:o2], w_g], axis=1).astype(BF16)
    mix_w = (w_in[:, o3:o4].astype(BF16), w_in[:, o4:].astype(BF16), w_attn_up[perm].astype(BF16),
             w_conv_out.astype(BF16), w_out.astype(BF16))
    return w_qkvg, mix_w


def kernel(x_prompt, x_sample, c_prompt, c_sample, cache_cmp_kv, cache_slc_kv, state_win_kv, state_conv,
           page_table, w_ada, b_ada, g_norm, w_ffn_gu, w_ffn_down, w_in, phi_pe, phi_w1, phi_b1, phi_w2,
           w_attn_up, conv_w, w_conv_out, w_out):
    n_seq, seq, _ = x_prompt.shape
    n_dec = x_sample.shape[0]
    assert w_ada.shape[0] == 1
    l = 0

    mod_all = _ada(jnp.concatenate([c_prompt, c_sample], axis=0), w_ada[l], b_ada[l])
    g3 = g_norm[l].reshape(6, 1, D_MODEL)
    w_gu = [w_ffn_gu[l, k].astype(BF16) for k in range(2)]
    w_down = [w_ffn_down[l, k].astype(BF16) for k in range(2)]
    w_qkvg, mix_w = _layer_weights(w_in[l], w_attn_up[l], w_conv_out[l], w_out[l])
    cwp = _compress_weights_paged(phi_pe[l], phi_w1[l], phi_b1[l], phi_w2[l])

    tm = 512
    tq = 256
    m = n_seq * seq
    mod_p = _Mod(mod_all[:n_seq], per_row=False, tiles_per_seq=seq // tm)
    x0 = x_prompt.reshape(m, D_MODEL)
    x1 = _ffn(x0, mod_p, g3, w_gu[0], w_down[0], 0, tm)
    tables = _rope_tables(jnp.arange(seq, dtype=jnp.int32))
    _, qt, cmp_pages, cmpt, slct, wint, kvtb, kvb, _, gatest = _proj(x1, mod_p, g3, tables, w_qkvg, tm)
    nb = seq // NSA_BLOCK
    kvc = _gather_compress(cmp_pages, None, cwp, pages_per_seq=seq // PAGE_SIZE)
    kc = kvc[0].reshape(n_seq * nb, LANES).astype(BF16)
    vct = kvc[1].transpose(0, 2, 1).astype(BF16)
    o_att = _attn_prompt(qt, gatest, kc, vct, kvb, kvtb, n_seq, seq, tq)
    x2, ulast = _mix(x1, o_att, mod_p, g3, conv_w[l], mix_w, tm, tiles_per_seq=seq // tm)
    y_prompt = _ffn(x2, mod_p, g3, w_gu[1], w_down[1], 2, tm).reshape(n_seq, seq, D_MODEL)

    def token_major(t):
        n, _, tt = t.shape
        return t.reshape(n, 2, N_KV_HEADS, HEAD_DIM, tt).transpose(0, 4, 1, 2, 3)[None]

    def feature_major(a):
        n, tt = a.shape[:2]
        return a.transpose(0, 2, 3, 4, 1).reshape(n, KV_BRANCH_COLS, tt)

    cmp_kv_prompt = token_major(cmpt)
    slc_kv_prompt = token_major(slct)
    win_keep = min(NSA_WINDOW, seq)
    win_kv_prompt = token_major(wint[:, :, seq - win_keep:])
    conv_prompt = ulast.reshape(n_seq, seq // tm, MIX_HALO, D_CONV)[:, -1, MIX_HALO - (CONV_WIDTH - 1):][None]

    n_pages = page_table.shape[1]
    past_len = n_pages * PAGE_SIZE
    pos_s = past_len + jnp.arange(x_sample.shape[1], dtype=jnp.int32)
    assert x_sample.shape[1] == 1 and past_len % NSA_BLOCK == 0 and state_win_kv.shape[2] <= past_len
    mod_s = _Mod(mod_all[n_seq:], per_row=True)
    xs1 = _ffn(x_sample.reshape(n_dec, D_MODEL), mod_s, g3, w_gu[0], w_down[0], 0, n_dec)
    tables_s = _rope_tables(jnp.broadcast_to(pos_s, (n_dec,)))
    q_s, _, _, cmpt_s, slct_s, wint_s, _, kvb_s, gates_s, _ = _proj(xs1, mod_s, g3, tables_s, w_qkvg, n_dec)
    nb_past = past_len // NSA_BLOCK
    nb_pad = -(-(nb_past + 1) // NSA_BLOCK) * NSA_BLOCK
    kvc_past = _gather_compress(feature_major(cache_cmp_kv[l]), page_table, cwp)
    new_pages = jnp.pad(cmpt_s[0].T[:, :, None], ((0, 0), (0, 0), (0, PAGE_SIZE - 1)))
    kvc_new = _gather_compress(new_pages, None, cwp, pages_per_seq=n_pages).reshape(2, n_dec, -1, LANES)[:, :, :1]
    kvc_s = jnp.concatenate(
        [kvc_past, kvc_new, jnp.zeros((2, n_dec, nb_pad - nb_past - 1, LANES), F32)], axis=2)
    win_state = feature_major(state_win_kv[l])
    assert win_state.shape[2] == NSA_WINDOW
    o_att_s, win_next = _attn_decode(q_s, kvb_s, gates_s, kvc_s, feature_major(cache_slc_kv[l]), win_state,
                                     wint_s[0].T[:, :, None], page_table, past_len)
    conv_state = state_conv[l]
    xs2, u_s = _mix(xs1, o_att_s.T, mod_s, g3, conv_w[l], mix_w, n_dec,
                    prev=(conv_state[:, CONV_WIDTH - 2], conv_state[:, CONV_WIDTH - 3]))
    y_sample = _ffn(xs2, mod_s, g3, w_gu[1], w_down[1], 2, n_dec).reshape(x_sample.shape)

    kvs = (1, n_dec, 1, 2, N_KV_HEADS, HEAD_DIM)
    conv_sample = jnp.concatenate([conv_state[:, 1:], u_s[:, None, :]], axis=1)[None]
    return (y_prompt, y_sample, cmp_kv_prompt, slc_kv_prompt, win_kv_prompt, conv_prompt,
            token_major(cmpt_s).reshape(kvs), token_major(slct_s).reshape(kvs), token_major(win_next), conv_sample)
```

```python
import functools

import jax
import jax.numpy as jnp
from jax import lax
from jax.experimental import pallas as pl
from jax.experimental.pallas import tpu as pltpu

D_MODEL = 1024
N_HEADS = 8
HEAD_DIM = 64
N_KV_HEADS = 2
HEADS_PER_GROUP = N_HEADS // N_KV_HEADS
ROT_DIM = HEAD_DIM // 4
ROPE_THETA = 500000.0
NSA_BLOCK = 64
NSA_TOP_N = 16
NSA_WINDOW = 512
PHI_HIDDEN = 2 * HEAD_DIM
D_CONV = 512
CONV_WIDTH = 3
D_FF = 2816
FFN_HALF = 0.5
NORM_EPS = 1e-6
PAGE_SIZE = 128
Q_COLS = N_HEADS * HEAD_DIM
KV_BRANCH_COLS = 2 * N_KV_HEADS * HEAD_DIM
KV_COLS = 3 * KV_BRANCH_COLS
NSA_GATE_COLS = 3 * N_HEADS
CONV_COLS = 3 * D_CONV
MERGE_COLS = 2 * D_MODEL
GD_ROWS = N_KV_HEADS * HEAD_DIM

LANES = 128
BF16_SUBLANES = 16
GATE_PAD = LANES
QKVG_COLS = Q_COLS + KV_COLS + GATE_PAD
LOG2_E = 1.4426950408889634
MASK_INIT = -1e30
MASK_BIAS = -2e30
VMEM_LIMIT = 56 * 1024 * 1024
SLAB_PITCH = HEAD_DIM + 8
DECODE_SEQS_PER_STEP = 4
MIX_HALO = 16

BF16 = jnp.bfloat16
F32 = jnp.float32


def _dot(a, b):
    return jnp.dot(a, b, preferred_element_type=F32)


def _dot_tn(a, b):
    return lax.dot_general(a, b, (((0,), (0,)), ((), ())), preferred_element_type=F32)


def _dot_nt(a, b):
    return lax.dot_general(a, b, (((1,), (1,)), ((), ())), preferred_element_type=F32)


def _rms(x, g):
    return x * lax.rsqrt(jnp.mean(x * x, axis=-1, keepdims=True) + NORM_EPS) * g


def _silu(x):
    return x * jax.nn.sigmoid(x)


def _params(*sem):
    return pltpu.CompilerParams(dimension_semantics=sem, vmem_limit_bytes=VMEM_LIMIT)


def _ada_kernel(c_ref, w_ref, b_ref, o_ref):
    c = _silu(c_ref[...]).astype(BF16)
    o_ref[...] = _dot(c, w_ref[...].astype(BF16)) + b_ref[...]


def _ada(c, w_ada, b_ada):
    rows = c.shape[0]
    n = w_ada.shape[1]
    tn = 9 * LANES
    return pl.pallas_call(
        _ada_kernel,
        grid=(n // tn,),
        in_specs=[
            pl.BlockSpec((rows, D_MODEL), lambda j: (0, 0)),
            pl.BlockSpec((D_MODEL, tn), lambda j: (0, j)),
            pl.BlockSpec((1, tn), lambda j: (0, j)),
        ],
        out_specs=pl.BlockSpec((rows, tn), lambda j: (0, j)),
        out_shape=jax.ShapeDtypeStruct((rows, n), F32),
        compiler_params=_params("parallel"),
        name="ada",
    )(c, w_ada, b_ada.reshape(1, n))


class _Mod:
    def __init__(self, mod, per_row, tiles_per_seq=1):
        self.per_row = per_row
        self.tiles_per_seq = tiles_per_seq
        self.arr = mod if per_row else mod.reshape(mod.shape[0] * 9, 1, D_MODEL)

    def spec(self, k):
        if self.per_row:
            return pl.BlockSpec((self.arr.shape[0], D_MODEL), lambda i, *_: (0, k))
        tps = self.tiles_per_seq
        return pl.BlockSpec((None, 1, D_MODEL), lambda i, *_: ((i // tps) * 9 + k, 0, 0))


def _gspec(k):
    return pl.BlockSpec((None, 1, D_MODEL), lambda i, *_: (k, 0, 0))


def _ffn_kernel(x_ref, shift_ref, scale_ref, gate_ref, gpre_ref, gpost_ref, wg_ref, wu_ref, wd_ref,
                o_ref, a_scr, acc_scr):
    j = pl.program_id(1)

    @pl.when(j == 0)
    def _():
        a = _rms(x_ref[...], gpre_ref[...]) * (1.0 + scale_ref[...]) + shift_ref[...]
        a_scr[...] = a.astype(BF16)
        acc_scr[...] = jnp.zeros_like(acc_scr)

    a = a_scr[...]
    h = _silu(_dot(a, wg_ref[...])) * _dot(a, wu_ref[...])
    acc_scr[...] += _dot(h.astype(BF16), wd_ref[...])

    @pl.when(j == pl.num_programs(1) - 1)
    def _():
        y = _rms(acc_scr[...], gpost_ref[...])
        o_ref[...] = x_ref[...] + FFN_HALF * gate_ref[...] * y


def _ffn(x, mod, g3, w_gu, w_down, sub, tm):
    m = x.shape[0]
    tf = D_FF // 2
    nf = D_FF // tf
    row = pl.BlockSpec((tm, D_MODEL), lambda i, j: (i, 0))
    return pl.pallas_call(
        _ffn_kernel,
        grid=(m // tm, nf),
        in_specs=[
            row, mod.spec(3 * sub), mod.spec(3 * sub + 1), mod.spec(3 * sub + 2),
            _gspec(2 * sub), _gspec(2 * sub + 1),
            pl.BlockSpec((D_MODEL, tf), lambda i, j: (0, j)),
            pl.BlockSpec((D_MODEL, tf), lambda i, j: (0, j + nf)),
            pl.BlockSpec((tf, D_MODEL), lambda i, j: (j, 0)),
        ],
        out_specs=row,
        out_shape=jax.ShapeDtypeStruct((m, D_MODEL), F32),
        scratch_shapes=[pltpu.VMEM((tm, D_MODEL), BF16), pltpu.VMEM((tm, D_MODEL), F32)],
        compiler_params=_params("parallel", "arbitrary"),
        name=f"ffn{sub}",
    )(x, mod.arr, mod.arr, mod.arr, g3, g3, w_gu, w_gu, w_down)


def _rope(x, cos, sin_lo, sin_hi):
    return x * cos + pltpu.roll(x, LANES - ROT_DIM // 2, 1) * sin_lo + pltpu.roll(x, ROT_DIM // 2, 1) * sin_hi


def _proj_kernel(x_ref, shift_ref, scale_ref, g_ref, cos_ref, slo_ref, shi_ref, w_ref,
                 q_ref, qt_ref, cmp_ref, cmpt_ref, slct_ref, wint_ref, kvtb_ref, kvb_ref, gate_ref, gatet_ref):
    a = (_rms(x_ref[...], g_ref[...]) * (1.0 + scale_ref[...]) + shift_ref[...]).astype(BF16)
    z = _dot(a, w_ref[...])
    cos, slo, shi = cos_ref[...], slo_ref[...], shi_ref[...]
    scale = HEAD_DIM ** -0.5 * LOG2_E
    for s in range(Q_COLS // LANES):
        q = _rope(z[:, s * LANES:(s + 1) * LANES], cos, slo, shi) * scale
        q_ref[:, s * LANES:(s + 1) * LANES] = q.astype(BF16)
        qt_ref[s * LANES:(s + 1) * LANES, :] = q.T.astype(BF16)
    kvt_refs = (cmpt_ref, slct_ref, wint_ref)
    for br in range(3):
        c0 = Q_COLS + br * KV_BRANCH_COLS
        k = _rope(z[:, c0:c0 + LANES], cos, slo, shi)
        v = z[:, c0 + LANES:c0 + 2 * LANES]
        for r0, x in ((0, k), (LANES, v)):
            xt = x.T
            kvt_refs[br][r0:r0 + LANES, :] = xt
            kvtb_ref[br, r0:r0 + LANES, :] = xt.astype(BF16)
            if br == 0:
                for p in range(cmp_ref.shape[0]):
                    cmp_ref[p, r0:r0 + LANES, :] = xt[:, p * PAGE_SIZE:(p + 1) * PAGE_SIZE]
        kvb_ref[:, br * KV_BRANCH_COLS:br * KV_BRANCH_COLS + LANES] = k.astype(BF16)
        kvb_ref[:, br * KV_BRANCH_COLS + LANES:(br + 1) * KV_BRANCH_COLS] = v.astype(BF16)
    gate = jax.nn.sigmoid(z[:, Q_COLS + KV_COLS:])
    gate_ref[...] = gate
    gatet_ref[...] = gate.T


def _proj(x, mod, g3, tables, w_qkvg, tm):
    assert tm % PAGE_SIZE == 0
    m = x.shape[0]
    t_rows = tables[0].shape[0]
    nt = t_rows // tm
    n_seq = m // t_rows
    row = lambda n: pl.BlockSpec((tm, n), lambda i: (i, 0))
    tab = pl.BlockSpec((tm, LANES), lambda i: (i % nt, 0))
    col = lambda n: pl.BlockSpec((n, tm), lambda i: (0, i))
    kvt_spec = pl.BlockSpec((None, KV_BRANCH_COLS, tm), lambda i: (i // nt, 0, i % nt))
    kvtb_spec = pl.BlockSpec((3, None, KV_BRANCH_COLS, tm), lambda i: (0, i // nt, 0, i % nt))
    return pl.pallas_call(
        _proj_kernel,
        grid=(m // tm,),
        in_specs=[row(D_MODEL), mod.spec(3), mod.spec(4), _gspec(2), tab, tab, tab,
                  pl.BlockSpec((D_MODEL, QKVG_COLS), lambda i: (0, 0))],
        out_specs=[row(Q_COLS), col(Q_COLS),
                   pl.BlockSpec((tm // PAGE_SIZE, KV_BRANCH_COLS, PAGE_SIZE), lambda i: (i, 0, 0)),
                   kvt_spec, kvt_spec, kvt_spec, kvtb_spec, row(KV_COLS), row(GATE_PAD), col(GATE_PAD)],
        out_shape=[
            jax.ShapeDtypeStruct((m, Q_COLS), BF16),
            jax.ShapeDtypeStruct((Q_COLS, m), BF16),
            jax.ShapeDtypeStruct((m // PAGE_SIZE, KV_BRANCH_COLS, PAGE_SIZE), F32),
            *[jax.ShapeDtypeStruct((n_seq, KV_BRANCH_COLS, t_rows), F32)] * 3,
            jax.ShapeDtypeStruct((3, n_seq, KV_BRANCH_COLS, t_rows), BF16),
            jax.ShapeDtypeStruct((m, KV_COLS), BF16),
            jax.ShapeDtypeStruct((m, GATE_PAD), F32),
            jax.ShapeDtypeStruct((GATE_PAD, m), F32),
        ],
        compiler_params=_params("parallel"),
        name="proj",
    )(x, mod.arr, mod.arr, g3, *tables, w_qkvg)


def _rope_tables(pos):
    half = ROT_DIM // 2
    inv = jnp.power(ROPE_THETA, -jnp.arange(half, dtype=F32) * 2.0 / ROT_DIM)
    ang = pos.astype(F32)[:, None] * inv[None, :]
    cos, sin = jnp.cos(ang), jnp.sin(ang)
    n = pos.shape[0]
    ones = jnp.ones((n, HEAD_DIM - ROT_DIM), F32)
    zeros = jnp.zeros((n, HEAD_DIM - ROT_DIM), F32)
    zh = jnp.zeros((n, half), F32)
    c = jnp.concatenate([cos, cos, ones], axis=1)
    lo = jnp.concatenate([-sin, zh, zeros], axis=1)
    hi = jnp.concatenate([zh, sin, zeros], axis=1)
    return tuple(jnp.tile(t, (1, LANES // HEAD_DIM)) for t in (c, lo, hi))


def _compress_weights_paged(phi_pe, phi_w1, phi_b1, phi_w2):
    blocks = PAGE_SIZE // NSA_BLOCK
    eye = jnp.eye(blocks, dtype=F32)
    w1 = phi_w1.reshape(2, NSA_BLOCK, HEAD_DIM, PHI_HIDDEN)
    w1 = jnp.einsum("bc,ktdj->kdbtcj", eye, w1).reshape(2, HEAD_DIM * PAGE_SIZE, blocks * PHI_HIDDEN)
    w2 = jnp.einsum("bc,kjd->kbjcd", eye, phi_w2).reshape(2, blocks * PHI_HIDDEN, blocks * HEAD_DIM)
    pe = jnp.tile(phi_pe.transpose(0, 2, 1), (1, 1, blocks)).reshape(2, HEAD_DIM, 1, PAGE_SIZE)
    b1 = jnp.tile(phi_b1, (1, blocks)).reshape(2, 1, blocks * PHI_HIDDEN)
    return pe, w1.astype(BF16), b1, w2.astype(BF16)


def _gather_compress_kernel(pt_ref, *refs, n_pages, group, steps, contiguous):
    del pt_ref
    if contiguous:
        pages = [refs[0].at[k] for k in range(n_pages)]
        refs = refs[1:]
    else:
        pages, refs = refs[:n_pages], refs[n_pages:]
    pe_ref, w1_ref, b1_ref, w2_ref, o_ref, xk_scr, xv_scr, c_scr = refs
    x_scr = (xk_scr, xv_scr)
    slot = pl.program_id(0) % steps
    for k in range(n_pages):
        for kv in range(2):
            for g in range(N_KV_HEADS):
                r0 = pl.multiple_of(((slot * n_pages + k) * N_KV_HEADS + g) * SLAB_PITCH, SLAB_PITCH)
                s0 = kv * GD_ROWS + g * HEAD_DIM
                x_scr[kv][pl.ds(r0, HEAD_DIM), :] = pages[k][s0:s0 + HEAD_DIM, :]

    @pl.when(slot == steps - 1)
    def _():
        rows = group * N_KV_HEADS
        for kv in range(2):
            acc = jnp.zeros((rows, w1_ref.shape[2]), F32)
            for d in range(0, HEAD_DIM, 2):
                xs = []
                for dd in (d, d + 1):
                    x = x_scr[kv][pl.ds(dd, rows, stride=SLAB_PITCH), :]
                    xs.append((x + pe_ref[kv, dd]).astype(BF16))
                acc = acc + _dot(jnp.concatenate(xs, axis=1), w1_ref[kv, d * PAGE_SIZE:(d + 2) * PAGE_SIZE, :])
            h = _silu(acc + b1_ref[kv])
            c_scr[...] = _dot(h.astype(BF16), w2_ref[kv])
            g0 = c_scr[pl.ds(0, rows // 2, stride=2), :]
            g1 = c_scr[pl.ds(1, rows // 2, stride=2), :]
            low = lax.broadcasted_iota(jnp.int32, (1, LANES), 1) < HEAD_DIM
            o_ref.at[kv][pl.ds(0, rows // 2, stride=2), :] = jnp.where(low, g0, pltpu.roll(g1, HEAD_DIM, 1))
            o_ref.at[kv][pl.ds(1, rows // 2, stride=2), :] = jnp.where(low, pltpu.roll(g0, HEAD_DIM, 1), g1)


def _page_specs(n_pages, seqs=1):
    return [pl.BlockSpec((None, KV_BRANCH_COLS, PAGE_SIZE), lambda s, pt, r=r, k=k: (pt[s * seqs + r, k], 0, 0))
            for r in range(seqs) for k in range(n_pages)]


def _gather_compress(cache_t, page_table, cwp, pages_per_seq=None):
    pe, w1, b1, w2 = cwp
    contiguous = page_table is None
    seqs = 4
    if contiguous:
        n_pages = pages_per_seq
        n_dec = cache_t.shape[0] // n_pages
        page_table = jnp.zeros((1, 1), jnp.int32)
        page_specs = [pl.BlockSpec((seqs * n_pages, KV_BRANCH_COLS, PAGE_SIZE), lambda s, pt: (s, 0, 0))]
        page_args = [cache_t]
    else:
        n_dec, n_pages = page_table.shape
        page_specs = _page_specs(n_pages, seqs)
        page_args = [cache_t] * (seqs * n_pages)
    steps = 2
    group = steps * seqs * n_pages
    assert n_dec % (steps * seqs) == 0
    rows = group * N_KV_HEADS
    blocks = PAGE_SIZE // NSA_BLOCK
    assert blocks == 2 and N_KV_HEADS == 2
    const = lambda a: pl.BlockSpec(a.shape, lambda s, pt: (0,) * a.ndim, pipeline_mode=pl.Buffered(1))
    out = pl.pallas_call(
        functools.partial(_gather_compress_kernel, n_pages=seqs * n_pages, group=group, steps=steps,
                          contiguous=contiguous),
        grid_spec=pltpu.PrefetchScalarGridSpec(
            num_scalar_prefetch=1,
            grid=(n_dec // seqs,),
            in_specs=page_specs + [const(pe), const(w1), const(b1), const(w2)],
            out_specs=pl.BlockSpec((2, rows, blocks * HEAD_DIM), lambda s, pt: (0, s // steps, 0)),
            scratch_shapes=[pltpu.VMEM((group * N_KV_HEADS * SLAB_PITCH, PAGE_SIZE), F32)] * 2
            + [pltpu.VMEM((rows, LANES), F32)],
        ),
        out_shape=jax.ShapeDtypeStruct((2, n_dec * n_pages * N_KV_HEADS, blocks * HEAD_DIM), F32),
        compiler_params=_params("arbitrary"),
        name="gather_compress",
    )(page_table, *page_args, pe, w1, b1, w2)
    return out.reshape(2, n_dec, n_pages * blocks, N_KV_HEADS * HEAD_DIM)


def _group_queries_t(qt, g):
    t = qt.shape[1]
    zeros = jnp.zeros((HEAD_DIM, t), qt.dtype)
    cols = []
    for j in range(HEADS_PER_GROUP):
        qh = qt[j * LANES + g * HEAD_DIM:j * LANES + (g + 1) * HEAD_DIM, :]
        cols.append(jnp.concatenate([qh, zeros] if g == 0 else [zeros, qh], axis=0))
    return jnp.concatenate(cols, axis=1)


def _tile_heads(x):
    return jnp.concatenate([x] * HEADS_PER_GROUP, axis=1)


def _select_blocks_t(imp, nblk, n_sel):
    nb = imp.shape[0]
    rank = jnp.zeros(imp.shape, F32)
    for m in range(nb):
        row = imp[m:m + 1, :]
        beats = jnp.where(row > imp, 1.0, jnp.where(row == imp, jnp.where(m < nblk, 1.0, 0.0), 0.0))
        rank = rank + beats
    return jnp.where(rank < n_sel, jnp.where(imp >= 0.0, 1.0, 0.0), 0.0)


def _select_blocks(imp, nblk, n_sel):
    nb = imp.shape[1]
    rank = jnp.zeros(imp.shape, F32)
    for m in range(nb):
        col = imp[:, m:m + 1]
        beats = jnp.where(col > imp, 1.0, jnp.where(col == imp, jnp.where(m < nblk, 1.0, 0.0), 0.0))
        rank = rank + beats
    return jnp.where(rank < n_sel, jnp.where(imp >= 0.0, 1.0, 0.0), 0.0)


def _attn_tile(carry, k, q, vt_ref, ks, tk, bias):
    m, l, acc = carry
    s = _dot(k, q)
    if bias is not None:
        s = s + bias
    m_new = jnp.maximum(m, jnp.max(s, axis=0, keepdims=True))
    alpha = jnp.exp2(m - m_new)
    pb = jnp.exp2(s - m_new).astype(BF16)
    half = pb.shape[1] // N_KV_HEADS
    ones = jnp.ones((BF16_SUBLANES, tk), BF16)
    pv = [_dot(jnp.concatenate([vt_ref[GD_ROWS + g * HEAD_DIM:GD_ROWS + (g + 1) * HEAD_DIM, pl.ds(ks, tk)], ones],
                               axis=0), pb[:, g * half:(g + 1) * half]) for g in range(N_KV_HEADS)]
    pv = jnp.concatenate(pv, axis=1)
    return m_new, alpha * l + pv[HEAD_DIM:HEAD_DIM + 1], alpha * acc + pv[:HEAD_DIM]


def _maybe(cond, fn, carry):
    return lax.fori_loop(0, jnp.where(cond, 1, 0), lambda _, c: fn(c), carry)


def _attn_kernel(qt_ref, gatet_ref, kc_ref, vct_ref, slc_k_ref, slc_vt_ref, win_k_ref, win_vt_ref, o_ref,
                 *, tq, nb):
    i = pl.program_id(1)
    p0 = i * tq
    tk = tq
    hq = HEADS_PER_GROUP * tq
    n_win_tiles = NSA_WINDOW // tk
    pos = p0 + lax.broadcasted_iota(jnp.int32, (1, tq), 1)
    nblk = lax.broadcasted_iota(jnp.int32, (nb, 1), 0)
    key_off = lax.broadcasted_iota(jnp.int32, (tk, 1), 0)
    qt = qt_ref[...]
    q_all = jnp.concatenate([_group_queries_t(qt, g) for g in range(N_KV_HEADS)], axis=1)
    tile_all = lambda x: jnp.concatenate([x] * N_HEADS, axis=1)

    cvalid = (nblk + 1) * NSA_BLOCK <= tile_all(pos) + 1
    sc = jnp.where(cvalid, _dot(kc_ref[...], q_all), MASK_INIT)
    e = jnp.exp2(sc - jnp.max(sc, axis=0, keepdims=True))
    pc = jnp.where(cvalid, e / jnp.sum(e, axis=0, keepdims=True), 0.0)
    pcb = pc.astype(BF16)
    o_cmp = jnp.concatenate([_dot(vct_ref[g * HEAD_DIM:(g + 1) * HEAD_DIM, :], pcb[:, g * hq:(g + 1) * hq])
                             for g in range(N_KV_HEADS)], axis=1)

    cur = pos // NSA_BLOCK
    forced = HEADS_PER_GROUP + 1.0
    sel_rows = []
    for g in range(N_KV_HEADS):
        imp = functools.reduce(jnp.add, [pc[:, g * hq + j * tq:g * hq + (j + 1) * tq]
                                         for j in range(HEADS_PER_GROUP)])
        imp = jnp.where(nblk == 0, forced, jnp.where(nblk == cur, forced, jnp.where(nblk == cur - 1, forced, imp)))
        imp = jnp.where(nblk * NSA_BLOCK <= pos, imp, -1.0)
        sel = _select_blocks_t(imp, nblk, NSA_TOP_N)
        sel_rows.append(_tile_heads(jnp.where(sel > 0.5, 0.0, MASK_BIAS)))
    sel_rows = jnp.concatenate(sel_rows, axis=1)
    sel_rows = jnp.concatenate([sel_rows, jnp.zeros((LANES - nb, sel_rows.shape[1]), F32)], axis=0)
    q_sel = jnp.concatenate([q_all, sel_rows.astype(BF16)], axis=0)
    blk_lane = lax.broadcasted_iota(jnp.int32, (1, LANES), 1)

    def slc_keys(ks):
        onehot = jnp.where(blk_lane == (ks + key_off) // NSA_BLOCK, 1.0, 0.0).astype(BF16)
        return jnp.concatenate([slc_k_ref[pl.ds(ks, tk), :GD_ROWS], onehot], axis=1)

    cols = q_all.shape[1]
    init = (jnp.full((1, cols), MASK_INIT, F32), jnp.zeros((1, cols), F32), jnp.zeros((HEAD_DIM, cols), F32))
    causal = tile_all(jnp.where(p0 + key_off <= pos, 0.0, MASK_BIAS))
    k_diag = pl.multiple_of(p0, tk)

    def slc_body(j, c):
        ks = pl.multiple_of(j * tk, tk)
        return _attn_tile(c, slc_keys(ks), q_sel, slc_vt_ref, ks, tk, None)

    c = lax.fori_loop(0, i, slc_body, init)
    _, l, acc = _attn_tile(c, slc_keys(k_diag), q_sel, slc_vt_ref, k_diag, tk, causal)
    o_slc = acc / l

    def win_tile(c, ks, bias):
        return _attn_tile(c, win_k_ref[pl.ds(ks, tk), :GD_ROWS], q_all, win_vt_ref, ks, tk, bias)

    def edge_tile(c):
        ks = pl.multiple_of((i - n_win_tiles) * tk, tk)
        bias = tile_all(jnp.where(pos - (ks + key_off) < NSA_WINDOW, 0.0, MASK_BIAS))
        return win_tile(c, ks, bias)

    c = _maybe(i >= n_win_tiles, edge_tile, init)
    c = lax.fori_loop(jnp.maximum(i - n_win_tiles + 1, 0), i,
                      lambda j, c: win_tile(c, pl.multiple_of(j * tk, tk), None), c)
    _, l, acc = win_tile(c, k_diag, causal)
    o_win = acc / l

    gates = gatet_ref[...]
    for g in range(N_KV_HEADS):
        for j in range(HEADS_PER_GROUP):
            head = g * HEADS_PER_GROUP + j
            cs = slice(g * hq + j * tq, g * hq + (j + 1) * tq)
            o = (gates[head:head + 1] * o_cmp[:, cs]
                 + gates[N_HEADS + head:N_HEADS + head + 1] * o_slc[:, cs]
                 + gates[2 * N_HEADS + head:2 * N_HEADS + head + 1] * o_win[:, cs])
            r0 = j * LANES + g * HEAD_DIM
            o_ref[r0:r0 + HEAD_DIM, :] = o.astype(o_ref.dtype)


def _attn_prompt(qt, gatest, kc, vct, kvb, kvtb, n_seq, seq, tq):
    m = qt.shape[1]
    nb = seq // NSA_BLOCK
    nq = seq // tq
    assert seq % tq == 0 and NSA_WINDOW % tq == 0 and tq % NSA_BLOCK == 0 and nb <= LANES
    col = lambda n: pl.BlockSpec((n, tq), lambda b, i: (0, b * nq + i))
    return pl.pallas_call(
        functools.partial(_attn_kernel, tq=tq, nb=nb),
        grid=(n_seq, nq),
        in_specs=[col(Q_COLS), col(4 * N_HEADS),
                  pl.BlockSpec((nb, LANES), lambda b, i: (b, 0)),
                  pl.BlockSpec((None, LANES, nb), lambda b, i: (b, 0, 0)),
                  pl.BlockSpec((seq, KV_BRANCH_COLS), lambda b, i: (b, 1)),
                  pl.BlockSpec((None, None, KV_BRANCH_COLS, seq), lambda b, i: (1, b, 0, 0)),
                  pl.BlockSpec((seq, KV_BRANCH_COLS), lambda b, i: (b, 2)),
                  pl.BlockSpec((None, None, KV_BRANCH_COLS, seq), lambda b, i: (2, b, 0, 0))],
        out_specs=col(Q_COLS),
        out_shape=jax.ShapeDtypeStruct((Q_COLS, m), BF16),
        compiler_params=_params("parallel", "arbitrary"),
        name="attn_prompt",
    )(qt, gatest, kc, vct, kvb, kvtb, kvb, kvtb)


def _softmax_pv(score_tiles, value_tiles):
    m = functools.reduce(jnp.maximum, [jnp.max(s, axis=-1, keepdims=True) for s in score_tiles])
    ps = [jnp.exp2(s - m) for s in score_tiles]
    l = functools.reduce(jnp.add, [jnp.sum(p, axis=-1, keepdims=True) for p in ps])
    o = functools.reduce(jnp.add, [(_dot_nt if fm else _dot)(p.astype(BF16), v)
                                   for p, (v, fm) in zip(ps, value_tiles)])
    return o / l


def _attn_decode_kernel(pt_ref, *refs, n_pages, seqs, pos, nbp):
    del pt_ref
    pages = refs[:seqs * n_pages]
    q_ref, kvb_ref, gate_ref, kvc_ref, win_ref, wnew_ref, o_ref, wnext_ref = refs[seqs * n_pages:]
    for r in range(seqs):
        _attn_decode_one(pages[r * n_pages:(r + 1) * n_pages], q_ref.at[r], kvb_ref.at[r], gate_ref.at[r],
                         kvc_ref.at[:, r], win_ref.at[r], wnew_ref.at[r], o_ref.at[r], wnext_ref.at[r],
                         pos=pos, nbp=nbp)


def _attn_decode_one(pages, q_ref, kvb_ref, gate_ref, kvc_ref, win_ref, wnew_ref, o_ref, wnext_ref, *, pos, nbp):
    n_pages = len(pages)
    tk = PAGE_SIZE
    lane = lax.broadcasted_iota(jnp.int32, (1, LANES), 1)
    head = lax.broadcasted_iota(jnp.int32, (N_HEADS, 1), 0)
    low_group = head < HEADS_PER_GROUP
    low_lanes = lane < HEAD_DIM

    q32 = q_ref[...].astype(F32)
    qm = jnp.zeros((N_HEADS, LANES), F32)
    for j in range(HEADS_PER_GROUP):
        qm = jnp.where(head % HEADS_PER_GROUP == j, q32[:, j * LANES:(j + 1) * LANES], qm)
    qm = jnp.where(low_group, jnp.where(low_lanes, qm, 0.0), jnp.where(low_lanes, 0.0, qm)).astype(BF16)

    gate_row = gate_ref[...]
    gate_col = [jnp.sum(jnp.where(lane == head + br * N_HEADS, gate_row, 0.0), axis=-1, keepdims=True)
                for br in range(3)]
    kv_new = kvb_ref[...].astype(F32)
    first_row = lax.broadcasted_iota(jnp.int32, (tk, 1), 0) == 0

    def new_key_tile(c0):
        return jnp.where(first_row, kv_new[:, c0:c0 + LANES], 0.0).astype(BF16)

    nblk = lax.broadcasted_iota(jnp.int32, (1, nbp), 1)
    cvalid = (nblk + 1) * NSA_BLOCK <= pos + 1
    sc = jnp.where(cvalid, _dot_nt(qm, kvc_ref[0].astype(BF16)), MASK_INIT)
    e = jnp.exp2(sc - jnp.max(sc, axis=-1, keepdims=True))
    pc = jnp.where(cvalid, e / jnp.sum(e, axis=-1, keepdims=True), 0.0)
    o_cmp = _dot(pc.astype(BF16), kvc_ref[1].astype(BF16))

    imp = jnp.where(low_group,
                    jnp.sum(jnp.where(low_group, pc, 0.0), axis=0, keepdims=True),
                    jnp.sum(jnp.where(low_group, 0.0, pc), axis=0, keepdims=True))
    cur = pos // NSA_BLOCK
    forced = HEADS_PER_GROUP + 1.0
    imp = jnp.where(nblk == 0, forced, jnp.where(nblk == cur, forced, jnp.where(nblk == cur - 1, forced, imp)))
    imp = jnp.where(nblk * NSA_BLOCK <= pos, imp, -1.0)
    sel = _select_blocks(imp, nblk, NSA_TOP_N)

    def sel_bias(n):
        return jnp.where(sel[:, n:n + 1] > 0.5, 0.0, MASK_BIAS)

    s_tiles, v_tiles = [], []
    blocks_per_page = tk // NSA_BLOCK
    for k in range(n_pages):
        page = pages[k]
        s = _dot(qm, page[:GD_ROWS, :].astype(BF16))
        bias = sel_bias(k * blocks_per_page + blocks_per_page - 1)
        for b in range(blocks_per_page - 2, -1, -1):
            bias = jnp.where(lane < (b + 1) * NSA_BLOCK, sel_bias(k * blocks_per_page + b), bias)
        s_tiles.append(s + bias)
        v_tiles.append((page[GD_ROWS:, :].astype(BF16), True))
    s_new = _dot_nt(qm, new_key_tile(KV_BRANCH_COLS))
    s_tiles.append(s_new + jnp.where(lane == 0, sel_bias(n_pages * blocks_per_page), MASK_BIAS))
    v_tiles.append((new_key_tile(KV_BRANCH_COLS + LANES), False))
    o_slc = _softmax_pv(s_tiles, v_tiles)

    n_win = win_ref.shape[1]
    s_tiles, v_tiles = [], []
    for k in range(n_win // tk):
        diff = n_win - (k * tk + lane)
        bias = jnp.where(diff < NSA_WINDOW, 0.0, MASK_BIAS)
        s_tiles.append(_dot(qm, win_ref[:GD_ROWS, k * tk:(k + 1) * tk].astype(BF16)) + bias)
        v_tiles.append((win_ref[GD_ROWS:, k * tk:(k + 1) * tk].astype(BF16), True))
    s_tiles.append(_dot_nt(qm, new_key_tile(2 * KV_BRANCH_COLS)) + jnp.where(lane == 0, 0.0, MASK_BIAS))
    v_tiles.append((new_key_tile(2 * KV_BRANCH_COLS + LANES), False))
    o_win = _softmax_pv(s_tiles, v_tiles)

    o = gate_col[0] * o_cmp + gate_col[1] * o_slc + gate_col[2] * o_win
    for j in range(HEADS_PER_GROUP):
        pair = jnp.where(low_lanes, o[j:j + 1], o[j + HEADS_PER_GROUP:j + HEADS_PER_GROUP + 1])
        o_ref[:, j * LANES:(j + 1) * LANES] = pair.astype(o_ref.dtype)

    last = lax.broadcasted_iota(jnp.int32, (1, n_win), 1) == n_win - 1
    wnext_ref[...] = jnp.where(last, wnew_ref[...], pltpu.roll(win_ref[...], n_win - 1, 1))


def _attn_decode(q, kvb, gates, kvc, cache_slc, win_state, win_new, page_table, pos):
    n_dec, n_pages = page_table.shape
    nbp = kvc.shape[2]
    n_win = win_state.shape[2]
    seqs = DECODE_SEQS_PER_STEP
    per_seq = lambda a: pl.BlockSpec((seqs, 1, a.shape[-1]), lambda s, pt: (s, 0, 0))
    win_spec = pl.BlockSpec((seqs, KV_BRANCH_COLS, n_win), lambda s, pt: (s, 0, 0))
    q3, kvb3, g3 = (a.reshape(n_dec, 1, a.shape[-1]) for a in (q, kvb, gates))
    out, win_next = pl.pallas_call(
        functools.partial(_attn_decode_kernel, n_pages=n_pages, seqs=seqs, pos=pos, nbp=nbp),
        grid_spec=pltpu.PrefetchScalarGridSpec(
            num_scalar_prefetch=1,
            grid=(n_dec // seqs,),
            in_specs=_page_specs(n_pages, seqs) + [
                per_seq(q3), per_seq(kvb3), per_seq(g3),
                pl.BlockSpec((2, seqs, nbp, LANES), lambda s, pt: (0, s, 0, 0)),
                win_spec, pl.BlockSpec((seqs, KV_BRANCH_COLS, 1), lambda s, pt: (s, 0, 0)),
            ],
            out_specs=[pl.BlockSpec((seqs, 1, Q_COLS), lambda s, pt: (s, 0, 0)), win_spec],
        ),
        out_shape=[jax.ShapeDtypeStruct((n_dec, 1, Q_COLS), BF16), jax.ShapeDtypeStruct(win_state.shape, F32)],
        compiler_params=_params("arbitrary"),
        name="attn_decode",
    )(page_table, *([cache_slc] * (seqs * n_pages)), q3, kvb3, g3, kvc, win_state, win_new)
    return out.reshape(n_dec, Q_COLS), win_next


def _mix_kernel(*refs, tm, halo, tiles_per_seq):
    if halo:
        (x_ref, xh_ref, o_ref, shift_ref, scale_ref, gate_ref, gpre_ref, gpost_ref, cw_ref,
         wconv_ref, wmerge_ref, wup_ref, wco_ref, wout_ref, y_ref, ulast_ref, u_scr) = refs
        x = x_ref[...]
        xe = jnp.concatenate([xh_ref[...], x], axis=0)
    else:
        (x_ref, um1_ref, um2_ref, o_ref, shift_ref, scale_ref, gate_ref, gpre_ref, gpost_ref, cw_ref,
         wconv_ref, wmerge_ref, wup_ref, wco_ref, wout_ref, y_ref, ulast_ref) = refs
        x = x_ref[...]
        xe = x
    h0 = xe.shape[0] - tm
    a = (_rms(xe, gpre_ref[...]) * (1.0 + scale_ref[...]) + shift_ref[...]).astype(BF16)
    zc = _dot(a, wconv_ref[...])
    u = zc[:, 2 * D_CONV:] * zc[:, :D_CONV]
    cb = zc[h0:, D_CONV:2 * D_CONV]
    if halo:
        keep_halo = jnp.where(pl.program_id(0) % tiles_per_seq == 0, 0.0, 1.0)
        rows = lax.broadcasted_iota(jnp.int32, (xe.shape[0], 1), 0)
        u = jnp.where(rows < h0, u * keep_halo, u)
        u_scr[...] = u
        um1 = u_scr[h0 - 1:h0 - 1 + tm, :]
        um2 = u_scr[h0 - 2:h0 - 2 + tm, :]
        u0 = u[h0:]
        ulast_ref[...] = u[tm:]
    else:
        um1, um2, u0 = um1_ref[...], um2_ref[...], u
        ulast_ref[...] = u
    cw = cw_ref[...]
    y = cw[0:1] * um2 + cw[1:2] * um1 + cw[2:3] * u0
    conv_out = _dot((cb * y).astype(BF16), wco_ref[...])
    attn_out = _dot_tn(o_ref[...], wup_ref[...])
    mg = jax.nn.sigmoid(_dot(a[h0:], wmerge_ref[...]))
    merged = mg[:, :D_MODEL] * attn_out + mg[:, D_MODEL:] * conv_out
    mixed = _dot(merged.astype(BF16), wout_ref[...])
    y_ref[...] = x + gate_ref[...] * _rms(mixed, gpost_ref[...])


def _mix(x, o_att, mod, g3, conv_w, weights, tm, tiles_per_seq=None, prev=None):
    m = x.shape[0]
    halo = prev is None
    h0 = MIX_HALO
    row = lambda n: pl.BlockSpec((tm, n), lambda i: (i, 0))
    full = lambda a: pl.BlockSpec(a.shape, lambda i: (0,) * a.ndim)
    att = pl.BlockSpec((Q_COLS, tm), lambda i: (0, i))
    common = [att, mod.spec(3), mod.spec(4), mod.spec(5), _gspec(2), _gspec(3), full(conv_w)]
    common += [full(w) for w in weights]
    common_args = [o_att, mod.arr, mod.arr, mod.arr, g3, g3, conv_w, *weights]
    if halo:
        hb = tm // h0
        in_specs = [row(D_MODEL), pl.BlockSpec((h0, D_MODEL), lambda i: (jnp.maximum(i * hb - 1, 0), 0))] + common
        args = [x, x] + common_args
        scratch = [pltpu.VMEM((tm + h0, D_CONV), F32)]
        ulast = (jax.ShapeDtypeStruct((m // tm * h0, D_CONV), F32), pl.BlockSpec((h0, D_CONV), lambda i: (i, 0)))
    else:
        in_specs = [row(D_MODEL), row(D_CONV), row(D_CONV)] + common
        args = [x, prev[0], prev[1]] + common_args
        scratch = []
        ulast = (jax.ShapeDtypeStruct((m, D_CONV), F32), row(D_CONV))
    return pl.pallas_call(
        functools.partial(_mix_kernel, tm=tm, halo=halo, tiles_per_seq=tiles_per_seq),
        grid=(m // tm,),
        in_specs=in_specs,
        out_specs=[row(D_MODEL), ulast[1]],
        out_shape=[jax.ShapeDtypeStruct((m, D_MODEL), F32), ulast[0]],
        scratch_shapes=scratch,
        compiler_params=_params("parallel"),
        name="mix",
    )(*args)


def _head_pair_perm():
    order = []
    for j in range(HEADS_PER_GROUP):
        for g in range(N_KV_HEADS):
            head = g * HEADS_PER_GROUP + j
            order.extend(range(head * HEAD_DIM, (head + 1) * HEAD_DIM))
    return jnp.array(order, dtype=jnp.int32)


def _layer_weights(w_in, w_attn_up, w_conv_out, w_out):
    perm = _head_pair_perm()
    o1 = Q_COLS
    o2 = o1 + KV_COLS
    o3 = o2 + NSA_GATE_COLS
    o4 = o3 + CONV_COLS
    w_q = w_in[:, :o1][:, perm]
    w_g = jnp.pad(w_in[:, o2:o3], ((0, 0), (0, GATE_PAD - NSA_GATE_COLS)))
    w_qkvg = jnp.concatenate([w_q, w_in[:, o1:o2], w_g], axis=1).astype(BF16)
    mix_w = (w_in[:, o3:o4].astype(BF16), w_in[:, o4:].astype(BF16), w_attn_up[perm].astype(BF16),
             w_conv_out.astype(BF16), w_out.astype(BF16))
    return w_qkvg, mix_w


def kernel(x_prompt, x_sample, c_prompt, c_sample, cache_cmp_kv, cache_slc_kv, state_win_kv, state_conv,
           page_table, w_ada, b_ada, g_norm, w_ffn_gu, w_ffn_down, w_in, phi_pe, phi_w1, phi_b1, phi_w2,
           w_attn_up, conv_w, w_conv_out, w_out):
    n_seq, seq, _ = x_prompt.shape
    n_dec = x_sample.shape[0]
    assert w_ada.shape[0] == 1
    l = 0

    mod_all = _ada(jnp.concatenate([c_prompt, c_sample], axis=0), w_ada[l], b_ada[l])
    g3 = g_norm[l].reshape(6, 1, D_MODEL)
    w_gu = [w_ffn_gu[l, k].astype(BF16) for k in range(2)]
    w_down = [w_ffn_down[l, k].astype(BF16) for k in range(2)]
    w_qkvg, mix_w = _layer_weights(w_in[l], w_attn_up[l], w_conv_out[l], w_out[l])
    cwp = _compress_weights_paged(phi_pe[l], phi_w1[l], phi_b1[l], phi_w2[l])

    tm = 512
    tq = 256
    m = n_seq * seq
    mod_p = _Mod(mod_all[:n_seq], per_row=False, tiles_per_seq=seq // tm)
    x0 = x_prompt.reshape(m, D_MODEL)
    x1 = _ffn(x0, mod_p, g3, w_gu[0], w_down[0], 0, tm)
    tables = _rope_tables(jnp.arange(seq, dtype=jnp.int32))
    _, qt, cmp_pages, cmpt, slct, wint, kvtb, kvb, _, gatest = _proj(x1, mod_p, g3, tables, w_qkvg, tm)
    nb = seq // NSA_BLOCK
    kvc = _gather_compress(cmp_pages, None, cwp, pages_per_seq=seq // PAGE_SIZE)
    kc = kvc[0].reshape(n_seq * nb, LANES).astype(BF16)
    vct = kvc[1].transpose(0, 2, 1).astype(BF16)
    o_att = _attn_prompt(qt, gatest, kc, vct, kvb, kvtb, n_seq, seq, tq)
    x2, ulast = _mix(x1, o_att, mod_p, g3, conv_w[l], mix_w, tm, tiles_per_seq=seq // tm)
    y_prompt = _ffn(x2, mod_p, g3, w_gu[1], w_down[1], 2, tm).reshape(n_seq, seq, D_MODEL)

    def token_major(t):
        n, _, tt = t.shape
        return t.reshape(n, 2, N_KV_HEADS, HEAD_DIM, tt).transpose(0, 4, 1, 2, 3)[None]

    def feature_major(a):
        n, tt = a.shape[:2]
        return a.transpose(0, 2, 3, 4, 1).reshape(n, KV_BRANCH_COLS, tt)

    cmp_kv_prompt = token_major(cmpt)
    slc_kv_prompt = token_major(slct)
    win_keep = min(NSA_WINDOW, seq)
    win_kv_prompt = token_major(wint[:, :, seq - win_keep:])
    conv_prompt = ulast.reshape(n_seq, seq // tm, MIX_HALO, D_CONV)[:, -1, MIX_HALO - (CONV_WIDTH - 1):][None]

    n_pages = page_table.shape[1]
    past_len = n_pages * PAGE_SIZE
    pos_s = past_len + jnp.arange(x_sample.shape[1], dtype=jnp.int32)
    assert x_sample.shape[1] == 1 and past_len % NSA_BLOCK == 0 and state_win_kv.shape[2] <= past_len
    mod_s = _Mod(mod_all[n_seq:], per_row=True)
    xs1 = _ffn(x_sample.reshape(n_dec, D_MODEL), mod_s, g3, w_gu[0], w_down[0], 0, n_dec)
    tables_s = _rope_tables(jnp.broadcast_to(pos_s, (n_dec,)))
    q_s, _, _, cmpt_s, slct_s, wint_s, _, kvb_s, gates_s, _ = _proj(xs1, mod_s, g3, tables_s, w_qkvg, n_dec)
    nb_past = past_len // NSA_BLOCK
    nb_pad = -(-(nb_past + 1) // NSA_BLOCK) * NSA_BLOCK
    kvc_past = _gather_compress(feature_major(cache_cmp_kv[l]), page_table, cwp)
    new_pages = jnp.pad(cmpt_s[0].T[:, :, None], ((0, 0), (0, 0), (0, PAGE_SIZE - 1)))
    kvc_new = _gather_compress(new_pages, None, cwp, pages_per_seq=n_pages).reshape(2, n_dec, -1, LANES)[:, :, :1]
    kvc_s = jnp.concatenate(
        [kvc_past, kvc_new, jnp.zeros((2, n_dec, nb_pad - nb_past - 1, LANES), F32)], axis=2)
    win_state = feature_major(state_win_kv[l])
    assert win_state.shape[2] == NSA_WINDOW
    o_att_s, win_next = _attn_decode(q_s, kvb_s, gates_s, kvc_s, feature_major(cache_slc_kv[l]), win_state,
                                     wint_s[0].T[:, :, None], page_table, past_len)
    conv_state = state_conv[l]
    xs2, u_s = _mix(xs1, o_att_s.T, mod_s, g3, conv_w[l], mix_w, n_dec,
                    prev=(conv_state[:, CONV_WIDTH - 2], conv_state[:, CONV_WIDTH - 3]))
    y_sample = _ffn(xs2, mod_s, g3, w_gu[1], w_down[1], 2, n_dec).reshape(x_sample.shape)

    kvs = (1, n_dec, 1, 2, N_KV_HEADS, HEAD_DIM)
    conv_sample = jnp.concatenate([conv_state[:, 1:], u_s[:, None, :]], axis=1)[None]
    return (y_prompt, y_sample, cmp_kv_prompt, slc_kv_prompt, win_kv_prompt, conv_prompt,
            token_major(cmpt_s).reshape(kvs), token_major(slct_s).reshape(kvs), token_major(win_next), conv_sample)
```

```python
import functools

import jax
import jax.numpy as jnp
from jax import lax
from jax.experimental import pallas as pl
from jax.experimental.pallas import tpu as pltpu

D_MODEL = 1024
N_HEADS = 8
HEAD_DIM = 64
N_KV_HEADS = 2
HEADS_PER_GROUP = N_HEADS // N_KV_HEADS
ROT_DIM = HEAD_DIM // 4
ROPE_THETA = 500000.0
NSA_BLOCK = 64
NSA_TOP_N = 16
NSA_WINDOW = 512
PHI_HIDDEN = 2 * HEAD_DIM
D_CONV = 512
CONV_WIDTH = 3
D_FF = 2816
FFN_HALF = 0.5
NORM_EPS = 1e-6
PAGE_SIZE = 128
Q_COLS = N_HEADS * HEAD_DIM
KV_BRANCH_COLS = 2 * N_KV_HEADS * HEAD_DIM
KV_COLS = 3 * KV_BRANCH_COLS
NSA_GATE_COLS = 3 * N_HEADS
CONV_COLS = 3 * D_CONV
MERGE_COLS = 2 * D_MODEL
GD_ROWS = N_KV_HEADS * HEAD_DIM

LANES = 128
BF16_SUBLANES = 16
GATE_PAD = LANES
QKVG_COLS = Q_COLS + KV_COLS + GATE_PAD
LOG2_E = 1.4426950408889634
MASK_INIT = -1e30
MASK_BIAS = -2e30
VMEM_LIMIT = 56 * 1024 * 1024
SLAB_PITCH = HEAD_DIM + 8
FFN_CHUNKS = 2
DECODE_SEQS_PER_STEP = 4
MIX_HALO = 16

BF16 = jnp.bfloat16
F32 = jnp.float32


def _dot(a, b):
    return jnp.dot(a, b, preferred_element_type=F32)


def _dot_tn(a, b):
    return lax.dot_general(a, b, (((0,), (0,)), ((), ())), preferred_element_type=F32)


def _dot_nt(a, b):
    return lax.dot_general(a, b, (((1,), (1,)), ((), ())), preferred_element_type=F32)


def _rms(x, g):
    return x * lax.rsqrt(jnp.mean(x * x, axis=-1, keepdims=True) + NORM_EPS) * g


def _silu(x):
    return x * jax.nn.sigmoid(x)


def _params(*sem):
    return pltpu.CompilerParams(dimension_semantics=sem, vmem_limit_bytes=VMEM_LIMIT)


def _ada_kernel(c_ref, w_ref, b_ref, o_ref):
    c = _silu(c_ref[...]).astype(BF16)
    o_ref[...] = _dot(c, w_ref[...].astype(BF16)) + b_ref[...]


def _ada(c, w_ada, b_ada):
    rows = c.shape[0]
    n = w_ada.shape[1]
    tn = 9 * LANES
    return pl.pallas_call(
        _ada_kernel,
        grid=(n // tn,),
        in_specs=[
            pl.BlockSpec((rows, D_MODEL), lambda j: (0, 0)),
            pl.BlockSpec((D_MODEL, tn), lambda j: (0, j)),
            pl.BlockSpec((1, tn), lambda j: (0, j)),
        ],
        out_specs=pl.BlockSpec((rows, tn), lambda j: (0, j)),
        out_shape=jax.ShapeDtypeStruct((rows, n), F32),
        compiler_params=_params("parallel"),
        name="ada",
    )(c, w_ada, b_ada.reshape(1, n))


class _Mod:
    def __init__(self, mod, per_row, tiles_per_seq=1):
        self.per_row = per_row
        self.tiles_per_seq = tiles_per_seq
        self.arr = mod if per_row else mod.reshape(mod.shape[0] * 9, 1, D_MODEL)

    def spec(self, k):
        if self.per_row:
            return pl.BlockSpec((self.arr.shape[0], D_MODEL), lambda i, *_: (0, k))
        tps = self.tiles_per_seq
        return pl.BlockSpec((None, 1, D_MODEL), lambda i, *_: ((i // tps) * 9 + k, 0, 0))


def _gspec(k):
    return pl.BlockSpec((None, 1, D_MODEL), lambda i, *_: (k, 0, 0))


def _ffn_kernel(x_ref, shift_ref, scale_ref, gate_ref, gpre_ref, gpost_ref, wgu_ref, wd_ref, o_ref,
                *, n_sub, n_chunk):
    ts = x_ref.shape[0] // n_sub
    tf = D_FF // n_chunk
    for s in range(n_sub):
        rows = slice(s * ts, (s + 1) * ts)
        mod = lambda ref: ref[...] if ref.shape[0] == 1 else ref[rows, :]
        x = x_ref[rows, :]
        a = (_rms(x, gpre_ref[...]) * (1.0 + mod(scale_ref)) + mod(shift_ref)).astype(BF16)
        y = None
        for c in range(n_chunk):
            g = _dot(a, wgu_ref[:, c * tf:(c + 1) * tf])
            u = _dot(a, wgu_ref[:, D_FF + c * tf:D_FF + (c + 1) * tf])
            d = _dot((_silu(g) * u).astype(BF16), wd_ref[c * tf:(c + 1) * tf, :])
            y = d if y is None else y + d
        o_ref[rows, :] = x + FFN_HALF * mod(gate_ref) * _rms(y, gpost_ref[...])


def _ffn(x, mod, g3, w_gu, w_down, sub, tm, n_sub=1):
    m = x.shape[0]
    row = pl.BlockSpec((tm, D_MODEL), lambda i: (i, 0))
    const = lambda a: pl.BlockSpec(a.shape, lambda i: (0,) * a.ndim, pipeline_mode=pl.Buffered(1))
    return pl.pallas_call(
        functools.partial(_ffn_kernel, n_sub=n_sub, n_chunk=FFN_CHUNKS),
        grid=(m // tm,),
        in_specs=[
            row, mod.spec(3 * sub), mod.spec(3 * sub + 1), mod.spec(3 * sub + 2),
            _gspec(2 * sub), _gspec(2 * sub + 1), const(w_gu), const(w_down),
        ],
        out_specs=row,
        out_shape=jax.ShapeDtypeStruct((m, D_MODEL), F32),
        compiler_params=_params("parallel"),
        name=f"ffn{sub}",
    )(x, mod.arr, mod.arr, mod.arr, g3, g3, w_gu, w_down)


def _rope(x, cos, sin_lo, sin_hi):
    return x * cos + pltpu.roll(x, LANES - ROT_DIM // 2, 1) * sin_lo + pltpu.roll(x, ROT_DIM // 2, 1) * sin_hi


def _proj_kernel(x_ref, shift_ref, scale_ref, g_ref, cos_ref, slo_ref, shi_ref, w_ref,
                 q_ref, qt_ref, cmp_ref, cmpt_ref, slct_ref, wint_ref, kvtb_ref, kvb_ref, gate_ref, gatet_ref):
    a = (_rms(x_ref[...], g_ref[...]) * (1.0 + scale_ref[...]) + shift_ref[...]).astype(BF16)
    z = _dot(a, w_ref[...])
    cos, slo, shi = cos_ref[...], slo_ref[...], shi_ref[...]
    scale = HEAD_DIM ** -0.5 * LOG2_E
    for s in range(Q_COLS // LANES):
        q = _rope(z[:, s * LANES:(s + 1) * LANES], cos, slo, shi) * scale
        q_ref[:, s * LANES:(s + 1) * LANES] = q.astype(BF16)
        qt_ref[s * LANES:(s + 1) * LANES, :] = q.T.astype(BF16)
    kvt_refs = (cmpt_ref, slct_ref, wint_ref)
    for br in range(3):
        c0 = Q_COLS + br * KV_BRANCH_COLS
        k = _rope(z[:, c0:c0 + LANES], cos, slo, shi)
        v = z[:, c0 + LANES:c0 + 2 * LANES]
        for r0, x in ((0, k), (LANES, v)):
            xt = x.T
            kvt_refs[br][r0:r0 + LANES, :] = xt
            kvtb_ref[br, r0:r0 + LANES, :] = xt.astype(BF16)
            if br == 0:
                for p in range(cmp_ref.shape[0]):
                    cmp_ref[p, r0:r0 + LANES, :] = xt[:, p * PAGE_SIZE:(p + 1) * PAGE_SIZE]
        kvb_ref[:, br * KV_BRANCH_COLS:br * KV_BRANCH_COLS + LANES] = k.astype(BF16)
        kvb_ref[:, br * KV_BRANCH_COLS + LANES:(br + 1) * KV_BRANCH_COLS] = v.astype(BF16)
    gate = jax.nn.sigmoid(z[:, Q_COLS + KV_COLS:])
    gate_ref[...] = gate
    gatet_ref[...] = gate.T


def _proj(x, mod, g3, tables, w_qkvg, tm):
    assert tm % PAGE_SIZE == 0
    m = x.shape[0]
    t_rows = tables[0].shape[0]
    nt = t_rows // tm
    n_seq = m // t_rows
    row = lambda n: pl.BlockSpec((tm, n), lambda i: (i, 0))
    tab = pl.BlockSpec((tm, LANES), lambda i: (i % nt, 0))
    col = lambda n: pl.BlockSpec((n, tm), lambda i: (0, i))
    kvt_spec = pl.BlockSpec((None, KV_BRANCH_COLS, tm), lambda i: (i // nt, 0, i % nt))
    kvtb_spec = pl.BlockSpec((3, None, KV_BRANCH_COLS, tm), lambda i: (0, i // nt, 0, i % nt))
    return pl.pallas_call(
        _proj_kernel,
        grid=(m // tm,),
        in_specs=[row(D_MODEL), mod.spec(3), mod.spec(4), _gspec(2), tab, tab, tab,
                  pl.BlockSpec((D_MODEL, QKVG_COLS), lambda i: (0, 0))],
        out_specs=[row(Q_COLS), col(Q_COLS),
                   pl.BlockSpec((tm // PAGE_SIZE, KV_BRANCH_COLS, PAGE_SIZE), lambda i: (i, 0, 0)),
                   kvt_spec, kvt_spec, kvt_spec, kvtb_spec, row(KV_COLS), row(GATE_PAD), col(GATE_PAD)],
        out_shape=[
            jax.ShapeDtypeStruct((m, Q_COLS), BF16),
            jax.ShapeDtypeStruct((Q_COLS, m), BF16),
            jax.ShapeDtypeStruct((m // PAGE_SIZE, KV_BRANCH_COLS, PAGE_SIZE), F32),
            *[jax.ShapeDtypeStruct((n_seq, KV_BRANCH_COLS, t_rows), F32)] * 3,
            jax.ShapeDtypeStruct((3, n_seq, KV_BRANCH_COLS, t_rows), BF16),
            jax.ShapeDtypeStruct((m, KV_COLS), BF16),
            jax.ShapeDtypeStruct((m, GATE_PAD), F32),
            jax.ShapeDtypeStruct((GATE_PAD, m), F32),
        ],
        compiler_params=_params("parallel"),
        name="proj",
    )(x, mod.arr, mod.arr, g3, *tables, w_qkvg)


def _rope_tables(pos):
    half = ROT_DIM // 2
    inv = jnp.power(ROPE_THETA, -jnp.arange(half, dtype=F32) * 2.0 / ROT_DIM)
    ang = pos.astype(F32)[:, None] * inv[None, :]
    cos, sin = jnp.cos(ang), jnp.sin(ang)
    n = pos.shape[0]
    ones = jnp.ones((n, HEAD_DIM - ROT_DIM), F32)
    zeros = jnp.zeros((n, HEAD_DIM - ROT_DIM), F32)
    zh = jnp.zeros((n, half), F32)
    c = jnp.concatenate([cos, cos, ones], axis=1)
    lo = jnp.concatenate([-sin, zh, zeros], axis=1)
    hi = jnp.concatenate([zh, sin, zeros], axis=1)
    return tuple(jnp.tile(t, (1, LANES // HEAD_DIM)) for t in (c, lo, hi))


def _compress_weights_paged(phi_pe, phi_w1, phi_b1, phi_w2):
    blocks = PAGE_SIZE // NSA_BLOCK
    eye = jnp.eye(blocks, dtype=F32)
    w1 = phi_w1.reshape(2, NSA_BLOCK, HEAD_DIM, PHI_HIDDEN)
    w1 = jnp.einsum("bc,ktdj->kdbtcj", eye, w1).reshape(2, HEAD_DIM * PAGE_SIZE, blocks * PHI_HIDDEN)
    w2 = jnp.einsum("bc,kjd->kbjcd", eye, phi_w2).reshape(2, blocks * PHI_HIDDEN, blocks * HEAD_DIM)
    pe = jnp.tile(phi_pe.transpose(0, 2, 1), (1, 1, blocks)).reshape(2, HEAD_DIM, 1, PAGE_SIZE)
    b1 = jnp.tile(phi_b1, (1, blocks)).reshape(2, 1, blocks * PHI_HIDDEN)
    return pe, w1.astype(BF16), b1, w2.astype(BF16)


def _gather_compress_kernel(pt_ref, *refs, n_pages, group, steps, contiguous):
    del pt_ref
    if contiguous:
        pages = [refs[0].at[k] for k in range(n_pages)]
        refs = refs[1:]
    else:
        pages, refs = refs[:n_pages], refs[n_pages:]
    pe_ref, w1_ref, b1_ref, w2_ref, o_ref, xk_scr, xv_scr, c_scr = refs
    x_scr = (xk_scr, xv_scr)
    slot = pl.program_id(0) % steps
    for k in range(n_pages):
        for kv in range(2):
            for g in range(N_KV_HEADS):
                r0 = pl.multiple_of(((slot * n_pages + k) * N_KV_HEADS + g) * SLAB_PITCH, SLAB_PITCH)
                s0 = kv * GD_ROWS + g * HEAD_DIM
                x_scr[kv][pl.ds(r0, HEAD_DIM), :] = pages[k][s0:s0 + HEAD_DIM, :]

    @pl.when(slot == steps - 1)
    def _():
        rows = group * N_KV_HEADS
        for kv in range(2):
            acc = jnp.zeros((rows, w1_ref.shape[2]), F32)
            for d in range(0, HEAD_DIM, 2):
                xs = []
                for dd in (d, d + 1):
                    x = x_scr[kv][pl.ds(dd, rows, stride=SLAB_PITCH), :]
                    xs.append((x + pe_ref[kv, dd]).astype(BF16))
                acc = acc + _dot(jnp.concatenate(xs, axis=1), w1_ref[kv, d * PAGE_SIZE:(d + 2) * PAGE_SIZE, :])
            h = _silu(acc + b1_ref[kv])
            c_scr[...] = _dot(h.astype(BF16), w2_ref[kv])
            g0 = c_scr[pl.ds(0, rows // 2, stride=2), :]
            g1 = c_scr[pl.ds(1, rows // 2, stride=2), :]
            low = lax.broadcasted_iota(jnp.int32, (1, LANES), 1) < HEAD_DIM
            o_ref.at[kv][pl.ds(0, rows // 2, stride=2), :] = jnp.where(low, g0, pltpu.roll(g1, HEAD_DIM, 1))
            o_ref.at[kv][pl.ds(1, rows // 2, stride=2), :] = jnp.where(low, pltpu.roll(g0, HEAD_DIM, 1), g1)


def _page_specs(n_pages, seqs=1):
    return [pl.BlockSpec((None, KV_BRANCH_COLS, PAGE_SIZE), lambda s, pt, r=r, k=k: (pt[s * seqs + r, k], 0, 0))
            for r in range(seqs) for k in range(n_pages)]


def _gather_compress(cache_t, page_table, cwp, pages_per_seq=None):
    pe, w1, b1, w2 = cwp
    contiguous = page_table is None
    seqs = 4
    if contiguous:
        n_pages = pages_per_seq
        n_dec = cache_t.shape[0] // n_pages
        page_table = jnp.zeros((1, 1), jnp.int32)
        page_specs = [pl.BlockSpec((seqs * n_pages, KV_BRANCH_COLS, PAGE_SIZE), lambda s, pt: (s, 0, 0))]
        page_args = [cache_t]
    else:
        n_dec, n_pages = page_table.shape
        page_specs = _page_specs(n_pages, seqs)
        page_args = [cache_t] * (seqs * n_pages)
    steps = 2
    group = steps * seqs * n_pages
    assert n_dec % (steps * seqs) == 0
    rows = group * N_KV_HEADS
    blocks = PAGE_SIZE // NSA_BLOCK
    assert blocks == 2 and N_KV_HEADS == 2
    const = lambda a: pl.BlockSpec(a.shape, lambda s, pt: (0,) * a.ndim, pipeline_mode=pl.Buffered(1))
    out = pl.pallas_call(
        functools.partial(_gather_compress_kernel, n_pages=seqs * n_pages, group=group, steps=steps,
                          contiguous=contiguous),
        grid_spec=pltpu.PrefetchScalarGridSpec(
            num_scalar_prefetch=1,
            grid=(n_dec // seqs,),
            in_specs=page_specs + [const(pe), const(w1), const(b1), const(w2)],
            out_specs=pl.BlockSpec((2, rows, blocks * HEAD_DIM), lambda s, pt: (0, s // steps, 0)),
            scratch_shapes=[pltpu.VMEM((group * N_KV_HEADS * SLAB_PITCH, PAGE_SIZE), F32)] * 2
            + [pltpu.VMEM((rows, LANES), F32)],
        ),
        out_shape=jax.ShapeDtypeStruct((2, n_dec * n_pages * N_KV_HEADS, blocks * HEAD_DIM), F32),
        compiler_params=_params("arbitrary"),
        name="gather_compress",
    )(page_table, *page_args, pe, w1, b1, w2)
    return out.reshape(2, n_dec, n_pages * blocks, N_KV_HEADS * HEAD_DIM)


def _group_queries_t(qt, g):
    t = qt.shape[1]
    zeros = jnp.zeros((HEAD_DIM, t), qt.dtype)
    cols = []
    for j in range(HEADS_PER_GROUP):
        qh = qt[j * LANES + g * HEAD_DIM:j * LANES + (g + 1) * HEAD_DIM, :]
        cols.append(jnp.concatenate([qh, zeros] if g == 0 else [zeros, qh], axis=0))
    return jnp.concatenate(cols, axis=1)


def _tile_heads(x):
    return jnp.concatenate([x] * HEADS_PER_GROUP, axis=1)


def _select_blocks_t(imp, nblk, n_sel):
    nb = imp.shape[0]
    rank = jnp.zeros(imp.shape, F32)
    for m in range(nb):
        row = imp[m:m + 1, :]
        beats = jnp.where(row > imp, 1.0, jnp.where(row == imp, jnp.where(m < nblk, 1.0, 0.0), 0.0))
        rank = rank + beats
    return jnp.where(rank < n_sel, jnp.where(imp >= 0.0, 1.0, 0.0), 0.0)


def _select_blocks(imp, nblk, n_sel):
    nb = imp.shape[1]
    rank = jnp.zeros(imp.shape, F32)
    for m in range(nb):
        col = imp[:, m:m + 1]
        beats = jnp.where(col > imp, 1.0, jnp.where(col == imp, jnp.where(m < nblk, 1.0, 0.0), 0.0))
        rank = rank + beats
    return jnp.where(rank < n_sel, jnp.where(imp >= 0.0, 1.0, 0.0), 0.0)


def _attn_tile(carry, k, q, vt_ref, ks, tk, bias):
    m, l, acc = carry
    s = _dot(k, q)
    if bias is not None:
        s = s + bias
    m_new = jnp.maximum(m, jnp.max(s, axis=0, keepdims=True))
    alpha = jnp.exp2(m - m_new)
    pb = jnp.exp2(s - m_new).astype(BF16)
    half = pb.shape[1] // N_KV_HEADS
    ones = jnp.ones((BF16_SUBLANES, tk), BF16)
    pv = [_dot(jnp.concatenate([vt_ref[GD_ROWS + g * HEAD_DIM:GD_ROWS + (g + 1) * HEAD_DIM, pl.ds(ks, tk)], ones],
                               axis=0), pb[:, g * half:(g + 1) * half]) for g in range(N_KV_HEADS)]
    pv = jnp.concatenate(pv, axis=1)
    return m_new, alpha * l + pv[HEAD_DIM:HEAD_DIM + 1], alpha * acc + pv[:HEAD_DIM]


def _maybe(cond, fn, carry):
    return lax.fori_loop(0, jnp.where(cond, 1, 0), lambda _, c: fn(c), carry)


def _attn_kernel(qt_ref, gatet_ref, kc_ref, vct_ref, slc_k_ref, slc_vt_ref, win_k_ref, win_vt_ref, o_ref,
                 *, tq, nb):
    i = pl.program_id(1)
    p0 = i * tq
    tk = tq
    hq = HEADS_PER_GROUP * tq
    n_win_tiles = NSA_WINDOW // tk
    pos = p0 + lax.broadcasted_iota(jnp.int32, (1, tq), 1)
    nblk = lax.broadcasted_iota(jnp.int32, (nb, 1), 0)
    key_off = lax.broadcasted_iota(jnp.int32, (tk, 1), 0)
    qt = qt_ref[...]
    q_all = jnp.concatenate([_group_queries_t(qt, g) for g in range(N_KV_HEADS)], axis=1)
    tile_all = lambda x: jnp.concatenate([x] * N_HEADS, axis=1)

    cvalid = (nblk + 1) * NSA_BLOCK <= tile_all(pos) + 1
    sc = jnp.where(cvalid, _dot(kc_ref[...], q_all), MASK_INIT)
    e = jnp.exp2(sc - jnp.max(sc, axis=0, keepdims=True))
    pc = jnp.where(cvalid, e / jnp.sum(e, axis=0, keepdims=True), 0.0)
    pcb = pc.astype(BF16)
    o_cmp = jnp.concatenate([_dot(vct_ref[g * HEAD_DIM:(g + 1) * HEAD_DIM, :], pcb[:, g * hq:(g + 1) * hq])
                             for g in range(N_KV_HEADS)], axis=1)

    cur = pos // NSA_BLOCK
    forced = HEADS_PER_GROUP + 1.0
    sel_rows = []
    for g in range(N_KV_HEADS):
        imp = functools.reduce(jnp.add, [pc[:, g * hq + j * tq:g * hq + (j + 1) * tq]
                                         for j in range(HEADS_PER_GROUP)])
        imp = jnp.where(nblk == 0, forced, jnp.where(nblk == cur, forced, jnp.where(nblk == cur - 1, forced, imp)))
        imp = jnp.where(nblk * NSA_BLOCK <= pos, imp, -1.0)
        sel = _select_blocks_t(imp, nblk, NSA_TOP_N)
        sel_rows.append(_tile_heads(jnp.where(sel > 0.5, 0.0, MASK_BIAS)))
    sel_rows = jnp.concatenate(sel_rows, axis=1)
    sel_rows = jnp.concatenate([sel_rows, jnp.zeros((LANES - nb, sel_rows.shape[1]), F32)], axis=0)
    q_sel = jnp.concatenate([q_all, sel_rows.astype(BF16)], axis=0)
    blk_lane = lax.broadcasted_iota(jnp.int32, (1, LANES), 1)

    def slc_keys(ks):
        onehot = jnp.where(blk_lane == (ks + key_off) // NSA_BLOCK, 1.0, 0.0).astype(BF16)
        return jnp.concatenate([slc_k_ref[pl.ds(ks, tk), :GD_ROWS], onehot], axis=1)

    cols = q_all.shape[1]
    init = (jnp.full((1, cols), MASK_INIT, F32), jnp.zeros((1, cols), F32), jnp.zeros((HEAD_DIM, cols), F32))
    causal = tile_all(jnp.where(p0 + key_off <= pos, 0.0, MASK_BIAS))
    k_diag = pl.multiple_of(p0, tk)

    def slc_body(j, c):
        ks = pl.multiple_of(j * tk, tk)
        return _attn_tile(c, slc_keys(ks), q_sel, slc_vt_ref, ks, tk, None)

    c = lax.fori_loop(0, i, slc_body, init)
    _, l, acc = _attn_tile(c, slc_keys(k_diag), q_sel, slc_vt_ref, k_diag, tk, causal)
    o_slc = acc / l

    def win_tile(c, ks, bias):
        return _attn_tile(c, win_k_ref[pl.ds(ks, tk), :GD_ROWS], q_all, win_vt_ref, ks, tk, bias)

    def edge_tile(c):
        ks = pl.multiple_of((i - n_win_tiles) * tk, tk)
        bias = tile_all(jnp.where(pos - (ks + key_off) < NSA_WINDOW, 0.0, MASK_BIAS))
        return win_tile(c, ks, bias)

    c = _maybe(i >= n_win_tiles, edge_tile, init)
    c = lax.fori_loop(jnp.maximum(i - n_win_tiles + 1, 0), i,
                      lambda j, c: win_tile(c, pl.multiple_of(j * tk, tk), None), c)
    _, l, acc = win_tile(c, k_diag, causal)
    o_win = acc / l

    gates = gatet_ref[...]
    for g in range(N_KV_HEADS):
        for j in range(HEADS_PER_GROUP):
            head = g * HEADS_PER_GROUP + j
            cs = slice(g * hq + j * tq, g * hq + (j + 1) * tq)
            o = (gates[head:head + 1] * o_cmp[:, cs]
                 + gates[N_HEADS + head:N_HEADS + head + 1] * o_slc[:, cs]
                 + gates[2 * N_HEADS + head:2 * N_HEADS + head + 1] * o_win[:, cs])
            r0 = j * LANES + g * HEAD_DIM
            o_ref[r0:r0 + HEAD_DIM, :] = o.astype(o_ref.dtype)


def _attn_prompt(qt, gatest, kc, vct, kvb, kvtb, n_seq, seq, tq):
    m = qt.shape[1]
    nb = seq // NSA_BLOCK
    nq = seq // tq
    assert seq % tq == 0 and NSA_WINDOW % tq == 0 and tq % NSA_BLOCK == 0 and nb <= LANES
    col = lambda n: pl.BlockSpec((n, tq), lambda b, i: (0, b * nq + i))
    return pl.pallas_call(
        functools.partial(_attn_kernel, tq=tq, nb=nb),
        grid=(n_seq, nq),
        in_specs=[col(Q_COLS), col(4 * N_HEADS),
                  pl.BlockSpec((nb, LANES), lambda b, i: (b, 0)),
                  pl.BlockSpec((None, LANES, nb), lambda b, i: (b, 0, 0)),
                  pl.BlockSpec((seq, KV_BRANCH_COLS), lambda b, i: (b, 1)),
                  pl.BlockSpec((None, None, KV_BRANCH_COLS, seq), lambda b, i: (1, b, 0, 0)),
                  pl.BlockSpec((seq, KV_BRANCH_COLS), lambda b, i: (b, 2)),
                  pl.BlockSpec((None, None, KV_BRANCH_COLS, seq), lambda b, i: (2, b, 0, 0))],
        out_specs=col(Q_COLS),
        out_shape=jax.ShapeDtypeStruct((Q_COLS, m), BF16),
        compiler_params=_params("parallel", "arbitrary"),
        name="attn_prompt",
    )(qt, gatest, kc, vct, kvb, kvtb, kvb, kvtb)


def _softmax_pv(score_tiles, value_tiles):
    m = functools.reduce(jnp.maximum, [jnp.max(s, axis=-1, keepdims=True) for s in score_tiles])
    ps = [jnp.exp2(s - m) for s in score_tiles]
    l = functools.reduce(jnp.add, [jnp.sum(p, axis=-1, keepdims=True) for p in ps])
    o = functools.reduce(jnp.add, [(_dot_nt if fm else _dot)(p.astype(BF16), v)
                                   for p, (v, fm) in zip(ps, value_tiles)])
    return o / l


def _attn_decode_kernel(pt_ref, *refs, n_pages, seqs, pos, nbp):
    del pt_ref
    pages = refs[:seqs * n_pages]
    q_ref, kvb_ref, gate_ref, kvc_ref, win_ref, wnew_ref, o_ref, wnext_ref = refs[seqs * n_pages:]
    for r in range(seqs):
        _attn_decode_one(pages[r * n_pages:(r + 1) * n_pages], q_ref.at[r], kvb_ref.at[r], gate_ref.at[r],
                         kvc_ref.at[:, r], win_ref.at[r], wnew_ref.at[r], o_ref.at[r], wnext_ref.at[r],
                         pos=pos, nbp=nbp)


def _attn_decode_one(pages, q_ref, kvb_ref, gate_ref, kvc_ref, win_ref, wnew_ref, o_ref, wnext_ref, *, pos, nbp):
    n_pages = len(pages)
    tk = PAGE_SIZE
    lane = lax.broadcasted_iota(jnp.int32, (1, LANES), 1)
    head = lax.broadcasted_iota(jnp.int32, (N_HEADS, 1), 0)
    low_group = head < HEADS_PER_GROUP
    low_lanes = lane < HEAD_DIM

    q32 = q_ref[...].astype(F32)
    qm = jnp.zeros((N_HEADS, LANES), F32)
    for j in range(HEADS_PER_GROUP):
        qm = jnp.where(head % HEADS_PER_GROUP == j, q32[:, j * LANES:(j + 1) * LANES], qm)
    qm = jnp.where(low_group, jnp.where(low_lanes, qm, 0.0), jnp.where(low_lanes, 0.0, qm)).astype(BF16)

    gate_row = gate_ref[...]
    gate_col = [jnp.sum(jnp.where(lane == head + br * N_HEADS, gate_row, 0.0), axis=-1, keepdims=True)
                for br in range(3)]
    kv_new = kvb_ref[...].astype(F32)
    first_row = lax.broadcasted_iota(jnp.int32, (tk, 1), 0) == 0

    def new_key_tile(c0):
        return jnp.where(first_row, kv_new[:, c0:c0 + LANES], 0.0).astype(BF16)

    nblk = lax.broadcasted_iota(jnp.int32, (1, nbp), 1)
    cvalid = (nblk + 1) * NSA_BLOCK <= pos + 1
    sc = jnp.where(cvalid, _dot_nt(qm, kvc_ref[0].astype(BF16)), MASK_INIT)
    e = jnp.exp2(sc - jnp.max(sc, axis=-1, keepdims=True))
    pc = jnp.where(cvalid, e / jnp.sum(e, axis=-1, keepdims=True), 0.0)
    o_cmp = _dot(pc.astype(BF16), kvc_ref[1].astype(BF16))

    imp = jnp.where(low_group,
                    jnp.sum(jnp.where(low_group, pc, 0.0), axis=0, keepdims=True),
                    jnp.sum(jnp.where(low_group, 0.0, pc), axis=0, keepdims=True))
    cur = pos // NSA_BLOCK
    forced = HEADS_PER_GROUP + 1.0
    imp = jnp.where(nblk == 0, forced, jnp.where(nblk == cur, forced, jnp.where(nblk == cur - 1, forced, imp)))
    imp = jnp.where(nblk * NSA_BLOCK <= pos, imp, -1.0)
    sel = _select_blocks(imp, nblk, NSA_TOP_N)

    def sel_bias(n):
        return jnp.where(sel[:, n:n + 1] > 0.5, 0.0, MASK_BIAS)

    s_tiles, v_tiles = [], []
    blocks_per_page = tk // NSA_BLOCK
    for k in range(n_pages):
        page = pages[k]
        s = _dot(qm, page[:GD_ROWS, :].astype(BF16))
        bias = sel_bias(k * blocks_per_page + blocks_per_page - 1)
        for b in range(blocks_per_page - 2, -1, -1):
            bias = jnp.where(lane < (b + 1) * NSA_BLOCK, sel_bias(k * blocks_per_page + b), bias)
        s_tiles.append(s + bias)
        v_tiles.append((page[GD_ROWS:, :].astype(BF16), True))
    s_new = _dot_nt(qm, new_key_tile(KV_BRANCH_COLS))
    s_tiles.append(s_new + jnp.where(lane == 0, sel_bias(n_pages * blocks_per_page), MASK_BIAS))
    v_tiles.append((new_key_tile(KV_BRANCH_COLS + LANES), False))
    o_slc = _softmax_pv(s_tiles, v_tiles)

    n_win = win_ref.shape[1]
    s_tiles, v_tiles = [], []
    for k in range(n_win // tk):
        diff = n_win - (k * tk + lane)
        bias = jnp.where(diff < NSA_WINDOW, 0.0, MASK_BIAS)
        s_tiles.append(_dot(qm, win_ref[:GD_ROWS, k * tk:(k + 1) * tk].astype(BF16)) + bias)
        v_tiles.append((win_ref[GD_ROWS:, k * tk:(k + 1) * tk].astype(BF16), True))
    s_tiles.append(_dot_nt(qm, new_key_tile(2 * KV_BRANCH_COLS)) + jnp.where(lane == 0, 0.0, MASK_BIAS))
    v_tiles.append((new_key_tile(2 * KV_BRANCH_COLS + LANES), False))
    o_win = _softmax_pv(s_tiles, v_tiles)

    o = gate_col[0] * o_cmp + gate_col[1] * o_slc + gate_col[2] * o_win
    for j in range(HEADS_PER_GROUP):
        pair = jnp.where(low_lanes, o[j:j + 1], o[j + HEADS_PER_GROUP:j + HEADS_PER_GROUP + 1])
        o_ref[:, j * LANES:(j + 1) * LANES] = pair.astype(o_ref.dtype)

    last = lax.broadcasted_iota(jnp.int32, (1, n_win), 1) == n_win - 1
    wnext_ref[...] = jnp.where(last, wnew_ref[...], pltpu.roll(win_ref[...], n_win - 1, 1))


def _attn_decode(q, kvb, gates, kvc, cache_slc, win_state, win_new, page_table, pos):
    n_dec, n_pages = page_table.shape
    nbp = kvc.shape[2]
    n_win = win_state.shape[2]
    seqs = DECODE_SEQS_PER_STEP
    per_seq = lambda a: pl.BlockSpec((seqs, 1, a.shape[-1]), lambda s, pt: (s, 0, 0))
    win_spec = pl.BlockSpec((seqs, KV_BRANCH_COLS, n_win), lambda s, pt: (s, 0, 0))
    q3, kvb3, g3 = (a.reshape(n_dec, 1, a.shape[-1]) for a in (q, kvb, gates))
    out, win_next = pl.pallas_call(
        functools.partial(_attn_decode_kernel, n_pages=n_pages, seqs=seqs, pos=pos, nbp=nbp),
        grid_spec=pltpu.PrefetchScalarGridSpec(
            num_scalar_prefetch=1,
            grid=(n_dec // seqs,),
            in_specs=_page_specs(n_pages, seqs) + [
                per_seq(q3), per_seq(kvb3), per_seq(g3),
                pl.BlockSpec((2, seqs, nbp, LANES), lambda s, pt: (0, s, 0, 0)),
                win_spec, pl.BlockSpec((seqs, KV_BRANCH_COLS, 1), lambda s, pt: (s, 0, 0)),
            ],
            out_specs=[pl.BlockSpec((seqs, 1, Q_COLS), lambda s, pt: (s, 0, 0)), win_spec],
        ),
        out_shape=[jax.ShapeDtypeStruct((n_dec, 1, Q_COLS), BF16), jax.ShapeDtypeStruct(win_state.shape, F32)],
        compiler_params=_params("arbitrary"),
        name="attn_decode",
    )(page_table, *([cache_slc] * (seqs * n_pages)), q3, kvb3, g3, kvc, win_state, win_new)
    return out.reshape(n_dec, Q_COLS), win_next


def _mix_kernel(*refs, tm, halo, tiles_per_seq):
    if halo:
        (x_ref, xh_ref, o_ref, shift_ref, scale_ref, gate_ref, gpre_ref, gpost_ref, cw_ref,
         wconv_ref, wmerge_ref, wup_ref, wco_ref, wout_ref, y_ref, ulast_ref, u_scr) = refs
        x = x_ref[...]
        xe = jnp.concatenate([xh_ref[...], x], axis=0)
    else:
        (x_ref, um1_ref, um2_ref, o_ref, shift_ref, scale_ref, gate_ref, gpre_ref, gpost_ref, cw_ref,
         wconv_ref, wmerge_ref, wup_ref, wco_ref, wout_ref, y_ref, ulast_ref) = refs
        x = x_ref[...]
        xe = x
    h0 = xe.shape[0] - tm
    a = (_rms(xe, gpre_ref[...]) * (1.0 + scale_ref[...]) + shift_ref[...]).astype(BF16)
    zc = _dot(a, wconv_ref[...])
    u = zc[:, 2 * D_CONV:] * zc[:, :D_CONV]
    cb = zc[h0:, D_CONV:2 * D_CONV]
    if halo:
        keep_halo = jnp.where(pl.program_id(0) % tiles_per_seq == 0, 0.0, 1.0)
        rows = lax.broadcasted_iota(jnp.int32, (xe.shape[0], 1), 0)
        u = jnp.where(rows < h0, u * keep_halo, u)
        u_scr[...] = u
        um1 = u_scr[h0 - 1:h0 - 1 + tm, :]
        um2 = u_scr[h0 - 2:h0 - 2 + tm, :]
        u0 = u[h0:]
        ulast_ref[...] = u[tm:]
    else:
        um1, um2, u0 = um1_ref[...], um2_ref[...], u
        ulast_ref[...] = u
    cw = cw_ref[...]
    y = cw[0:1] * um2 + cw[1:2] * um1 + cw[2:3] * u0
    conv_out = _dot((cb * y).astype(BF16), wco_ref[...])
    attn_out = _dot_tn(o_ref[...], wup_ref[...])
    mg = jax.nn.sigmoid(_dot(a[h0:], wmerge_ref[...]))
    merged = mg[:, :D_MODEL] * attn_out + mg[:, D_MODEL:] * conv_out
    mixed = _dot(merged.astype(BF16), wout_ref[...])
    y_ref[...] = x + gate_ref[...] * _rms(mixed, gpost_ref[...])


def _mix(x, o_att, mod, g3, conv_w, weights, tm, tiles_per_seq=None, prev=None):
    m = x.shape[0]
    halo = prev is None
    h0 = MIX_HALO
    row = lambda n: pl.BlockSpec((tm, n), lambda i: (i, 0))
    full = lambda a: pl.BlockSpec(a.shape, lambda i: (0,) * a.ndim, pipeline_mode=pl.Buffered(1))
    att = pl.BlockSpec((Q_COLS, tm), lambda i: (0, i))
    common = [att, mod.spec(3), mod.spec(4), mod.spec(5), _gspec(2), _gspec(3), full(conv_w)]
    common += [full(w) for w in weights]
    common_args = [o_att, mod.arr, mod.arr, mod.arr, g3, g3, conv_w, *weights]
    if halo:
        hb = tm // h0
        in_specs = [row(D_MODEL), pl.BlockSpec((h0, D_MODEL), lambda i: (jnp.maximum(i * hb - 1, 0), 0))] + common
        args = [x, x] + common_args
        scratch = [pltpu.VMEM((tm + h0, D_CONV), F32)]
        ulast = (jax.ShapeDtypeStruct((m // tm * h0, D_CONV), F32), pl.BlockSpec((h0, D_CONV), lambda i: (i, 0)))
    else:
        in_specs = [row(D_MODEL), row(D_CONV), row(D_CONV)] + common
        args = [x, prev[0], prev[1]] + common_args
        scratch = []
        ulast = (jax.ShapeDtypeStruct((m, D_CONV), F32), row(D_CONV))
    return pl.pallas_call(
        functools.partial(_mix_kernel, tm=tm, halo=halo, tiles_per_seq=tiles_per_seq),
        grid=(m // tm,),
        in_specs=in_specs,
        out_specs=[row(D_MODEL), ulast[1]],
        out_shape=[jax.ShapeDtypeStruct((m, D_MODEL), F32), ulast[0]],
        scratch_shapes=scratch,
        compiler_params=_params("parallel"),
        name="mix",
    )(*args)


def _head_pair_perm():
    order = []
    for j in range(HEADS_PER_GROUP):
        for g in range(N_KV_HEADS):
            head = g * HEADS_PER_GROUP + j
            order.extend(range(head * HEAD_DIM, (head + 1) * HEAD_DIM))
    return jnp.array(order, dtype=jnp.int32)


def _layer_weights(w_in, w_attn_up, w_conv_out, w_out):
    perm = _head_pair_perm()
    o1 = Q_COLS
    o2 = o1 + KV_COLS
    o3 = o2 + NSA_GATE_COLS
    o4 = o3 + CONV_COLS
    w_q = w_in[:, :o1][:, perm]
    w_g = jnp.pad(w_in[:, o2:o3], ((0, 0), (0, GATE_PAD - NSA_GATE_COLS)))
    w_qkvg = jnp.concatenate([w_q, w_in[:, o1:o2], w_g], axis=1).astype(BF16)
    mix_w = (w_in[:, o3:o4].astype(BF16), w_in[:, o4:].astype(BF16), w_attn_up[perm].astype(BF16),
             w_conv_out.astype(BF16), w_out.astype(BF16))
    return w_qkvg, mix_w


def kernel(x_prompt, x_sample, c_prompt, c_sample, cache_cmp_kv, cache_slc_kv, state_win_kv, state_conv,
           page_table, w_ada, b_ada, g_norm, w_ffn_gu, w_ffn_down, w_in, phi_pe, phi_w1, phi_b1, phi_w2,
           w_attn_up, conv_w, w_conv_out, w_out):
    n_seq, seq, _ = x_prompt.shape
    n_dec = x_sample.shape[0]
    assert w_ada.shape[0] == 1
    l = 0

    mod_all = _ada(jnp.concatenate([c_prompt, c_sample], axis=0), w_ada[l], b_ada[l])
    g3 = g_norm[l].reshape(6, 1, D_MODEL)
    w_gu = [w_ffn_gu[l, k].astype(BF16) for k in range(2)]
    w_down = [w_ffn_down[l, k].astype(BF16) for k in range(2)]
    w_qkvg, mix_w = _layer_weights(w_in[l], w_attn_up[l], w_conv_out[l], w_out[l])
    cwp = _compress_weights_paged(phi_pe[l], phi_w1[l], phi_b1[l], phi_w2[l])

    tm = 512
    tq = 256
    m = n_seq * seq
    mod_p = _Mod(mod_all[:n_seq], per_row=False, tiles_per_seq=seq // tm)
    tm_ffn, ffn_sub = 1024, 2
    mod_pf = _Mod(mod_all[:n_seq], per_row=False, tiles_per_seq=seq // tm_ffn)
    x0 = x_prompt.reshape(m, D_MODEL)
    x1 = _ffn(x0, mod_pf, g3, w_gu[0], w_down[0], 0, tm_ffn, ffn_sub)
    tables = _rope_tables(jnp.arange(seq, dtype=jnp.int32))
    _, qt, cmp_pages, cmpt, slct, wint, kvtb, kvb, _, gatest = _proj(x1, mod_p, g3, tables, w_qkvg, tm)
    nb = seq // NSA_BLOCK
    kvc = _gather_compress(cmp_pages, None, cwp, pages_per_seq=seq // PAGE_SIZE)
    kc = kvc[0].reshape(n_seq * nb, LANES).astype(BF16)
    vct = kvc[1].transpose(0, 2, 1).astype(BF16)
    o_att = _attn_prompt(qt, gatest, kc, vct, kvb, kvtb, n_seq, seq, tq)
    x2, ulast = _mix(x1, o_att, mod_pf, g3, conv_w[l], mix_w, tm_ffn, tiles_per_seq=seq // tm_ffn)
    y_prompt = _ffn(x2, mod_pf, g3, w_gu[1], w_down[1], 2, tm_ffn, ffn_sub).reshape(n_seq, seq, D_MODEL)

    def token_major(t):
        n, _, tt = t.shape
        return t.reshape(n, 2, N_KV_HEADS, HEAD_DIM, tt).transpose(0, 4, 1, 2, 3)[None]

    def feature_major(a):
        n, tt = a.shape[:2]
        return a.transpose(0, 2, 3, 4, 1).reshape(n, KV_BRANCH_COLS, tt)

    cmp_kv_prompt = token_major(cmpt)
    slc_kv_prompt = token_major(slct)
    win_keep = min(NSA_WINDOW, seq)
    win_kv_prompt = token_major(wint[:, :, seq - win_keep:])
    conv_prompt = ulast.reshape(n_seq, seq // tm_ffn, MIX_HALO, D_CONV)[:, -1, MIX_HALO - (CONV_WIDTH - 1):][None]

    n_pages = page_table.shape[1]
    past_len = n_pages * PAGE_SIZE
    pos_s = past_len + jnp.arange(x_sample.shape[1], dtype=jnp.int32)
    assert x_sample.shape[1] == 1 and past_len % NSA_BLOCK == 0 and state_win_kv.shape[2] <= past_len
    mod_s = _Mod(mod_all[n_seq:], per_row=True)
    xs1 = _ffn(x_sample.reshape(n_dec, D_MODEL), mod_s, g3, w_gu[0], w_down[0], 0, n_dec)
    tables_s = _rope_tables(jnp.broadcast_to(pos_s, (n_dec,)))
    q_s, _, _, cmpt_s, slct_s, wint_s, _, kvb_s, gates_s, _ = _proj(xs1, mod_s, g3, tables_s, w_qkvg, n_dec)
    nb_past = past_len // NSA_BLOCK
    nb_pad = -(-(nb_past + 1) // NSA_BLOCK) * NSA_BLOCK
    kvc_past = _gather_compress(feature_major(cache_cmp_kv[l]), page_table, cwp)
    new_pages = jnp.pad(cmpt_s[0].T[:, :, None], ((0, 0), (0, 0), (0, PAGE_SIZE - 1)))
    kvc_new = _gather_compress(new_pages, None, cwp, pages_per_seq=n_pages).reshape(2, n_dec, -1, LANES)[:, :, :1]
    kvc_s = jnp.concatenate(
        [kvc_past, kvc_new, jnp.zeros((2, n_dec, nb_pad - nb_past - 1, LANES), F32)], axis=2)
    win_state = feature_major(state_win_kv[l])
    assert win_state.shape[2] == NSA_WINDOW
    o_att_s, win_next = _attn_decode(q_s, kvb_s, gates_s, kvc_s, feature_major(cache_slc_kv[l]), win_state,
                                     wint_s[0].T[:, :, None], page_table, past_len)
    conv_state = state_conv[l]
    xs2, u_s = _mix(xs1, o_att_s.T, mod_s, g3, conv_w[l], mix_w, n_dec,
                    prev=(conv_state[:, CONV_WIDTH - 2], conv_state[:, CONV_WIDTH - 3]))
    y_sample = _ffn(xs2, mod_s, g3, w_gu[1], w_down[1], 2, n_dec).reshape(x_sample.shape)

    kvs = (1, n_dec, 1, 2, N_KV_HEADS, HEAD_DIM)
    conv_sample = jnp.concatenate([conv_state[:, 1:], u_s[:, None, :]], axis=1)[None]
    return (y_prompt, y_sample, cmp_kv_prompt, slc_kv_prompt, win_kv_prompt, conv_prompt,
            token_major(cmpt_s).reshape(kvs), token_major(slct_s).reshape(kvs), token_major(win_next), conv_sample)
```

```python
import functools

import jax
import jax.numpy as jnp
from jax import lax
from jax.experimental import pallas as pl
from jax.experimental.pallas import tpu as pltpu

D_MODEL = 1024
N_HEADS = 8
HEAD_DIM = 64
N_KV_HEADS = 2
HEADS_PER_GROUP = N_HEADS // N_KV_HEADS
ROT_DIM = HEAD_DIM // 4
ROPE_THETA = 500000.0
NSA_BLOCK = 64
NSA_TOP_N = 16
NSA_WINDOW = 512
PHI_HIDDEN = 2 * HEAD_DIM
D_CONV = 512
CONV_WIDTH = 3
D_FF = 2816
FFN_HALF = 0.5
NORM_EPS = 1e-6
PAGE_SIZE = 128
Q_COLS = N_HEADS * HEAD_DIM
KV_BRANCH_COLS = 2 * N_KV_HEADS * HEAD_DIM
KV_COLS = 3 * KV_BRANCH_COLS
NSA_GATE_COLS = 3 * N_HEADS
CONV_COLS = 3 * D_CONV
MERGE_COLS = 2 * D_MODEL
GD_ROWS = N_KV_HEADS * HEAD_DIM

LANES = 128
BF16_SUBLANES = 16
GATE_PAD = LANES
QKVG_COLS = Q_COLS + KV_COLS + GATE_PAD
LOG2_E = 1.4426950408889634
MASK_INIT = -1e30
MASK_BIAS = -2e30
VMEM_LIMIT = 56 * 1024 * 1024
SLAB_PITCH = HEAD_DIM + 8
FFN_CHUNKS = 2
DECODE_SEQS_PER_STEP = 4
MIX_HALO = 16

BF16 = jnp.bfloat16
F32 = jnp.float32


def _dot(a, b):
    return jnp.dot(a, b, preferred_element_type=F32)


def _dot_tn(a, b):
    return lax.dot_general(a, b, (((0,), (0,)), ((), ())), preferred_element_type=F32)


def _dot_nt(a, b):
    return lax.dot_general(a, b, (((1,), (1,)), ((), ())), preferred_element_type=F32)


def _rms(x, g):
    return x * lax.rsqrt(jnp.mean(x * x, axis=-1, keepdims=True) + NORM_EPS) * g


def _silu(x):
    return x * jax.nn.sigmoid(x)


def _params(*sem):
    return pltpu.CompilerParams(dimension_semantics=sem, vmem_limit_bytes=VMEM_LIMIT)


def _ada_kernel(c_ref, w_ref, b_ref, o_ref):
    c = _silu(c_ref[...]).astype(BF16)
    o_ref[...] = _dot(c, w_ref[...].astype(BF16)) + b_ref[...]


def _ada(c, w_ada, b_ada):
    rows = c.shape[0]
    n = w_ada.shape[1]
    tn = 9 * LANES
    return pl.pallas_call(
        _ada_kernel,
        grid=(n // tn,),
        in_specs=[
            pl.BlockSpec((rows, D_MODEL), lambda j: (0, 0)),
            pl.BlockSpec((D_MODEL, tn), lambda j: (0, j)),
            pl.BlockSpec((1, tn), lambda j: (0, j)),
        ],
        out_specs=pl.BlockSpec((rows, tn), lambda j: (0, j)),
        out_shape=jax.ShapeDtypeStruct((rows, n), F32),
        compiler_params=_params("parallel"),
        name="ada",
    )(c, w_ada, b_ada.reshape(1, n))


class _Mod:
    def __init__(self, mod, per_row, tiles_per_seq=1):
        self.per_row = per_row
        self.tiles_per_seq = tiles_per_seq
        self.arr = mod if per_row else mod.reshape(mod.shape[0] * 9, 1, D_MODEL)

    def spec(self, k):
        if self.per_row:
            return pl.BlockSpec((self.arr.shape[0], D_MODEL), lambda i, *_: (0, k))
        tps = self.tiles_per_seq
        return pl.BlockSpec((None, 1, D_MODEL), lambda i, *_: ((i // tps) * 9 + k, 0, 0))


def _gspec(k):
    return pl.BlockSpec((None, 1, D_MODEL), lambda i, *_: (k, 0, 0))


def _ffn_kernel(x_ref, shift_ref, scale_ref, gate_ref, gpre_ref, gpost_ref, wgu_ref, wd_ref, o_ref,
                *, n_sub, n_chunk):
    ts = x_ref.shape[0] // n_sub
    tf = D_FF // n_chunk
    for s in range(n_sub):
        rows = slice(s * ts, (s + 1) * ts)
        mod = lambda ref: ref[...] if ref.shape[0] == 1 else ref[rows, :]
        x = x_ref[rows, :]
        a = (_rms(x, gpre_ref[...]) * (1.0 + mod(scale_ref)) + mod(shift_ref)).astype(BF16)
        y = None
        for c in range(n_chunk):
            g = _dot(a, wgu_ref[:, c * tf:(c + 1) * tf])
            u = _dot(a, wgu_ref[:, D_FF + c * tf:D_FF + (c + 1) * tf])
            d = _dot((_silu(g) * u).astype(BF16), wd_ref[c * tf:(c + 1) * tf, :])
            y = d if y is None else y + d
        o_ref[rows, :] = x + FFN_HALF * mod(gate_ref) * _rms(y, gpost_ref[...])


def _ffn(x, mod, g3, w_gu, w_down, sub, tm, n_sub=1):
    m = x.shape[0]
    row = pl.BlockSpec((tm, D_MODEL), lambda i: (i, 0))
    const = lambda a: pl.BlockSpec((None,) + a.shape[1:], lambda i: (sub // 2, 0, 0), pipeline_mode=pl.Buffered(1))
    return pl.pallas_call(
        functools.partial(_ffn_kernel, n_sub=n_sub, n_chunk=FFN_CHUNKS),
        grid=(m // tm,),
        in_specs=[
            row, mod.spec(3 * sub), mod.spec(3 * sub + 1), mod.spec(3 * sub + 2),
            _gspec(2 * sub), _gspec(2 * sub + 1), const(w_gu), const(w_down),
        ],
        out_specs=row,
        out_shape=jax.ShapeDtypeStruct((m, D_MODEL), F32),
        compiler_params=_params("parallel"),
        name=f"ffn{sub}",
    )(x, mod.arr, mod.arr, mod.arr, g3, g3, w_gu, w_down)


def _rope(x, cos, sin_lo, sin_hi):
    return x * cos + pltpu.roll(x, LANES - ROT_DIM // 2, 1) * sin_lo + pltpu.roll(x, ROT_DIM // 2, 1) * sin_hi


def _proj_kernel(x_ref, shift_ref, scale_ref, g_ref, cos_ref, slo_ref, shi_ref, w_ref,
                 q_ref, qt_ref, cmp_ref, cmpt_ref, slct_ref, wint_ref, kvtb_ref, kvb_ref, gate_ref, gatet_ref):
    a = (_rms(x_ref[...], g_ref[...]) * (1.0 + scale_ref[...]) + shift_ref[...]).astype(BF16)
    z = _dot(a, w_ref[...])
    cos, slo, shi = cos_ref[...], slo_ref[...], shi_ref[...]
    scale = HEAD_DIM ** -0.5 * LOG2_E
    for s in range(Q_COLS // LANES):
        q = _rope(z[:, s * LANES:(s + 1) * LANES], cos, slo, shi) * scale
        q_ref[:, s * LANES:(s + 1) * LANES] = q.astype(BF16)
        qt_ref[s * LANES:(s + 1) * LANES, :] = q.T.astype(BF16)
    kvt_refs = (cmpt_ref, slct_ref, wint_ref)
    for br in range(3):
        c0 = Q_COLS + br * KV_BRANCH_COLS
        k = _rope(z[:, c0:c0 + LANES], cos, slo, shi)
        v = z[:, c0 + LANES:c0 + 2 * LANES]
        for r0, x in ((0, k), (LANES, v)):
            xt = x.T
            kvt_refs[br][r0:r0 + LANES, :] = xt
            kvtb_ref[br, r0:r0 + LANES, :] = xt.astype(BF16)
            if br == 0:
                for p in range(cmp_ref.shape[0]):
                    cmp_ref[p, r0:r0 + LANES, :] = xt[:, p * PAGE_SIZE:(p + 1) * PAGE_SIZE]
        kvb_ref[:, br * KV_BRANCH_COLS:br * KV_BRANCH_COLS + LANES] = k.astype(BF16)
        kvb_ref[:, br * KV_BRANCH_COLS + LANES:(br + 1) * KV_BRANCH_COLS] = v.astype(BF16)
    gate = jax.nn.sigmoid(z[:, Q_COLS + KV_COLS:])
    gate_ref[...] = gate
    gatet_ref[...] = gate.T


def _proj(x, mod, g3, tables, w_qkvg, tm):
    assert tm % PAGE_SIZE == 0
    m = x.shape[0]
    t_rows = tables[0].shape[0]
    nt = t_rows // tm
    n_seq = m // t_rows
    row = lambda n: pl.BlockSpec((tm, n), lambda i: (i, 0))
    tab = pl.BlockSpec((tm, LANES), lambda i: (i % nt, 0))
    col = lambda n: pl.BlockSpec((n, tm), lambda i: (0, i))
    kvt_spec = pl.BlockSpec((None, KV_BRANCH_COLS, tm), lambda i: (i // nt, 0, i % nt))
    kvtb_spec = pl.BlockSpec((3, None, KV_BRANCH_COLS, tm), lambda i: (0, i // nt, 0, i % nt))
    return pl.pallas_call(
        _proj_kernel,
        grid=(m // tm,),
        in_specs=[row(D_MODEL), mod.spec(3), mod.spec(4), _gspec(2), tab, tab, tab,
                  pl.BlockSpec((D_MODEL, QKVG_COLS), lambda i: (0, 0))],
        out_specs=[row(Q_COLS), col(Q_COLS),
                   pl.BlockSpec((tm // PAGE_SIZE, KV_BRANCH_COLS, PAGE_SIZE), lambda i: (i, 0, 0)),
                   kvt_spec, kvt_spec, kvt_spec, kvtb_spec, row(KV_COLS), row(GATE_PAD), col(GATE_PAD)],
        out_shape=[
            jax.ShapeDtypeStruct((m, Q_COLS), BF16),
            jax.ShapeDtypeStruct((Q_COLS, m), BF16),
            jax.ShapeDtypeStruct((m // PAGE_SIZE, KV_BRANCH_COLS, PAGE_SIZE), F32),
            *[jax.ShapeDtypeStruct((n_seq, KV_BRANCH_COLS, t_rows), F32)] * 3,
            jax.ShapeDtypeStruct((3, n_seq, KV_BRANCH_COLS, t_rows), BF16),
            jax.ShapeDtypeStruct((m, KV_COLS), BF16),
            jax.ShapeDtypeStruct((m, GATE_PAD), F32),
            jax.ShapeDtypeStruct((GATE_PAD, m), F32),
        ],
        compiler_params=_params("parallel"),
        name="proj",
    )(x, mod.arr, mod.arr, g3, *tables, w_qkvg)


def _rope_tables(pos):
    half = ROT_DIM // 2
    inv = jnp.power(ROPE_THETA, -jnp.arange(half, dtype=F32) * 2.0 / ROT_DIM)
    ang = pos.astype(F32)[:, None] * inv[None, :]
    cos, sin = jnp.cos(ang), jnp.sin(ang)
    n = pos.shape[0]
    ones = jnp.ones((n, HEAD_DIM - ROT_DIM), F32)
    zeros = jnp.zeros((n, HEAD_DIM - ROT_DIM), F32)
    zh = jnp.zeros((n, half), F32)
    c = jnp.concatenate([cos, cos, ones], axis=1)
    lo = jnp.concatenate([-sin, zh, zeros], axis=1)
    hi = jnp.concatenate([zh, sin, zeros], axis=1)
    return tuple(jnp.tile(t, (1, LANES // HEAD_DIM)) for t in (c, lo, hi))


def _compress_weights_paged(phi_pe, phi_w1, phi_b1, phi_w2):
    blocks = PAGE_SIZE // NSA_BLOCK
    eye = jnp.eye(blocks, dtype=F32)
    w1 = phi_w1.reshape(2, NSA_BLOCK, HEAD_DIM, PHI_HIDDEN)
    w1 = jnp.einsum("bc,ktdj->kdbtcj", eye, w1).reshape(2, HEAD_DIM * PAGE_SIZE, blocks * PHI_HIDDEN)
    w2 = jnp.einsum("bc,kjd->kbjcd", eye, phi_w2).reshape(2, blocks * PHI_HIDDEN, blocks * HEAD_DIM)
    pe = jnp.tile(phi_pe.transpose(0, 2, 1), (1, 1, blocks)).reshape(2, HEAD_DIM, 1, PAGE_SIZE)
    b1 = jnp.tile(phi_b1, (1, blocks)).reshape(2, 1, blocks * PHI_HIDDEN)
    return pe, w1.astype(BF16), b1, w2.astype(BF16)


def _gather_compress_kernel(pt_ref, *refs, seqs, n_pages, group, steps, contiguous, new_tokens):
    del pt_ref
    if contiguous:
        pages = [refs[0].at[k] for k in range(seqs * n_pages)]
        refs = refs[1:]
    else:
        pages, refs = refs[:seqs * n_pages], refs[seqs * n_pages:]
    if new_tokens:
        newt_ref, refs = refs[0], refs[1:]
    pe_ref, w1_ref, b1_ref, w2_ref, o_ref, xk_scr, xv_scr, c_scr = refs
    x_scr = (xk_scr, xv_scr)
    slot = pl.program_id(0) % steps
    per_seq = n_pages + (1 if new_tokens else 0)
    for r in range(seqs):
        slabs = [pages[r * n_pages + k] for k in range(n_pages)]
        if new_tokens:
            seq_lane = lax.broadcasted_iota(jnp.int32, (1, newt_ref.shape[1]), 1) == pl.program_id(0) * seqs + r
            col = jnp.sum(jnp.where(seq_lane, newt_ref[...], 0.0), axis=1, keepdims=True)
            first = lax.broadcasted_iota(jnp.int32, (1, PAGE_SIZE), 1) == 0
            slabs.append(jnp.where(first, col, 0.0))
        for k, slab in enumerate(slabs):
            for kv in range(2):
                for g in range(N_KV_HEADS):
                    page = (slot * seqs + r) * per_seq + k
                    r0 = pl.multiple_of((page * N_KV_HEADS + g) * SLAB_PITCH, SLAB_PITCH)
                    s0 = kv * GD_ROWS + g * HEAD_DIM
                    x_scr[kv][pl.ds(r0, HEAD_DIM), :] = slab[s0:s0 + HEAD_DIM, :]

    @pl.when(slot == steps - 1)
    def _():
        rows = group * N_KV_HEADS
        for kv in range(2):
            acc = jnp.zeros((rows, w1_ref.shape[2]), F32)
            for d in range(0, HEAD_DIM, 2):
                xs = []
                for dd in (d, d + 1):
                    x = x_scr[kv][pl.ds(dd, rows, stride=SLAB_PITCH), :]
                    xs.append((x + pe_ref[kv, dd]).astype(BF16))
                acc = acc + _dot(jnp.concatenate(xs, axis=1), w1_ref[kv, d * PAGE_SIZE:(d + 2) * PAGE_SIZE, :])
            h = _silu(acc + b1_ref[kv])
            c_scr[...] = _dot(h.astype(BF16), w2_ref[kv])
            g0 = c_scr[pl.ds(0, rows // 2, stride=2), :]
            g1 = c_scr[pl.ds(1, rows // 2, stride=2), :]
            low = lax.broadcasted_iota(jnp.int32, (1, LANES), 1) < HEAD_DIM
            o_ref.at[kv][pl.ds(0, rows // 2, stride=2), :] = jnp.where(low, g0, pltpu.roll(g1, HEAD_DIM, 1))
            o_ref.at[kv][pl.ds(1, rows // 2, stride=2), :] = jnp.where(low, pltpu.roll(g0, HEAD_DIM, 1), g1)


def _page_specs(n_pages, seqs=1):
    return [pl.BlockSpec((None, KV_BRANCH_COLS, PAGE_SIZE), lambda s, pt, r=r, k=k: (pt[s * seqs + r, k], 0, 0))
            for r in range(seqs) for k in range(n_pages)]


def _gather_compress(cache_t, page_table, cwp, pages_per_seq=None, new_tokens=None):
    pe, w1, b1, w2 = cwp
    contiguous = page_table is None
    seqs = 4
    if contiguous:
        n_pages = pages_per_seq
        n_dec = cache_t.shape[0] // n_pages
        page_table = jnp.zeros((1, 1), jnp.int32)
        page_specs = [pl.BlockSpec((seqs * n_pages, KV_BRANCH_COLS, PAGE_SIZE), lambda s, pt: (s, 0, 0))]
        page_args = [cache_t]
    else:
        n_dec, n_pages = page_table.shape
        page_specs = _page_specs(n_pages, seqs)
        page_args = [cache_t] * (seqs * n_pages)
    const = lambda a: pl.BlockSpec(a.shape, lambda s, pt: (0,) * a.ndim, pipeline_mode=pl.Buffered(1))
    per_seq = n_pages
    if new_tokens is not None:
        per_seq += 1
        page_specs.append(const(new_tokens))
        page_args.append(new_tokens)
    steps = 2
    group = steps * seqs * per_seq
    assert n_dec % (steps * seqs) == 0
    rows = group * N_KV_HEADS
    blocks = PAGE_SIZE // NSA_BLOCK
    assert blocks == 2 and N_KV_HEADS == 2
    out = pl.pallas_call(
        functools.partial(_gather_compress_kernel, seqs=seqs, n_pages=n_pages, group=group, steps=steps,
                          contiguous=contiguous, new_tokens=new_tokens is not None),
        grid_spec=pltpu.PrefetchScalarGridSpec(
            num_scalar_prefetch=1,
            grid=(n_dec // seqs,),
            in_specs=page_specs + [const(pe), const(w1), const(b1), const(w2)],
            out_specs=pl.BlockSpec((2, rows, blocks * HEAD_DIM), lambda s, pt: (0, s // steps, 0)),
            scratch_shapes=[pltpu.VMEM((group * N_KV_HEADS * SLAB_PITCH, PAGE_SIZE), F32)] * 2
            + [pltpu.VMEM((rows, LANES), F32)],
        ),
        out_shape=jax.ShapeDtypeStruct((2, n_dec * per_seq * N_KV_HEADS, blocks * HEAD_DIM), F32),
        compiler_params=_params("arbitrary"),
        name="gather_compress",
    )(page_table, *page_args, pe, w1, b1, w2)
    return out.reshape(2, n_dec, per_seq * blocks, N_KV_HEADS * HEAD_DIM)


def _group_queries_t(qt, g):
    t = qt.shape[1]
    zeros = jnp.zeros((HEAD_DIM, t), qt.dtype)
    cols = []
    for j in range(HEADS_PER_GROUP):
        qh = qt[j * LANES + g * HEAD_DIM:j * LANES + (g + 1) * HEAD_DIM, :]
        cols.append(jnp.concatenate([qh, zeros] if g == 0 else [zeros, qh], axis=0))
    return jnp.concatenate(cols, axis=1)


def _tile_heads(x):
    return jnp.concatenate([x] * HEADS_PER_GROUP, axis=1)


def _select_blocks_t(imp, nblk, n_sel):
    nb = imp.shape[0]
    rank = jnp.zeros(imp.shape, F32)
    for m in range(nb):
        row = imp[m:m + 1, :]
        beats = jnp.where(row > imp, 1.0, jnp.where(row == imp, jnp.where(m < nblk, 1.0, 0.0), 0.0))
        rank = rank + beats
    return jnp.where(rank < n_sel, jnp.where(imp >= 0.0, 1.0, 0.0), 0.0)


def _select_blocks(imp, nblk, n_sel):
    nb = imp.shape[1]
    rank = jnp.zeros(imp.shape, F32)
    for m in range(nb):
        col = imp[:, m:m + 1]
        beats = jnp.where(col > imp, 1.0, jnp.where(col == imp, jnp.where(m < nblk, 1.0, 0.0), 0.0))
        rank = rank + beats
    return jnp.where(rank < n_sel, jnp.where(imp >= 0.0, 1.0, 0.0), 0.0)


def _attn_tile(carry, k, q, vt_ref, ks, tk, bias):
    m, l, acc = carry
    s = _dot(k, q)
    if bias is not None:
        s = s + bias
    m_new = jnp.maximum(m, jnp.max(s, axis=0, keepdims=True))
    alpha = jnp.exp2(m - m_new)
    pb = jnp.exp2(s - m_new).astype(BF16)
    half = pb.shape[1] // N_KV_HEADS
    ones = jnp.ones((BF16_SUBLANES, tk), BF16)
    pv = [_dot(jnp.concatenate([vt_ref[GD_ROWS + g * HEAD_DIM:GD_ROWS + (g + 1) * HEAD_DIM, pl.ds(ks, tk)], ones],
                               axis=0), pb[:, g * half:(g + 1) * half]) for g in range(N_KV_HEADS)]
    pv = jnp.concatenate(pv, axis=1)
    return m_new, alpha * l + pv[HEAD_DIM:HEAD_DIM + 1], alpha * acc + pv[:HEAD_DIM]


def _maybe(cond, fn, carry):
    return lax.fori_loop(0, jnp.where(cond, 1, 0), lambda _, c: fn(c), carry)


def _attn_kernel(qt_ref, gatet_ref, kc_ref, vct_ref, slc_k_ref, slc_vt_ref, win_k_ref, win_vt_ref, o_ref,
                 *, tq, nb):
    i = pl.program_id(1)
    p0 = i * tq
    tk = tq
    hq = HEADS_PER_GROUP * tq
    n_win_tiles = NSA_WINDOW // tk
    pos = p0 + lax.broadcasted_iota(jnp.int32, (1, tq), 1)
    nblk = lax.broadcasted_iota(jnp.int32, (nb, 1), 0)
    key_off = lax.broadcasted_iota(jnp.int32, (tk, 1), 0)
    qt = qt_ref[...]
    q_all = jnp.concatenate([_group_queries_t(qt, g) for g in range(N_KV_HEADS)], axis=1)
    tile_all = lambda x: jnp.concatenate([x] * N_HEADS, axis=1)

    cvalid = (nblk + 1) * NSA_BLOCK <= tile_all(pos) + 1
    sc = jnp.where(cvalid, _dot(kc_ref[...], q_all), MASK_INIT)
    e = jnp.exp2(sc - jnp.max(sc, axis=0, keepdims=True))
    pc = jnp.where(cvalid, e / jnp.sum(e, axis=0, keepdims=True), 0.0)
    pcb = pc.astype(BF16)
    o_cmp = jnp.concatenate([_dot(vct_ref[g * HEAD_DIM:(g + 1) * HEAD_DIM, :], pcb[:, g * hq:(g + 1) * hq])
                             for g in range(N_KV_HEADS)], axis=1)

    cur = pos // NSA_BLOCK
    forced = HEADS_PER_GROUP + 1.0
    sel_rows = []
    for g in range(N_KV_HEADS):
        imp = functools.reduce(jnp.add, [pc[:, g * hq + j * tq:g * hq + (j + 1) * tq]
                                         for j in range(HEADS_PER_GROUP)])
        imp = jnp.where(nblk == 0, forced, jnp.where(nblk == cur, forced, jnp.where(nblk == cur - 1, forced, imp)))
        imp = jnp.where(nblk * NSA_BLOCK <= pos, imp, -1.0)
        sel = _select_blocks_t(imp, nblk, NSA_TOP_N)
        sel_rows.append(_tile_heads(jnp.where(sel > 0.5, 0.0, MASK_BIAS)))
    sel_rows = jnp.concatenate(sel_rows, axis=1)
    sel_rows = jnp.concatenate([sel_rows, jnp.zeros((LANES - nb, sel_rows.shape[1]), F32)], axis=0)
    q_sel = jnp.concatenate([q_all, sel_rows.astype(BF16)], axis=0)
    blk_lane = lax.broadcasted_iota(jnp.int32, (1, LANES), 1)

    def slc_keys(ks):
        onehot = jnp.where(blk_lane == (ks + key_off) // NSA_BLOCK, 1.0, 0.0).astype(BF16)
        return jnp.concatenate([slc_k_ref[pl.ds(ks, tk), :GD_ROWS], onehot], axis=1)

    cols = q_all.shape[1]
    init = (jnp.full((1, cols), MASK_INIT, F32), jnp.zeros((1, cols), F32), jnp.zeros((HEAD_DIM, cols), F32))
    causal = tile_all(jnp.where(p0 + key_off <= pos, 0.0, MASK_BIAS))
    k_diag = pl.multiple_of(p0, tk)

    def slc_body(j, c):
        ks = pl.multiple_of(j * tk, tk)
        return _attn_tile(c, slc_keys(ks), q_sel, slc_vt_ref, ks, tk, None)

    c = lax.fori_loop(0, i, slc_body, init)
    _, l, acc = _attn_tile(c, slc_keys(k_diag), q_sel, slc_vt_ref, k_diag, tk, causal)
    o_slc = acc / l

    def win_tile(c, ks, bias):
        return _attn_tile(c, win_k_ref[pl.ds(ks, tk), :GD_ROWS], q_all, win_vt_ref, ks, tk, bias)

    def edge_tile(c):
        ks = pl.multiple_of((i - n_win_tiles) * tk, tk)
        bias = tile_all(jnp.where(pos - (ks + key_off) < NSA_WINDOW, 0.0, MASK_BIAS))
        return win_tile(c, ks, bias)

    c = _maybe(i >= n_win_tiles, edge_tile, init)
    c = lax.fori_loop(jnp.maximum(i - n_win_tiles + 1, 0), i,
                      lambda j, c: win_tile(c, pl.multiple_of(j * tk, tk), None), c)
    _, l, acc = win_tile(c, k_diag, causal)
    o_win = acc / l

    gates = gatet_ref[...]
    for g in range(N_KV_HEADS):
        for j in range(HEADS_PER_GROUP):
            head = g * HEADS_PER_GROUP + j
            cs = slice(g * hq + j * tq, g * hq + (j + 1) * tq)
            o = (gates[head:head + 1] * o_cmp[:, cs]
                 + gates[N_HEADS + head:N_HEADS + head + 1] * o_slc[:, cs]
                 + gates[2 * N_HEADS + head:2 * N_HEADS + head + 1] * o_win[:, cs])
            r0 = j * LANES + g * HEAD_DIM
            o_ref[r0:r0 + HEAD_DIM, :] = o.astype(o_ref.dtype)


def _attn_prompt(qt, gatest, kc, vct, kvb, kvtb, n_seq, seq, tq):
    m = qt.shape[1]
    nb = seq // NSA_BLOCK
    nq = seq // tq
    assert seq % tq == 0 and NSA_WINDOW % tq == 0 and tq % NSA_BLOCK == 0 and nb <= LANES
    col = lambda n: pl.BlockSpec((n, tq), lambda b, i: (0, b * nq + i))
    return pl.pallas_call(
        functools.partial(_attn_kernel, tq=tq, nb=nb),
        grid=(n_seq, nq),
        in_specs=[col(Q_COLS), col(4 * N_HEADS),
                  pl.BlockSpec((nb, LANES), lambda b, i: (b, 0)),
                  pl.BlockSpec((None, LANES, nb), lambda b, i: (b, 0, 0)),
                  pl.BlockSpec((seq, KV_BRANCH_COLS), lambda b, i: (b, 1)),
                  pl.BlockSpec((None, None, KV_BRANCH_COLS, seq), lambda b, i: (1, b, 0, 0)),
                  pl.BlockSpec((seq, KV_BRANCH_COLS), lambda b, i: (b, 2)),
                  pl.BlockSpec((None, None, KV_BRANCH_COLS, seq), lambda b, i: (2, b, 0, 0))],
        out_specs=col(Q_COLS),
        out_shape=jax.ShapeDtypeStruct((Q_COLS, m), BF16),
        compiler_params=_params("parallel", "arbitrary"),
        name="attn_prompt",
    )(qt, gatest, kc, vct, kvb, kvtb, kvb, kvtb)


def _softmax_pv(score_tiles, value_tiles):
    m = functools.reduce(jnp.maximum, [jnp.max(s, axis=-1, keepdims=True) for s in score_tiles])
    ps = [jnp.exp2(s - m) for s in score_tiles]
    l = functools.reduce(jnp.add, [jnp.sum(p, axis=-1, keepdims=True) for p in ps])
    o = functools.reduce(jnp.add, [(_dot_nt if fm else _dot)(p.astype(BF16), v)
                                   for p, (v, fm) in zip(ps, value_tiles)])
    return o / l


def _attn_decode_kernel(pt_ref, *refs, n_pages, seqs, pos, nbp):
    del pt_ref
    pages = refs[:seqs * n_pages]
    q_ref, kvb_ref, gate_ref, kvc_ref, win_ref, wnew_ref, o_ref, wnext_ref = refs[seqs * n_pages:]
    for r in range(seqs):
        _attn_decode_one(pages[r * n_pages:(r + 1) * n_pages], q_ref.at[r], kvb_ref.at[r], gate_ref.at[r],
                         kvc_ref.at[:, r], win_ref.at[r], wnew_ref.at[r], o_ref.at[r], wnext_ref.at[r],
                         pos=pos, nbp=nbp)


def _attn_decode_one(pages, q_ref, kvb_ref, gate_ref, kvc_ref, win_ref, wnew_ref, o_ref, wnext_ref, *, pos, nbp):
    n_pages = len(pages)
    tk = PAGE_SIZE
    lane = lax.broadcasted_iota(jnp.int32, (1, LANES), 1)
    head = lax.broadcasted_iota(jnp.int32, (N_HEADS, 1), 0)
    low_group = head < HEADS_PER_GROUP
    low_lanes = lane < HEAD_DIM

    q32 = q_ref[...].astype(F32)
    qm = jnp.zeros((N_HEADS, LANES), F32)
    for j in range(HEADS_PER_GROUP):
        qm = jnp.where(head % HEADS_PER_GROUP == j, q32[:, j * LANES:(j + 1) * LANES], qm)
    qm = jnp.where(low_group, jnp.where(low_lanes, qm, 0.0), jnp.where(low_lanes, 0.0, qm)).astype(BF16)

    gate_row = gate_ref[...]
    gate_col = [jnp.sum(jnp.where(lane == head + br * N_HEADS, gate_row, 0.0), axis=-1, keepdims=True)
                for br in range(3)]
    kv_new = kvb_ref[...].astype(F32)
    first_row = lax.broadcasted_iota(jnp.int32, (tk, 1), 0) == 0

    def new_key_tile(c0):
        return jnp.where(first_row, kv_new[:, c0:c0 + LANES], 0.0).astype(BF16)

    nblk = lax.broadcasted_iota(jnp.int32, (1, nbp), 1)
    cvalid = (nblk + 1) * NSA_BLOCK <= pos + 1
    sc = jnp.where(cvalid, _dot_nt(qm, kvc_ref[0].astype(BF16)), MASK_INIT)
    e = jnp.exp2(sc - jnp.max(sc, axis=-1, keepdims=True))
    pc = jnp.where(cvalid, e / jnp.sum(e, axis=-1, keepdims=True), 0.0)
    o_cmp = _dot(pc.astype(BF16), kvc_ref[1].astype(BF16))

    imp = jnp.where(low_group,
                    jnp.sum(jnp.where(low_group, pc, 0.0), axis=0, keepdims=True),
                    jnp.sum(jnp.where(low_group, 0.0, pc), axis=0, keepdims=True))
    cur = pos // NSA_BLOCK
    forced = HEADS_PER_GROUP + 1.0
    imp = jnp.where(nblk == 0, forced, jnp.where(nblk == cur, forced, jnp.where(nblk == cur - 1, forced, imp)))
    imp = jnp.where(nblk * NSA_BLOCK <= pos, imp, -1.0)
    sel = _select_blocks(imp, nblk, NSA_TOP_N)

    def sel_bias(n):
        return jnp.where(sel[:, n:n + 1] > 0.5, 0.0, MASK_BIAS)

    s_tiles, v_tiles = [], []
    blocks_per_page = tk // NSA_BLOCK
    for k in range(n_pages):
        page = pages[k]
        s = _dot(qm, page[:GD_ROWS, :].astype(BF16))
        bias = sel_bias(k * blocks_per_page + blocks_per_page - 1)
        for b in range(blocks_per_page - 2, -1, -1):
            bias = jnp.where(lane < (b + 1) * NSA_BLOCK, sel_bias(k * blocks_per_page + b), bias)
        s_tiles.append(s + bias)
        v_tiles.append((page[GD_ROWS:, :].astype(BF16), True))
    s_new = _dot_nt(qm, new_key_tile(KV_BRANCH_COLS))
    s_tiles.append(s_new + jnp.where(lane == 0, sel_bias(n_pages * blocks_per_page), MASK_BIAS))
    v_tiles.append((new_key_tile(KV_BRANCH_COLS + LANES), False))
    o_slc = _softmax_pv(s_tiles, v_tiles)

    n_win = win_ref.shape[1]
    s_tiles, v_tiles = [], []
    for k in range(n_win // tk):
        diff = n_win - (k * tk + lane)
        bias = jnp.where(diff < NSA_WINDOW, 0.0, MASK_BIAS)
        s_tiles.append(_dot(qm, win_ref[:GD_ROWS, k * tk:(k + 1) * tk].astype(BF16)) + bias)
        v_tiles.append((win_ref[GD_ROWS:, k * tk:(k + 1) * tk].astype(BF16), True))
    s_tiles.append(_dot_nt(qm, new_key_tile(2 * KV_BRANCH_COLS)) + jnp.where(lane == 0, 0.0, MASK_BIAS))
    v_tiles.append((new_key_tile(2 * KV_BRANCH_COLS + LANES), False))
    o_win = _softmax_pv(s_tiles, v_tiles)

    o = gate_col[0] * o_cmp + gate_col[1] * o_slc + gate_col[2] * o_win
    for j in range(HEADS_PER_GROUP):
        pair = jnp.where(low_lanes, o[j:j + 1], o[j + HEADS_PER_GROUP:j + HEADS_PER_GROUP + 1])
        o_ref[:, j * LANES:(j + 1) * LANES] = pair.astype(o_ref.dtype)

    last = lax.broadcasted_iota(jnp.int32, (1, n_win), 1) == n_win - 1
    wnext_ref[...] = jnp.where(last, wnew_ref[...], pltpu.roll(win_ref[...], n_win - 1, 1))


def _attn_decode(q, kvb, gates, kvc, cache_slc, win_state, win_new, page_table, pos):
    n_dec, n_pages = page_table.shape
    nbp = kvc.shape[2]
    n_win = win_state.shape[2]
    seqs = DECODE_SEQS_PER_STEP
    per_seq = lambda a: pl.BlockSpec((seqs, 1, a.shape[-1]), lambda s, pt: (s, 0, 0))
    win_spec = pl.BlockSpec((seqs, KV_BRANCH_COLS, n_win), lambda s, pt: (s, 0, 0))
    q3, kvb3, g3 = (a.reshape(n_dec, 1, a.shape[-1]) for a in (q, kvb, gates))
    out, win_next = pl.pallas_call(
        functools.partial(_attn_decode_kernel, n_pages=n_pages, seqs=seqs, pos=pos, nbp=nbp),
        grid_spec=pltpu.PrefetchScalarGridSpec(
            num_scalar_prefetch=1,
            grid=(n_dec // seqs,),
            in_specs=_page_specs(n_pages, seqs) + [
                per_seq(q3), per_seq(kvb3), per_seq(g3),
                pl.BlockSpec((2, seqs, nbp, LANES), lambda s, pt: (0, s, 0, 0)),
                win_spec, pl.BlockSpec((seqs, KV_BRANCH_COLS, 1), lambda s, pt: (s, 0, 0)),
            ],
            out_specs=[pl.BlockSpec((seqs, 1, Q_COLS), lambda s, pt: (s, 0, 0)), win_spec],
        ),
        out_shape=[jax.ShapeDtypeStruct((n_dec, 1, Q_COLS), BF16), jax.ShapeDtypeStruct(win_state.shape, F32)],
        compiler_params=_params("arbitrary"),
        name="attn_decode",
    )(page_table, *([cache_slc] * (seqs * n_pages)), q3, kvb3, g3, kvc, win_state, win_new)
    return out.reshape(n_dec, Q_COLS), win_next


def _mix_kernel(*refs, tm, halo, tiles_per_seq):
    if halo:
        (x_ref, xh_ref, o_ref, shift_ref, scale_ref, gate_ref, gpre_ref, gpost_ref, cw_ref,
         wconv_ref, wmerge_ref, wup_ref, wco_ref, wout_ref, y_ref, ulast_ref, u_scr) = refs
        x = x_ref[...]
        xe = jnp.concatenate([xh_ref[...], x], axis=0)
    else:
        (x_ref, um1_ref, um2_ref, o_ref, shift_ref, scale_ref, gate_ref, gpre_ref, gpost_ref, cw_ref,
         wconv_ref, wmerge_ref, wup_ref, wco_ref, wout_ref, y_ref, ulast_ref) = refs
        x = x_ref[...]
        xe = x
    h0 = xe.shape[0] - tm
    a = (_rms(xe, gpre_ref[...]) * (1.0 + scale_ref[...]) + shift_ref[...]).astype(BF16)
    zc = _dot(a, wconv_ref[...])
    u = zc[:, 2 * D_CONV:] * zc[:, :D_CONV]
    cb = zc[h0:, D_CONV:2 * D_CONV]
    if halo:
        keep_halo = jnp.where(pl.program_id(0) % tiles_per_seq == 0, 0.0, 1.0)
        rows = lax.broadcasted_iota(jnp.int32, (xe.shape[0], 1), 0)
        u = jnp.where(rows < h0, u * keep_halo, u)
        u_scr[...] = u
        um1 = u_scr[h0 - 1:h0 - 1 + tm, :]
        um2 = u_scr[h0 - 2:h0 - 2 + tm, :]
        u0 = u[h0:]
        ulast_ref[...] = u[tm:]
    else:
        um1, um2, u0 = um1_ref[...], um2_ref[...], u
        ulast_ref[...] = u
    cw = cw_ref[...]
    y = cw[0:1] * um2 + cw[1:2] * um1 + cw[2:3] * u0
    conv_out = _dot((cb * y).astype(BF16), wco_ref[...])
    attn_out = _dot_tn(o_ref[...], wup_ref[...])
    mg = jax.nn.sigmoid(_dot(a[h0:], wmerge_ref[...]))
    merged = mg[:, :D_MODEL] * attn_out + mg[:, D_MODEL:] * conv_out
    mixed = _dot(merged.astype(BF16), wout_ref[...])
    y_ref[...] = x + gate_ref[...] * _rms(mixed, gpost_ref[...])


def _mix(x, o_att, mod, g3, conv_w, weights, tm, tiles_per_seq=None, prev=None):
    m = x.shape[0]
    halo = prev is None
    h0 = MIX_HALO
    row = lambda n: pl.BlockSpec((tm, n), lambda i: (i, 0))
    full = lambda a: pl.BlockSpec(a.shape, lambda i: (0,) * a.ndim, pipeline_mode=pl.Buffered(1))
    att = pl.BlockSpec((Q_COLS, tm), lambda i: (0, i))
    common = [att, mod.spec(3), mod.spec(4), mod.spec(5), _gspec(2), _gspec(3), full(conv_w)]
    common += [full(w) for w in weights]
    common_args = [o_att, mod.arr, mod.arr, mod.arr, g3, g3, conv_w, *weights]
    if halo:
        hb = tm // h0
        in_specs = [row(D_MODEL), pl.BlockSpec((h0, D_MODEL), lambda i: (jnp.maximum(i * hb - 1, 0), 0))] + common
        args = [x, x] + common_args
        scratch = [pltpu.VMEM((tm + h0, D_CONV), F32)]
        ulast = (jax.ShapeDtypeStruct((m // tm * h0, D_CONV), F32), pl.BlockSpec((h0, D_CONV), lambda i: (i, 0)))
    else:
        in_specs = [row(D_MODEL), row(D_CONV), row(D_CONV)] + common
        args = [x, prev[0], prev[1]] + common_args
        scratch = []
        ulast = (jax.ShapeDtypeStruct((m, D_CONV), F32), row(D_CONV))
    return pl.pallas_call(
        functools.partial(_mix_kernel, tm=tm, halo=halo, tiles_per_seq=tiles_per_seq),
        grid=(m // tm,),
        in_specs=in_specs,
        out_specs=[row(D_MODEL), ulast[1]],
        out_shape=[jax.ShapeDtypeStruct((m, D_MODEL), F32), ulast[0]],
        scratch_shapes=scratch,
        compiler_params=_params("parallel"),
        name="mix",
    )(*args)


def _head_pair_perm():
    order = []
    for j in range(HEADS_PER_GROUP):
        for g in range(N_KV_HEADS):
            head = g * HEADS_PER_GROUP + j
            order.extend(range(head * HEAD_DIM, (head + 1) * HEAD_DIM))
    return jnp.array(order, dtype=jnp.int32)


def _layer_weights(w_in, w_attn_up, w_conv_out, w_out):
    perm = _head_pair_perm()
    o1 = Q_COLS
    o2 = o1 + KV_COLS
    o3 = o2 + NSA_GATE_COLS
    o4 = o3 + CONV_COLS
    w_q = w_in[:, :o1][:, perm]
    w_g = jnp.pad(w_in[:, o2:o3], ((0, 0), (0, GATE_PAD - NSA_GATE_COLS)))
    w_qkvg = jnp.concatenate([w_q, w_in[:, o1:o2], w_g], axis=1).astype(BF16)
    mix_w = (w_in[:, o3:o4].astype(BF16), w_in[:, o4:].astype(BF16), w_attn_up[perm].astype(BF16),
             w_conv_out.astype(BF16), w_out.astype(BF16))
    return w_qkvg, mix_w


def kernel(x_prompt, x_sample, c_prompt, c_sample, cache_cmp_kv, cache_slc_kv, state_win_kv, state_conv,
           page_table, w_ada, b_ada, g_norm, w_ffn_gu, w_ffn_down, w_in, phi_pe, phi_w1, phi_b1, phi_w2,
           w_attn_up, conv_w, w_conv_out, w_out):
    n_seq, seq, _ = x_prompt.shape
    n_dec = x_sample.shape[0]
    assert w_ada.shape[0] == 1
    l = 0

    mod_all = _ada(jnp.concatenate([c_prompt, c_sample], axis=0), w_ada[l], b_ada[l])
    g3 = g_norm[l].reshape(6, 1, D_MODEL)
    w_gu = w_ffn_gu[l].astype(BF16)
    w_down = w_ffn_down[l].astype(BF16)
    w_qkvg, mix_w = _layer_weights(w_in[l], w_attn_up[l], w_conv_out[l], w_out[l])
    cwp = _compress_weights_paged(phi_pe[l], phi_w1[l], phi_b1[l], phi_w2[l])

    tm = 512
    tq = 256
    m = n_seq * seq
    mod_p = _Mod(mod_all[:n_seq], per_row=False, tiles_per_seq=seq // tm)
    tm_ffn, ffn_sub = 1024, 2
    mod_pf = _Mod(mod_all[:n_seq], per_row=False, tiles_per_seq=seq // tm_ffn)
    x0 = x_prompt.reshape(m, D_MODEL)
    x1 = _ffn(x0, mod_pf, g3, w_gu, w_down, 0, tm_ffn, ffn_sub)
    tables = _rope_tables(jnp.arange(seq, dtype=jnp.int32))
    _, qt, cmp_pages, cmpt, slct, wint, kvtb, kvb, _, gatest = _proj(x1, mod_p, g3, tables, w_qkvg, tm)
    nb = seq // NSA_BLOCK
    kvc = _gather_compress(cmp_pages, None, cwp, pages_per_seq=seq // PAGE_SIZE)
    kc = kvc[0].reshape(n_seq * nb, LANES).astype(BF16)
    vct = kvc[1].transpose(0, 2, 1).astype(BF16)
    o_att = _attn_prompt(qt, gatest, kc, vct, kvb, kvtb, n_seq, seq, tq)
    x2, ulast = _mix(x1, o_att, mod_pf, g3, conv_w[l], mix_w, tm_ffn, tiles_per_seq=seq // tm_ffn)
    y_prompt = _ffn(x2, mod_pf, g3, w_gu, w_down, 2, tm_ffn, ffn_sub).reshape(n_seq, seq, D_MODEL)

    def token_major(t):
        n, _, tt = t.shape
        return t.reshape(n, 2, N_KV_HEADS, HEAD_DIM, tt).transpose(0, 4, 1, 2, 3)[None]

    def feature_major(a):
        n, tt = a.shape[:2]
        return a.transpose(0, 2, 3, 4, 1).reshape(n, KV_BRANCH_COLS, tt)

    cmp_kv_prompt = token_major(cmpt)
    slc_kv_prompt = token_major(slct)
    win_keep = min(NSA_WINDOW, seq)
    win_kv_prompt = token_major(wint[:, :, seq - win_keep:])
    conv_prompt = ulast.reshape(n_seq, seq // tm_ffn, MIX_HALO, D_CONV)[:, -1, MIX_HALO - (CONV_WIDTH - 1):][None]

    n_pages = page_table.shape[1]
    past_len = n_pages * PAGE_SIZE
    pos_s = past_len + jnp.arange(x_sample.shape[1], dtype=jnp.int32)
    assert x_sample.shape[1] == 1 and past_len % NSA_BLOCK == 0 and state_win_kv.shape[2] <= past_len
    mod_s = _Mod(mod_all[n_seq:], per_row=True)
    xs1 = _ffn(x_sample.reshape(n_dec, D_MODEL), mod_s, g3, w_gu, w_down, 0, n_dec)
    tables_s = _rope_tables(jnp.broadcast_to(pos_s, (n_dec,)))
    q_s, _, _, cmpt_s, slct_s, wint_s, _, kvb_s, gates_s, _ = _proj(xs1, mod_s, g3, tables_s, w_qkvg, n_dec)
    nb_past = past_len // NSA_BLOCK
    nb_pad = -(-(nb_past + 1) // NSA_BLOCK) * NSA_BLOCK
    kvc_s = _gather_compress(feature_major(cache_cmp_kv[l]), page_table, cwp, new_tokens=cmpt_s[0])
    kvc_s = jnp.concatenate(
        [kvc_s[:, :, :nb_past + 1], jnp.zeros((2, n_dec, nb_pad - nb_past - 1, LANES), F32)], axis=2)
    win_state = feature_major(state_win_kv[l])
    assert win_state.shape[2] == NSA_WINDOW
    o_att_s, win_next = _attn_decode(q_s, kvb_s, gates_s, kvc_s, feature_major(cache_slc_kv[l]), win_state,
                                     wint_s[0].T[:, :, None], page_table, past_len)
    conv_state = state_conv[l]
    xs2, u_s = _mix(xs1, o_att_s.T, mod_s, g3, conv_w[l], mix_w, n_dec,
                    prev=(conv_state[:, CONV_WIDTH - 2], conv_state[:, CONV_WIDTH - 3]))
    y_sample = _ffn(xs2, mod_s, g3, w_gu, w_down, 2, n_dec).reshape(x_sample.shape)

    kvs = (1, n_dec, 1, 2, N_KV_HEADS, HEAD_DIM)
    conv_sample = jnp.concatenate([conv_state[:, 1:], u_s[:, None, :]], axis=1)[None]
    return (y_prompt, y_sample, cmp_kv_prompt, slc_kv_prompt, win_kv_prompt, conv_prompt,
            token_major(cmpt_s).reshape(kvs), token_major(slct_s).reshape(kvs), token_major(win_next), conv_sample)
```

```python
import functools

import jax
import jax.numpy as jnp
from jax import lax
from jax.experimental import pallas as pl
from jax.experimental.pallas import tpu as pltpu

D_MODEL = 1024
N_HEADS = 8
HEAD_DIM = 64
N_KV_HEADS = 2
HEADS_PER_GROUP = N_HEADS // N_KV_HEADS
ROT_DIM = HEAD_DIM // 4
ROPE_THETA = 500000.0
NSA_BLOCK = 64
NSA_TOP_N = 16
NSA_WINDOW = 512
PHI_HIDDEN = 2 * HEAD_DIM
D_CONV = 512
CONV_WIDTH = 3
D_FF = 2816
FFN_HALF = 0.5
NORM_EPS = 1e-6
PAGE_SIZE = 128
Q_COLS = N_HEADS * HEAD_DIM
KV_BRANCH_COLS = 2 * N_KV_HEADS * HEAD_DIM
KV_COLS = 3 * KV_BRANCH_COLS
NSA_GATE_COLS = 3 * N_HEADS
CONV_COLS = 3 * D_CONV
MERGE_COLS = 2 * D_MODEL
GD_ROWS = N_KV_HEADS * HEAD_DIM

LANES = 128
BF16_SUBLANES = 16
GATE_PAD = LANES
QKVG_COLS = Q_COLS + KV_COLS + GATE_PAD
LOG2_E = 1.4426950408889634
MASK_INIT = -1e30
MASK_BIAS = -2e30
VMEM_LIMIT = 56 * 1024 * 1024
SLAB_PITCH = HEAD_DIM + 8
FFN_CHUNKS = 2
DECODE_SEQS_PER_STEP = 4
MIX_HALO = 16

BF16 = jnp.bfloat16
F32 = jnp.float32


def _dot(a, b):
    return jnp.dot(a, b, preferred_element_type=F32)


def _dot_tn(a, b):
    return lax.dot_general(a, b, (((0,), (0,)), ((), ())), preferred_element_type=F32)


def _dot_nt(a, b):
    return lax.dot_general(a, b, (((1,), (1,)), ((), ())), preferred_element_type=F32)


def _rms(x, g):
    return x * lax.rsqrt(jnp.mean(x * x, axis=-1, keepdims=True) + NORM_EPS) * g


def _silu(x):
    return x * jax.nn.sigmoid(x)


def _params(*sem):
    return pltpu.CompilerParams(dimension_semantics=sem, vmem_limit_bytes=VMEM_LIMIT)


def _ada_kernel(c_ref, w_ref, b_ref, o_ref):
    c = _silu(c_ref[...]).astype(BF16)
    o_ref[...] = _dot(c, w_ref[...].astype(BF16)) + b_ref[...]


def _ada(c, w_ada, b_ada):
    rows = c.shape[0]
    n = w_ada.shape[1]
    tn = 9 * LANES
    return pl.pallas_call(
        _ada_kernel,
        grid=(n // tn,),
        in_specs=[
            pl.BlockSpec((rows, D_MODEL), lambda j: (0, 0)),
            pl.BlockSpec((D_MODEL, tn), lambda j: (0, j)),
            pl.BlockSpec((1, tn), lambda j: (0, j)),
        ],
        out_specs=pl.BlockSpec((rows, tn), lambda j: (0, j)),
        out_shape=jax.ShapeDtypeStruct((rows, n), F32),
        compiler_params=_params("parallel"),
        name="ada",
    )(c, w_ada, b_ada.reshape(1, n))


class _Mod:
    def __init__(self, mod, per_row, tiles_per_seq=1):
        self.per_row = per_row
        self.tiles_per_seq = tiles_per_seq
        self.arr = mod if per_row else mod.reshape(mod.shape[0] * 9, 1, D_MODEL)

    def spec(self, k):
        if self.per_row:
            return pl.BlockSpec((self.arr.shape[0], D_MODEL), lambda i, *_: (0, k))
        tps = self.tiles_per_seq
        return pl.BlockSpec((None, 1, D_MODEL), lambda i, *_: ((i // tps) * 9 + k, 0, 0))


def _gspec(k):
    return pl.BlockSpec((None, 1, D_MODEL), lambda i, *_: (k, 0, 0))


def _ffn_kernel(x_ref, shift_ref, scale_ref, gate_ref, gpre_ref, gpost_ref, wgu_ref, wd_ref, o_ref,
                *, n_sub, n_chunk):
    ts = x_ref.shape[0] // n_sub
    tf = D_FF // n_chunk
    for s in range(n_sub):
        rows = slice(s * ts, (s + 1) * ts)
        mod = lambda ref: ref[...] if ref.shape[0] == 1 else ref[rows, :]
        x = x_ref[rows, :]
        a = (_rms(x, gpre_ref[...]) * (1.0 + mod(scale_ref)) + mod(shift_ref)).astype(BF16)
        y = None
        for c in range(n_chunk):
            g = _dot(a, wgu_ref[:, c * tf:(c + 1) * tf])
            u = _dot(a, wgu_ref[:, D_FF + c * tf:D_FF + (c + 1) * tf])
            d = _dot((_silu(g) * u).astype(BF16), wd_ref[c * tf:(c + 1) * tf, :])
            y = d if y is None else y + d
        o_ref[rows, :] = x + FFN_HALF * mod(gate_ref) * _rms(y, gpost_ref[...])


def _ffn(x, mod, g3, w_gu, w_down, sub, tm, n_sub=1):
    m = x.shape[0]
    row = pl.BlockSpec((tm, D_MODEL), lambda i: (i, 0))
    const = lambda a: pl.BlockSpec((None,) + a.shape[1:], lambda i: (sub // 2, 0, 0), pipeline_mode=pl.Buffered(1))
    return pl.pallas_call(
        functools.partial(_ffn_kernel, n_sub=n_sub, n_chunk=FFN_CHUNKS),
        grid=(m // tm,),
        in_specs=[
            row, mod.spec(3 * sub), mod.spec(3 * sub + 1), mod.spec(3 * sub + 2),
            _gspec(2 * sub), _gspec(2 * sub + 1), const(w_gu), const(w_down),
        ],
        out_specs=row,
        out_shape=jax.ShapeDtypeStruct((m, D_MODEL), F32),
        compiler_params=_params("parallel"),
        name=f"ffn{sub}",
    )(x, mod.arr, mod.arr, mod.arr, g3, g3, w_gu, w_down)


def _rope(x, cos, sin_lo, sin_hi):
    return x * cos + pltpu.roll(x, LANES - ROT_DIM // 2, 1) * sin_lo + pltpu.roll(x, ROT_DIM // 2, 1) * sin_hi


def _proj_kernel(x_ref, shift_ref, scale_ref, g_ref, cos_ref, slo_ref, shi_ref, w_ref,
                 q_ref, qt_ref, cmp_ref, cmpt_ref, slct_ref, wint_ref, kvtb_ref, kvb_ref, gate_ref, gatet_ref):
    a = (_rms(x_ref[...], g_ref[...]) * (1.0 + scale_ref[...]) + shift_ref[...]).astype(BF16)
    z = _dot(a, w_ref[...])
    cos, slo, shi = cos_ref[...], slo_ref[...], shi_ref[...]
    scale = HEAD_DIM ** -0.5 * LOG2_E
    for s in range(Q_COLS // LANES):
        q = _rope(z[:, s * LANES:(s + 1) * LANES], cos, slo, shi) * scale
        q_ref[:, s * LANES:(s + 1) * LANES] = q.astype(BF16)
        qt_ref[s * LANES:(s + 1) * LANES, :] = q.T.astype(BF16)
    kvt_refs = (cmpt_ref, slct_ref, wint_ref)
    for br in range(3):
        c0 = Q_COLS + br * KV_BRANCH_COLS
        k = _rope(z[:, c0:c0 + LANES], cos, slo, shi)
        v = z[:, c0 + LANES:c0 + 2 * LANES]
        for r0, x in ((0, k), (LANES, v)):
            xt = x.T
            kvt_refs[br][r0:r0 + LANES, :] = xt
            kvtb_ref[br, r0:r0 + LANES, :] = xt.astype(BF16)
            if br == 0:
                for p in range(cmp_ref.shape[0]):
                    cmp_ref[p, r0:r0 + LANES, :] = xt[:, p * PAGE_SIZE:(p + 1) * PAGE_SIZE]
        kvb_ref[:, br * KV_BRANCH_COLS:br * KV_BRANCH_COLS + LANES] = k.astype(BF16)
        kvb_ref[:, br * KV_BRANCH_COLS + LANES:(br + 1) * KV_BRANCH_COLS] = v.astype(BF16)
    gate = jax.nn.sigmoid(z[:, Q_COLS + KV_COLS:])
    gate_ref[...] = gate
    gatet_ref[...] = gate.T


def _proj(x, mod, g3, tables, w_qkvg, tm):
    assert tm % PAGE_SIZE == 0
    m = x.shape[0]
    t_rows = tables[0].shape[0]
    nt = t_rows // tm
    n_seq = m // t_rows
    row = lambda n: pl.BlockSpec((tm, n), lambda i: (i, 0))
    tab = pl.BlockSpec((tm, LANES), lambda i: (i % nt, 0))
    col = lambda n: pl.BlockSpec((n, tm), lambda i: (0, i))
    kvt_spec = pl.BlockSpec((None, KV_BRANCH_COLS, tm), lambda i: (i // nt, 0, i % nt))
    kvtb_spec = pl.BlockSpec((3, None, KV_BRANCH_COLS, tm), lambda i: (0, i // nt, 0, i % nt))
    return pl.pallas_call(
        _proj_kernel,
        grid=(m // tm,),
        in_specs=[row(D_MODEL), mod.spec(3), mod.spec(4), _gspec(2), tab, tab, tab,
                  pl.BlockSpec((D_MODEL, QKVG_COLS), lambda i: (0, 0))],
        out_specs=[row(Q_COLS), col(Q_COLS),
                   pl.BlockSpec((tm // PAGE_SIZE, KV_BRANCH_COLS, PAGE_SIZE), lambda i: (i, 0, 0)),
                   kvt_spec, kvt_spec, kvt_spec, kvtb_spec, row(KV_COLS), row(GATE_PAD), col(GATE_PAD)],
        out_shape=[
            jax.ShapeDtypeStruct((m, Q_COLS), BF16),
            jax.ShapeDtypeStruct((Q_COLS, m), BF16),
            jax.ShapeDtypeStruct((m // PAGE_SIZE, KV_BRANCH_COLS, PAGE_SIZE), F32),
            *[jax.ShapeDtypeStruct((n_seq, KV_BRANCH_COLS, t_rows), F32)] * 3,
            jax.ShapeDtypeStruct((3, n_seq, KV_BRANCH_COLS, t_rows), BF16),
            jax.ShapeDtypeStruct((m, KV_COLS), BF16),
            jax.ShapeDtypeStruct((m, GATE_PAD), F32),
            jax.ShapeDtypeStruct((GATE_PAD, m), F32),
        ],
        compiler_params=_params("parallel"),
        name="proj",
    )(x, mod.arr, mod.arr, g3, *tables, w_qkvg)


def _rope_tables(pos):
    half = ROT_DIM // 2
    inv = jnp.power(ROPE_THETA, -jnp.arange(half, dtype=F32) * 2.0 / ROT_DIM)
    ang = pos.astype(F32)[:, None] * inv[None, :]
    cos, sin = jnp.cos(ang), jnp.sin(ang)
    n = pos.shape[0]
    ones = jnp.ones((n, HEAD_DIM - ROT_DIM), F32)
    zeros = jnp.zeros((n, HEAD_DIM - ROT_DIM), F32)
    zh = jnp.zeros((n, half), F32)
    c = jnp.concatenate([cos, cos, ones], axis=1)
    lo = jnp.concatenate([-sin, zh, zeros], axis=1)
    hi = jnp.concatenate([zh, sin, zeros], axis=1)
    return tuple(jnp.tile(t, (1, LANES // HEAD_DIM)) for t in (c, lo, hi))


def _compress_weights_paged(phi_pe, phi_w1, phi_b1, phi_w2):
    blocks = PAGE_SIZE // NSA_BLOCK
    assert blocks == 2
    eye = jnp.eye(blocks, dtype=F32)
    wt = phi_w1.reshape(2, NSA_BLOCK, HEAD_DIM, PHI_HIDDEN).transpose(0, 2, 1, 3).astype(BF16)
    zero = jnp.zeros_like(wt)
    w1 = jnp.stack([jnp.concatenate([wt, zero], axis=-1), jnp.concatenate([zero, wt], axis=-1)], axis=2)
    w1 = w1.reshape(2, HEAD_DIM * PAGE_SIZE, blocks * PHI_HIDDEN)
    w2 = jnp.einsum("bc,kjd->kbjcd", eye, phi_w2).reshape(2, blocks * PHI_HIDDEN, blocks * HEAD_DIM)
    pe = jnp.tile(phi_pe.transpose(0, 2, 1), (1, 1, blocks)).reshape(2, HEAD_DIM, 1, PAGE_SIZE)
    b1 = jnp.tile(phi_b1, (1, blocks)).reshape(2, 1, blocks * PHI_HIDDEN)
    return pe, w1, b1, w2.astype(BF16)


def _gather_compress_kernel(pt_ref, *refs, seqs, n_pages, group, steps, contiguous, new_tokens):
    del pt_ref
    if contiguous:
        pages = [refs[0].at[k] for k in range(seqs * n_pages)]
        refs = refs[1:]
    else:
        pages, refs = refs[:seqs * n_pages], refs[seqs * n_pages:]
    if new_tokens:
        newt_ref, refs = refs[0], refs[1:]
    pe_ref, w1_ref, b1_ref, w2_ref, o_ref, xk_scr, xv_scr, c_scr = refs
    x_scr = (xk_scr, xv_scr)
    slot = pl.program_id(0) % steps
    per_seq = n_pages + (1 if new_tokens else 0)
    for r in range(seqs):
        slabs = [pages[r * n_pages + k] for k in range(n_pages)]
        if new_tokens:
            seq_lane = lax.broadcasted_iota(jnp.int32, (1, newt_ref.shape[1]), 1) == pl.program_id(0) * seqs + r
            col = jnp.sum(jnp.where(seq_lane, newt_ref[...], 0.0), axis=1, keepdims=True)
            first = lax.broadcasted_iota(jnp.int32, (1, PAGE_SIZE), 1) == 0
            slabs.append(jnp.where(first, col, 0.0))
        for k, slab in enumerate(slabs):
            for kv in range(2):
                for g in range(N_KV_HEADS):
                    page = (slot * seqs + r) * per_seq + k
                    r0 = pl.multiple_of((page * N_KV_HEADS + g) * SLAB_PITCH, SLAB_PITCH)
                    s0 = kv * GD_ROWS + g * HEAD_DIM
                    x_scr[kv][pl.ds(r0, HEAD_DIM), :] = slab[s0:s0 + HEAD_DIM, :]

    @pl.when(slot == steps - 1)
    def _():
        rows = group * N_KV_HEADS
        accs = [jnp.zeros((rows, w1_ref.shape[2]), F32) for _ in range(2)]
        for d in range(0, HEAD_DIM, 2):
            for kv in range(2):
                xs = []
                for dd in (d, d + 1):
                    x = x_scr[kv][pl.ds(dd, rows, stride=SLAB_PITCH), :]
                    xs.append((x + pe_ref[kv, dd]).astype(BF16))
                accs[kv] = accs[kv] + _dot(jnp.concatenate(xs, axis=1),
                                           w1_ref[kv, d * PAGE_SIZE:(d + 2) * PAGE_SIZE, :])
        for kv in range(2):
            h = _silu(accs[kv] + b1_ref[kv])
            c_scr[...] = _dot(h.astype(BF16), w2_ref[kv])
            g0 = c_scr[pl.ds(0, rows // 2, stride=2), :]
            g1 = c_scr[pl.ds(1, rows // 2, stride=2), :]
            low = lax.broadcasted_iota(jnp.int32, (1, LANES), 1) < HEAD_DIM
            o_ref.at[kv][pl.ds(0, rows // 2, stride=2), :] = jnp.where(low, g0, pltpu.roll(g1, HEAD_DIM, 1))
            o_ref.at[kv][pl.ds(1, rows // 2, stride=2), :] = jnp.where(low, pltpu.roll(g0, HEAD_DIM, 1), g1)


def _page_specs(n_pages, seqs=1):
    return [pl.BlockSpec((None, KV_BRANCH_COLS, PAGE_SIZE), lambda s, pt, r=r, k=k: (pt[s * seqs + r, k], 0, 0))
            for r in range(seqs) for k in range(n_pages)]


def _gather_compress(cache_t, page_table, cwp, pages_per_seq=None, new_tokens=None):
    pe, w1, b1, w2 = cwp
    contiguous = page_table is None
    seqs = 4
    if contiguous:
        n_pages = pages_per_seq
        n_dec = cache_t.shape[0] // n_pages
        page_table = jnp.zeros((1, 1), jnp.int32)
        page_specs = [pl.BlockSpec((seqs * n_pages, KV_BRANCH_COLS, PAGE_SIZE), lambda s, pt: (s, 0, 0))]
        page_args = [cache_t]
    else:
        n_dec, n_pages = page_table.shape
        page_specs = _page_specs(n_pages, seqs)
        page_args = [cache_t] * (seqs * n_pages)
    const = lambda a: pl.BlockSpec(a.shape, lambda s, pt: (0,) * a.ndim, pipeline_mode=pl.Buffered(1))
    per_seq = n_pages
    if new_tokens is not None:
        per_seq += 1
        page_specs.append(const(new_tokens))
        page_args.append(new_tokens)
    steps = 2
    group = steps * seqs * per_seq
    assert n_dec % (steps * seqs) == 0
    rows = group * N_KV_HEADS
    blocks = PAGE_SIZE // NSA_BLOCK
    assert blocks == 2 and N_KV_HEADS == 2
    out = pl.pallas_call(
        functools.partial(_gather_compress_kernel, seqs=seqs, n_pages=n_pages, group=group, steps=steps,
                          contiguous=contiguous, new_tokens=new_tokens is not None),
        grid_spec=pltpu.PrefetchScalarGridSpec(
            num_scalar_prefetch=1,
            grid=(n_dec // seqs,),
            in_specs=page_specs + [const(pe), const(w1), const(b1), const(w2)],
            out_specs=pl.BlockSpec((2, rows, blocks * HEAD_DIM), lambda s, pt: (0, s // steps, 0)),
            scratch_shapes=[pltpu.VMEM((group * N_KV_HEADS * SLAB_PITCH, PAGE_SIZE), F32)] * 2
            + [pltpu.VMEM((rows, LANES), F32)],
        ),
        out_shape=jax.ShapeDtypeStruct((2, n_dec * per_seq * N_KV_HEADS, blocks * HEAD_DIM), F32),
        compiler_params=_params("arbitrary"),
        name="gather_compress",
    )(page_table, *page_args, pe, w1, b1, w2)
    return out.reshape(2, n_dec, per_seq * blocks, N_KV_HEADS * HEAD_DIM)


def _group_queries_t(qt, g):
    t = qt.shape[1]
    zeros = jnp.zeros((HEAD_DIM, t), qt.dtype)
    cols = []
    for j in range(HEADS_PER_GROUP):
        qh = qt[j * LANES + g * HEAD_DIM:j * LANES + (g + 1) * HEAD_DIM, :]
        cols.append(jnp.concatenate([qh, zeros] if g == 0 else [zeros, qh], axis=0))
    return jnp.concatenate(cols, axis=1)


def _tile_heads(x):
    return jnp.concatenate([x] * HEADS_PER_GROUP, axis=1)


def _select_blocks_t(imp, nblk, n_sel):
    nb = imp.shape[0]
    rank = jnp.zeros(imp.shape, F32)
    for m in range(nb):
        row = imp[m:m + 1, :]
        beats = jnp.where(row > imp, 1.0, jnp.where(row == imp, jnp.where(m < nblk, 1.0, 0.0), 0.0))
        rank = rank + beats
    return jnp.where(rank < n_sel, jnp.where(imp >= 0.0, 1.0, 0.0), 0.0)


def _select_blocks(imp, nblk, n_sel):
    nb = imp.shape[1]
    rank = jnp.zeros(imp.shape, F32)
    for m in range(nb):
        col = imp[:, m:m + 1]
        beats = jnp.where(col > imp, 1.0, jnp.where(col == imp, jnp.where(m < nblk, 1.0, 0.0), 0.0))
        rank = rank + beats
    return jnp.where(rank < n_sel, jnp.where(imp >= 0.0, 1.0, 0.0), 0.0)


def _attn_tile(carry, k, q, vt_ref, ks, tk, bias):
    m, l, acc = carry
    s = _dot(k, q)
    if bias is not None:
        s = s + bias
    m_new = jnp.maximum(m, jnp.max(s, axis=0, keepdims=True))
    alpha = jnp.exp2(m - m_new)
    pb = jnp.exp2(s - m_new).astype(BF16)
    half = pb.shape[1] // N_KV_HEADS
    ones = jnp.ones((BF16_SUBLANES, tk), BF16)
    pv = [_dot(jnp.concatenate([vt_ref[GD_ROWS + g * HEAD_DIM:GD_ROWS + (g + 1) * HEAD_DIM, pl.ds(ks, tk)], ones],
                               axis=0), pb[:, g * half:(g + 1) * half]) for g in range(N_KV_HEADS)]
    pv = jnp.concatenate(pv, axis=1)
    return m_new, alpha * l + pv[HEAD_DIM:HEAD_DIM + 1], alpha * acc + pv[:HEAD_DIM]


def _maybe(cond, fn, carry):
    return lax.fori_loop(0, jnp.where(cond, 1, 0), lambda _, c: fn(c), carry)


def _attn_kernel(qt_ref, gatet_ref, kc_ref, vct_ref, slc_k_ref, slc_vt_ref, win_k_ref, win_vt_ref, o_ref,
                 *, tq, nb):
    i = pl.program_id(1)
    p0 = i * tq
    tk = tq
    hq = HEADS_PER_GROUP * tq
    n_win_tiles = NSA_WINDOW // tk
    pos = p0 + lax.broadcasted_iota(jnp.int32, (1, tq), 1)
    nblk = lax.broadcasted_iota(jnp.int32, (nb, 1), 0)
    key_off = lax.broadcasted_iota(jnp.int32, (tk, 1), 0)
    qt = qt_ref[...]
    q_all = jnp.concatenate([_group_queries_t(qt, g) for g in range(N_KV_HEADS)], axis=1)
    tile_all = lambda x: jnp.concatenate([x] * N_HEADS, axis=1)

    cvalid = (nblk + 1) * NSA_BLOCK <= tile_all(pos) + 1
    sc = jnp.where(cvalid, _dot(kc_ref[...], q_all), MASK_INIT)
    e = jnp.exp2(sc - jnp.max(sc, axis=0, keepdims=True))
    pc = jnp.where(cvalid, e / jnp.sum(e, axis=0, keepdims=True), 0.0)
    pcb = pc.astype(BF16)
    o_cmp = jnp.concatenate([_dot(vct_ref[g * HEAD_DIM:(g + 1) * HEAD_DIM, :], pcb[:, g * hq:(g + 1) * hq])
                             for g in range(N_KV_HEADS)], axis=1)

    cur = pos // NSA_BLOCK
    forced = HEADS_PER_GROUP + 1.0
    sel_rows = []
    for g in range(N_KV_HEADS):
        imp = functools.reduce(jnp.add, [pc[:, g * hq + j * tq:g * hq + (j + 1) * tq]
                                         for j in range(HEADS_PER_GROUP)])
        imp = jnp.where(nblk == 0, forced, jnp.where(nblk == cur, forced, jnp.where(nblk == cur - 1, forced, imp)))
        imp = jnp.where(nblk * NSA_BLOCK <= pos, imp, -1.0)
        sel = _select_blocks_t(imp, nblk, NSA_TOP_N)
        sel_rows.append(_tile_heads(jnp.where(sel > 0.5, 0.0, MASK_BIAS)))
    sel_rows = jnp.concatenate(sel_rows, axis=1)
    sel_rows = jnp.concatenate([sel_rows, jnp.zeros((LANES - nb, sel_rows.shape[1]), F32)], axis=0)
    q_sel = jnp.concatenate([q_all, sel_rows.astype(BF16)], axis=0)
    blk_lane = lax.broadcasted_iota(jnp.int32, (1, LANES), 1)

    def slc_keys(ks):
        onehot = jnp.where(blk_lane == (ks + key_off) // NSA_BLOCK, 1.0, 0.0).astype(BF16)
        return jnp.concatenate([slc_k_ref[pl.ds(ks, tk), :GD_ROWS], onehot], axis=1)

    cols = q_all.shape[1]
    init = (jnp.full((1, cols), MASK_INIT, F32), jnp.zeros((1, cols), F32), jnp.zeros((HEAD_DIM, cols), F32))
    causal = tile_all(jnp.where(p0 + key_off <= pos, 0.0, MASK_BIAS))
    k_diag = pl.multiple_of(p0, tk)

    def slc_body(j, c):
        ks = pl.multiple_of(j * tk, tk)
        return _attn_tile(c, slc_keys(ks), q_sel, slc_vt_ref, ks, tk, None)

    c = lax.fori_loop(0, i, slc_body, init)
    _, l, acc = _attn_tile(c, slc_keys(k_diag), q_sel, slc_vt_ref, k_diag, tk, causal)
    o_slc = acc / l

    def win_tile(c, ks, bias):
        return _attn_tile(c, win_k_ref[pl.ds(ks, tk), :GD_ROWS], q_all, win_vt_ref, ks, tk, bias)

    def edge_tile(c):
        ks = pl.multiple_of((i - n_win_tiles) * tk, tk)
        bias = tile_all(jnp.where(pos - (ks + key_off) < NSA_WINDOW, 0.0, MASK_BIAS))
        return win_tile(c, ks, bias)

    c = _maybe(i >= n_win_tiles, edge_tile, init)
    c = lax.fori_loop(jnp.maximum(i - n_win_tiles + 1, 0), i,
                      lambda j, c: win_tile(c, pl.multiple_of(j * tk, tk), None), c)
    _, l, acc = win_tile(c, k_diag, causal)
    o_win = acc / l

    gates = gatet_ref[...]
    for g in range(N_KV_HEADS):
        for j in range(HEADS_PER_GROUP):
            head = g * HEADS_PER_GROUP + j
            cs = slice(g * hq + j * tq, g * hq + (j + 1) * tq)
            o = (gates[head:head + 1] * o_cmp[:, cs]
                 + gates[N_HEADS + head:N_HEADS + head + 1] * o_slc[:, cs]
                 + gates[2 * N_HEADS + head:2 * N_HEADS + head + 1] * o_win[:, cs])
            r0 = j * LANES + g * HEAD_DIM
            o_ref[r0:r0 + HEAD_DIM, :] = o.astype(o_ref.dtype)


def _attn_prompt(qt, gatest, kc, vct, kvb, kvtb, n_seq, seq, tq):
    m = qt.shape[1]
    nb = seq // NSA_BLOCK
    nq = seq // tq
    assert seq % tq == 0 and NSA_WINDOW % tq == 0 and tq % NSA_BLOCK == 0 and nb <= LANES
    col = lambda n: pl.BlockSpec((n, tq), lambda b, i: (0, b * nq + i))
    return pl.pallas_call(
        functools.partial(_attn_kernel, tq=tq, nb=nb),
        grid=(n_seq, nq),
        in_specs=[col(Q_COLS), col(4 * N_HEADS),
                  pl.BlockSpec((nb, LANES), lambda b, i: (b, 0)),
                  pl.BlockSpec((None, LANES, nb), lambda b, i: (b, 0, 0)),
                  pl.BlockSpec((seq, KV_BRANCH_COLS), lambda b, i: (b, 1)),
                  pl.BlockSpec((None, None, KV_BRANCH_COLS, seq), lambda b, i: (1, b, 0, 0)),
                  pl.BlockSpec((seq, KV_BRANCH_COLS), lambda b, i: (b, 2)),
                  pl.BlockSpec((None, None, KV_BRANCH_COLS, seq), lambda b, i: (2, b, 0, 0))],
        out_specs=col(Q_COLS),
        out_shape=jax.ShapeDtypeStruct((Q_COLS, m), BF16),
        compiler_params=_params("parallel", "arbitrary"),
        name="attn_prompt",
    )(qt, gatest, kc, vct, kvb, kvtb, kvb, kvtb)


def _softmax_pv(score_tiles, value_tiles):
    m = functools.reduce(jnp.maximum, [jnp.max(s, axis=-1, keepdims=True) for s in score_tiles])
    ps = [jnp.exp2(s - m) for s in score_tiles]
    l = functools.reduce(jnp.add, [jnp.sum(p, axis=-1, keepdims=True) for p in ps])
    o = functools.reduce(jnp.add, [(_dot_nt if fm else _dot)(p.astype(BF16), v)
                                   for p, (v, fm) in zip(ps, value_tiles)])
    return o / l


def _attn_decode_kernel(pt_ref, *refs, n_pages, seqs, pos, nbp):
    del pt_ref
    pages = refs[:seqs * n_pages]
    q_ref, kvb_ref, gate_ref, kvc_ref, win_ref, wnew_ref, o_ref, wnext_ref = refs[seqs * n_pages:]
    for r in range(seqs):
        _attn_decode_one(pages[r * n_pages:(r + 1) * n_pages], q_ref.at[r], kvb_ref.at[r], gate_ref.at[r],
                         kvc_ref.at[:, r], win_ref.at[r], wnew_ref.at[r], o_ref.at[r], wnext_ref.at[r],
                         pos=pos, nbp=nbp)


def _attn_decode_one(pages, q_ref, kvb_ref, gate_ref, kvc_ref, win_ref, wnew_ref, o_ref, wnext_ref, *, pos, nbp):
    n_pages = len(pages)
    tk = PAGE_SIZE
    lane = lax.broadcasted_iota(jnp.int32, (1, LANES), 1)
    head = lax.broadcasted_iota(jnp.int32, (N_HEADS, 1), 0)
    low_group = head < HEADS_PER_GROUP
    low_lanes = lane < HEAD_DIM

    q32 = q_ref[...].astype(F32)
    qm = jnp.zeros((N_HEADS, LANES), F32)
    for j in range(HEADS_PER_GROUP):
        qm = jnp.where(head % HEADS_PER_GROUP == j, q32[:, j * LANES:(j + 1) * LANES], qm)
    qm = jnp.where(low_group, jnp.where(low_lanes, qm, 0.0), jnp.where(low_lanes, 0.0, qm)).astype(BF16)

    gate_row = gate_ref[...]
    gate_col = [jnp.sum(jnp.where(lane == head + br * N_HEADS, gate_row, 0.0), axis=-1, keepdims=True)
                for br in range(3)]
    kv_new = kvb_ref[...].astype(F32)
    first_row = lax.broadcasted_iota(jnp.int32, (tk, 1), 0) == 0

    def new_key_tile(c0):
        return jnp.where(first_row, kv_new[:, c0:c0 + LANES], 0.0).astype(BF16)

    nblk = lax.broadcasted_iota(jnp.int32, (1, nbp), 1)
    cvalid = (nblk + 1) * NSA_BLOCK <= pos + 1
    sc = jnp.where(cvalid, _dot_nt(qm, kvc_ref[0].astype(BF16)), MASK_INIT)
    e = jnp.exp2(sc - jnp.max(sc, axis=-1, keepdims=True))
    pc = jnp.where(cvalid, e / jnp.sum(e, axis=-1, keepdims=True), 0.0)
    o_cmp = _dot(pc.astype(BF16), kvc_ref[1].astype(BF16))

    imp = jnp.where(low_group,
                    jnp.sum(jnp.where(low_group, pc, 0.0), axis=0, keepdims=True),
                    jnp.sum(jnp.where(low_group, 0.0, pc), axis=0, keepdims=True))
    cur = pos // NSA_BLOCK
    forced = HEADS_PER_GROUP + 1.0
    imp = jnp.where(nblk == 0, forced, jnp.where(nblk == cur, forced, jnp.where(nblk == cur - 1, forced, imp)))
    imp = jnp.where(nblk * NSA_BLOCK <= pos, imp, -1.0)
    sel = _select_blocks(imp, nblk, NSA_TOP_N)

    def sel_bias(n):
        return jnp.where(sel[:, n:n + 1] > 0.5, 0.0, MASK_BIAS)

    s_tiles, v_tiles = [], []
    blocks_per_page = tk // NSA_BLOCK
    for k in range(n_pages):
        page = pages[k]
        s = _dot(qm, page[:GD_ROWS, :].astype(BF16))
        bias = sel_bias(k * blocks_per_page + blocks_per_page - 1)
        for b in range(blocks_per_page - 2, -1, -1):
            bias = jnp.where(lane < (b + 1) * NSA_BLOCK, sel_bias(k * blocks_per_page + b), bias)
        s_tiles.append(s + bias)
        v_tiles.append((page[GD_ROWS:, :].astype(BF16), True))
    s_new = _dot_nt(qm, new_key_tile(KV_BRANCH_COLS))
    s_tiles.append(s_new + jnp.where(lane == 0, sel_bias(n_pages * blocks_per_page), MASK_BIAS))
    v_tiles.append((new_key_tile(KV_BRANCH_COLS + LANES), False))
    o_slc = _softmax_pv(s_tiles, v_tiles)

    n_win = win_ref.shape[1]
    s_tiles, v_tiles = [], []
    for k in range(n_win // tk):
        diff = n_win - (k * tk + lane)
        bias = jnp.where(diff < NSA_WINDOW, 0.0, MASK_BIAS)
        s_tiles.append(_dot(qm, win_ref[:GD_ROWS, k * tk:(k + 1) * tk].astype(BF16)) + bias)
        v_tiles.append((win_ref[GD_ROWS:, k * tk:(k + 1) * tk].astype(BF16), True))
    s_tiles.append(_dot_nt(qm, new_key_tile(2 * KV_BRANCH_COLS)) + jnp.where(lane == 0, 0.0, MASK_BIAS))
    v_tiles.append((new_key_tile(2 * KV_BRANCH_COLS + LANES), False))
    o_win = _softmax_pv(s_tiles, v_tiles)

    o = gate_col[0] * o_cmp + gate_col[1] * o_slc + gate_col[2] * o_win
    for j in range(HEADS_PER_GROUP):
        pair = jnp.where(low_lanes, o[j:j + 1], o[j + HEADS_PER_GROUP:j + HEADS_PER_GROUP + 1])
        o_ref[:, j * LANES:(j + 1) * LANES] = pair.astype(o_ref.dtype)

    last = lax.broadcasted_iota(jnp.int32, (1, n_win), 1) == n_win - 1
    wnext_ref[...] = jnp.where(last, wnew_ref[...], pltpu.roll(win_ref[...], n_win - 1, 1))


def _attn_decode(q, kvb, gates, kvc, cache_slc, win_state, win_new, page_table, pos):
    n_dec, n_pages = page_table.shape
    nbp = kvc.shape[2]
    n_win = win_state.shape[2]
    seqs = DECODE_SEQS_PER_STEP
    per_seq = lambda a: pl.BlockSpec((seqs, 1, a.shape[-1]), lambda s, pt: (s, 0, 0))
    win_spec = pl.BlockSpec((seqs, KV_BRANCH_COLS, n_win), lambda s, pt: (s, 0, 0))
    q3, kvb3, g3 = (a.reshape(n_dec, 1, a.shape[-1]) for a in (q, kvb, gates))
    out, win_next = pl.pallas_call(
        functools.partial(_attn_decode_kernel, n_pages=n_pages, seqs=seqs, pos=pos, nbp=nbp),
        grid_spec=pltpu.PrefetchScalarGridSpec(
            num_scalar_prefetch=1,
            grid=(n_dec // seqs,),
            in_specs=_page_specs(n_pages, seqs) + [
                per_seq(q3), per_seq(kvb3), per_seq(g3),
                pl.BlockSpec((2, seqs, nbp, LANES), lambda s, pt: (0, s, 0, 0)),
                win_spec, pl.BlockSpec((seqs, KV_BRANCH_COLS, 1), lambda s, pt: (s, 0, 0)),
            ],
            out_specs=[pl.BlockSpec((seqs, 1, Q_COLS), lambda s, pt: (s, 0, 0)), win_spec],
        ),
        out_shape=[jax.ShapeDtypeStruct((n_dec, 1, Q_COLS), BF16), jax.ShapeDtypeStruct(win_state.shape, F32)],
        compiler_params=_params("arbitrary"),
        name="attn_decode",
    )(page_table, *([cache_slc] * (seqs * n_pages)), q3, kvb3, g3, kvc, win_state, win_new)
    return out.reshape(n_dec, Q_COLS), win_next


def _mix_kernel(*refs, tm, halo, tiles_per_seq):
    if halo:
        (x_ref, xh_ref, o_ref, shift_ref, scale_ref, gate_ref, gpre_ref, gpost_ref, cw_ref,
         wconv_ref, wmerge_ref, wup_ref, wco_ref, wout_ref, y_ref, ulast_ref, u_scr) = refs
        x = x_ref[...]
        xe = jnp.concatenate([xh_ref[...], x], axis=0)
    else:
        (x_ref, um1_ref, um2_ref, o_ref, shift_ref, scale_ref, gate_ref, gpre_ref, gpost_ref, cw_ref,
         wconv_ref, wmerge_ref, wup_ref, wco_ref, wout_ref, y_ref, ulast_ref) = refs
        x = x_ref[...]
        xe = x
    h0 = xe.shape[0] - tm
    a = (_rms(xe, gpre_ref[...]) * (1.0 + scale_ref[...]) + shift_ref[...]).astype(BF16)
    zc = _dot(a, wconv_ref[...])
    u = zc[:, 2 * D_CONV:] * zc[:, :D_CONV]
    cb = zc[h0:, D_CONV:2 * D_CONV]
    if halo:
        keep_halo = jnp.where(pl.program_id(0) % tiles_per_seq == 0, 0.0, 1.0)
        rows = lax.broadcasted_iota(jnp.int32, (xe.shape[0], 1), 0)
        u = jnp.where(rows < h0, u * keep_halo, u)
        u_scr[...] = u
        um1 = u_scr[h0 - 1:h0 - 1 + tm, :]
        um2 = u_scr[h0 - 2:h0 - 2 + tm, :]
        u0 = u[h0:]
        ulast_ref[...] = u[tm:]
    else:
        um1, um2, u0 = um1_ref[...], um2_ref[...], u
        ulast_ref[...] = u
    cw = cw_ref[...]
    y = cw[0:1] * um2 + cw[1:2] * um1 + cw[2:3] * u0
    conv_out = _dot((cb * y).astype(BF16), wco_ref[...])
    attn_out = _dot_tn(o_ref[...], wup_ref[...])
    mg = jax.nn.sigmoid(_dot(a[h0:], wmerge_ref[...]))
    merged = mg[:, :D_MODEL] * attn_out + mg[:, D_MODEL:] * conv_out
    mixed = _dot(merged.astype(BF16), wout_ref[...])
    y_ref[...] = x + gate_ref[...] * _rms(mixed, gpost_ref[...])


def _mix(x, o_att, mod, g3, conv_w, weights, tm, tiles_per_seq=None, prev=None):
    m = x.shape[0]
    halo = prev is None
    h0 = MIX_HALO
    row = lambda n: pl.BlockSpec((tm, n), lambda i: (i, 0))
    full = lambda a: pl.BlockSpec(a.shape, lambda i: (0,) * a.ndim, pipeline_mode=pl.Buffered(1))
    att = pl.BlockSpec((Q_COLS, tm), lambda i: (0, i))
    common = [att, mod.spec(3), mod.spec(4), mod.spec(5), _gspec(2), _gspec(3), full(conv_w)]
    common += [full(w) for w in weights]
    common_args = [o_att, mod.arr, mod.arr, mod.arr, g3, g3, conv_w, *weights]
    if halo:
        hb = tm // h0
        in_specs = [row(D_MODEL), pl.BlockSpec((h0, D_MODEL), lambda i: (jnp.maximum(i * hb - 1, 0), 0))] + common
        args = [x, x] + common_args
        scratch = [pltpu.VMEM((tm + h0, D_CONV), F32)]
        ulast = (jax.ShapeDtypeStruct((m // tm * h0, D_CONV), F32), pl.BlockSpec((h0, D_CONV), lambda i: (i, 0)))
    else:
        in_specs = [row(D_MODEL), row(D_CONV), row(D_CONV)] + common
        args = [x, prev[0], prev[1]] + common_args
        scratch = []
        ulast = (jax.ShapeDtypeStruct((m, D_CONV), F32), row(D_CONV))
    return pl.pallas_call(
        functools.partial(_mix_kernel, tm=tm, halo=halo, tiles_per_seq=tiles_per_seq),
        grid=(m // tm,),
        in_specs=in_specs,
        out_specs=[row(D_MODEL), ulast[1]],
        out_shape=[jax.ShapeDtypeStruct((m, D_MODEL), F32), ulast[0]],
        scratch_shapes=scratch,
        compiler_params=_params("parallel"),
        name="mix",
    )(*args)


def _head_pair_perm():
    order = []
    for j in range(HEADS_PER_GROUP):
        for g in range(N_KV_HEADS):
            head = g * HEADS_PER_GROUP + j
            order.extend(range(head * HEAD_DIM, (head + 1) * HEAD_DIM))
    return jnp.array(order, dtype=jnp.int32)


def _layer_weights(w_in, w_attn_up, w_conv_out, w_out):
    perm = _head_pair_perm()
    o1 = Q_COLS
    o2 = o1 + KV_COLS
    o3 = o2 + NSA_GATE_COLS
    o4 = o3 + CONV_COLS
    w_q = w_in[:, :o1][:, perm]
    w_g = jnp.pad(w_in[:, o2:o3], ((0, 0), (0, GATE_PAD - NSA_GATE_COLS)))
    w_qkvg = jnp.concatenate([w_q, w_in[:, o1:o2], w_g], axis=1).astype(BF16)
    mix_w = (w_in[:, o3:o4].astype(BF16), w_in[:, o4:].astype(BF16), w_attn_up[perm].astype(BF16),
             w_conv_out.astype(BF16), w_out.astype(BF16))
    return w_qkvg, mix_w


def kernel(x_prompt, x_sample, c_prompt, c_sample, cache_cmp_kv, cache_slc_kv, state_win_kv, state_conv,
           page_table, w_ada, b_ada, g_norm, w_ffn_gu, w_ffn_down, w_in, phi_pe, phi_w1, phi_b1, phi_w2,
           w_attn_up, conv_w, w_conv_out, w_out):
    n_seq, seq, _ = x_prompt.shape
    n_dec = x_sample.shape[0]
    assert w_ada.shape[0] == 1
    l = 0

    mod_all = _ada(jnp.concatenate([c_prompt, c_sample], axis=0), w_ada[l], b_ada[l])
    g3 = g_norm[l].reshape(6, 1, D_MODEL)
    w_gu = w_ffn_gu[l].astype(BF16)
    w_down = w_ffn_down[l].astype(BF16)
    w_qkvg, mix_w = _layer_weights(w_in[l], w_attn_up[l], w_conv_out[l], w_out[l])
    cwp = _compress_weights_paged(phi_pe[l], phi_w1[l], phi_b1[l], phi_w2[l])

    tm = 512
    tq = 256
    m = n_seq * seq
    mod_p = _Mod(mod_all[:n_seq], per_row=False, tiles_per_seq=seq // tm)
    tm_ffn, ffn_sub = 1024, 2
    mod_pf = _Mod(mod_all[:n_seq], per_row=False, tiles_per_seq=seq // tm_ffn)
    x0 = x_prompt.reshape(m, D_MODEL)
    x1 = _ffn(x0, mod_pf, g3, w_gu, w_down, 0, tm_ffn, ffn_sub)
    tables = _rope_tables(jnp.arange(seq, dtype=jnp.int32))
    _, qt, cmp_pages, cmpt, slct, wint, kvtb, kvb, _, gatest = _proj(x1, mod_p, g3, tables, w_qkvg, tm)
    nb = seq // NSA_BLOCK
    kvc = _gather_compress(cmp_pages, None, cwp, pages_per_seq=seq // PAGE_SIZE)
    kc = kvc[0].reshape(n_seq * nb, LANES).astype(BF16)
    vct = kvc[1].transpose(0, 2, 1).astype(BF16)
    o_att = _attn_prompt(qt, gatest, kc, vct, kvb, kvtb, n_seq, seq, tq)
    x2, ulast = _mix(x1, o_att, mod_pf, g3, conv_w[l], mix_w, tm_ffn, tiles_per_seq=seq // tm_ffn)
    y_prompt = _ffn(x2, mod_pf, g3, w_gu, w_down, 2, tm_ffn, ffn_sub).reshape(n_seq, seq, D_MODEL)

    def token_major(t):
        n, _, tt = t.shape
        return t.reshape(n, 2, N_KV_HEADS, HEAD_DIM, tt).transpose(0, 4, 1, 2, 3)[None]

    def feature_major(a):
        n, tt = a.shape[:2]
        return a.transpose(0, 2, 3, 4, 1).reshape(n, KV_BRANCH_COLS, tt)

    cmp_kv_prompt = token_major(cmpt)
    slc_kv_prompt = token_major(slct)
    win_keep = min(NSA_WINDOW, seq)
    win_kv_prompt = token_major(wint[:, :, seq - win_keep:])
    conv_prompt = ulast.reshape(n_seq, seq // tm_ffn, MIX_HALO, D_CONV)[:, -1, MIX_HALO - (CONV_WIDTH - 1):][None]

    n_pages = page_table.shape[1]
    past_len = n_pages * PAGE_SIZE
    pos_s = past_len + jnp.arange(x_sample.shape[1], dtype=jnp.int32)
    assert x_sample.shape[1] == 1 and past_len % NSA_BLOCK == 0 and state_win_kv.shape[2] <= past_len
    mod_s = _Mod(mod_all[n_seq:], per_row=True)
    xs1 = _ffn(x_sample.reshape(n_dec, D_MODEL), mod_s, g3, w_gu, w_down, 0, n_dec)
    tables_s = _rope_tables(jnp.broadcast_to(pos_s, (n_dec,)))
    q_s, _, _, cmpt_s, slct_s, wint_s, _, kvb_s, gates_s, _ = _proj(xs1, mod_s, g3, tables_s, w_qkvg, n_dec)
    nb_past = past_len // NSA_BLOCK
    nb_pad = -(-(nb_past + 1) // NSA_BLOCK) * NSA_BLOCK
    kvc_s = _gather_compress(feature_major(cache_cmp_kv[l]), page_table, cwp, new_tokens=cmpt_s[0])
    kvc_s = jnp.concatenate(
        [kvc_s[:, :, :nb_past + 1], jnp.zeros((2, n_dec, nb_pad - nb_past - 1, LANES), F32)], axis=2)
    win_state = feature_major(state_win_kv[l])
    assert win_state.shape[2] == NSA_WINDOW
    o_att_s, win_next = _attn_decode(q_s, kvb_s, gates_s, kvc_s, feature_major(cache_slc_kv[l]), win_state,
                                     wint_s[0].T[:, :, None], page_table, past_len)
    conv_state = state_conv[l]
    xs2, u_s = _mix(xs1, o_att_s.T, mod_s, g3, conv_w[l], mix_w, n_dec,
                    prev=(conv_state[:, CONV_WIDTH - 2], conv_state[:, CONV_WIDTH - 3]))
    y_sample = _ffn(xs2, mod_s, g3, w_gu, w_down, 2, n_dec).reshape(x_sample.shape)

    kvs = (1, n_dec, 1, 2, N_KV_HEADS, HEAD_DIM)
    conv_sample = jnp.concatenate([conv_state[:, 1:], u_s[:, None, :]], axis=1)[None]
    return (y_prompt, y_sample, cmp_kv_prompt, slc_kv_prompt, win_kv_prompt, conv_prompt,
            token_major(cmpt_s).reshape(kvs), token_major(slct_s).reshape(kvs), token_major(win_next), conv_sample)
```

```python
import functools

import jax
import jax.numpy as jnp
from jax import lax
from jax.experimental import pallas as pl
from jax.experimental.pallas import tpu as pltpu

D_MODEL = 1024
N_HEADS = 8
HEAD_DIM = 64
N_KV_HEADS = 2
HEADS_PER_GROUP = N_HEADS // N_KV_HEADS
ROT_DIM = HEAD_DIM // 4
ROPE_THETA = 500000.0
NSA_BLOCK = 64
NSA_TOP_N = 16
NSA_WINDOW = 512
PHI_HIDDEN = 2 * HEAD_DIM
D_CONV = 512
CONV_WIDTH = 3
D_FF = 2816
FFN_HALF = 0.5
NORM_EPS = 1e-6
PAGE_SIZE = 128
Q_COLS = N_HEADS * HEAD_DIM
KV_BRANCH_COLS = 2 * N_KV_HEADS * HEAD_DIM
KV_COLS = 3 * KV_BRANCH_COLS
NSA_GATE_COLS = 3 * N_HEADS
CONV_COLS = 3 * D_CONV
MERGE_COLS = 2 * D_MODEL
GD_ROWS = N_KV_HEADS * HEAD_DIM

LANES = 128
BF16_SUBLANES = 16
GATE_PAD = LANES
QKVG_COLS = Q_COLS + KV_COLS + GATE_PAD
LOG2_E = 1.4426950408889634
MASK_INIT = -1e30
MASK_BIAS = -2e30
VMEM_LIMIT = 56 * 1024 * 1024
SLAB_PITCH = HEAD_DIM + 8
MXU_DIM = 256
FFN_CHUNKS = D_FF // MXU_DIM
DECODE_SEQS_PER_STEP = 4
MIX_HALO = 16

BF16 = jnp.bfloat16
F32 = jnp.float32


def _dot(a, b):
    return jnp.dot(a, b, preferred_element_type=F32)


def _dot_tn(a, b):
    return lax.dot_general(a, b, (((0,), (0,)), ((), ())), preferred_element_type=F32)


def _dot_nt(a, b):
    return lax.dot_general(a, b, (((1,), (1,)), ((), ())), preferred_element_type=F32)


def _rms(x, g):
    return x * lax.rsqrt(jnp.mean(x * x, axis=-1, keepdims=True) + NORM_EPS) * g


def _silu(x):
    return x * jax.nn.sigmoid(x)


def _params(*sem):
    return pltpu.CompilerParams(dimension_semantics=sem, vmem_limit_bytes=VMEM_LIMIT)


def _ada_kernel(c_ref, w_ref, b_ref, o_ref):
    c = _silu(c_ref[...]).astype(BF16)
    o_ref[...] = _dot(c, w_ref[...].astype(BF16)) + b_ref[...]


def _ada(c, w_ada, b_ada):
    rows = c.shape[0]
    n = w_ada.shape[1]
    tn = 9 * LANES
    return pl.pallas_call(
        _ada_kernel,
        grid=(n // tn,),
        in_specs=[
            pl.BlockSpec((rows, D_MODEL), lambda j: (0, 0)),
            pl.BlockSpec((D_MODEL, tn), lambda j: (0, j)),
            pl.BlockSpec((1, tn), lambda j: (0, j)),
        ],
        out_specs=pl.BlockSpec((rows, tn), lambda j: (0, j)),
        out_shape=jax.ShapeDtypeStruct((rows, n), F32),
        compiler_params=_params("parallel"),
        name="ada",
    )(c, w_ada, b_ada.reshape(1, n))


class _Mod:
    def __init__(self, mod, per_row, tiles_per_seq=1):
        self.per_row = per_row
        self.tiles_per_seq = tiles_per_seq
        self.arr = mod if per_row else mod.reshape(mod.shape[0] * 9, 1, D_MODEL)

    def spec(self, k):
        if self.per_row:
            return pl.BlockSpec((self.arr.shape[0], D_MODEL), lambda i, *_: (0, k))
        tps = self.tiles_per_seq
        return pl.BlockSpec((None, 1, D_MODEL), lambda i, *_: ((i // tps) * 9 + k, 0, 0))


def _gspec(k):
    return pl.BlockSpec((None, 1, D_MODEL), lambda i, *_: (k, 0, 0))


def _ffn_kernel(x_ref, shift_ref, scale_ref, gate_ref, gpre_ref, gpost_ref, wgu_ref, wd_ref, o_ref,
                *, n_sub, n_chunk):
    ts = x_ref.shape[0] // n_sub
    tf = D_FF // n_chunk
    for s in range(n_sub):
        rows = slice(s * ts, (s + 1) * ts)
        mod = lambda ref: ref[...] if ref.shape[0] == 1 else ref[rows, :]
        x = x_ref[rows, :]
        a = (_rms(x, gpre_ref[...]) * (1.0 + mod(scale_ref)) + mod(shift_ref)).astype(BF16)
        y = None
        for c in range(n_chunk):
            g = _dot(a, wgu_ref[:, c * tf:(c + 1) * tf])
            u = _dot(a, wgu_ref[:, D_FF + c * tf:D_FF + (c + 1) * tf])
            d = _dot((_silu(g) * u).astype(BF16), wd_ref[c * tf:(c + 1) * tf, :])
            y = d if y is None else y + d
        o_ref[rows, :] = x + FFN_HALF * mod(gate_ref) * _rms(y, gpost_ref[...])


def _ffn(x, mod, g3, w_gu, w_down, sub, tm, n_sub=1):
    m = x.shape[0]
    row = pl.BlockSpec((tm, D_MODEL), lambda i: (i, 0))
    const = lambda a: pl.BlockSpec((None,) + a.shape[1:], lambda i: (sub // 2, 0, 0), pipeline_mode=pl.Buffered(1))
    return pl.pallas_call(
        functools.partial(_ffn_kernel, n_sub=n_sub, n_chunk=FFN_CHUNKS),
        grid=(m // tm,),
        in_specs=[
            row, mod.spec(3 * sub), mod.spec(3 * sub + 1), mod.spec(3 * sub + 2),
            _gspec(2 * sub), _gspec(2 * sub + 1), const(w_gu), const(w_down),
        ],
        out_specs=row,
        out_shape=jax.ShapeDtypeStruct((m, D_MODEL), F32),
        compiler_params=_params("parallel"),
        name=f"ffn{sub}",
    )(x, mod.arr, mod.arr, mod.arr, g3, g3, w_gu, w_down)


def _rope(x, cos, sin_lo, sin_hi):
    return x * cos + pltpu.roll(x, LANES - ROT_DIM // 2, 1) * sin_lo + pltpu.roll(x, ROT_DIM // 2, 1) * sin_hi


def _proj_kernel(x_ref, shift_ref, scale_ref, g_ref, cos_ref, slo_ref, shi_ref, w_ref,
                 q_ref, qt_ref, cmp_ref, cmpt_ref, slct_ref, wint_ref, kvtb_ref, kvb_ref, gate_ref, gatet_ref):
    a = (_rms(x_ref[...], g_ref[...]) * (1.0 + scale_ref[...]) + shift_ref[...]).astype(BF16)
    z = _dot(a, w_ref[...])
    cos, slo, shi = cos_ref[...], slo_ref[...], shi_ref[...]
    scale = HEAD_DIM ** -0.5 * LOG2_E
    for s in range(Q_COLS // LANES):
        q = _rope(z[:, s * LANES:(s + 1) * LANES], cos, slo, shi) * scale
        q_ref[:, s * LANES:(s + 1) * LANES] = q.astype(BF16)
        qt_ref[s * LANES:(s + 1) * LANES, :] = q.T.astype(BF16)
    kvt_refs = (cmpt_ref, slct_ref, wint_ref)
    for br in range(3):
        c0 = Q_COLS + br * KV_BRANCH_COLS
        k = _rope(z[:, c0:c0 + LANES], cos, slo, shi)
        v = z[:, c0 + LANES:c0 + 2 * LANES]
        for r0, x in ((0, k), (LANES, v)):
            xt = x.T
            kvt_refs[br][r0:r0 + LANES, :] = xt
            kvtb_ref[br, r0:r0 + LANES, :] = xt.astype(BF16)
            if br == 0:
                for p in range(cmp_ref.shape[0]):
                    cmp_ref[p, r0:r0 + LANES, :] = xt[:, p * PAGE_SIZE:(p + 1) * PAGE_SIZE]
        kvb_ref[:, br * KV_BRANCH_COLS:br * KV_BRANCH_COLS + LANES] = k.astype(BF16)
        kvb_ref[:, br * KV_BRANCH_COLS + LANES:(br + 1) * KV_BRANCH_COLS] = v.astype(BF16)
    gate = jax.nn.sigmoid(z[:, Q_COLS + KV_COLS:])
    gate_ref[...] = gate
    gatet_ref[...] = gate.T


def _proj(x, mod, g3, tables, w_qkvg, tm):
    assert tm % PAGE_SIZE == 0
    m = x.shape[0]
    t_rows = tables[0].shape[0]
    nt = t_rows // tm
    n_seq = m // t_rows
    row = lambda n: pl.BlockSpec((tm, n), lambda i: (i, 0))
    tab = pl.BlockSpec((tm, LANES), lambda i: (i % nt, 0))
    col = lambda n: pl.BlockSpec((n, tm), lambda i: (0, i))
    kvt_spec = pl.BlockSpec((None, KV_BRANCH_COLS, tm), lambda i: (i // nt, 0, i % nt))
    kvtb_spec = pl.BlockSpec((3, None, KV_BRANCH_COLS, tm), lambda i: (0, i // nt, 0, i % nt))
    return pl.pallas_call(
        _proj_kernel,
        grid=(m // tm,),
        in_specs=[row(D_MODEL), mod.spec(3), mod.spec(4), _gspec(2), tab, tab, tab,
                  pl.BlockSpec((D_MODEL, QKVG_COLS), lambda i: (0, 0))],
        out_specs=[row(Q_COLS), col(Q_COLS),
                   pl.BlockSpec((tm // PAGE_SIZE, KV_BRANCH_COLS, PAGE_SIZE), lambda i: (i, 0, 0)),
                   kvt_spec, kvt_spec, kvt_spec, kvtb_spec, row(KV_COLS), row(GATE_PAD), col(GATE_PAD)],
        out_shape=[
            jax.ShapeDtypeStruct((m, Q_COLS), BF16),
            jax.ShapeDtypeStruct((Q_COLS, m), BF16),
            jax.ShapeDtypeStruct((m // PAGE_SIZE, KV_BRANCH_COLS, PAGE_SIZE), F32),
            *[jax.ShapeDtypeStruct((n_seq, KV_BRANCH_COLS, t_rows), F32)] * 3,
            jax.ShapeDtypeStruct((3, n_seq, KV_BRANCH_COLS, t_rows), BF16),
            jax.ShapeDtypeStruct((m, KV_COLS), BF16),
            jax.ShapeDtypeStruct((m, GATE_PAD), F32),
            jax.ShapeDtypeStruct((GATE_PAD, m), F32),
        ],
        compiler_params=_params("parallel"),
        name="proj",
    )(x, mod.arr, mod.arr, g3, *tables, w_qkvg)


def _rope_tables(pos):
    half = ROT_DIM // 2
    inv = jnp.power(ROPE_THETA, -jnp.arange(half, dtype=F32) * 2.0 / ROT_DIM)
    ang = pos.astype(F32)[:, None] * inv[None, :]
    cos, sin = jnp.cos(ang), jnp.sin(ang)
    n = pos.shape[0]
    ones = jnp.ones((n, HEAD_DIM - ROT_DIM), F32)
    zeros = jnp.zeros((n, HEAD_DIM - ROT_DIM), F32)
    zh = jnp.zeros((n, half), F32)
    c = jnp.concatenate([cos, cos, ones], axis=1)
    lo = jnp.concatenate([-sin, zh, zeros], axis=1)
    hi = jnp.concatenate([zh, sin, zeros], axis=1)
    return tuple(jnp.tile(t, (1, LANES // HEAD_DIM)) for t in (c, lo, hi))


def _compress_weights_paged(phi_pe, phi_w1, phi_b1, phi_w2):
    blocks = PAGE_SIZE // NSA_BLOCK
    assert blocks == 2
    eye = jnp.eye(blocks, dtype=F32)
    wt = phi_w1.reshape(2, NSA_BLOCK, HEAD_DIM, PHI_HIDDEN).transpose(0, 2, 1, 3).astype(BF16)
    zero = jnp.zeros_like(wt)
    w1 = jnp.stack([jnp.concatenate([wt, zero], axis=-1), jnp.concatenate([zero, wt], axis=-1)], axis=2)
    w1 = w1.reshape(2, HEAD_DIM * PAGE_SIZE, blocks * PHI_HIDDEN)
    w2 = jnp.einsum("bc,kjd->kbjcd", eye, phi_w2).reshape(2, blocks * PHI_HIDDEN, blocks * HEAD_DIM)
    pe = jnp.tile(phi_pe.transpose(0, 2, 1), (1, 1, blocks)).reshape(2, HEAD_DIM, 1, PAGE_SIZE)
    b1 = jnp.tile(phi_b1, (1, blocks)).reshape(2, 1, blocks * PHI_HIDDEN)
    return pe, w1, b1, w2.astype(BF16)


def _gather_compress_kernel(pt_ref, *refs, seqs, n_pages, group, steps, contiguous, new_tokens):
    del pt_ref
    if contiguous:
        pages = [refs[0].at[k] for k in range(seqs * n_pages)]
        refs = refs[1:]
    else:
        pages, refs = refs[:seqs * n_pages], refs[seqs * n_pages:]
    if new_tokens:
        newt_ref, refs = refs[0], refs[1:]
    pe_ref, w1_ref, b1_ref, w2_ref, o_ref, xk_scr, xv_scr, c_scr = refs
    x_scr = (xk_scr, xv_scr)
    slot = pl.program_id(0) % steps
    per_seq = n_pages + (1 if new_tokens else 0)
    for r in range(seqs):
        slabs = [pages[r * n_pages + k] for k in range(n_pages)]
        if new_tokens:
            seq_lane = lax.broadcasted_iota(jnp.int32, (1, newt_ref.shape[1]), 1) == pl.program_id(0) * seqs + r
            col = jnp.sum(jnp.where(seq_lane, newt_ref[...], 0.0), axis=1, keepdims=True)
            first = lax.broadcasted_iota(jnp.int32, (1, PAGE_SIZE), 1) == 0
            slabs.append(jnp.where(first, col, 0.0))
        for k, slab in enumerate(slabs):
            for kv in range(2):
                for g in range(N_KV_HEADS):
                    page = (slot * seqs + r) * per_seq + k
                    r0 = pl.multiple_of((page * N_KV_HEADS + g) * SLAB_PITCH, SLAB_PITCH)
                    s0 = kv * GD_ROWS + g * HEAD_DIM
                    x_scr[kv][pl.ds(r0, HEAD_DIM), :] = slab[s0:s0 + HEAD_DIM, :]

    @pl.when(slot == steps - 1)
    def _():
        rows = group * N_KV_HEADS
        accs = [jnp.zeros((rows, w1_ref.shape[2]), F32) for _ in range(2)]
        for d in range(0, HEAD_DIM, 2):
            for kv in range(2):
                xs = []
                for dd in (d, d + 1):
                    x = x_scr[kv][pl.ds(dd, rows, stride=SLAB_PITCH), :]
                    xs.append((x + pe_ref[kv, dd]).astype(BF16))
                accs[kv] = accs[kv] + _dot(jnp.concatenate(xs, axis=1),
                                           w1_ref[kv, d * PAGE_SIZE:(d + 2) * PAGE_SIZE, :])
        for kv in range(2):
            h = _silu(accs[kv] + b1_ref[kv])
            c_scr[...] = _dot(h.astype(BF16), w2_ref[kv])
            g0 = c_scr[pl.ds(0, rows // 2, stride=2), :]
            g1 = c_scr[pl.ds(1, rows // 2, stride=2), :]
            low = lax.broadcasted_iota(jnp.int32, (1, LANES), 1) < HEAD_DIM
            o_ref.at[kv][pl.ds(0, rows // 2, stride=2), :] = jnp.where(low, g0, pltpu.roll(g1, HEAD_DIM, 1))
            o_ref.at[kv][pl.ds(1, rows // 2, stride=2), :] = jnp.where(low, pltpu.roll(g0, HEAD_DIM, 1), g1)


def _page_specs(n_pages, seqs=1):
    return [pl.BlockSpec((None, KV_BRANCH_COLS, PAGE_SIZE), lambda s, pt, r=r, k=k: (pt[s * seqs + r, k], 0, 0))
            for r in range(seqs) for k in range(n_pages)]


def _gather_compress(cache_t, page_table, cwp, pages_per_seq=None, new_tokens=None):
    pe, w1, b1, w2 = cwp
    contiguous = page_table is None
    seqs = 4
    if contiguous:
        n_pages = pages_per_seq
        n_dec = cache_t.shape[0] // n_pages
        page_table = jnp.zeros((1, 1), jnp.int32)
        page_specs = [pl.BlockSpec((seqs * n_pages, KV_BRANCH_COLS, PAGE_SIZE), lambda s, pt: (s, 0, 0))]
        page_args = [cache_t]
    else:
        n_dec, n_pages = page_table.shape
        page_specs = _page_specs(n_pages, seqs)
        page_args = [cache_t] * (seqs * n_pages)
    const = lambda a: pl.BlockSpec(a.shape, lambda s, pt: (0,) * a.ndim, pipeline_mode=pl.Buffered(1))
    per_seq = n_pages
    if new_tokens is not None:
        per_seq += 1
        page_specs.append(const(new_tokens))
        page_args.append(new_tokens)
    steps = 2
    group = steps * seqs * per_seq
    assert n_dec % (steps * seqs) == 0
    rows = group * N_KV_HEADS
    blocks = PAGE_SIZE // NSA_BLOCK
    assert blocks == 2 and N_KV_HEADS == 2
    out = pl.pallas_call(
        functools.partial(_gather_compress_kernel, seqs=seqs, n_pages=n_pages, group=group, steps=steps,
                          contiguous=contiguous, new_tokens=new_tokens is not None),
        grid_spec=pltpu.PrefetchScalarGridSpec(
            num_scalar_prefetch=1,
            grid=(n_dec // seqs,),
            in_specs=page_specs + [const(pe), const(w1), const(b1), const(w2)],
            out_specs=pl.BlockSpec((2, rows, blocks * HEAD_DIM), lambda s, pt: (0, s // steps, 0)),
            scratch_shapes=[pltpu.VMEM((group * N_KV_HEADS * SLAB_PITCH, PAGE_SIZE), F32)] * 2
            + [pltpu.VMEM((rows, LANES), F32)],
        ),
        out_shape=jax.ShapeDtypeStruct((2, n_dec * per_seq * N_KV_HEADS, blocks * HEAD_DIM), F32),
        compiler_params=_params("arbitrary"),
        name="gather_compress",
    )(page_table, *page_args, pe, w1, b1, w2)
    return out.reshape(2, n_dec, per_seq * blocks, N_KV_HEADS * HEAD_DIM)


def _group_queries_t(qt, g):
    t = qt.shape[1]
    zeros = jnp.zeros((HEAD_DIM, t), qt.dtype)
    cols = []
    for j in range(HEADS_PER_GROUP):
        qh = qt[j * LANES + g * HEAD_DIM:j * LANES + (g + 1) * HEAD_DIM, :]
        cols.append(jnp.concatenate([qh, zeros] if g == 0 else [zeros, qh], axis=0))
    return jnp.concatenate(cols, axis=1)


def _tile_heads(x):
    return jnp.concatenate([x] * HEADS_PER_GROUP, axis=1)


def _select_blocks_t(imp, nblk, n_sel):
    nb = imp.shape[0]
    rank = jnp.zeros(imp.shape, F32)
    for m in range(nb):
        row = imp[m:m + 1, :]
        beats = jnp.where(row > imp, 1.0, jnp.where(row == imp, jnp.where(m < nblk, 1.0, 0.0), 0.0))
        rank = rank + beats
    return jnp.where(rank < n_sel, jnp.where(imp >= 0.0, 1.0, 0.0), 0.0)


def _select_blocks(imp, nblk, n_sel):
    nb = imp.shape[1]
    rank = jnp.zeros(imp.shape, F32)
    for m in range(nb):
        col = imp[:, m:m + 1]
        beats = jnp.where(col > imp, 1.0, jnp.where(col == imp, jnp.where(m < nblk, 1.0, 0.0), 0.0))
        rank = rank + beats
    return jnp.where(rank < n_sel, jnp.where(imp >= 0.0, 1.0, 0.0), 0.0)


def _attn_tile(carry, k, q, vt_ref, ks, tk, bias):
    m, l, acc = carry
    s = _dot(k, q)
    if bias is not None:
        s = s + bias
    m_new = jnp.maximum(m, jnp.max(s, axis=0, keepdims=True))
    alpha = jnp.exp2(m - m_new)
    pb = jnp.exp2(s - m_new).astype(BF16)
    half = pb.shape[1] // N_KV_HEADS
    ones = jnp.ones((BF16_SUBLANES, tk), BF16)
    pv = [_dot(jnp.concatenate([vt_ref[GD_ROWS + g * HEAD_DIM:GD_ROWS + (g + 1) * HEAD_DIM, pl.ds(ks, tk)], ones],
                               axis=0), pb[:, g * half:(g + 1) * half]) for g in range(N_KV_HEADS)]
    pv = jnp.concatenate(pv, axis=1)
    return m_new, alpha * l + pv[HEAD_DIM:HEAD_DIM + 1], alpha * acc + pv[:HEAD_DIM]


def _maybe(cond, fn, carry):
    return lax.fori_loop(0, jnp.where(cond, 1, 0), lambda _, c: fn(c), carry)


def _attn_kernel(qt_ref, gatet_ref, kc_ref, vct_ref, slc_k_ref, slc_vt_ref, win_k_ref, win_vt_ref, o_ref,
                 *, tq, nb):
    i = pl.program_id(1)
    p0 = i * tq
    tk = tq
    hq = HEADS_PER_GROUP * tq
    n_win_tiles = NSA_WINDOW // tk
    pos = p0 + lax.broadcasted_iota(jnp.int32, (1, tq), 1)
    nblk = lax.broadcasted_iota(jnp.int32, (nb, 1), 0)
    key_off = lax.broadcasted_iota(jnp.int32, (tk, 1), 0)
    qt = qt_ref[...]
    q_all = jnp.concatenate([_group_queries_t(qt, g) for g in range(N_KV_HEADS)], axis=1)
    tile_all = lambda x: jnp.concatenate([x] * N_HEADS, axis=1)

    cvalid = (nblk + 1) * NSA_BLOCK <= tile_all(pos) + 1
    sc = jnp.where(cvalid, _dot(kc_ref[...], q_all), MASK_INIT)
    e = jnp.exp2(sc - jnp.max(sc, axis=0, keepdims=True))
    pc = jnp.where(cvalid, e / jnp.sum(e, axis=0, keepdims=True), 0.0)
    pcb = pc.astype(BF16)
    o_cmp = jnp.concatenate([_dot(vct_ref[g * HEAD_DIM:(g + 1) * HEAD_DIM, :], pcb[:, g * hq:(g + 1) * hq])
                             for g in range(N_KV_HEADS)], axis=1)

    cur = pos // NSA_BLOCK
    forced = HEADS_PER_GROUP + 1.0
    sel_rows = []
    for g in range(N_KV_HEADS):
        imp = functools.reduce(jnp.add, [pc[:, g * hq + j * tq:g * hq + (j + 1) * tq]
                                         for j in range(HEADS_PER_GROUP)])
        imp = jnp.where(nblk == 0, forced, jnp.where(nblk == cur, forced, jnp.where(nblk == cur - 1, forced, imp)))
        imp = jnp.where(nblk * NSA_BLOCK <= pos, imp, -1.0)
        sel = _select_blocks_t(imp, nblk, NSA_TOP_N)
        sel_rows.append(_tile_heads(jnp.where(sel > 0.5, 0.0, MASK_BIAS)))
    sel_rows = jnp.concatenate(sel_rows, axis=1)
    sel_rows = jnp.concatenate([sel_rows, jnp.zeros((LANES - nb, sel_rows.shape[1]), F32)], axis=0)
    q_sel = jnp.concatenate([q_all, sel_rows.astype(BF16)], axis=0)
    blk_lane = lax.broadcasted_iota(jnp.int32, (1, LANES), 1)

    def slc_keys(ks):
        onehot = jnp.where(blk_lane == (ks + key_off) // NSA_BLOCK, 1.0, 0.0).astype(BF16)
        return jnp.concatenate([slc_k_ref[pl.ds(ks, tk), :GD_ROWS], onehot], axis=1)

    cols = q_all.shape[1]
    init = (jnp.full((1, cols), MASK_INIT, F32), jnp.zeros((1, cols), F32), jnp.zeros((HEAD_DIM, cols), F32))
    causal = tile_all(jnp.where(p0 + key_off <= pos, 0.0, MASK_BIAS))
    k_diag = pl.multiple_of(p0, tk)

    def slc_body(j, c):
        ks = pl.multiple_of(j * tk, tk)
        return _attn_tile(c, slc_keys(ks), q_sel, slc_vt_ref, ks, tk, None)

    c = lax.fori_loop(0, i, slc_body, init)
    _, l, acc = _attn_tile(c, slc_keys(k_diag), q_sel, slc_vt_ref, k_diag, tk, causal)
    o_slc = acc / l

    def win_tile(c, ks, bias):
        return _attn_tile(c, win_k_ref[pl.ds(ks, tk), :GD_ROWS], q_all, win_vt_ref, ks, tk, bias)

    def edge_tile(c):
        ks = pl.multiple_of((i - n_win_tiles) * tk, tk)
        bias = tile_all(jnp.where(pos - (ks + key_off) < NSA_WINDOW, 0.0, MASK_BIAS))
        return win_tile(c, ks, bias)

    c = _maybe(i >= n_win_tiles, edge_tile, init)
    c = lax.fori_loop(jnp.maximum(i - n_win_tiles + 1, 0), i,
                      lambda j, c: win_tile(c, pl.multiple_of(j * tk, tk), None), c)
    _, l, acc = win_tile(c, k_diag, causal)
    o_win = acc / l

    gates = gatet_ref[...]
    for g in range(N_KV_HEADS):
        for j in range(HEADS_PER_GROUP):
            head = g * HEADS_PER_GROUP + j
            cs = slice(g * hq + j * tq, g * hq + (j + 1) * tq)
            o = (gates[head:head + 1] * o_cmp[:, cs]
                 + gates[N_HEADS + head:N_HEADS + head + 1] * o_slc[:, cs]
                 + gates[2 * N_HEADS + head:2 * N_HEADS + head + 1] * o_win[:, cs])
            r0 = j * LANES + g * HEAD_DIM
            o_ref[r0:r0 + HEAD_DIM, :] = o.astype(o_ref.dtype)


def _attn_prompt(qt, gatest, kc, vct, kvb, kvtb, n_seq, seq, tq):
    m = qt.shape[1]
    nb = seq // NSA_BLOCK
    nq = seq // tq
    assert seq % tq == 0 and NSA_WINDOW % tq == 0 and tq % NSA_BLOCK == 0 and nb <= LANES
    col = lambda n: pl.BlockSpec((n, tq), lambda b, i: (0, b * nq + i))
    return pl.pallas_call(
        functools.partial(_attn_kernel, tq=tq, nb=nb),
        grid=(n_seq, nq),
        in_specs=[col(Q_COLS), col(4 * N_HEADS),
                  pl.BlockSpec((nb, LANES), lambda b, i: (b, 0)),
                  pl.BlockSpec((None, LANES, nb), lambda b, i: (b, 0, 0)),
                  pl.BlockSpec((seq, KV_BRANCH_COLS), lambda b, i: (b, 1)),
                  pl.BlockSpec((None, None, KV_BRANCH_COLS, seq), lambda b, i: (1, b, 0, 0)),
                  pl.BlockSpec((seq, KV_BRANCH_COLS), lambda b, i: (b, 2)),
                  pl.BlockSpec((None, None, KV_BRANCH_COLS, seq), lambda b, i: (2, b, 0, 0))],
        out_specs=col(Q_COLS),
        out_shape=jax.ShapeDtypeStruct((Q_COLS, m), BF16),
        compiler_params=_params("parallel", "arbitrary"),
        name="attn_prompt",
    )(qt, gatest, kc, vct, kvb, kvtb, kvb, kvtb)


def _softmax_pv(score_tiles, value_tiles):
    m = functools.reduce(jnp.maximum, [jnp.max(s, axis=-1, keepdims=True) for s in score_tiles])
    ps = [jnp.exp2(s - m) for s in score_tiles]
    l = functools.reduce(jnp.add, [jnp.sum(p, axis=-1, keepdims=True) for p in ps])
    o = functools.reduce(jnp.add, [(_dot_nt if fm else _dot)(p.astype(BF16), v)
                                   for p, (v, fm) in zip(ps, value_tiles)])
    return o / l


def _attn_decode_kernel(pt_ref, *refs, n_pages, seqs, pos, nbp):
    del pt_ref
    pages = refs[:seqs * n_pages]
    q_ref, kvb_ref, gate_ref, kvc_ref, win_ref, wnew_ref, o_ref, wnext_ref = refs[seqs * n_pages:]
    for r in range(seqs):
        _attn_decode_one(pages[r * n_pages:(r + 1) * n_pages], q_ref.at[r], kvb_ref.at[r], gate_ref.at[r],
                         kvc_ref.at[:, r], win_ref.at[r], wnew_ref.at[r], o_ref.at[r], wnext_ref.at[r],
                         pos=pos, nbp=nbp)


def _attn_decode_one(pages, q_ref, kvb_ref, gate_ref, kvc_ref, win_ref, wnew_ref, o_ref, wnext_ref, *, pos, nbp):
    n_pages = len(pages)
    tk = PAGE_SIZE
    lane = lax.broadcasted_iota(jnp.int32, (1, LANES), 1)
    head = lax.broadcasted_iota(jnp.int32, (N_HEADS, 1), 0)
    low_group = head < HEADS_PER_GROUP
    low_lanes = lane < HEAD_DIM

    q32 = q_ref[...].astype(F32)
    qm = jnp.zeros((N_HEADS, LANES), F32)
    for j in range(HEADS_PER_GROUP):
        qm = jnp.where(head % HEADS_PER_GROUP == j, q32[:, j * LANES:(j + 1) * LANES], qm)
    qm = jnp.where(low_group, jnp.where(low_lanes, qm, 0.0), jnp.where(low_lanes, 0.0, qm)).astype(BF16)

    gate_row = gate_ref[...]
    gate_col = [jnp.sum(jnp.where(lane == head + br * N_HEADS, gate_row, 0.0), axis=-1, keepdims=True)
                for br in range(3)]
    kv_new = kvb_ref[...].astype(F32)
    first_row = lax.broadcasted_iota(jnp.int32, (tk, 1), 0) == 0

    def new_key_tile(c0):
        return jnp.where(first_row, kv_new[:, c0:c0 + LANES], 0.0).astype(BF16)

    nblk = lax.broadcasted_iota(jnp.int32, (1, nbp), 1)
    cvalid = (nblk + 1) * NSA_BLOCK <= pos + 1
    sc = jnp.where(cvalid, _dot_nt(qm, kvc_ref[0].astype(BF16)), MASK_INIT)
    e = jnp.exp2(sc - jnp.max(sc, axis=-1, keepdims=True))
    pc = jnp.where(cvalid, e / jnp.sum(e, axis=-1, keepdims=True), 0.0)
    o_cmp = _dot(pc.astype(BF16), kvc_ref[1].astype(BF16))

    imp = jnp.where(low_group,
                    jnp.sum(jnp.where(low_group, pc, 0.0), axis=0, keepdims=True),
                    jnp.sum(jnp.where(low_group, 0.0, pc), axis=0, keepdims=True))
    cur = pos // NSA_BLOCK
    forced = HEADS_PER_GROUP + 1.0
    imp = jnp.where(nblk == 0, forced, jnp.where(nblk == cur, forced, jnp.where(nblk == cur - 1, forced, imp)))
    imp = jnp.where(nblk * NSA_BLOCK <= pos, imp, -1.0)
    sel = _select_blocks(imp, nblk, NSA_TOP_N)

    def sel_bias(n):
        return jnp.where(sel[:, n:n + 1] > 0.5, 0.0, MASK_BIAS)

    s_tiles, v_tiles = [], []
    blocks_per_page = tk // NSA_BLOCK
    for k in range(n_pages):
        page = pages[k]
        s = _dot(qm, page[:GD_ROWS, :].astype(BF16))
        bias = sel_bias(k * blocks_per_page + blocks_per_page - 1)
        for b in range(blocks_per_page - 2, -1, -1):
            bias = jnp.where(lane < (b + 1) * NSA_BLOCK, sel_bias(k * blocks_per_page + b), bias)
        s_tiles.append(s + bias)
        v_tiles.append((page[GD_ROWS:, :].astype(BF16), True))
    s_new = _dot_nt(qm, new_key_tile(KV_BRANCH_COLS))
    s_tiles.append(s_new + jnp.where(lane == 0, sel_bias(n_pages * blocks_per_page), MASK_BIAS))
    v_tiles.append((new_key_tile(KV_BRANCH_COLS + LANES), False))
    o_slc = _softmax_pv(s_tiles, v_tiles)

    n_win = win_ref.shape[1]
    s_tiles, v_tiles = [], []
    for k in range(n_win // tk):
        diff = n_win - (k * tk + lane)
        bias = jnp.where(diff < NSA_WINDOW, 0.0, MASK_BIAS)
        s_tiles.append(_dot(qm, win_ref[:GD_ROWS, k * tk:(k + 1) * tk].astype(BF16)) + bias)
        v_tiles.append((win_ref[GD_ROWS:, k * tk:(k + 1) * tk].astype(BF16), True))
    s_tiles.append(_dot_nt(qm, new_key_tile(2 * KV_BRANCH_COLS)) + jnp.where(lane == 0, 0.0, MASK_BIAS))
    v_tiles.append((new_key_tile(2 * KV_BRANCH_COLS + LANES), False))
    o_win = _softmax_pv(s_tiles, v_tiles)

    o = gate_col[0] * o_cmp + gate_col[1] * o_slc + gate_col[2] * o_win
    for j in range(HEADS_PER_GROUP):
        pair = jnp.where(low_lanes, o[j:j + 1], o[j + HEADS_PER_GROUP:j + HEADS_PER_GROUP + 1])
        o_ref[:, j * LANES:(j + 1) * LANES] = pair.astype(o_ref.dtype)

    last = lax.broadcasted_iota(jnp.int32, (1, n_win), 1) == n_win - 1
    wnext_ref[...] = jnp.where(last, wnew_ref[...], pltpu.roll(win_ref[...], n_win - 1, 1))


def _attn_decode(q, kvb, gates, kvc, cache_slc, win_state, win_new, page_table, pos):
    n_dec, n_pages = page_table.shape
    nbp = kvc.shape[2]
    n_win = win_state.shape[2]
    seqs = DECODE_SEQS_PER_STEP
    per_seq = lambda a: pl.BlockSpec((seqs, 1, a.shape[-1]), lambda s, pt: (s, 0, 0))
    win_spec = pl.BlockSpec((seqs, KV_BRANCH_COLS, n_win), lambda s, pt: (s, 0, 0))
    q3, kvb3, g3 = (a.reshape(n_dec, 1, a.shape[-1]) for a in (q, kvb, gates))
    out, win_next = pl.pallas_call(
        functools.partial(_attn_decode_kernel, n_pages=n_pages, seqs=seqs, pos=pos, nbp=nbp),
        grid_spec=pltpu.PrefetchScalarGridSpec(
            num_scalar_prefetch=1,
            grid=(n_dec // seqs,),
            in_specs=_page_specs(n_pages, seqs) + [
                per_seq(q3), per_seq(kvb3), per_seq(g3),
                pl.BlockSpec((2, seqs, nbp, LANES), lambda s, pt: (0, s, 0, 0)),
                win_spec, pl.BlockSpec((seqs, KV_BRANCH_COLS, 1), lambda s, pt: (s, 0, 0)),
            ],
            out_specs=[pl.BlockSpec((seqs, 1, Q_COLS), lambda s, pt: (s, 0, 0)), win_spec],
        ),
        out_shape=[jax.ShapeDtypeStruct((n_dec, 1, Q_COLS), BF16), jax.ShapeDtypeStruct(win_state.shape, F32)],
        compiler_params=_params("arbitrary"),
        name="attn_decode",
    )(page_table, *([cache_slc] * (seqs * n_pages)), q3, kvb3, g3, kvc, win_state, win_new)
    return out.reshape(n_dec, Q_COLS), win_next


def _mix_kernel(*refs, tm, halo, tiles_per_seq):
    if halo:
        (x_ref, xh_ref, o_ref, shift_ref, scale_ref, gate_ref, gpre_ref, gpost_ref, cw_ref,
         wconv_ref, wmerge_ref, wup_ref, wco_ref, wout_ref, y_ref, ulast_ref, u_scr) = refs
        x = x_ref[...]
        xe = jnp.concatenate([xh_ref[...], x], axis=0)
    else:
        (x_ref, um1_ref, um2_ref, o_ref, shift_ref, scale_ref, gate_ref, gpre_ref, gpost_ref, cw_ref,
         wconv_ref, wmerge_ref, wup_ref, wco_ref, wout_ref, y_ref, ulast_ref) = refs
        x = x_ref[...]
        xe = x
    h0 = xe.shape[0] - tm
    a = (_rms(xe, gpre_ref[...]) * (1.0 + scale_ref[...]) + shift_ref[...]).astype(BF16)
    zc = _dot(a, wconv_ref[...])
    u = zc[:, 2 * D_CONV:] * zc[:, :D_CONV]
    cb = zc[h0:, D_CONV:2 * D_CONV]
    if halo:
        keep_halo = jnp.where(pl.program_id(0) % tiles_per_seq == 0, 0.0, 1.0)
        rows = lax.broadcasted_iota(jnp.int32, (xe.shape[0], 1), 0)
        u = jnp.where(rows < h0, u * keep_halo, u)
        u_scr[...] = u
        um1 = u_scr[h0 - 1:h0 - 1 + tm, :]
        um2 = u_scr[h0 - 2:h0 - 2 + tm, :]
        u0 = u[h0:]
        ulast_ref[...] = u[tm:]
    else:
        um1, um2, u0 = um1_ref[...], um2_ref[...], u
        ulast_ref[...] = u
    cw = cw_ref[...]
    y = cw[0:1] * um2 + cw[1:2] * um1 + cw[2:3] * u0
    conv_out = _dot((cb * y).astype(BF16), wco_ref[...])
    attn_out = _dot_tn(o_ref[...], wup_ref[...])
    mg = jax.nn.sigmoid(_dot(a[h0:], wmerge_ref[...]))
    merged = mg[:, :D_MODEL] * attn_out + mg[:, D_MODEL:] * conv_out
    mixed = _dot(merged.astype(BF16), wout_ref[...])
    y_ref[...] = x + gate_ref[...] * _rms(mixed, gpost_ref[...])


def _mix(x, o_att, mod, g3, conv_w, weights, tm, tiles_per_seq=None, prev=None):
    m = x.shape[0]
    halo = prev is None
    h0 = MIX_HALO
    row = lambda n: pl.BlockSpec((tm, n), lambda i: (i, 0))
    full = lambda a: pl.BlockSpec(a.shape, lambda i: (0,) * a.ndim, pipeline_mode=pl.Buffered(1))
    att = pl.BlockSpec((Q_COLS, tm), lambda i: (0, i))
    common = [att, mod.spec(3), mod.spec(4), mod.spec(5), _gspec(2), _gspec(3), full(conv_w)]
    common += [full(w) for w in weights]
    common_args = [o_att, mod.arr, mod.arr, mod.arr, g3, g3, conv_w, *weights]
    if halo:
        hb = tm // h0
        in_specs = [row(D_MODEL), pl.BlockSpec((h0, D_MODEL), lambda i: (jnp.maximum(i * hb - 1, 0), 0))] + common
        args = [x, x] + common_args
        scratch = [pltpu.VMEM((tm + h0, D_CONV), F32)]
        ulast = (jax.ShapeDtypeStruct((m // tm * h0, D_CONV), F32), pl.BlockSpec((h0, D_CONV), lambda i: (i, 0)))
    else:
        in_specs = [row(D_MODEL), row(D_CONV), row(D_CONV)] + common
        args = [x, prev[0], prev[1]] + common_args
        scratch = []
        ulast = (jax.ShapeDtypeStruct((m, D_CONV), F32), row(D_CONV))
    return pl.pallas_call(
        functools.partial(_mix_kernel, tm=tm, halo=halo, tiles_per_seq=tiles_per_seq),
        grid=(m // tm,),
        in_specs=in_specs,
        out_specs=[row(D_MODEL), ulast[1]],
        out_shape=[jax.ShapeDtypeStruct((m, D_MODEL), F32), ulast[0]],
        scratch_shapes=scratch,
        compiler_params=_params("parallel"),
        name="mix",
    )(*args)


def _head_pair_perm():
    order = []
    for j in range(HEADS_PER_GROUP):
        for g in range(N_KV_HEADS):
            head = g * HEADS_PER_GROUP + j
            order.extend(range(head * HEAD_DIM, (head + 1) * HEAD_DIM))
    return jnp.array(order, dtype=jnp.int32)


def _layer_weights(w_in, w_attn_up, w_conv_out, w_out):
    perm = _head_pair_perm()
    o1 = Q_COLS
    o2 = o1 + KV_COLS
    o3 = o2 + NSA_GATE_COLS
    o4 = o3 + CONV_COLS
    w_q = w_in[:, :o1][:, perm]
    w_g = jnp.pad(w_in[:, o2:o3], ((0, 0), (0, GATE_PAD - NSA_GATE_COLS)))
    w_qkvg = jnp.concatenate([w_q, w_in[:, o1:o2], w_g], axis=1).astype(BF16)
    mix_w = (w_in[:, o3:o4].astype(BF16), w_in[:, o4:].astype(BF16), w_attn_up[perm].astype(BF16),
             w_conv_out.astype(BF16), w_out.astype(BF16))
    return w_qkvg, mix_w


def kernel(x_prompt, x_sample, c_prompt, c_sample, cache_cmp_kv, cache_slc_kv, state_win_kv, state_conv,
           page_table, w_ada, b_ada, g_norm, w_ffn_gu, w_ffn_down, w_in, phi_pe, phi_w1, phi_b1, phi_w2,
           w_attn_up, conv_w, w_conv_out, w_out):
    n_seq, seq, _ = x_prompt.shape
    n_dec = x_sample.shape[0]
    assert w_ada.shape[0] == 1
    l = 0

    mod_all = _ada(jnp.concatenate([c_prompt, c_sample], axis=0), w_ada[l], b_ada[l])
    g3 = g_norm[l].reshape(6, 1, D_MODEL)
    w_gu = w_ffn_gu[l].astype(BF16)
    w_down = w_ffn_down[l].astype(BF16)
    w_qkvg, mix_w = _layer_weights(w_in[l], w_attn_up[l], w_conv_out[l], w_out[l])
    cwp = _compress_weights_paged(phi_pe[l], phi_w1[l], phi_b1[l], phi_w2[l])

    tm = 512
    tq = 256
    m = n_seq * seq
    mod_p = _Mod(mod_all[:n_seq], per_row=False, tiles_per_seq=seq // tm)
    tm_ffn, ffn_sub = 1024, 2
    mod_pf = _Mod(mod_all[:n_seq], per_row=False, tiles_per_seq=seq // tm_ffn)
    x0 = x_prompt.reshape(m, D_MODEL)
    x1 = _ffn(x0, mod_pf, g3, w_gu, w_down, 0, tm_ffn, ffn_sub)
    tables = _rope_tables(jnp.arange(seq, dtype=jnp.int32))
    _, qt, cmp_pages, cmpt, slct, wint, kvtb, kvb, _, gatest = _proj(x1, mod_p, g3, tables, w_qkvg, tm)
    nb = seq // NSA_BLOCK
    kvc = _gather_compress(cmp_pages, None, cwp, pages_per_seq=seq // PAGE_SIZE)
    kc = kvc[0].reshape(n_seq * nb, LANES).astype(BF16)
    vct = kvc[1].transpose(0, 2, 1).astype(BF16)
    o_att = _attn_prompt(qt, gatest, kc, vct, kvb, kvtb, n_seq, seq, tq)
    x2, ulast = _mix(x1, o_att, mod_pf, g3, conv_w[l], mix_w, tm_ffn, tiles_per_seq=seq // tm_ffn)
    y_prompt = _ffn(x2, mod_pf, g3, w_gu, w_down, 2, tm_ffn, ffn_sub).reshape(n_seq, seq, D_MODEL)

    def token_major(t):
        n, _, tt = t.shape
        return t.reshape(n, 2, N_KV_HEADS, HEAD_DIM, tt).transpose(0, 4, 1, 2, 3)[None]

    def feature_major(a):
        n, tt = a.shape[:2]
        return a.transpose(0, 2, 3, 4, 1).reshape(n, KV_BRANCH_COLS, tt)

    cmp_kv_prompt = token_major(cmpt)
    slc_kv_prompt = token_major(slct)
    win_keep = min(NSA_WINDOW, seq)
    win_kv_prompt = token_major(wint[:, :, seq - win_keep:])
    conv_prompt = ulast.reshape(n_seq, seq // tm_ffn, MIX_HALO, D_CONV)[:, -1, MIX_HALO - (CONV_WIDTH - 1):][None]

    n_pages = page_table.shape[1]
    past_len = n_pages * PAGE_SIZE
    pos_s = past_len + jnp.arange(x_sample.shape[1], dtype=jnp.int32)
    assert x_sample.shape[1] == 1 and past_len % NSA_BLOCK == 0 and state_win_kv.shape[2] <= past_len
    mod_s = _Mod(mod_all[n_seq:], per_row=True)
    xs1 = _ffn(x_sample.reshape(n_dec, D_MODEL), mod_s, g3, w_gu, w_down, 0, n_dec)
    tables_s = _rope_tables(jnp.broadcast_to(pos_s, (n_dec,)))
    q_s, _, _, cmpt_s, slct_s, wint_s, _, kvb_s, gates_s, _ = _proj(xs1, mod_s, g3, tables_s, w_qkvg, n_dec)
    nb_past = past_len // NSA_BLOCK
    nb_pad = -(-(nb_past + 1) // NSA_BLOCK) * NSA_BLOCK
    kvc_s = _gather_compress(feature_major(cache_cmp_kv[l]), page_table, cwp, new_tokens=cmpt_s[0])
    kvc_s = jnp.concatenate(
        [kvc_s[:, :, :nb_past + 1], jnp.zeros((2, n_dec, nb_pad - nb_past - 1, LANES), F32)], axis=2)
    win_state = feature_major(state_win_kv[l])
    assert win_state.shape[2] == NSA_WINDOW
    o_att_s, win_next = _attn_decode(q_s, kvb_s, gates_s, kvc_s, feature_major(cache_slc_kv[l]), win_state,
                                     wint_s[0].T[:, :, None], page_table, past_len)
    conv_state = state_conv[l]
    xs2, u_s = _mix(xs1, o_att_s.T, mod_s, g3, conv_w[l], mix_w, n_dec,
                    prev=(conv_state[:, CONV_WIDTH - 2], conv_state[:, CONV_WIDTH - 3]))
    y_sample = _ffn(xs2, mod_s, g3, w_gu, w_down, 2, n_dec).reshape(x_sample.shape)

    kvs = (1, n_dec, 1, 2, N_KV_HEADS, HEAD_DIM)
    conv_sample = jnp.concatenate([conv_state[:, 1:], u_s[:, None, :]], axis=1)[None]
    return (y_prompt, y_sample, cmp_kv_prompt, slc_kv_prompt, win_kv_prompt, conv_prompt,
            token_major(cmpt_s).reshape(kvs), token_major(slct_s).reshape(kvs), token_major(win_next), conv_sample)
```

```python
import functools

import jax
import jax.numpy as jnp
from jax import lax
from jax.experimental import pallas as pl
from jax.experimental.pallas import tpu as pltpu

D_MODEL = 1024
N_HEADS = 8
HEAD_DIM = 64
N_KV_HEADS = 2
HEADS_PER_GROUP = N_HEADS // N_KV_HEADS
ROT_DIM = HEAD_DIM // 4
ROPE_THETA = 500000.0
NSA_BLOCK = 64
NSA_TOP_N = 16
NSA_WINDOW = 512
PHI_HIDDEN = 2 * HEAD_DIM
D_CONV = 512
CONV_WIDTH = 3
D_FF = 2816
FFN_HALF = 0.5
NORM_EPS = 1e-6
PAGE_SIZE = 128
Q_COLS = N_HEADS * HEAD_DIM
KV_BRANCH_COLS = 2 * N_KV_HEADS * HEAD_DIM
KV_COLS = 3 * KV_BRANCH_COLS
NSA_GATE_COLS = 3 * N_HEADS
CONV_COLS = 3 * D_CONV
MERGE_COLS = 2 * D_MODEL
GD_ROWS = N_KV_HEADS * HEAD_DIM

LANES = 128
BF16_SUBLANES = 16
GATE_PAD = LANES
QKVG_COLS = Q_COLS + KV_COLS + GATE_PAD
LOG2_E = 1.4426950408889634
MASK_INIT = -1e30
MASK_BIAS = -2e30
VMEM_LIMIT = 56 * 1024 * 1024
SLAB_PITCH = HEAD_DIM + 8
MXU_DIM = 256
FFN_CHUNKS = D_FF // MXU_DIM
DECODE_SEQS_PER_STEP = 4
MIX_HALO = 16

BF16 = jnp.bfloat16
F32 = jnp.float32


def _dot(a, b):
    return jnp.dot(a, b, preferred_element_type=F32)


def _dot_tn(a, b):
    return lax.dot_general(a, b, (((0,), (0,)), ((), ())), preferred_element_type=F32)


def _dot_nt(a, b):
    return lax.dot_general(a, b, (((1,), (1,)), ((), ())), preferred_element_type=F32)


def _rms(x, g):
    return x * lax.rsqrt(jnp.mean(x * x, axis=-1, keepdims=True) + NORM_EPS) * g


def _silu(x):
    return x * jax.nn.sigmoid(x)


def _params(*sem):
    return pltpu.CompilerParams(dimension_semantics=sem, vmem_limit_bytes=VMEM_LIMIT)


def _ada_kernel(c_ref, w_ref, b_ref, o_ref):
    c = _silu(c_ref[...]).astype(BF16)
    o_ref[...] = _dot(c, w_ref[...].astype(BF16)) + b_ref[...]


def _ada(c, w_ada, b_ada):
    rows = c.shape[0]
    n = w_ada.shape[1]
    tn = 9 * LANES
    return pl.pallas_call(
        _ada_kernel,
        grid=(n // tn,),
        in_specs=[
            pl.BlockSpec((rows, D_MODEL), lambda j: (0, 0)),
            pl.BlockSpec((D_MODEL, tn), lambda j: (0, j)),
            pl.BlockSpec((1, tn), lambda j: (0, j)),
        ],
        out_specs=pl.BlockSpec((rows, tn), lambda j: (0, j)),
        out_shape=jax.ShapeDtypeStruct((rows, n), F32),
        compiler_params=_params("parallel"),
        name="ada",
    )(c, w_ada, b_ada.reshape(1, n))


class _Mod:
    def __init__(self, mod, per_row, tiles_per_seq=1):
        self.per_row = per_row
        self.tiles_per_seq = tiles_per_seq
        self.arr = mod if per_row else mod.reshape(mod.shape[0] * 9, 1, D_MODEL)

    def spec(self, k):
        if self.per_row:
            return pl.BlockSpec((self.arr.shape[0], D_MODEL), lambda i, *_: (0, k))
        tps = self.tiles_per_seq
        return pl.BlockSpec((None, 1, D_MODEL), lambda i, *_: ((i // tps) * 9 + k, 0, 0))


def _gspec(k):
    return pl.BlockSpec((None, 1, D_MODEL), lambda i, *_: (k, 0, 0))


def _ffn_kernel(x_ref, shift_ref, scale_ref, gate_ref, gpre_ref, gpost_ref, wgu_ref, wd_ref, o_ref,
                *, n_sub, n_chunk):
    ts = x_ref.shape[0] // n_sub
    tf = D_FF // n_chunk
    for s in range(n_sub):
        rows = slice(s * ts, (s + 1) * ts)
        mod = lambda ref: ref[...] if ref.shape[0] == 1 else ref[rows, :]
        x = x_ref[rows, :]
        a = (_rms(x, gpre_ref[...]) * (1.0 + mod(scale_ref)) + mod(shift_ref)).astype(BF16)
        y = None
        for c in range(n_chunk):
            g = _dot(a, wgu_ref[:, c * tf:(c + 1) * tf])
            u = _dot(a, wgu_ref[:, D_FF + c * tf:D_FF + (c + 1) * tf])
            d = _dot((_silu(g) * u).astype(BF16), wd_ref[c * tf:(c + 1) * tf, :])
            y = d if y is None else y + d
        o_ref[rows, :] = x + FFN_HALF * mod(gate_ref) * _rms(y, gpost_ref[...])


def _ffn(x, mod, g3, w_gu, w_down, sub, tm, n_sub=1):
    m = x.shape[0]
    row = pl.BlockSpec((tm, D_MODEL), lambda i: (i, 0))
    const = lambda a: pl.BlockSpec((None,) + a.shape[1:], lambda i: (sub // 2, 0, 0), pipeline_mode=pl.Buffered(1))
    return pl.pallas_call(
        functools.partial(_ffn_kernel, n_sub=n_sub, n_chunk=FFN_CHUNKS),
        grid=(m // tm,),
        in_specs=[
            row, mod.spec(3 * sub), mod.spec(3 * sub + 1), mod.spec(3 * sub + 2),
            _gspec(2 * sub), _gspec(2 * sub + 1), const(w_gu), const(w_down),
        ],
        out_specs=row,
        out_shape=jax.ShapeDtypeStruct((m, D_MODEL), F32),
        compiler_params=_params("parallel"),
        name=f"ffn{sub}",
    )(x, mod.arr, mod.arr, mod.arr, g3, g3, w_gu, w_down)


def _rope(x, cos, sin_lo, sin_hi):
    return x * cos + pltpu.roll(x, LANES - ROT_DIM // 2, 1) * sin_lo + pltpu.roll(x, ROT_DIM // 2, 1) * sin_hi


def _proj_kernel(x_ref, shift_ref, scale_ref, g_ref, cos_ref, slo_ref, shi_ref, w_ref,
                 q_ref, qt_ref, cmp_ref, cmpt_ref, slct_ref, wint_ref, kvtb_ref, kvb_ref, gate_ref, gatet_ref):
    a = (_rms(x_ref[...], g_ref[...]) * (1.0 + scale_ref[...]) + shift_ref[...]).astype(BF16)
    z = _dot(a, w_ref[...])
    cos, slo, shi = cos_ref[...], slo_ref[...], shi_ref[...]
    scale = HEAD_DIM ** -0.5 * LOG2_E
    for s in range(Q_COLS // LANES):
        q = _rope(z[:, s * LANES:(s + 1) * LANES], cos, slo, shi) * scale
        q_ref[:, s * LANES:(s + 1) * LANES] = q.astype(BF16)
        qt_ref[s * LANES:(s + 1) * LANES, :] = q.T.astype(BF16)
    kvt_refs = (cmpt_ref, slct_ref, wint_ref)
    for br in range(3):
        c0 = Q_COLS + br * KV_BRANCH_COLS
        k = _rope(z[:, c0:c0 + LANES], cos, slo, shi)
        v = z[:, c0 + LANES:c0 + 2 * LANES]
        for r0, x in ((0, k), (LANES, v)):
            xt = x.T
            kvt_refs[br][r0:r0 + LANES, :] = xt
            kvtb_ref[br, r0:r0 + LANES, :] = xt.astype(BF16)
            if br == 0:
                for p in range(cmp_ref.shape[0]):
                    cmp_ref[p, r0:r0 + LANES, :] = xt[:, p * PAGE_SIZE:(p + 1) * PAGE_SIZE]
        kvb_ref[:, br * KV_BRANCH_COLS:br * KV_BRANCH_COLS + LANES] = k.astype(BF16)
        kvb_ref[:, br * KV_BRANCH_COLS + LANES:(br + 1) * KV_BRANCH_COLS] = v.astype(BF16)
    gate = jax.nn.sigmoid(z[:, Q_COLS + KV_COLS:])
    gate_ref[...] = gate
    gatet_ref[...] = gate.T


def _proj(x, mod, g3, tables, w_qkvg, tm):
    assert tm % PAGE_SIZE == 0
    m = x.shape[0]
    t_rows = tables[0].shape[0]
    nt = t_rows // tm
    n_seq = m // t_rows
    row = lambda n: pl.BlockSpec((tm, n), lambda i: (i, 0))
    tab = pl.BlockSpec((tm, LANES), lambda i: (i % nt, 0))
    col = lambda n: pl.BlockSpec((n, tm), lambda i: (0, i))
    kvt_spec = pl.BlockSpec((None, KV_BRANCH_COLS, tm), lambda i: (i // nt, 0, i % nt))
    kvtb_spec = pl.BlockSpec((3, None, KV_BRANCH_COLS, tm), lambda i: (0, i // nt, 0, i % nt))
    return pl.pallas_call(
        _proj_kernel,
        grid=(m // tm,),
        in_specs=[row(D_MODEL), mod.spec(3), mod.spec(4), _gspec(2), tab, tab, tab,
                  pl.BlockSpec((D_MODEL, QKVG_COLS), lambda i: (0, 0))],
        out_specs=[row(Q_COLS), col(Q_COLS),
                   pl.BlockSpec((tm // PAGE_SIZE, KV_BRANCH_COLS, PAGE_SIZE), lambda i: (i, 0, 0)),
                   kvt_spec, kvt_spec, kvt_spec, kvtb_spec, row(KV_COLS), row(GATE_PAD), col(GATE_PAD)],
        out_shape=[
            jax.ShapeDtypeStruct((m, Q_COLS), BF16),
            jax.ShapeDtypeStruct((Q_COLS, m), BF16),
            jax.ShapeDtypeStruct((m // PAGE_SIZE, KV_BRANCH_COLS, PAGE_SIZE), F32),
            *[jax.ShapeDtypeStruct((n_seq, KV_BRANCH_COLS, t_rows), F32)] * 3,
            jax.ShapeDtypeStruct((3, n_seq, KV_BRANCH_COLS, t_rows), BF16),
            jax.ShapeDtypeStruct((m, KV_COLS), BF16),
            jax.ShapeDtypeStruct((m, GATE_PAD), F32),
            jax.ShapeDtypeStruct((GATE_PAD, m), F32),
        ],
        compiler_params=_params("parallel"),
        name="proj",
    )(x, mod.arr, mod.arr, g3, *tables, w_qkvg)


def _rope_tables(pos):
    half = ROT_DIM // 2
    inv = jnp.power(ROPE_THETA, -jnp.arange(half, dtype=F32) * 2.0 / ROT_DIM)
    ang = pos.astype(F32)[:, None] * inv[None, :]
    cos, sin = jnp.cos(ang), jnp.sin(ang)
    n = pos.shape[0]
    ones = jnp.ones((n, HEAD_DIM - ROT_DIM), F32)
    zeros = jnp.zeros((n, HEAD_DIM - ROT_DIM), F32)
    zh = jnp.zeros((n, half), F32)
    c = jnp.concatenate([cos, cos, ones], axis=1)
    lo = jnp.concatenate([-sin, zh, zeros], axis=1)
    hi = jnp.concatenate([zh, sin, zeros], axis=1)
    return tuple(jnp.tile(t, (1, LANES // HEAD_DIM)) for t in (c, lo, hi))


def _compress_weights_paged(phi_pe, phi_w1, phi_b1, phi_w2):
    blocks = PAGE_SIZE // NSA_BLOCK
    assert blocks == 2
    eye = jnp.eye(blocks, dtype=F32)
    wt = phi_w1.reshape(2, NSA_BLOCK, HEAD_DIM, PHI_HIDDEN).transpose(0, 2, 1, 3).astype(BF16)
    zero = jnp.zeros_like(wt)
    w1 = jnp.stack([jnp.concatenate([wt, zero], axis=-1), jnp.concatenate([zero, wt], axis=-1)], axis=2)
    w1 = w1.reshape(2, HEAD_DIM * PAGE_SIZE, blocks * PHI_HIDDEN)
    w2 = jnp.einsum("bc,kjd->kbjcd", eye, phi_w2).reshape(2, blocks * PHI_HIDDEN, blocks * HEAD_DIM)
    pe = jnp.tile(phi_pe.transpose(0, 2, 1), (1, 1, blocks)).reshape(2, HEAD_DIM, 1, PAGE_SIZE)
    b1 = jnp.tile(phi_b1, (1, blocks)).reshape(2, 1, blocks * PHI_HIDDEN)
    return pe, w1, b1, w2.astype(BF16)


def _gather_compress_kernel(pt_ref, *refs, seqs, n_pages, group, steps, contiguous, new_tokens, pad_blocks):
    del pt_ref
    if contiguous:
        pages = [refs[0].at[k] for k in range(seqs * n_pages)]
        refs = refs[1:]
    else:
        pages, refs = refs[:seqs * n_pages], refs[seqs * n_pages:]
    if new_tokens:
        newt_ref, refs = refs[0], refs[1:]
    pe_ref, w1_ref, b1_ref, w2_ref, o_ref, xk_scr, xv_scr, c_scr = refs
    x_scr = (xk_scr, xv_scr)
    slot = pl.program_id(0) % steps
    per_seq = n_pages + (1 if new_tokens else 0)
    for r in range(seqs):
        slabs = [pages[r * n_pages + k] for k in range(n_pages)]
        if new_tokens:
            seq_lane = lax.broadcasted_iota(jnp.int32, (1, newt_ref.shape[1]), 1) == pl.program_id(0) * seqs + r
            col = jnp.sum(jnp.where(seq_lane, newt_ref[...], 0.0), axis=1, keepdims=True)
            first = lax.broadcasted_iota(jnp.int32, (1, PAGE_SIZE), 1) == 0
            slabs.append(jnp.where(first, col, 0.0))
        for k, slab in enumerate(slabs):
            for kv in range(2):
                for g in range(N_KV_HEADS):
                    page = (slot * seqs + r) * per_seq + k
                    r0 = pl.multiple_of((page * N_KV_HEADS + g) * SLAB_PITCH, SLAB_PITCH)
                    s0 = kv * GD_ROWS + g * HEAD_DIM
                    x_scr[kv][pl.ds(r0, HEAD_DIM), :] = slab[s0:s0 + HEAD_DIM, :]

    @pl.when(slot == steps - 1)
    def _():
        rows = group * N_KV_HEADS
        accs = [jnp.zeros((rows, w1_ref.shape[2]), F32) for _ in range(2)]
        for d in range(0, HEAD_DIM, 2):
            for kv in range(2):
                xs = []
                for dd in (d, d + 1):
                    x = x_scr[kv][pl.ds(dd, rows, stride=SLAB_PITCH), :]
                    xs.append((x + pe_ref[kv, dd]).astype(BF16))
                accs[kv] = accs[kv] + _dot(jnp.concatenate(xs, axis=1),
                                           w1_ref[kv, d * PAGE_SIZE:(d + 2) * PAGE_SIZE, :])
        for kv in range(2):
            h = _silu(accs[kv] + b1_ref[kv])
            c_scr[...] = _dot(h.astype(BF16), w2_ref[kv])
            g0 = c_scr[pl.ds(0, rows // 2, stride=2), :]
            g1 = c_scr[pl.ds(1, rows // 2, stride=2), :]
            low = lax.broadcasted_iota(jnp.int32, (1, LANES), 1) < HEAD_DIM
            even = jnp.where(low, g0, pltpu.roll(g1, HEAD_DIM, 1))
            odd = jnp.where(low, pltpu.roll(g0, HEAD_DIM, 1), g1)
            if pad_blocks is None:
                o_ref.at[kv][pl.ds(0, rows // 2, stride=2), :] = even
                o_ref.at[kv][pl.ds(1, rows // 2, stride=2), :] = odd
            else:
                c_scr[pl.ds(0, rows // 2, stride=2), :] = even
                c_scr[pl.ds(1, rows // 2, stride=2), :] = odd
                nblk = per_seq * (PAGE_SIZE // NSA_BLOCK)
                o_ref[kv] = jnp.zeros(o_ref.shape[1:], o_ref.dtype)
                for r in range(steps * seqs):
                    o_ref[kv, r * pad_blocks:r * pad_blocks + nblk, :] = c_scr[r * nblk:(r + 1) * nblk, :]


def _page_specs(n_pages, seqs=1):
    return [pl.BlockSpec((None, KV_BRANCH_COLS, PAGE_SIZE), lambda s, pt, r=r, k=k: (pt[s * seqs + r, k], 0, 0))
            for r in range(seqs) for k in range(n_pages)]


def _gather_compress(cache_t, page_table, cwp, pages_per_seq=None, new_tokens=None, pad_blocks=None):
    pe, w1, b1, w2 = cwp
    contiguous = page_table is None
    seqs = 4
    if contiguous:
        n_pages = pages_per_seq
        n_dec = cache_t.shape[0] // n_pages
        page_table = jnp.zeros((1, 1), jnp.int32)
        page_specs = [pl.BlockSpec((seqs * n_pages, KV_BRANCH_COLS, PAGE_SIZE), lambda s, pt: (s, 0, 0))]
        page_args = [cache_t]
    else:
        n_dec, n_pages = page_table.shape
        page_specs = _page_specs(n_pages, seqs)
        page_args = [cache_t] * (seqs * n_pages)
    const = lambda a: pl.BlockSpec(a.shape, lambda s, pt: (0,) * a.ndim, pipeline_mode=pl.Buffered(1))
    per_seq = n_pages
    if new_tokens is not None:
        per_seq += 1
        page_specs.append(const(new_tokens))
        page_args.append(new_tokens)
    steps = 2
    group = steps * seqs * per_seq
    assert n_dec % (steps * seqs) == 0
    rows = group * N_KV_HEADS
    blocks = PAGE_SIZE // NSA_BLOCK
    assert blocks == 2 and N_KV_HEADS == 2
    out_rows = rows if pad_blocks is None else steps * seqs * pad_blocks
    assert pad_blocks is None or (pad_blocks >= per_seq * blocks and pad_blocks % 8 == 0)
    out = pl.pallas_call(
        functools.partial(_gather_compress_kernel, seqs=seqs, n_pages=n_pages, group=group, steps=steps,
                          contiguous=contiguous, new_tokens=new_tokens is not None, pad_blocks=pad_blocks),
        grid_spec=pltpu.PrefetchScalarGridSpec(
            num_scalar_prefetch=1,
            grid=(n_dec // seqs,),
            in_specs=page_specs + [const(pe), const(w1), const(b1), const(w2)],
            out_specs=pl.BlockSpec((2, out_rows, blocks * HEAD_DIM), lambda s, pt: (0, s // steps, 0)),
            scratch_shapes=[pltpu.VMEM((group * N_KV_HEADS * SLAB_PITCH, PAGE_SIZE), F32)] * 2
            + [pltpu.VMEM((rows, LANES), F32)],
        ),
        out_shape=jax.ShapeDtypeStruct((2, n_dec // (steps * seqs) * out_rows, blocks * HEAD_DIM), F32),
        compiler_params=_params("arbitrary"),
        name="gather_compress",
    )(page_table, *page_args, pe, w1, b1, w2)
    return out.reshape(2, n_dec, -1, N_KV_HEADS * HEAD_DIM)


def _group_queries_t(qt, g):
    t = qt.shape[1]
    zeros = jnp.zeros((HEAD_DIM, t), qt.dtype)
    cols = []
    for j in range(HEADS_PER_GROUP):
        qh = qt[j * LANES + g * HEAD_DIM:j * LANES + (g + 1) * HEAD_DIM, :]
        cols.append(jnp.concatenate([qh, zeros] if g == 0 else [zeros, qh], axis=0))
    return jnp.concatenate(cols, axis=1)


def _tile_heads(x):
    return jnp.concatenate([x] * HEADS_PER_GROUP, axis=1)


def _select_blocks_t(imp, nblk, n_sel):
    nb = imp.shape[0]
    rank = jnp.zeros(imp.shape, F32)
    for m in range(nb):
        row = imp[m:m + 1, :]
        beats = jnp.where(row > imp, 1.0, jnp.where(row == imp, jnp.where(m < nblk, 1.0, 0.0), 0.0))
        rank = rank + beats
    return jnp.where(rank < n_sel, jnp.where(imp >= 0.0, 1.0, 0.0), 0.0)


def _select_blocks(imp, nblk, n_sel):
    nb = imp.shape[1]
    rank = jnp.zeros(imp.shape, F32)
    for m in range(nb):
        col = imp[:, m:m + 1]
        beats = jnp.where(col > imp, 1.0, jnp.where(col == imp, jnp.where(m < nblk, 1.0, 0.0), 0.0))
        rank = rank + beats
    return jnp.where(rank < n_sel, jnp.where(imp >= 0.0, 1.0, 0.0), 0.0)


def _attn_tile(carry, k, q, vt_ref, ks, tk, bias):
    m, l, acc = carry
    s = _dot(k, q)
    if bias is not None:
        s = s + bias
    m_new = jnp.maximum(m, jnp.max(s, axis=0, keepdims=True))
    alpha = jnp.exp2(m - m_new)
    pb = jnp.exp2(s - m_new).astype(BF16)
    half = pb.shape[1] // N_KV_HEADS
    ones = jnp.ones((BF16_SUBLANES, tk), BF16)
    pv = [_dot(jnp.concatenate([vt_ref[GD_ROWS + g * HEAD_DIM:GD_ROWS + (g + 1) * HEAD_DIM, pl.ds(ks, tk)], ones],
                               axis=0), pb[:, g * half:(g + 1) * half]) for g in range(N_KV_HEADS)]
    pv = jnp.concatenate(pv, axis=1)
    return m_new, alpha * l + pv[HEAD_DIM:HEAD_DIM + 1], alpha * acc + pv[:HEAD_DIM]


def _maybe(cond, fn, carry):
    return lax.fori_loop(0, jnp.where(cond, 1, 0), lambda _, c: fn(c), carry)


def _attn_kernel(qt_ref, gatet_ref, kc_ref, vct_ref, slc_k_ref, slc_vt_ref, win_k_ref, win_vt_ref, o_ref,
                 *, tq, nb):
    i = pl.program_id(1)
    p0 = i * tq
    tk = tq
    hq = HEADS_PER_GROUP * tq
    n_win_tiles = NSA_WINDOW // tk
    pos = p0 + lax.broadcasted_iota(jnp.int32, (1, tq), 1)
    nblk = lax.broadcasted_iota(jnp.int32, (nb, 1), 0)
    key_off = lax.broadcasted_iota(jnp.int32, (tk, 1), 0)
    qt = qt_ref[...]
    q_all = jnp.concatenate([_group_queries_t(qt, g) for g in range(N_KV_HEADS)], axis=1)
    tile_all = lambda x: jnp.concatenate([x] * N_HEADS, axis=1)

    cvalid = (nblk + 1) * NSA_BLOCK <= tile_all(pos) + 1
    sc = jnp.where(cvalid, _dot(kc_ref[...], q_all), MASK_INIT)
    e = jnp.exp2(sc - jnp.max(sc, axis=0, keepdims=True))
    pc = jnp.where(cvalid, e / jnp.sum(e, axis=0, keepdims=True), 0.0)
    pcb = pc.astype(BF16)
    o_cmp = jnp.concatenate([_dot(vct_ref[g * HEAD_DIM:(g + 1) * HEAD_DIM, :], pcb[:, g * hq:(g + 1) * hq])
                             for g in range(N_KV_HEADS)], axis=1)

    cur = pos // NSA_BLOCK
    forced = HEADS_PER_GROUP + 1.0
    sel_rows = []
    for g in range(N_KV_HEADS):
        imp = functools.reduce(jnp.add, [pc[:, g * hq + j * tq:g * hq + (j + 1) * tq]
                                         for j in range(HEADS_PER_GROUP)])
        imp = jnp.where(nblk == 0, forced, jnp.where(nblk == cur, forced, jnp.where(nblk == cur - 1, forced, imp)))
        imp = jnp.where(nblk * NSA_BLOCK <= pos, imp, -1.0)
        sel = _select_blocks_t(imp, nblk, NSA_TOP_N)
        sel_rows.append(_tile_heads(jnp.where(sel > 0.5, 0.0, MASK_BIAS)))
    sel_rows = jnp.concatenate(sel_rows, axis=1)
    sel_rows = jnp.concatenate([sel_rows, jnp.zeros((LANES - nb, sel_rows.shape[1]), F32)], axis=0)
    q_sel = jnp.concatenate([q_all, sel_rows.astype(BF16)], axis=0)
    blk_lane = lax.broadcasted_iota(jnp.int32, (1, LANES), 1)

    def slc_keys(ks):
        onehot = jnp.where(blk_lane == (ks + key_off) // NSA_BLOCK, 1.0, 0.0).astype(BF16)
        return jnp.concatenate([slc_k_ref[pl.ds(ks, tk), :GD_ROWS], onehot], axis=1)

    cols = q_all.shape[1]
    init = (jnp.full((1, cols), MASK_INIT, F32), jnp.zeros((1, cols), F32), jnp.zeros((HEAD_DIM, cols), F32))
    causal = tile_all(jnp.where(p0 + key_off <= pos, 0.0, MASK_BIAS))
    k_diag = pl.multiple_of(p0, tk)

    def slc_body(j, c):
        ks = pl.multiple_of(j * tk, tk)
        return _attn_tile(c, slc_keys(ks), q_sel, slc_vt_ref, ks, tk, None)

    c = lax.fori_loop(0, i, slc_body, init)
    _, l, acc = _attn_tile(c, slc_keys(k_diag), q_sel, slc_vt_ref, k_diag, tk, causal)
    o_slc = acc / l

    def win_tile(c, ks, bias):
        return _attn_tile(c, win_k_ref[pl.ds(ks, tk), :GD_ROWS], q_all, win_vt_ref, ks, tk, bias)

    def edge_tile(c):
        ks = pl.multiple_of((i - n_win_tiles) * tk, tk)
        bias = tile_all(jnp.where(pos - (ks + key_off) < NSA_WINDOW, 0.0, MASK_BIAS))
        return win_tile(c, ks, bias)

    c = _maybe(i >= n_win_tiles, edge_tile, init)
    c = lax.fori_loop(jnp.maximum(i - n_win_tiles + 1, 0), i,
                      lambda j, c: win_tile(c, pl.multiple_of(j * tk, tk), None), c)
    _, l, acc = win_tile(c, k_diag, causal)
    o_win = acc / l

    gates = gatet_ref[...]
    for g in range(N_KV_HEADS):
        for j in range(HEADS_PER_GROUP):
            head = g * HEADS_PER_GROUP + j
            cs = slice(g * hq + j * tq, g * hq + (j + 1) * tq)
            o = (gates[head:head + 1] * o_cmp[:, cs]
                 + gates[N_HEADS + head:N_HEADS + head + 1] * o_slc[:, cs]
                 + gates[2 * N_HEADS + head:2 * N_HEADS + head + 1] * o_win[:, cs])
            r0 = j * LANES + g * HEAD_DIM
            o_ref[r0:r0 + HEAD_DIM, :] = o.astype(o_ref.dtype)


def _attn_prompt(qt, gatest, kc, vct, kvb, kvtb, n_seq, seq, tq):
    m = qt.shape[1]
    nb = seq // NSA_BLOCK
    nq = seq // tq
    assert seq % tq == 0 and NSA_WINDOW % tq == 0 and tq % NSA_BLOCK == 0 and nb <= LANES
    col = lambda n: pl.BlockSpec((n, tq), lambda b, i: (0, b * nq + i))
    return pl.pallas_call(
        functools.partial(_attn_kernel, tq=tq, nb=nb),
        grid=(n_seq, nq),
        in_specs=[col(Q_COLS), col(4 * N_HEADS),
                  pl.BlockSpec((nb, LANES), lambda b, i: (b, 0)),
                  pl.BlockSpec((None, LANES, nb), lambda b, i: (b, 0, 0)),
                  pl.BlockSpec((seq, KV_BRANCH_COLS), lambda b, i: (b, 1)),
                  pl.BlockSpec((None, None, KV_BRANCH_COLS, seq), lambda b, i: (1, b, 0, 0)),
                  pl.BlockSpec((seq, KV_BRANCH_COLS), lambda b, i: (b, 2)),
                  pl.BlockSpec((None, None, KV_BRANCH_COLS, seq), lambda b, i: (2, b, 0, 0))],
        out_specs=col(Q_COLS),
        out_shape=jax.ShapeDtypeStruct((Q_COLS, m), BF16),
        compiler_params=_params("parallel", "arbitrary"),
        name="attn_prompt",
    )(qt, gatest, kc, vct, kvb, kvtb, kvb, kvtb)


def _softmax_pv(score_tiles, value_tiles):
    m = functools.reduce(jnp.maximum, [jnp.max(s, axis=-1, keepdims=True) for s in score_tiles])
    ps = [jnp.exp2(s - m) for s in score_tiles]
    l = functools.reduce(jnp.add, [jnp.sum(p, axis=-1, keepdims=True) for p in ps])
    o = functools.reduce(jnp.add, [(_dot_nt if fm else _dot)(p.astype(BF16), v)
                                   for p, (v, fm) in zip(ps, value_tiles)])
    return o / l


def _attn_decode_kernel(pt_ref, *refs, n_pages, seqs, pos, nbp):
    del pt_ref
    pages = refs[:seqs * n_pages]
    q_ref, kvb_ref, gate_ref, kvc_ref, win_ref, wnew_ref, o_ref, wnext_ref = refs[seqs * n_pages:]
    for r in range(seqs):
        _attn_decode_one(pages[r * n_pages:(r + 1) * n_pages], q_ref.at[r], kvb_ref.at[r], gate_ref.at[r],
                         kvc_ref.at[:, r], win_ref.at[r], wnew_ref.at[r], o_ref.at[r], wnext_ref.at[r],
                         pos=pos, nbp=nbp)


def _attn_decode_one(pages, q_ref, kvb_ref, gate_ref, kvc_ref, win_ref, wnew_ref, o_ref, wnext_ref, *, pos, nbp):
    n_pages = len(pages)
    tk = PAGE_SIZE
    lane = lax.broadcasted_iota(jnp.int32, (1, LANES), 1)
    head = lax.broadcasted_iota(jnp.int32, (N_HEADS, 1), 0)
    low_group = head < HEADS_PER_GROUP
    low_lanes = lane < HEAD_DIM

    q32 = q_ref[...].astype(F32)
    qm = jnp.zeros((N_HEADS, LANES), F32)
    for j in range(HEADS_PER_GROUP):
        qm = jnp.where(head % HEADS_PER_GROUP == j, q32[:, j * LANES:(j + 1) * LANES], qm)
    qm = jnp.where(low_group, jnp.where(low_lanes, qm, 0.0), jnp.where(low_lanes, 0.0, qm)).astype(BF16)

    gate_row = gate_ref[...]
    gate_col = [jnp.sum(jnp.where(lane == head + br * N_HEADS, gate_row, 0.0), axis=-1, keepdims=True)
                for br in range(3)]
    kv_new = kvb_ref[...].astype(F32)
    first_row = lax.broadcasted_iota(jnp.int32, (tk, 1), 0) == 0

    def new_key_tile(c0):
        return jnp.where(first_row, kv_new[:, c0:c0 + LANES], 0.0).astype(BF16)

    nblk = lax.broadcasted_iota(jnp.int32, (1, nbp), 1)
    cvalid = (nblk + 1) * NSA_BLOCK <= pos + 1
    sc = jnp.where(cvalid, _dot_nt(qm, kvc_ref[0].astype(BF16)), MASK_INIT)
    e = jnp.exp2(sc - jnp.max(sc, axis=-1, keepdims=True))
    pc = jnp.where(cvalid, e / jnp.sum(e, axis=-1, keepdims=True), 0.0)
    o_cmp = _dot(pc.astype(BF16), kvc_ref[1].astype(BF16))

    imp = jnp.where(low_group,
                    jnp.sum(jnp.where(low_group, pc, 0.0), axis=0, keepdims=True),
                    jnp.sum(jnp.where(low_group, 0.0, pc), axis=0, keepdims=True))
    cur = pos // NSA_BLOCK
    forced = HEADS_PER_GROUP + 1.0
    imp = jnp.where(nblk == 0, forced, jnp.where(nblk == cur, forced, jnp.where(nblk == cur - 1, forced, imp)))
    imp = jnp.where(nblk * NSA_BLOCK <= pos, imp, -1.0)
    sel = _select_blocks(imp, nblk, NSA_TOP_N)

    def sel_bias(n):
        return jnp.where(sel[:, n:n + 1] > 0.5, 0.0, MASK_BIAS)

    s_tiles, v_tiles = [], []
    blocks_per_page = tk // NSA_BLOCK
    for k in range(n_pages):
        page = pages[k]
        s = _dot(qm, page[:GD_ROWS, :].astype(BF16))
        bias = sel_bias(k * blocks_per_page + blocks_per_page - 1)
        for b in range(blocks_per_page - 2, -1, -1):
            bias = jnp.where(lane < (b + 1) * NSA_BLOCK, sel_bias(k * blocks_per_page + b), bias)
        s_tiles.append(s + bias)
        v_tiles.append((page[GD_ROWS:, :].astype(BF16), True))
    s_new = _dot_nt(qm, new_key_tile(KV_BRANCH_COLS))
    s_tiles.append(s_new + jnp.where(lane == 0, sel_bias(n_pages * blocks_per_page), MASK_BIAS))
    v_tiles.append((new_key_tile(KV_BRANCH_COLS + LANES), False))
    o_slc = _softmax_pv(s_tiles, v_tiles)

    n_win = win_ref.shape[1]
    s_tiles, v_tiles = [], []
    for k in range(n_win // tk):
        diff = n_win - (k * tk + lane)
        bias = jnp.where(diff < NSA_WINDOW, 0.0, MASK_BIAS)
        s_tiles.append(_dot(qm, win_ref[:GD_ROWS, k * tk:(k + 1) * tk].astype(BF16)) + bias)
        v_tiles.append((win_ref[GD_ROWS:, k * tk:(k + 1) * tk].astype(BF16), True))
    s_tiles.append(_dot_nt(qm, new_key_tile(2 * KV_BRANCH_COLS)) + jnp.where(lane == 0, 0.0, MASK_BIAS))
    v_tiles.append((new_key_tile(2 * KV_BRANCH_COLS + LANES), False))
    o_win = _softmax_pv(s_tiles, v_tiles)

    o = gate_col[0] * o_cmp + gate_col[1] * o_slc + gate_col[2] * o_win
    for j in range(HEADS_PER_GROUP):
        pair = jnp.where(low_lanes, o[j:j + 1], o[j + HEADS_PER_GROUP:j + HEADS_PER_GROUP + 1])
        o_ref[:, j * LANES:(j + 1) * LANES] = pair.astype(o_ref.dtype)

    last = lax.broadcasted_iota(jnp.int32, (1, n_win), 1) == n_win - 1
    wnext_ref[...] = jnp.where(last, wnew_ref[...], pltpu.roll(win_ref[...], n_win - 1, 1))


def _attn_decode(q, kvb, gates, kvc, cache_slc, win_state, win_new, page_table, pos):
    n_dec, n_pages = page_table.shape
    nbp = kvc.shape[2]
    n_win = win_state.shape[2]
    seqs = DECODE_SEQS_PER_STEP
    per_seq = lambda a: pl.BlockSpec((seqs, 1, a.shape[-1]), lambda s, pt: (s, 0, 0))
    win_spec = pl.BlockSpec((seqs, KV_BRANCH_COLS, n_win), lambda s, pt: (s, 0, 0))
    q3, kvb3, g3 = (a.reshape(n_dec, 1, a.shape[-1]) for a in (q, kvb, gates))
    out, win_next = pl.pallas_call(
        functools.partial(_attn_decode_kernel, n_pages=n_pages, seqs=seqs, pos=pos, nbp=nbp),
        grid_spec=pltpu.PrefetchScalarGridSpec(
            num_scalar_prefetch=1,
            grid=(n_dec // seqs,),
            in_specs=_page_specs(n_pages, seqs) + [
                per_seq(q3), per_seq(kvb3), per_seq(g3),
                pl.BlockSpec((2, seqs, nbp, LANES), lambda s, pt: (0, s, 0, 0)),
                win_spec, pl.BlockSpec((seqs, KV_BRANCH_COLS, 1), lambda s, pt: (s, 0, 0)),
            ],
            out_specs=[pl.BlockSpec((seqs, 1, Q_COLS), lambda s, pt: (s, 0, 0)), win_spec],
        ),
        out_shape=[jax.ShapeDtypeStruct((n_dec, 1, Q_COLS), BF16), jax.ShapeDtypeStruct(win_state.shape, F32)],
        compiler_params=_params("arbitrary"),
        name="attn_decode",
    )(page_table, *([cache_slc] * (seqs * n_pages)), q3, kvb3, g3, kvc, win_state, win_new)
    return out.reshape(n_dec, Q_COLS), win_next


def _mix_kernel(*refs, tm, halo, tiles_per_seq):
    if halo:
        (x_ref, xh_ref, o_ref, shift_ref, scale_ref, gate_ref, gpre_ref, gpost_ref, cw_ref,
         wconv_ref, wmerge_ref, wup_ref, wco_ref, wout_ref, y_ref, ulast_ref, u_scr) = refs
        x = x_ref[...]
        xe = jnp.concatenate([xh_ref[...], x], axis=0)
    else:
        (x_ref, um1_ref, um2_ref, o_ref, shift_ref, scale_ref, gate_ref, gpre_ref, gpost_ref, cw_ref,
         wconv_ref, wmerge_ref, wup_ref, wco_ref, wout_ref, y_ref, ulast_ref) = refs
        x = x_ref[...]
        xe = x
    h0 = xe.shape[0] - tm
    a = (_rms(xe, gpre_ref[...]) * (1.0 + scale_ref[...]) + shift_ref[...]).astype(BF16)
    zc = _dot(a, wconv_ref[...])
    u = zc[:, 2 * D_CONV:] * zc[:, :D_CONV]
    cb = zc[h0:, D_CONV:2 * D_CONV]
    if halo:
        keep_halo = jnp.where(pl.program_id(0) % tiles_per_seq == 0, 0.0, 1.0)
        rows = lax.broadcasted_iota(jnp.int32, (xe.shape[0], 1), 0)
        u = jnp.where(rows < h0, u * keep_halo, u)
        u_scr[...] = u
        um1 = u_scr[h0 - 1:h0 - 1 + tm, :]
        um2 = u_scr[h0 - 2:h0 - 2 + tm, :]
        u0 = u[h0:]
        ulast_ref[...] = u[tm:]
    else:
        um1, um2, u0 = um1_ref[...], um2_ref[...], u
        ulast_ref[...] = u
    cw = cw_ref[...]
    y = cw[0:1] * um2 + cw[1:2] * um1 + cw[2:3] * u0
    conv_out = _dot((cb * y).astype(BF16), wco_ref[...])
    attn_out = _dot_tn(o_ref[...], wup_ref[...])
    mg = jax.nn.sigmoid(_dot(a[h0:], wmerge_ref[...]))
    merged = mg[:, :D_MODEL] * attn_out + mg[:, D_MODEL:] * conv_out
    mixed = _dot(merged.astype(BF16), wout_ref[...])
    y_ref[...] = x + gate_ref[...] * _rms(mixed, gpost_ref[...])


def _mix(x, o_att, mod, g3, conv_w, weights, tm, tiles_per_seq=None, prev=None):
    m = x.shape[0]
    halo = prev is None
    h0 = MIX_HALO
    row = lambda n: pl.BlockSpec((tm, n), lambda i: (i, 0))
    full = lambda a: pl.BlockSpec(a.shape, lambda i: (0,) * a.ndim, pipeline_mode=pl.Buffered(1))
    att = pl.BlockSpec((Q_COLS, tm), lambda i: (0, i))
    common = [att, mod.spec(3), mod.spec(4), mod.spec(5), _gspec(2), _gspec(3), full(conv_w)]
    common += [full(w) for w in weights]
    common_args = [o_att, mod.arr, mod.arr, mod.arr, g3, g3, conv_w, *weights]
    if halo:
        hb = tm // h0
        in_specs = [row(D_MODEL), pl.BlockSpec((h0, D_MODEL), lambda i: (jnp.maximum(i * hb - 1, 0), 0))] + common
        args = [x, x] + common_args
        scratch = [pltpu.VMEM((tm + h0, D_CONV), F32)]
        ulast = (jax.ShapeDtypeStruct((m // tm * h0, D_CONV), F32), pl.BlockSpec((h0, D_CONV), lambda i: (i, 0)))
    else:
        in_specs = [row(D_MODEL), row(D_CONV), row(D_CONV)] + common
        args = [x, prev[0], prev[1]] + common_args
        scratch = []
        ulast = (jax.ShapeDtypeStruct((m, D_CONV), F32), row(D_CONV))
    return pl.pallas_call(
        functools.partial(_mix_kernel, tm=tm, halo=halo, tiles_per_seq=tiles_per_seq),
        grid=(m // tm,),
        in_specs=in_specs,
        out_specs=[row(D_MODEL), ulast[1]],
        out_shape=[jax.ShapeDtypeStruct((m, D_MODEL), F32), ulast[0]],
        scratch_shapes=scratch,
        compiler_params=_params("parallel"),
        name="mix",
    )(*args)


def _head_pair_perm():
    order = []
    for j in range(HEADS_PER_GROUP):
        for g in range(N_KV_HEADS):
            head = g * HEADS_PER_GROUP + j
            order.extend(range(head * HEAD_DIM, (head + 1) * HEAD_DIM))
    return jnp.array(order, dtype=jnp.int32)


def _layer_weights(w_in, w_attn_up, w_conv_out, w_out):
    perm = _head_pair_perm()
    o1 = Q_COLS
    o2 = o1 + KV_COLS
    o3 = o2 + NSA_GATE_COLS
    o4 = o3 + CONV_COLS
    w_q = w_in[:, :o1][:, perm]
    w_g = jnp.pad(w_in[:, o2:o3], ((0, 0), (0, GATE_PAD - NSA_GATE_COLS)))
    w_qkvg = jnp.concatenate([w_q, w_in[:, o1:o2], w_g], axis=1).astype(BF16)
    mix_w = (w_in[:, o3:o4].astype(BF16), w_in[:, o4:].astype(BF16), w_attn_up[perm].astype(BF16),
             w_conv_out.astype(BF16), w_out.astype(BF16))
    return w_qkvg, mix_w


def kernel(x_prompt, x_sample, c_prompt, c_sample, cache_cmp_kv, cache_slc_kv, state_win_kv, state_conv,
           page_table, w_ada, b_ada, g_norm, w_ffn_gu, w_ffn_down, w_in, phi_pe, phi_w1, phi_b1, phi_w2,
           w_attn_up, conv_w, w_conv_out, w_out):
    n_seq, seq, _ = x_prompt.shape
    n_dec = x_sample.shape[0]
    assert w_ada.shape[0] == 1
    l = 0

    mod_all = _ada(jnp.concatenate([c_prompt, c_sample], axis=0), w_ada[l], b_ada[l])
    g3 = g_norm[l].reshape(6, 1, D_MODEL)
    w_gu = w_ffn_gu[l].astype(BF16)
    w_down = w_ffn_down[l].astype(BF16)
    w_qkvg, mix_w = _layer_weights(w_in[l], w_attn_up[l], w_conv_out[l], w_out[l])
    cwp = _compress_weights_paged(phi_pe[l], phi_w1[l], phi_b1[l], phi_w2[l])

    tm = 512
    tq = 256
    m = n_seq * seq
    mod_p = _Mod(mod_all[:n_seq], per_row=False, tiles_per_seq=seq // tm)
    tm_ffn, ffn_sub = 1024, 2
    mod_pf = _Mod(mod_all[:n_seq], per_row=False, tiles_per_seq=seq // tm_ffn)
    x0 = x_prompt.reshape(m, D_MODEL)
    x1 = _ffn(x0, mod_pf, g3, w_gu, w_down, 0, tm_ffn, ffn_sub)
    tables = _rope_tables(jnp.arange(seq, dtype=jnp.int32))
    _, qt, cmp_pages, cmpt, slct, wint, kvtb, kvb, _, gatest = _proj(x1, mod_p, g3, tables, w_qkvg, tm)
    nb = seq // NSA_BLOCK
    kvc = _gather_compress(cmp_pages, None, cwp, pages_per_seq=seq // PAGE_SIZE)
    kc = kvc[0].reshape(n_seq * nb, LANES).astype(BF16)
    vct = kvc[1].transpose(0, 2, 1).astype(BF16)
    o_att = _attn_prompt(qt, gatest, kc, vct, kvb, kvtb, n_seq, seq, tq)
    x2, ulast = _mix(x1, o_att, mod_pf, g3, conv_w[l], mix_w, tm_ffn, tiles_per_seq=seq // tm_ffn)
    y_prompt = _ffn(x2, mod_pf, g3, w_gu, w_down, 2, tm_ffn, ffn_sub).reshape(n_seq, seq, D_MODEL)

    def token_major(t):
        n, _, tt = t.shape
        return t.reshape(n, 2, N_KV_HEADS, HEAD_DIM, tt).transpose(0, 4, 1, 2, 3)[None]

    def feature_major(a):
        n, tt = a.shape[:2]
        return a.transpose(0, 2, 3, 4, 1).reshape(n, KV_BRANCH_COLS, tt)

    cmp_kv_prompt = token_major(cmpt)
    slc_kv_prompt = token_major(slct)
    win_keep = min(NSA_WINDOW, seq)
    win_kv_prompt = token_major(wint[:, :, seq - win_keep:])
    conv_prompt = ulast.reshape(n_seq, seq // tm_ffn, MIX_HALO, D_CONV)[:, -1, MIX_HALO - (CONV_WIDTH - 1):][None]

    n_pages = page_table.shape[1]
    past_len = n_pages * PAGE_SIZE
    pos_s = past_len + jnp.arange(x_sample.shape[1], dtype=jnp.int32)
    assert x_sample.shape[1] == 1 and past_len % NSA_BLOCK == 0 and state_win_kv.shape[2] <= past_len
    mod_s = _Mod(mod_all[n_seq:], per_row=True)
    xs1 = _ffn(x_sample.reshape(n_dec, D_MODEL), mod_s, g3, w_gu, w_down, 0, n_dec)
    tables_s = _rope_tables(jnp.broadcast_to(pos_s, (n_dec,)))
    q_s, _, _, cmpt_s, slct_s, wint_s, _, kvb_s, gates_s, _ = _proj(xs1, mod_s, g3, tables_s, w_qkvg, n_dec)
    nb_past = past_len // NSA_BLOCK
    nb_pad = -(-(nb_past + 1) // NSA_BLOCK) * NSA_BLOCK
    kvc_s = _gather_compress(feature_major(cache_cmp_kv[l]), page_table, cwp, new_tokens=cmpt_s[0],
                             pad_blocks=nb_pad)
    win_state = feature_major(state_win_kv[l])
    assert win_state.shape[2] == NSA_WINDOW
    o_att_s, win_next = _attn_decode(q_s, kvb_s, gates_s, kvc_s, feature_major(cache_slc_kv[l]), win_state,
                                     wint_s[0].T[:, :, None], page_table, past_len)
    conv_state = state_conv[l]
    xs2, u_s = _mix(xs1, o_att_s.T, mod_s, g3, conv_w[l], mix_w, n_dec,
                    prev=(conv_state[:, CONV_WIDTH - 2], conv_state[:, CONV_WIDTH - 3]))
    y_sample = _ffn(xs2, mod_s, g3, w_gu, w_down, 2, n_dec).reshape(x_sample.shape)

    kvs = (1, n_dec, 1, 2, N_KV_HEADS, HEAD_DIM)
    conv_sample = jnp.concatenate([conv_state[:, 1:], u_s[:, None, :]], axis=1)[None]
    return (y_prompt, y_sample, cmp_kv_prompt, slc_kv_prompt, win_kv_prompt, conv_prompt,
            token_major(cmpt_s).reshape(kvs), token_major(slct_s).reshape(kvs), token_major(win_next), conv_sample)
```

```python
import functools

import jax
import jax.numpy as jnp
from jax import lax
from jax.experimental import pallas as pl
from jax.experimental.pallas import tpu as pltpu

D_MODEL = 1024
N_HEADS = 8
HEAD_DIM = 64
N_KV_HEADS = 2
HEADS_PER_GROUP = N_HEADS // N_KV_HEADS
ROT_DIM = HEAD_DIM // 4
ROPE_THETA = 500000.0
NSA_BLOCK = 64
NSA_TOP_N = 16
NSA_WINDOW = 512
PHI_HIDDEN = 2 * HEAD_DIM
D_CONV = 512
CONV_WIDTH = 3
D_FF = 2816
FFN_HALF = 0.5
NORM_EPS = 1e-6
PAGE_SIZE = 128
Q_COLS = N_HEADS * HEAD_DIM
KV_BRANCH_COLS = 2 * N_KV_HEADS * HEAD_DIM
KV_COLS = 3 * KV_BRANCH_COLS
NSA_GATE_COLS = 3 * N_HEADS
CONV_COLS = 3 * D_CONV
MERGE_COLS = 2 * D_MODEL
GD_ROWS = N_KV_HEADS * HEAD_DIM

LANES = 128
BF16_SUBLANES = 16
GATE_PAD = LANES
QKVG_COLS = Q_COLS + KV_COLS + GATE_PAD
LOG2_E = 1.4426950408889634
MASK_INIT = -1e30
MASK_BIAS = -2e30
VMEM_LIMIT = 56 * 1024 * 1024
SLAB_PITCH = HEAD_DIM + 8
MXU_DIM = 256
FFN_CHUNKS = D_FF // MXU_DIM
DECODE_SEQS_PER_STEP = 4
DECODE_PAGES_PER_DOT = 4
MIX_HALO = 16

BF16 = jnp.bfloat16
F32 = jnp.float32


def _dot(a, b):
    return jnp.dot(a, b, preferred_element_type=F32)


def _dot_tn(a, b):
    return lax.dot_general(a, b, (((0,), (0,)), ((), ())), preferred_element_type=F32)


def _dot_nt(a, b):
    return lax.dot_general(a, b, (((1,), (1,)), ((), ())), preferred_element_type=F32)


def _rms(x, g):
    return x * lax.rsqrt(jnp.mean(x * x, axis=-1, keepdims=True) + NORM_EPS) * g


def _silu(x):
    return x * jax.nn.sigmoid(x)


def _params(*sem):
    return pltpu.CompilerParams(dimension_semantics=sem, vmem_limit_bytes=VMEM_LIMIT)


def _ada_kernel(c_ref, w_ref, b_ref, o_ref):
    c = _silu(c_ref[...]).astype(BF16)
    o_ref[...] = _dot(c, w_ref[...].astype(BF16)) + b_ref[...]


def _ada(c, w_ada, b_ada):
    rows = c.shape[0]
    n = w_ada.shape[1]
    tn = 9 * LANES
    return pl.pallas_call(
        _ada_kernel,
        grid=(n // tn,),
        in_specs=[
            pl.BlockSpec((rows, D_MODEL), lambda j: (0, 0)),
            pl.BlockSpec((D_MODEL, tn), lambda j: (0, j)),
            pl.BlockSpec((1, tn), lambda j: (0, j)),
        ],
        out_specs=pl.BlockSpec((rows, tn), lambda j: (0, j)),
        out_shape=jax.ShapeDtypeStruct((rows, n), F32),
        compiler_params=_params("parallel"),
        name="ada",
    )(c, w_ada, b_ada.reshape(1, n))


class _Mod:
    def __init__(self, mod, per_row, tiles_per_seq=1):
        self.per_row = per_row
        self.tiles_per_seq = tiles_per_seq
        self.arr = mod if per_row else mod.reshape(mod.shape[0] * 9, 1, D_MODEL)

    def spec(self, k):
        if self.per_row:
            return pl.BlockSpec((self.arr.shape[0], D_MODEL), lambda i, *_: (0, k))
        tps = self.tiles_per_seq
        return pl.BlockSpec((None, 1, D_MODEL), lambda i, *_: ((i // tps) * 9 + k, 0, 0))


def _gspec(k):
    return pl.BlockSpec((None, 1, D_MODEL), lambda i, *_: (k, 0, 0))


def _ffn_kernel(x_ref, shift_ref, scale_ref, gate_ref, gpre_ref, gpost_ref, wgu_ref, wd_ref, o_ref,
                *, n_sub, n_chunk):
    ts = x_ref.shape[0] // n_sub
    tf = D_FF // n_chunk
    for s in range(n_sub):
        rows = slice(s * ts, (s + 1) * ts)
        mod = lambda ref: ref[...] if ref.shape[0] == 1 else ref[rows, :]
        x = x_ref[rows, :]
        a = (_rms(x, gpre_ref[...]) * (1.0 + mod(scale_ref)) + mod(shift_ref)).astype(BF16)
        y = None
        for c in range(n_chunk):
            g = _dot(a, wgu_ref[:, c * tf:(c + 1) * tf])
            u = _dot(a, wgu_ref[:, D_FF + c * tf:D_FF + (c + 1) * tf])
            d = _dot((_silu(g) * u).astype(BF16), wd_ref[c * tf:(c + 1) * tf, :])
            y = d if y is None else y + d
        o_ref[rows, :] = x + FFN_HALF * mod(gate_ref) * _rms(y, gpost_ref[...])


def _ffn(x, mod, g3, w_gu, w_down, sub, tm, n_sub=1):
    m = x.shape[0]
    row = pl.BlockSpec((tm, D_MODEL), lambda i: (i, 0))
    const = lambda a: pl.BlockSpec((None,) + a.shape[1:], lambda i: (sub // 2, 0, 0), pipeline_mode=pl.Buffered(1))
    return pl.pallas_call(
        functools.partial(_ffn_kernel, n_sub=n_sub, n_chunk=FFN_CHUNKS),
        grid=(m // tm,),
        in_specs=[
            row, mod.spec(3 * sub), mod.spec(3 * sub + 1), mod.spec(3 * sub + 2),
            _gspec(2 * sub), _gspec(2 * sub + 1), const(w_gu), const(w_down),
        ],
        out_specs=row,
        out_shape=jax.ShapeDtypeStruct((m, D_MODEL), F32),
        compiler_params=_params("parallel"),
        name=f"ffn{sub}",
    )(x, mod.arr, mod.arr, mod.arr, g3, g3, w_gu, w_down)


def _rope(x, cos, sin_lo, sin_hi):
    return x * cos + pltpu.roll(x, LANES - ROT_DIM // 2, 1) * sin_lo + pltpu.roll(x, ROT_DIM // 2, 1) * sin_hi


def _proj_kernel(x_ref, shift_ref, scale_ref, g_ref, cos_ref, slo_ref, shi_ref, w_ref,
                 q_ref, qt_ref, cmp_ref, cmpt_ref, slct_ref, wint_ref, kvtb_ref, kvb_ref, gate_ref, gatet_ref):
    a = (_rms(x_ref[...], g_ref[...]) * (1.0 + scale_ref[...]) + shift_ref[...]).astype(BF16)
    z = _dot(a, w_ref[...])
    cos, slo, shi = cos_ref[...], slo_ref[...], shi_ref[...]
    scale = HEAD_DIM ** -0.5 * LOG2_E
    for s in range(Q_COLS // LANES):
        q = _rope(z[:, s * LANES:(s + 1) * LANES], cos, slo, shi) * scale
        q_ref[:, s * LANES:(s + 1) * LANES] = q.astype(BF16)
        qt_ref[s * LANES:(s + 1) * LANES, :] = q.T.astype(BF16)
    kvt_refs = (cmpt_ref, slct_ref, wint_ref)
    for br in range(3):
        c0 = Q_COLS + br * KV_BRANCH_COLS
        k = _rope(z[:, c0:c0 + LANES], cos, slo, shi)
        v = z[:, c0 + LANES:c0 + 2 * LANES]
        for r0, x in ((0, k), (LANES, v)):
            xt = x.T
            kvt_refs[br][r0:r0 + LANES, :] = xt
            kvtb_ref[br, r0:r0 + LANES, :] = xt.astype(BF16)
            if br == 0:
                for p in range(cmp_ref.shape[0]):
                    cmp_ref[p, r0:r0 + LANES, :] = xt[:, p * PAGE_SIZE:(p + 1) * PAGE_SIZE]
        kvb_ref[:, br * KV_BRANCH_COLS:br * KV_BRANCH_COLS + LANES] = k.astype(BF16)
        kvb_ref[:, br * KV_BRANCH_COLS + LANES:(br + 1) * KV_BRANCH_COLS] = v.astype(BF16)
    gate = jax.nn.sigmoid(z[:, Q_COLS + KV_COLS:])
    gate_ref[...] = gate
    gatet_ref[...] = gate.T


def _proj(x, mod, g3, tables, w_qkvg, tm):
    assert tm % PAGE_SIZE == 0
    m = x.shape[0]
    t_rows = tables[0].shape[0]
    nt = t_rows // tm
    n_seq = m // t_rows
    row = lambda n: pl.BlockSpec((tm, n), lambda i: (i, 0))
    tab = pl.BlockSpec((tm, LANES), lambda i: (i % nt, 0))
    col = lambda n: pl.BlockSpec((n, tm), lambda i: (0, i))
    kvt_spec = pl.BlockSpec((None, KV_BRANCH_COLS, tm), lambda i: (i // nt, 0, i % nt))
    kvtb_spec = pl.BlockSpec((3, None, KV_BRANCH_COLS, tm), lambda i: (0, i // nt, 0, i % nt))
    return pl.pallas_call(
        _proj_kernel,
        grid=(m // tm,),
        in_specs=[row(D_MODEL), mod.spec(3), mod.spec(4), _gspec(2), tab, tab, tab,
                  pl.BlockSpec((D_MODEL, QKVG_COLS), lambda i: (0, 0))],
        out_specs=[row(Q_COLS), col(Q_COLS),
                   pl.BlockSpec((tm // PAGE_SIZE, KV_BRANCH_COLS, PAGE_SIZE), lambda i: (i, 0, 0)),
                   kvt_spec, kvt_spec, kvt_spec, kvtb_spec, row(KV_COLS), row(GATE_PAD), col(GATE_PAD)],
        out_shape=[
            jax.ShapeDtypeStruct((m, Q_COLS), BF16),
            jax.ShapeDtypeStruct((Q_COLS, m), BF16),
            jax.ShapeDtypeStruct((m // PAGE_SIZE, KV_BRANCH_COLS, PAGE_SIZE), F32),
            *[jax.ShapeDtypeStruct((n_seq, KV_BRANCH_COLS, t_rows), F32)] * 3,
            jax.ShapeDtypeStruct((3, n_seq, KV_BRANCH_COLS, t_rows), BF16),
            jax.ShapeDtypeStruct((m, KV_COLS), BF16),
            jax.ShapeDtypeStruct((m, GATE_PAD), F32),
            jax.ShapeDtypeStruct((GATE_PAD, m), F32),
        ],
        compiler_params=_params("parallel"),
        name="proj",
    )(x, mod.arr, mod.arr, g3, *tables, w_qkvg)


def _rope_tables(pos):
    half = ROT_DIM // 2
    inv = jnp.power(ROPE_THETA, -jnp.arange(half, dtype=F32) * 2.0 / ROT_DIM)
    ang = pos.astype(F32)[:, None] * inv[None, :]
    cos, sin = jnp.cos(ang), jnp.sin(ang)
    n = pos.shape[0]
    ones = jnp.ones((n, HEAD_DIM - ROT_DIM), F32)
    zeros = jnp.zeros((n, HEAD_DIM - ROT_DIM), F32)
    zh = jnp.zeros((n, half), F32)
    c = jnp.concatenate([cos, cos, ones], axis=1)
    lo = jnp.concatenate([-sin, zh, zeros], axis=1)
    hi = jnp.concatenate([zh, sin, zeros], axis=1)
    return tuple(jnp.tile(t, (1, LANES // HEAD_DIM)) for t in (c, lo, hi))


def _compress_weights_paged(phi_pe, phi_w1, phi_b1, phi_w2):
    blocks = PAGE_SIZE // NSA_BLOCK
    assert blocks == 2
    eye = jnp.eye(blocks, dtype=F32)
    wt = phi_w1.reshape(2, NSA_BLOCK, HEAD_DIM, PHI_HIDDEN).transpose(0, 2, 1, 3).astype(BF16)
    zero = jnp.zeros_like(wt)
    w1 = jnp.stack([jnp.concatenate([wt, zero], axis=-1), jnp.concatenate([zero, wt], axis=-1)], axis=2)
    w1 = w1.reshape(2, HEAD_DIM * PAGE_SIZE, blocks * PHI_HIDDEN)
    w2 = jnp.einsum("bc,kjd->kbjcd", eye, phi_w2).reshape(2, blocks * PHI_HIDDEN, blocks * HEAD_DIM)
    pe = jnp.tile(phi_pe.transpose(0, 2, 1), (1, 1, blocks)).reshape(2, HEAD_DIM, 1, PAGE_SIZE)
    b1 = jnp.tile(phi_b1, (1, blocks)).reshape(2, 1, blocks * PHI_HIDDEN)
    return pe, w1, b1, w2.astype(BF16)


def _gather_compress_kernel(pt_ref, *refs, seqs, n_pages, group, steps, contiguous, new_tokens, pad_blocks):
    del pt_ref
    if contiguous:
        pages = [refs[0].at[k] for k in range(seqs * n_pages)]
        refs = refs[1:]
    else:
        pages, refs = refs[:seqs * n_pages], refs[seqs * n_pages:]
    if new_tokens:
        newt_ref, refs = refs[0], refs[1:]
    pe_ref, w1_ref, b1_ref, w2_ref, o_ref, xk_scr, xv_scr, c_scr = refs
    x_scr = (xk_scr, xv_scr)
    slot = pl.program_id(0) % steps
    per_seq = n_pages + (1 if new_tokens else 0)
    for r in range(seqs):
        slabs = [pages[r * n_pages + k] for k in range(n_pages)]
        if new_tokens:
            seq_lane = lax.broadcasted_iota(jnp.int32, (1, newt_ref.shape[1]), 1) == pl.program_id(0) * seqs + r
            col = jnp.sum(jnp.where(seq_lane, newt_ref[...], 0.0), axis=1, keepdims=True)
            first = lax.broadcasted_iota(jnp.int32, (1, PAGE_SIZE), 1) == 0
            slabs.append(jnp.where(first, col, 0.0))
        for k, slab in enumerate(slabs):
            for kv in range(2):
                for g in range(N_KV_HEADS):
                    page = (slot * seqs + r) * per_seq + k
                    r0 = pl.multiple_of((page * N_KV_HEADS + g) * SLAB_PITCH, SLAB_PITCH)
                    s0 = kv * GD_ROWS + g * HEAD_DIM
                    x_scr[kv][pl.ds(r0, HEAD_DIM), :] = slab[s0:s0 + HEAD_DIM, :]

    @pl.when(slot == steps - 1)
    def _():
        rows = group * N_KV_HEADS
        accs = [jnp.zeros((rows, w1_ref.shape[2]), F32) for _ in range(2)]
        for d in range(0, HEAD_DIM, 2):
            for kv in range(2):
                xs = []
                for dd in (d, d + 1):
                    x = x_scr[kv][pl.ds(dd, rows, stride=SLAB_PITCH), :]
                    xs.append((x + pe_ref[kv, dd]).astype(BF16))
                accs[kv] = accs[kv] + _dot(jnp.concatenate(xs, axis=1),
                                           w1_ref[kv, d * PAGE_SIZE:(d + 2) * PAGE_SIZE, :])
        for kv in range(2):
            h = _silu(accs[kv] + b1_ref[kv])
            c_scr[...] = _dot(h.astype(BF16), w2_ref[kv])
            g0 = c_scr[pl.ds(0, rows // 2, stride=2), :]
            g1 = c_scr[pl.ds(1, rows // 2, stride=2), :]
            low = lax.broadcasted_iota(jnp.int32, (1, LANES), 1) < HEAD_DIM
            even = jnp.where(low, g0, pltpu.roll(g1, HEAD_DIM, 1))
            odd = jnp.where(low, pltpu.roll(g0, HEAD_DIM, 1), g1)
            if pad_blocks is None:
                o_ref.at[kv][pl.ds(0, rows // 2, stride=2), :] = even
                o_ref.at[kv][pl.ds(1, rows // 2, stride=2), :] = odd
            else:
                c_scr[pl.ds(0, rows // 2, stride=2), :] = even
                c_scr[pl.ds(1, rows // 2, stride=2), :] = odd
                nblk = per_seq * (PAGE_SIZE // NSA_BLOCK)
                o_ref[kv] = jnp.zeros(o_ref.shape[1:], o_ref.dtype)
                for r in range(steps * seqs):
                    o_ref[kv, r * pad_blocks:r * pad_blocks + nblk, :] = c_scr[r * nblk:(r + 1) * nblk, :]


def _page_specs(n_pages, seqs=1):
    return [pl.BlockSpec((None, KV_BRANCH_COLS, PAGE_SIZE), lambda s, pt, r=r, k=k: (pt[s * seqs + r, k], 0, 0))
            for r in range(seqs) for k in range(n_pages)]


def _gather_compress(cache_t, page_table, cwp, pages_per_seq=None, new_tokens=None, pad_blocks=None):
    pe, w1, b1, w2 = cwp
    contiguous = page_table is None
    seqs = 4
    if contiguous:
        n_pages = pages_per_seq
        n_dec = cache_t.shape[0] // n_pages
        page_table = jnp.zeros((1, 1), jnp.int32)
        page_specs = [pl.BlockSpec((seqs * n_pages, KV_BRANCH_COLS, PAGE_SIZE), lambda s, pt: (s, 0, 0))]
        page_args = [cache_t]
    else:
        n_dec, n_pages = page_table.shape
        page_specs = _page_specs(n_pages, seqs)
        page_args = [cache_t] * (seqs * n_pages)
    const = lambda a: pl.BlockSpec(a.shape, lambda s, pt: (0,) * a.ndim, pipeline_mode=pl.Buffered(1))
    per_seq = n_pages
    if new_tokens is not None:
        per_seq += 1
        page_specs.append(const(new_tokens))
        page_args.append(new_tokens)
    steps = 2
    group = steps * seqs * per_seq
    assert n_dec % (steps * seqs) == 0
    rows = group * N_KV_HEADS
    blocks = PAGE_SIZE // NSA_BLOCK
    assert blocks == 2 and N_KV_HEADS == 2
    out_rows = rows if pad_blocks is None else steps * seqs * pad_blocks
    assert pad_blocks is None or (pad_blocks >= per_seq * blocks and pad_blocks % 8 == 0)
    out = pl.pallas_call(
        functools.partial(_gather_compress_kernel, seqs=seqs, n_pages=n_pages, group=group, steps=steps,
                          contiguous=contiguous, new_tokens=new_tokens is not None, pad_blocks=pad_blocks),
        grid_spec=pltpu.PrefetchScalarGridSpec(
            num_scalar_prefetch=1,
            grid=(n_dec // seqs,),
            in_specs=page_specs + [const(pe), const(w1), const(b1), const(w2)],
            out_specs=pl.BlockSpec((2, out_rows, blocks * HEAD_DIM), lambda s, pt: (0, s // steps, 0)),
            scratch_shapes=[pltpu.VMEM((group * N_KV_HEADS * SLAB_PITCH, PAGE_SIZE), F32)] * 2
            + [pltpu.VMEM((rows, LANES), F32)],
        ),
        out_shape=jax.ShapeDtypeStruct((2, n_dec // (steps * seqs) * out_rows, blocks * HEAD_DIM), F32),
        compiler_params=_params("arbitrary"),
        name="gather_compress",
    )(page_table, *page_args, pe, w1, b1, w2)
    return out.reshape(2, n_dec, -1, N_KV_HEADS * HEAD_DIM)


def _group_queries_t(qt, g):
    t = qt.shape[1]
    zeros = jnp.zeros((HEAD_DIM, t), qt.dtype)
    cols = []
    for j in range(HEADS_PER_GROUP):
        qh = qt[j * LANES + g * HEAD_DIM:j * LANES + (g + 1) * HEAD_DIM, :]
        cols.append(jnp.concatenate([qh, zeros] if g == 0 else [zeros, qh], axis=0))
    return jnp.concatenate(cols, axis=1)


def _tile_heads(x):
    return jnp.concatenate([x] * HEADS_PER_GROUP, axis=1)


def _select_blocks_t(imp, nblk, n_sel):
    nb = imp.shape[0]
    rank = jnp.zeros(imp.shape, F32)
    for m in range(nb):
        row = imp[m:m + 1, :]
        beats = jnp.where(row > imp, 1.0, jnp.where(row == imp, jnp.where(m < nblk, 1.0, 0.0), 0.0))
        rank = rank + beats
    return jnp.where(rank < n_sel, jnp.where(imp >= 0.0, 1.0, 0.0), 0.0)


def _select_blocks(imp, nblk, n_sel):
    nb = imp.shape[1]
    rank = jnp.zeros(imp.shape, F32)
    for m in range(nb):
        col = imp[:, m:m + 1]
        beats = jnp.where(col > imp, 1.0, jnp.where(col == imp, jnp.where(m < nblk, 1.0, 0.0), 0.0))
        rank = rank + beats
    return jnp.where(rank < n_sel, jnp.where(imp >= 0.0, 1.0, 0.0), 0.0)


def _attn_tile(carry, k, q, vt_ref, ks, tk, bias):
    m, l, acc = carry
    s = _dot(k, q)
    if bias is not None:
        s = s + bias
    m_new = jnp.maximum(m, jnp.max(s, axis=0, keepdims=True))
    alpha = jnp.exp2(m - m_new)
    pb = jnp.exp2(s - m_new).astype(BF16)
    half = pb.shape[1] // N_KV_HEADS
    ones = jnp.ones((BF16_SUBLANES, tk), BF16)
    pv = [_dot(jnp.concatenate([vt_ref[GD_ROWS + g * HEAD_DIM:GD_ROWS + (g + 1) * HEAD_DIM, pl.ds(ks, tk)], ones],
                               axis=0), pb[:, g * half:(g + 1) * half]) for g in range(N_KV_HEADS)]
    pv = jnp.concatenate(pv, axis=1)
    return m_new, alpha * l + pv[HEAD_DIM:HEAD_DIM + 1], alpha * acc + pv[:HEAD_DIM]


def _maybe(cond, fn, carry):
    return lax.fori_loop(0, jnp.where(cond, 1, 0), lambda _, c: fn(c), carry)


def _attn_kernel(qt_ref, gatet_ref, kc_ref, vct_ref, slc_k_ref, slc_vt_ref, win_k_ref, win_vt_ref, o_ref,
                 *, tq, nb):
    i = pl.program_id(1)
    p0 = i * tq
    tk = tq
    hq = HEADS_PER_GROUP * tq
    n_win_tiles = NSA_WINDOW // tk
    pos = p0 + lax.broadcasted_iota(jnp.int32, (1, tq), 1)
    nblk = lax.broadcasted_iota(jnp.int32, (nb, 1), 0)
    key_off = lax.broadcasted_iota(jnp.int32, (tk, 1), 0)
    qt = qt_ref[...]
    q_all = jnp.concatenate([_group_queries_t(qt, g) for g in range(N_KV_HEADS)], axis=1)
    tile_all = lambda x: jnp.concatenate([x] * N_HEADS, axis=1)

    cvalid = (nblk + 1) * NSA_BLOCK <= tile_all(pos) + 1
    sc = jnp.where(cvalid, _dot(kc_ref[...], q_all), MASK_INIT)
    e = jnp.exp2(sc - jnp.max(sc, axis=0, keepdims=True))
    pc = jnp.where(cvalid, e / jnp.sum(e, axis=0, keepdims=True), 0.0)
    pcb = pc.astype(BF16)
    o_cmp = jnp.concatenate([_dot(vct_ref[g * HEAD_DIM:(g + 1) * HEAD_DIM, :], pcb[:, g * hq:(g + 1) * hq])
                             for g in range(N_KV_HEADS)], axis=1)

    cur = pos // NSA_BLOCK
    forced = HEADS_PER_GROUP + 1.0
    sel_rows = []
    for g in range(N_KV_HEADS):
        imp = functools.reduce(jnp.add, [pc[:, g * hq + j * tq:g * hq + (j + 1) * tq]
                                         for j in range(HEADS_PER_GROUP)])
        imp = jnp.where(nblk == 0, forced, jnp.where(nblk == cur, forced, jnp.where(nblk == cur - 1, forced, imp)))
        imp = jnp.where(nblk * NSA_BLOCK <= pos, imp, -1.0)
        sel = _select_blocks_t(imp, nblk, NSA_TOP_N)
        sel_rows.append(_tile_heads(jnp.where(sel > 0.5, 0.0, MASK_BIAS)))
    sel_rows = jnp.concatenate(sel_rows, axis=1)
    sel_rows = jnp.concatenate([sel_rows, jnp.zeros((LANES - nb, sel_rows.shape[1]), F32)], axis=0)
    q_sel = jnp.concatenate([q_all, sel_rows.astype(BF16)], axis=0)
    blk_lane = lax.broadcasted_iota(jnp.int32, (1, LANES), 1)

    def slc_keys(ks):
        onehot = jnp.where(blk_lane == (ks + key_off) // NSA_BLOCK, 1.0, 0.0).astype(BF16)
        return jnp.concatenate([slc_k_ref[pl.ds(ks, tk), :GD_ROWS], onehot], axis=1)

    cols = q_all.shape[1]
    init = (jnp.full((1, cols), MASK_INIT, F32), jnp.zeros((1, cols), F32), jnp.zeros((HEAD_DIM, cols), F32))
    causal = tile_all(jnp.where(p0 + key_off <= pos, 0.0, MASK_BIAS))
    k_diag = pl.multiple_of(p0, tk)

    def slc_body(j, c):
        ks = pl.multiple_of(j * tk, tk)
        return _attn_tile(c, slc_keys(ks), q_sel, slc_vt_ref, ks, tk, None)

    c = lax.fori_loop(0, i, slc_body, init)
    _, l, acc = _attn_tile(c, slc_keys(k_diag), q_sel, slc_vt_ref, k_diag, tk, causal)
    o_slc = acc / l

    def win_tile(c, ks, bias):
        return _attn_tile(c, win_k_ref[pl.ds(ks, tk), :GD_ROWS], q_all, win_vt_ref, ks, tk, bias)

    def edge_tile(c):
        ks = pl.multiple_of((i - n_win_tiles) * tk, tk)
        bias = tile_all(jnp.where(pos - (ks + key_off) < NSA_WINDOW, 0.0, MASK_BIAS))
        return win_tile(c, ks, bias)

    c = _maybe(i >= n_win_tiles, edge_tile, init)
    c = lax.fori_loop(jnp.maximum(i - n_win_tiles + 1, 0), i,
                      lambda j, c: win_tile(c, pl.multiple_of(j * tk, tk), None), c)
    _, l, acc = win_tile(c, k_diag, causal)
    o_win = acc / l

    gates = gatet_ref[...]
    for g in range(N_KV_HEADS):
        for j in range(HEADS_PER_GROUP):
            head = g * HEADS_PER_GROUP + j
            cs = slice(g * hq + j * tq, g * hq + (j + 1) * tq)
            o = (gates[head:head + 1] * o_cmp[:, cs]
                 + gates[N_HEADS + head:N_HEADS + head + 1] * o_slc[:, cs]
                 + gates[2 * N_HEADS + head:2 * N_HEADS + head + 1] * o_win[:, cs])
            r0 = j * LANES + g * HEAD_DIM
            o_ref[r0:r0 + HEAD_DIM, :] = o.astype(o_ref.dtype)


def _attn_prompt(qt, gatest, kc, vct, kvb, kvtb, n_seq, seq, tq):
    m = qt.shape[1]
    nb = seq // NSA_BLOCK
    nq = seq // tq
    assert seq % tq == 0 and NSA_WINDOW % tq == 0 and tq % NSA_BLOCK == 0 and nb <= LANES
    col = lambda n: pl.BlockSpec((n, tq), lambda b, i: (0, b * nq + i))
    return pl.pallas_call(
        functools.partial(_attn_kernel, tq=tq, nb=nb),
        grid=(n_seq, nq),
        in_specs=[col(Q_COLS), col(4 * N_HEADS),
                  pl.BlockSpec((nb, LANES), lambda b, i: (b, 0)),
                  pl.BlockSpec((None, LANES, nb), lambda b, i: (b, 0, 0)),
                  pl.BlockSpec((seq, KV_BRANCH_COLS), lambda b, i: (b, 1)),
                  pl.BlockSpec((None, None, KV_BRANCH_COLS, seq), lambda b, i: (1, b, 0, 0)),
                  pl.BlockSpec((seq, KV_BRANCH_COLS), lambda b, i: (b, 2)),
                  pl.BlockSpec((None, None, KV_BRANCH_COLS, seq), lambda b, i: (2, b, 0, 0))],
        out_specs=col(Q_COLS),
        out_shape=jax.ShapeDtypeStruct((Q_COLS, m), BF16),
        compiler_params=_params("parallel", "arbitrary"),
        name="attn_prompt",
    )(qt, gatest, kc, vct, kvb, kvtb, kvb, kvtb)


def _softmax_pv(score_tiles, value_tiles):
    m = functools.reduce(jnp.maximum, [jnp.max(s, axis=-1, keepdims=True) for s in score_tiles])
    ps = [jnp.exp2(s - m) for s in score_tiles]
    l = functools.reduce(jnp.add, [jnp.sum(p, axis=-1, keepdims=True) for p in ps])
    o = functools.reduce(jnp.add, [(_dot_nt if fm else _dot)(p.astype(BF16), v)
                                   for p, (v, fm) in zip(ps, value_tiles)])
    return o / l


def _attn_decode_kernel(pt_ref, *refs, n_pages, seqs, pos, nbp):
    del pt_ref
    pages = refs[:seqs * n_pages]
    q_ref, kvb_ref, gate_ref, kvc_ref, win_ref, wnew_ref, o_ref, wnext_ref = refs[seqs * n_pages:]
    for r in range(seqs):
        _attn_decode_one(pages[r * n_pages:(r + 1) * n_pages], q_ref.at[r], kvb_ref.at[r], gate_ref.at[r],
                         kvc_ref.at[:, r], win_ref.at[r], wnew_ref.at[r], o_ref.at[r], wnext_ref.at[r],
                         pos=pos, nbp=nbp)


def _attn_decode_one(pages, q_ref, kvb_ref, gate_ref, kvc_ref, win_ref, wnew_ref, o_ref, wnext_ref, *, pos, nbp):
    n_pages = len(pages)
    tk = PAGE_SIZE
    lane = lax.broadcasted_iota(jnp.int32, (1, LANES), 1)
    head = lax.broadcasted_iota(jnp.int32, (N_HEADS, 1), 0)
    low_group = head < HEADS_PER_GROUP
    low_lanes = lane < HEAD_DIM

    q32 = q_ref[...].astype(F32)
    qm = jnp.zeros((N_HEADS, LANES), F32)
    for j in range(HEADS_PER_GROUP):
        qm = jnp.where(head % HEADS_PER_GROUP == j, q32[:, j * LANES:(j + 1) * LANES], qm)
    qm = jnp.where(low_group, jnp.where(low_lanes, qm, 0.0), jnp.where(low_lanes, 0.0, qm)).astype(BF16)

    gate_row = gate_ref[...]
    gate_col = [jnp.sum(jnp.where(lane == head + br * N_HEADS, gate_row, 0.0), axis=-1, keepdims=True)
                for br in range(3)]
    kv_new = kvb_ref[...].astype(F32)
    first_row = lax.broadcasted_iota(jnp.int32, (tk, 1), 0) == 0

    def new_key_tile(c0):
        return jnp.where(first_row, kv_new[:, c0:c0 + LANES], 0.0).astype(BF16)

    nblk = lax.broadcasted_iota(jnp.int32, (1, nbp), 1)
    cvalid = (nblk + 1) * NSA_BLOCK <= pos + 1
    sc = jnp.where(cvalid, _dot_nt(qm, kvc_ref[0].astype(BF16)), MASK_INIT)
    e = jnp.exp2(sc - jnp.max(sc, axis=-1, keepdims=True))
    pc = jnp.where(cvalid, e / jnp.sum(e, axis=-1, keepdims=True), 0.0)
    o_cmp = _dot(pc.astype(BF16), kvc_ref[1].astype(BF16))

    imp = jnp.where(low_group,
                    jnp.sum(jnp.where(low_group, pc, 0.0), axis=0, keepdims=True),
                    jnp.sum(jnp.where(low_group, 0.0, pc), axis=0, keepdims=True))
    cur = pos // NSA_BLOCK
    forced = HEADS_PER_GROUP + 1.0
    imp = jnp.where(nblk == 0, forced, jnp.where(nblk == cur, forced, jnp.where(nblk == cur - 1, forced, imp)))
    imp = jnp.where(nblk * NSA_BLOCK <= pos, imp, -1.0)
    sel = _select_blocks(imp, nblk, NSA_TOP_N)

    def sel_bias(n):
        return jnp.where(sel[:, n:n + 1] > 0.5, 0.0, MASK_BIAS)

    s_tiles, v_tiles = [], []
    blocks_per_page = tk // NSA_BLOCK
    for k0 in range(0, n_pages, DECODE_PAGES_PER_DOT):
        ks = range(k0, min(k0 + DECODE_PAGES_PER_DOT, n_pages))
        biases = []
        for k in ks:
            bias = sel_bias(k * blocks_per_page + blocks_per_page - 1)
            for b in range(blocks_per_page - 2, -1, -1):
                bias = jnp.where(lane < (b + 1) * NSA_BLOCK, sel_bias(k * blocks_per_page + b), bias)
            biases.append(bias)
        keys = jnp.concatenate([pages[k][:GD_ROWS, :].astype(BF16) for k in ks], axis=1)
        s_tiles.append(_dot(qm, keys) + jnp.concatenate(biases, axis=1))
        v_tiles.append((jnp.concatenate([pages[k][GD_ROWS:, :].astype(BF16) for k in ks], axis=1), True))
    s_new = _dot_nt(qm, new_key_tile(KV_BRANCH_COLS))
    s_tiles.append(s_new + jnp.where(lane == 0, sel_bias(n_pages * blocks_per_page), MASK_BIAS))
    v_tiles.append((new_key_tile(KV_BRANCH_COLS + LANES), False))
    o_slc = _softmax_pv(s_tiles, v_tiles)

    n_win = win_ref.shape[1]
    diff = n_win - lax.broadcasted_iota(jnp.int32, (1, n_win), 1)
    s_tiles = [_dot(qm, win_ref[:GD_ROWS, :].astype(BF16)) + jnp.where(diff < NSA_WINDOW, 0.0, MASK_BIAS)]
    v_tiles = [(win_ref[GD_ROWS:, :].astype(BF16), True)]
    s_tiles.append(_dot_nt(qm, new_key_tile(2 * KV_BRANCH_COLS)) + jnp.where(lane == 0, 0.0, MASK_BIAS))
    v_tiles.append((new_key_tile(2 * KV_BRANCH_COLS + LANES), False))
    o_win = _softmax_pv(s_tiles, v_tiles)

    o = gate_col[0] * o_cmp + gate_col[1] * o_slc + gate_col[2] * o_win
    for j in range(HEADS_PER_GROUP):
        pair = jnp.where(low_lanes, o[j:j + 1], o[j + HEADS_PER_GROUP:j + HEADS_PER_GROUP + 1])
        o_ref[:, j * LANES:(j + 1) * LANES] = pair.astype(o_ref.dtype)

    last = lax.broadcasted_iota(jnp.int32, (1, n_win), 1) == n_win - 1
    wnext_ref[...] = jnp.where(last, wnew_ref[...], pltpu.roll(win_ref[...], n_win - 1, 1))


def _attn_decode(q, kvb, gates, kvc, cache_slc, win_state, win_new, page_table, pos):
    n_dec, n_pages = page_table.shape
    nbp = kvc.shape[2]
    n_win = win_state.shape[2]
    seqs = DECODE_SEQS_PER_STEP
    per_seq = lambda a: pl.BlockSpec((seqs, 1, a.shape[-1]), lambda s, pt: (s, 0, 0))
    win_spec = pl.BlockSpec((seqs, KV_BRANCH_COLS, n_win), lambda s, pt: (s, 0, 0))
    q3, kvb3, g3 = (a.reshape(n_dec, 1, a.shape[-1]) for a in (q, kvb, gates))
    out, win_next = pl.pallas_call(
        functools.partial(_attn_decode_kernel, n_pages=n_pages, seqs=seqs, pos=pos, nbp=nbp),
        grid_spec=pltpu.PrefetchScalarGridSpec(
            num_scalar_prefetch=1,
            grid=(n_dec // seqs,),
            in_specs=_page_specs(n_pages, seqs) + [
                per_seq(q3), per_seq(kvb3), per_seq(g3),
                pl.BlockSpec((2, seqs, nbp, LANES), lambda s, pt: (0, s, 0, 0)),
                win_spec, pl.BlockSpec((seqs, KV_BRANCH_COLS, 1), lambda s, pt: (s, 0, 0)),
            ],
            out_specs=[pl.BlockSpec((seqs, 1, Q_COLS), lambda s, pt: (s, 0, 0)), win_spec],
        ),
        out_shape=[jax.ShapeDtypeStruct((n_dec, 1, Q_COLS), BF16), jax.ShapeDtypeStruct(win_state.shape, F32)],
        compiler_params=_params("arbitrary"),
        name="attn_decode",
    )(page_table, *([cache_slc] * (seqs * n_pages)), q3, kvb3, g3, kvc, win_state, win_new)
    return out.reshape(n_dec, Q_COLS), win_next


def _mix_kernel(*refs, tm, halo, tiles_per_seq):
    if halo:
        (x_ref, xh_ref, o_ref, shift_ref, scale_ref, gate_ref, gpre_ref, gpost_ref, cw_ref,
         wconv_ref, wmerge_ref, wup_ref, wco_ref, wout_ref, y_ref, ulast_ref, u_scr) = refs
        x = x_ref[...]
        xe = jnp.concatenate([xh_ref[...], x], axis=0)
    else:
        (x_ref, um1_ref, um2_ref, o_ref, shift_ref, scale_ref, gate_ref, gpre_ref, gpost_ref, cw_ref,
         wconv_ref, wmerge_ref, wup_ref, wco_ref, wout_ref, y_ref, ulast_ref) = refs
        x = x_ref[...]
        xe = x
    h0 = xe.shape[0] - tm
    a = (_rms(xe, gpre_ref[...]) * (1.0 + scale_ref[...]) + shift_ref[...]).astype(BF16)
    zc = _dot(a, wconv_ref[...])
    u = zc[:, 2 * D_CONV:] * zc[:, :D_CONV]
    cb = zc[h0:, D_CONV:2 * D_CONV]
    if halo:
        keep_halo = jnp.where(pl.program_id(0) % tiles_per_seq == 0, 0.0, 1.0)
        rows = lax.broadcasted_iota(jnp.int32, (xe.shape[0], 1), 0)
        u = jnp.where(rows < h0, u * keep_halo, u)
        u_scr[...] = u
        um1 = u_scr[h0 - 1:h0 - 1 + tm, :]
        um2 = u_scr[h0 - 2:h0 - 2 + tm, :]
        u0 = u[h0:]
        ulast_ref[...] = u[tm:]
    else:
        um1, um2, u0 = um1_ref[...], um2_ref[...], u
        ulast_ref[...] = u
    cw = cw_ref[...]
    y = cw[0:1] * um2 + cw[1:2] * um1 + cw[2:3] * u0
    conv_out = _dot((cb * y).astype(BF16), wco_ref[...])
    attn_out = _dot_tn(o_ref[...], wup_ref[...])
    mg = jax.nn.sigmoid(_dot(a[h0:], wmerge_ref[...]))
    merged = mg[:, :D_MODEL] * attn_out + mg[:, D_MODEL:] * conv_out
    mixed = _dot(merged.astype(BF16), wout_ref[...])
    y_ref[...] = x + gate_ref[...] * _rms(mixed, gpost_ref[...])


def _mix(x, o_att, mod, g3, conv_w, weights, tm, tiles_per_seq=None, prev=None):
    m = x.shape[0]
    halo = prev is None
    h0 = MIX_HALO
    row = lambda n: pl.BlockSpec((tm, n), lambda i: (i, 0))
    full = lambda a: pl.BlockSpec(a.shape, lambda i: (0,) * a.ndim, pipeline_mode=pl.Buffered(1))
    att = pl.BlockSpec((Q_COLS, tm), lambda i: (0, i))
    common = [att, mod.spec(3), mod.spec(4), mod.spec(5), _gspec(2), _gspec(3), full(conv_w)]
    common += [full(w) for w in weights]
    common_args = [o_att, mod.arr, mod.arr, mod.arr, g3, g3, conv_w, *weights]
    if halo:
        hb = tm // h0
        in_specs = [row(D_MODEL), pl.BlockSpec((h0, D_MODEL), lambda i: (jnp.maximum(i * hb - 1, 0), 0))] + common
        args = [x, x] + common_args
        scratch = [pltpu.VMEM((tm + h0, D_CONV), F32)]
        ulast = (jax.ShapeDtypeStruct((m // tm * h0, D_CONV), F32), pl.BlockSpec((h0, D_CONV), lambda i: (i, 0)))
    else:
        in_specs = [row(D_MODEL), row(D_CONV), row(D_CONV)] + common
        args = [x, prev[0], prev[1]] + common_args
        scratch = []
        ulast = (jax.ShapeDtypeStruct((m, D_CONV), F32), row(D_CONV))
    return pl.pallas_call(
        functools.partial(_mix_kernel, tm=tm, halo=halo, tiles_per_seq=tiles_per_seq),
        grid=(m // tm,),
        in_specs=in_specs,
        out_specs=[row(D_MODEL), ulast[1]],
        out_shape=[jax.ShapeDtypeStruct((m, D_MODEL), F32), ulast[0]],
        scratch_shapes=scratch,
        compiler_params=_params("parallel"),
        name="mix",
    )(*args)


def _head_pair_perm():
    order = []
    for j in range(HEADS_PER_GROUP):
        for g in range(N_KV_HEADS):
            head = g * HEADS_PER_GROUP + j
            order.extend(range(head * HEAD_DIM, (head + 1) * HEAD_DIM))
    return jnp.array(order, dtype=jnp.int32)


def _layer_weights(w_in, w_attn_up, w_conv_out, w_out):
    perm = _head_pair_perm()
    o1 = Q_COLS
    o2 = o1 + KV_COLS
    o3 = o2 + NSA_GATE_COLS
    o4 = o3 + CONV_COLS
    w_q = w_in[:, :o1][:, perm]
    w_g = jnp.pad(w_in[:, o2:o3], ((0, 0), (0, GATE_PAD - NSA_GATE_COLS)))
    w_qkvg = jnp.concatenate([w_q, w_in[:, o1:o2], w_g], axis=1).astype(BF16)
    mix_w = (w_in[:, o3:o4].astype(BF16), w_in[:, o4:].astype(BF16), w_attn_up[perm].astype(BF16),
             w_conv_out.astype(BF16), w_out.astype(BF16))
    return w_qkvg, mix_w


def kernel(x_prompt, x_sample, c_prompt, c_sample, cache_cmp_kv, cache_slc_kv, state_win_kv, state_conv,
           page_table, w_ada, b_ada, g_norm, w_ffn_gu, w_ffn_down, w_in, phi_pe, phi_w1, phi_b1, phi_w2,
           w_attn_up, conv_w, w_conv_out, w_out):
    n_seq, seq, _ = x_prompt.shape
    n_dec = x_sample.shape[0]
    assert w_ada.shape[0] == 1
    l = 0

    mod_all = _ada(jnp.concatenate([c_prompt, c_sample], axis=0), w_ada[l], b_ada[l])
    g3 = g_norm[l].reshape(6, 1, D_MODEL)
    w_gu = w_ffn_gu[l].astype(BF16)
    w_down = w_ffn_down[l].astype(BF16)
    w_qkvg, mix_w = _layer_weights(w_in[l], w_attn_up[l], w_conv_out[l], w_out[l])
    cwp = _compress_weights_paged(phi_pe[l], phi_w1[l], phi_b1[l], phi_w2[l])

    tm = 512
    tq = 256
    m = n_seq * seq
    mod_p = _Mod(mod_all[:n_seq], per_row=False, tiles_per_seq=seq // tm)
    tm_ffn, ffn_sub = 1024, 2
    mod_pf = _Mod(mod_all[:n_seq], per_row=False, tiles_per_seq=seq // tm_ffn)
    x0 = x_prompt.reshape(m, D_MODEL)
    x1 = _ffn(x0, mod_pf, g3, w_gu, w_down, 0, tm_ffn, ffn_sub)
    tables = _rope_tables(jnp.arange(seq, dtype=jnp.int32))
    _, qt, cmp_pages, cmpt, slct, wint, kvtb, kvb, _, gatest = _proj(x1, mod_p, g3, tables, w_qkvg, tm)
    nb = seq // NSA_BLOCK
    kvc = _gather_compress(cmp_pages, None, cwp, pages_per_seq=seq // PAGE_SIZE)
    kc = kvc[0].reshape(n_seq * nb, LANES).astype(BF16)
    vct = kvc[1].transpose(0, 2, 1).astype(BF16)
    o_att = _attn_prompt(qt, gatest, kc, vct, kvb, kvtb, n_seq, seq, tq)
    x2, ulast = _mix(x1, o_att, mod_pf, g3, conv_w[l], mix_w, tm_ffn, tiles_per_seq=seq // tm_ffn)
    y_prompt = _ffn(x2, mod_pf, g3, w_gu, w_down, 2, tm_ffn, ffn_sub).reshape(n_seq, seq, D_MODEL)

    def token_major(t):
        n, _, tt = t.shape
        return t.reshape(n, 2, N_KV_HEADS, HEAD_DIM, tt).transpose(0, 4, 1, 2, 3)[None]

    def feature_major(a):
        n, tt = a.shape[:2]
        return a.transpose(0, 2, 3, 4, 1).reshape(n, KV_BRANCH_COLS, tt)

    cmp_kv_prompt = token_major(cmpt)
    slc_kv_prompt = token_major(slct)
    win_keep = min(NSA_WINDOW, seq)
    win_kv_prompt = token_major(wint[:, :, seq - win_keep:])
    conv_prompt = ulast.reshape(n_seq, seq // tm_ffn, MIX_HALO, D_CONV)[:, -1, MIX_HALO - (CONV_WIDTH - 1):][None]

    n_pages = page_table.shape[1]
    past_len = n_pages * PAGE_SIZE
    pos_s = past_len + jnp.arange(x_sample.shape[1], dtype=jnp.int32)
    assert x_sample.shape[1] == 1 and past_len % NSA_BLOCK == 0 and state_win_kv.shape[2] <= past_len
    mod_s = _Mod(mod_all[n_seq:], per_row=True)
    xs1 = _ffn(x_sample.reshape(n_dec, D_MODEL), mod_s, g3, w_gu, w_down, 0, n_dec)
    tables_s = _rope_tables(jnp.broadcast_to(pos_s, (n_dec,)))
    q_s, _, _, cmpt_s, slct_s, wint_s, _, kvb_s, gates_s, _ = _proj(xs1, mod_s, g3, tables_s, w_qkvg, n_dec)
    nb_past = past_len // NSA_BLOCK
    nb_pad = -(-(nb_past + PAGE_SIZE // NSA_BLOCK) // BF16_SUBLANES) * BF16_SUBLANES
    kvc_s = _gather_compress(feature_major(cache_cmp_kv[l]), page_table, cwp, new_tokens=cmpt_s[0],
                             pad_blocks=nb_pad)
    win_state = feature_major(state_win_kv[l])
    assert win_state.shape[2] == NSA_WINDOW
    o_att_s, win_next = _attn_decode(q_s, kvb_s, gates_s, kvc_s, feature_major(cache_slc_kv[l]), win_state,
                                     wint_s[0].T[:, :, None], page_table, past_len)
    conv_state = state_conv[l]
    xs2, u_s = _mix(xs1, o_att_s.T, mod_s, g3, conv_w[l], mix_w, n_dec,
                    prev=(conv_state[:, CONV_WIDTH - 2], conv_state[:, CONV_WIDTH - 3]))
    y_sample = _ffn(xs2, mod_s, g3, w_gu, w_down, 2, n_dec).reshape(x_sample.shape)

    kvs = (1, n_dec, 1, 2, N_KV_HEADS, HEAD_DIM)
    conv_sample = jnp.concatenate([conv_state[:, 1:], u_s[:, None, :]], axis=1)[None]
    return (y_prompt, y_sample, cmp_kv_prompt, slc_kv_prompt, win_kv_prompt, conv_prompt,
            token_major(cmpt_s).reshape(kvs), token_major(slct_s).reshape(kvs), token_major(win_next), conv_sample)
```

```python
import functools

import jax
import jax.numpy as jnp
from jax import lax
from jax.experimental import pallas as pl
from jax.experimental.pallas import tpu as pltpu

D_MODEL = 1024
N_HEADS = 8
HEAD_DIM = 64
N_KV_HEADS = 2
HEADS_PER_GROUP = N_HEADS // N_KV_HEADS
ROT_DIM = HEAD_DIM // 4
ROPE_THETA = 500000.0
NSA_BLOCK = 64
NSA_TOP_N = 16
NSA_WINDOW = 512
PHI_HIDDEN = 2 * HEAD_DIM
D_CONV = 512
CONV_WIDTH = 3
D_FF = 2816
FFN_HALF = 0.5
NORM_EPS = 1e-6
PAGE_SIZE = 128
Q_COLS = N_HEADS * HEAD_DIM
KV_BRANCH_COLS = 2 * N_KV_HEADS * HEAD_DIM
KV_COLS = 3 * KV_BRANCH_COLS
NSA_GATE_COLS = 3 * N_HEADS
CONV_COLS = 3 * D_CONV
MERGE_COLS = 2 * D_MODEL
GD_ROWS = N_KV_HEADS * HEAD_DIM

LANES = 128
BF16_SUBLANES = 16
GATE_PAD = LANES
QKVG_COLS = Q_COLS + KV_COLS + GATE_PAD
LOG2_E = 1.4426950408889634
MASK_INIT = -1e30
MASK_BIAS = -2e30
VMEM_LIMIT = 56 * 1024 * 1024
SLAB_PITCH = HEAD_DIM + 8
MXU_DIM = 256
FFN_CHUNKS = D_FF // MXU_DIM
DECODE_SEQS_PER_STEP = 4
DECODE_PAGES_PER_DOT = 4
MIX_HALO = 16

BF16 = jnp.bfloat16
F32 = jnp.float32


def _dot(a, b):
    return jnp.dot(a, b, preferred_element_type=F32)


def _dot_tn(a, b):
    return lax.dot_general(a, b, (((0,), (0,)), ((), ())), preferred_element_type=F32)


def _dot_nt(a, b):
    return lax.dot_general(a, b, (((1,), (1,)), ((), ())), preferred_element_type=F32)


def _rms(x, g):
    return x * lax.rsqrt(jnp.mean(x * x, axis=-1, keepdims=True) + NORM_EPS) * g


def _silu(x):
    return x * jax.nn.sigmoid(x)


def _params(*sem):
    return pltpu.CompilerParams(dimension_semantics=sem, vmem_limit_bytes=VMEM_LIMIT)


def _ada_kernel(c_ref, w_ref, b_ref, o_ref):
    c = _silu(c_ref[...]).astype(BF16)
    o_ref[...] = _dot(c, w_ref[...].astype(BF16)) + b_ref[...]


def _ada(c, w_ada, b_ada):
    rows = c.shape[0]
    n = w_ada.shape[1]
    tn = 9 * LANES
    return pl.pallas_call(
        _ada_kernel,
        grid=(n // tn,),
        in_specs=[
            pl.BlockSpec((rows, D_MODEL), lambda j: (0, 0)),
            pl.BlockSpec((D_MODEL, tn), lambda j: (0, j)),
            pl.BlockSpec((1, tn), lambda j: (0, j)),
        ],
        out_specs=pl.BlockSpec((rows, tn), lambda j: (0, j)),
        out_shape=jax.ShapeDtypeStruct((rows, n), F32),
        compiler_params=_params("parallel"),
        name="ada",
    )(c, w_ada, b_ada.reshape(1, n))


class _Mod:
    def __init__(self, mod, per_row, tiles_per_seq=1):
        self.per_row = per_row
        self.tiles_per_seq = tiles_per_seq
        self.arr = mod if per_row else mod.reshape(mod.shape[0] * 9, 1, D_MODEL)

    def spec(self, k):
        if self.per_row:
            return pl.BlockSpec((self.arr.shape[0], D_MODEL), lambda i, *_: (0, k))
        tps = self.tiles_per_seq
        return pl.BlockSpec((None, 1, D_MODEL), lambda i, *_: ((i // tps) * 9 + k, 0, 0))


def _gspec(k):
    return pl.BlockSpec((None, 1, D_MODEL), lambda i, *_: (k, 0, 0))


def _ffn_kernel(x_ref, shift_ref, scale_ref, gate_ref, gpre_ref, gpost_ref, wgu_ref, wd_ref, o_ref,
                *, n_sub, n_chunk):
    ts = x_ref.shape[0] // n_sub
    tf = D_FF // n_chunk
    for s in range(n_sub):
        rows = slice(s * ts, (s + 1) * ts)
        mod = lambda ref: ref[...] if ref.shape[0] == 1 else ref[rows, :]
        x = x_ref[rows, :]
        a = (_rms(x, gpre_ref[...]) * (1.0 + mod(scale_ref)) + mod(shift_ref)).astype(BF16)
        y = None
        for c in range(n_chunk):
            g = _dot(a, wgu_ref[:, c * tf:(c + 1) * tf])
            u = _dot(a, wgu_ref[:, D_FF + c * tf:D_FF + (c + 1) * tf])
            d = _dot((_silu(g) * u).astype(BF16), wd_ref[c * tf:(c + 1) * tf, :])
            y = d if y is None else y + d
        o_ref[rows, :] = x + FFN_HALF * mod(gate_ref) * _rms(y, gpost_ref[...])


def _ffn(x, mod, g3, w_gu, w_down, sub, tm, n_sub=1):
    m = x.shape[0]
    row = pl.BlockSpec((tm, D_MODEL), lambda i: (i, 0))
    const = lambda a: pl.BlockSpec((None,) + a.shape[1:], lambda i: (sub // 2, 0, 0), pipeline_mode=pl.Buffered(1))
    return pl.pallas_call(
        functools.partial(_ffn_kernel, n_sub=n_sub, n_chunk=FFN_CHUNKS),
        grid=(m // tm,),
        in_specs=[
            row, mod.spec(3 * sub), mod.spec(3 * sub + 1), mod.spec(3 * sub + 2),
            _gspec(2 * sub), _gspec(2 * sub + 1), const(w_gu), const(w_down),
        ],
        out_specs=row,
        out_shape=jax.ShapeDtypeStruct((m, D_MODEL), F32),
        compiler_params=_params("parallel"),
        name=f"ffn{sub}",
    )(x, mod.arr, mod.arr, mod.arr, g3, g3, w_gu, w_down)


def _rope(x, cos, sin_lo, sin_hi):
    return x * cos + pltpu.roll(x, LANES - ROT_DIM // 2, 1) * sin_lo + pltpu.roll(x, ROT_DIM // 2, 1) * sin_hi


def _proj_kernel(x_ref, shift_ref, scale_ref, g_ref, cos_ref, slo_ref, shi_ref, w_ref,
                 q_ref, qt_ref, cmp_ref, cmpt_ref, slct_ref, wint_ref, kvtb_ref, kvb_ref, gate_ref, gatet_ref):
    a = (_rms(x_ref[...], g_ref[...]) * (1.0 + scale_ref[...]) + shift_ref[...]).astype(BF16)
    z = _dot(a, w_ref[...])
    cos, slo, shi = cos_ref[...], slo_ref[...], shi_ref[...]
    scale = HEAD_DIM ** -0.5 * LOG2_E
    for s in range(Q_COLS // LANES):
        q = _rope(z[:, s * LANES:(s + 1) * LANES], cos, slo, shi) * scale
        q_ref[:, s * LANES:(s + 1) * LANES] = q.astype(BF16)
        qt_ref[s * LANES:(s + 1) * LANES, :] = q.T.astype(BF16)
    kvt_refs = (cmpt_ref, slct_ref, wint_ref)
    for br in range(3):
        c0 = Q_COLS + br * KV_BRANCH_COLS
        k = _rope(z[:, c0:c0 + LANES], cos, slo, shi)
        v = z[:, c0 + LANES:c0 + 2 * LANES]
        for r0, x in ((0, k), (LANES, v)):
            xt = x.T
            kvt_refs[br][r0:r0 + LANES, :] = xt
            kvtb_ref[br, r0:r0 + LANES, :] = xt.astype(BF16)
            if br == 0:
                for p in range(cmp_ref.shape[0]):
                    cmp_ref[p, r0:r0 + LANES, :] = xt[:, p * PAGE_SIZE:(p + 1) * PAGE_SIZE]
        kvb_ref[:, br * KV_BRANCH_COLS:br * KV_BRANCH_COLS + LANES] = k.astype(BF16)
        kvb_ref[:, br * KV_BRANCH_COLS + LANES:(br + 1) * KV_BRANCH_COLS] = v.astype(BF16)
    gate = jax.nn.sigmoid(z[:, Q_COLS + KV_COLS:])
    gate_ref[...] = gate
    gatet_ref[...] = gate.T


def _proj(x, mod, g3, tables, w_qkvg, tm):
    assert tm % PAGE_SIZE == 0
    m = x.shape[0]
    t_rows = tables[0].shape[0]
    nt = t_rows // tm
    n_seq = m // t_rows
    row = lambda n: pl.BlockSpec((tm, n), lambda i: (i, 0))
    tab = pl.BlockSpec((tm, LANES), lambda i: (i % nt, 0))
    col = lambda n: pl.BlockSpec((n, tm), lambda i: (0, i))
    kvt_spec = pl.BlockSpec((None, KV_BRANCH_COLS, tm), lambda i: (i // nt, 0, i % nt))
    kvtb_spec = pl.BlockSpec((3, None, KV_BRANCH_COLS, tm), lambda i: (0, i // nt, 0, i % nt))
    return pl.pallas_call(
        _proj_kernel,
        grid=(m // tm,),
        in_specs=[row(D_MODEL), mod.spec(3), mod.spec(4), _gspec(2), tab, tab, tab,
                  pl.BlockSpec((D_MODEL, QKVG_COLS), lambda i: (0, 0))],
        out_specs=[row(Q_COLS), col(Q_COLS),
                   pl.BlockSpec((tm // PAGE_SIZE, KV_BRANCH_COLS, PAGE_SIZE), lambda i: (i, 0, 0)),
                   kvt_spec, kvt_spec, kvt_spec, kvtb_spec, row(KV_COLS), row(GATE_PAD), col(GATE_PAD)],
        out_shape=[
            jax.ShapeDtypeStruct((m, Q_COLS), BF16),
            jax.ShapeDtypeStruct((Q_COLS, m), BF16),
            jax.ShapeDtypeStruct((m // PAGE_SIZE, KV_BRANCH_COLS, PAGE_SIZE), F32),
            *[jax.ShapeDtypeStruct((n_seq, KV_BRANCH_COLS, t_rows), F32)] * 3,
            jax.ShapeDtypeStruct((3, n_seq, KV_BRANCH_COLS, t_rows), BF16),
            jax.ShapeDtypeStruct((m, KV_COLS), BF16),
            jax.ShapeDtypeStruct((m, GATE_PAD), F32),
            jax.ShapeDtypeStruct((GATE_PAD, m), F32),
        ],
        compiler_params=_params("parallel"),
        name="proj",
    )(x, mod.arr, mod.arr, g3, *tables, w_qkvg)


def _rope_tables(pos):
    half = ROT_DIM // 2
    inv = jnp.power(ROPE_THETA, -jnp.arange(half, dtype=F32) * 2.0 / ROT_DIM)
    ang = pos.astype(F32)[:, None] * inv[None, :]
    cos, sin = jnp.cos(ang), jnp.sin(ang)
    n = pos.shape[0]
    ones = jnp.ones((n, HEAD_DIM - ROT_DIM), F32)
    zeros = jnp.zeros((n, HEAD_DIM - ROT_DIM), F32)
    zh = jnp.zeros((n, half), F32)
    c = jnp.concatenate([cos, cos, ones], axis=1)
    lo = jnp.concatenate([-sin, zh, zeros], axis=1)
    hi = jnp.concatenate([zh, sin, zeros], axis=1)
    return tuple(jnp.tile(t, (1, LANES // HEAD_DIM)) for t in (c, lo, hi))


def _compress_weights_paged(phi_pe, phi_w1, phi_b1, phi_w2):
    blocks = PAGE_SIZE // NSA_BLOCK
    assert blocks == 2
    eye = jnp.eye(blocks, dtype=F32)
    wt = phi_w1.reshape(2, NSA_BLOCK, HEAD_DIM, PHI_HIDDEN).transpose(0, 2, 1, 3).astype(BF16)
    zero = jnp.zeros_like(wt)
    w1 = jnp.stack([jnp.concatenate([wt, zero], axis=-1), jnp.concatenate([zero, wt], axis=-1)], axis=2)
    w1 = w1.reshape(2, HEAD_DIM * PAGE_SIZE, blocks * PHI_HIDDEN)
    w2 = jnp.einsum("bc,kjd->kbjcd", eye, phi_w2).reshape(2, blocks * PHI_HIDDEN, blocks * HEAD_DIM)
    pe = jnp.tile(phi_pe.transpose(0, 2, 1), (1, 1, blocks)).reshape(2, HEAD_DIM, 1, PAGE_SIZE)
    b1 = jnp.tile(phi_b1, (1, blocks)).reshape(2, 1, blocks * PHI_HIDDEN)
    return pe, w1, b1, w2.astype(BF16)


def _gather_compress_kernel(pt_ref, *refs, seqs, n_pages, group, steps, contiguous, new_tokens, pad_blocks):
    del pt_ref
    if contiguous:
        pages = [refs[0].at[k] for k in range(seqs * n_pages)]
        refs = refs[1:]
    else:
        pages, refs = refs[:seqs * n_pages], refs[seqs * n_pages:]
    if new_tokens:
        newt_ref, refs = refs[0], refs[1:]
    pe_ref, w1_ref, b1_ref, w2_ref, o_ref, xk_scr, xv_scr, c_scr = refs
    x_scr = (xk_scr, xv_scr)
    slot = pl.program_id(0) % steps
    per_seq = n_pages + (1 if new_tokens else 0)
    for r in range(seqs):
        slabs = [pages[r * n_pages + k] for k in range(n_pages)]
        if new_tokens:
            seq_lane = lax.broadcasted_iota(jnp.int32, (1, newt_ref.shape[1]), 1) == pl.program_id(0) * seqs + r
            col = jnp.sum(jnp.where(seq_lane, newt_ref[...], 0.0), axis=1, keepdims=True)
            first = lax.broadcasted_iota(jnp.int32, (1, PAGE_SIZE), 1) == 0
            slabs.append(jnp.where(first, col, 0.0))
        for k, slab in enumerate(slabs):
            for kv in range(2):
                for g in range(N_KV_HEADS):
                    page = (slot * seqs + r) * per_seq + k
                    r0 = pl.multiple_of((page * N_KV_HEADS + g) * SLAB_PITCH, SLAB_PITCH)
                    s0 = kv * GD_ROWS + g * HEAD_DIM
                    x_scr[kv][pl.ds(r0, HEAD_DIM), :] = slab[s0:s0 + HEAD_DIM, :]

    @pl.when(slot == steps - 1)
    def _():
        rows = group * N_KV_HEADS
        accs = [jnp.zeros((rows, w1_ref.shape[2]), F32) for _ in range(2)]
        for d in range(0, HEAD_DIM, 2):
            for kv in range(2):
                xs = []
                for dd in (d, d + 1):
                    x = x_scr[kv][pl.ds(dd, rows, stride=SLAB_PITCH), :]
                    xs.append((x + pe_ref[kv, dd]).astype(BF16))
                accs[kv] = accs[kv] + _dot(jnp.concatenate(xs, axis=1),
                                           w1_ref[kv, d * PAGE_SIZE:(d + 2) * PAGE_SIZE, :])
        for kv in range(2):
            h = _silu(accs[kv] + b1_ref[kv])
            c_scr[...] = _dot(h.astype(BF16), w2_ref[kv])
            g0 = c_scr[pl.ds(0, rows // 2, stride=2), :]
            g1 = c_scr[pl.ds(1, rows // 2, stride=2), :]
            low = lax.broadcasted_iota(jnp.int32, (1, LANES), 1) < HEAD_DIM
            even = jnp.where(low, g0, pltpu.roll(g1, HEAD_DIM, 1))
            odd = jnp.where(low, pltpu.roll(g0, HEAD_DIM, 1), g1)
            if pad_blocks is None:
                o_ref.at[kv][pl.ds(0, rows // 2, stride=2), :] = even
                o_ref.at[kv][pl.ds(1, rows // 2, stride=2), :] = odd
            else:
                c_scr[pl.ds(0, rows // 2, stride=2), :] = even
                c_scr[pl.ds(1, rows // 2, stride=2), :] = odd
                nblk = per_seq * (PAGE_SIZE // NSA_BLOCK)
                o_ref[kv] = jnp.zeros(o_ref.shape[1:], o_ref.dtype)
                for r in range(steps * seqs):
                    o_ref[kv, r * pad_blocks:r * pad_blocks + nblk, :] = c_scr[r * nblk:(r + 1) * nblk, :]


def _page_specs(n_pages, seqs=1):
    return [pl.BlockSpec((None, KV_BRANCH_COLS, PAGE_SIZE), lambda s, pt, r=r, k=k: (pt[s * seqs + r, k], 0, 0))
            for r in range(seqs) for k in range(n_pages)]


def _gather_compress(cache_t, page_table, cwp, pages_per_seq=None, new_tokens=None, pad_blocks=None):
    pe, w1, b1, w2 = cwp
    contiguous = page_table is None
    seqs = 4
    if contiguous:
        n_pages = pages_per_seq
        n_dec = cache_t.shape[0] // n_pages
        page_table = jnp.zeros((1, 1), jnp.int32)
        page_specs = [pl.BlockSpec((seqs * n_pages, KV_BRANCH_COLS, PAGE_SIZE), lambda s, pt: (s, 0, 0))]
        page_args = [cache_t]
    else:
        n_dec, n_pages = page_table.shape
        page_specs = _page_specs(n_pages, seqs)
        page_args = [cache_t] * (seqs * n_pages)
    const = lambda a: pl.BlockSpec(a.shape, lambda s, pt: (0,) * a.ndim, pipeline_mode=pl.Buffered(1))
    per_seq = n_pages
    if new_tokens is not None:
        per_seq += 1
        page_specs.append(const(new_tokens))
        page_args.append(new_tokens)
    steps = 2
    group = steps * seqs * per_seq
    assert n_dec % (steps * seqs) == 0
    rows = group * N_KV_HEADS
    blocks = PAGE_SIZE // NSA_BLOCK
    assert blocks == 2 and N_KV_HEADS == 2
    out_rows = rows if pad_blocks is None else steps * seqs * pad_blocks
    assert pad_blocks is None or (pad_blocks >= per_seq * blocks and pad_blocks % 8 == 0)
    out = pl.pallas_call(
        functools.partial(_gather_compress_kernel, seqs=seqs, n_pages=n_pages, group=group, steps=steps,
                          contiguous=contiguous, new_tokens=new_tokens is not None, pad_blocks=pad_blocks),
        grid_spec=pltpu.PrefetchScalarGridSpec(
            num_scalar_prefetch=1,
            grid=(n_dec // seqs,),
            in_specs=page_specs + [const(pe), const(w1), const(b1), const(w2)],
            out_specs=pl.BlockSpec((2, out_rows, blocks * HEAD_DIM), lambda s, pt: (0, s // steps, 0)),
            scratch_shapes=[pltpu.VMEM((group * N_KV_HEADS * SLAB_PITCH, PAGE_SIZE), F32)] * 2
            + [pltpu.VMEM((rows, LANES), F32)],
        ),
        out_shape=jax.ShapeDtypeStruct((2, n_dec // (steps * seqs) * out_rows, blocks * HEAD_DIM), F32),
        compiler_params=_params("arbitrary"),
        name="gather_compress",
    )(page_table, *page_args, pe, w1, b1, w2)
    return out.reshape(2, n_dec, -1, N_KV_HEADS * HEAD_DIM)


def _group_queries_t(qt, g):
    t = qt.shape[1]
    zeros = jnp.zeros((HEAD_DIM, t), qt.dtype)
    cols = []
    for j in range(HEADS_PER_GROUP):
        qh = qt[j * LANES + g * HEAD_DIM:j * LANES + (g + 1) * HEAD_DIM, :]
        cols.append(jnp.concatenate([qh, zeros] if g == 0 else [zeros, qh], axis=0))
    return jnp.concatenate(cols, axis=1)


def _tile_heads(x):
    return jnp.concatenate([x] * HEADS_PER_GROUP, axis=1)


def _select_blocks_t(imp, nblk, n_sel):
    nb = imp.shape[0]
    rank = jnp.zeros(imp.shape, F32)
    for m in range(nb):
        row = imp[m:m + 1, :]
        beats = jnp.where(row > imp, 1.0, jnp.where(row == imp, jnp.where(m < nblk, 1.0, 0.0), 0.0))
        rank = rank + beats
    return jnp.where(rank < n_sel, jnp.where(imp >= 0.0, 1.0, 0.0), 0.0)


def _select_blocks(imp, nblk, n_sel):
    nb = imp.shape[1]
    rank = jnp.zeros(imp.shape, F32)
    for m in range(nb):
        col = imp[:, m:m + 1]
        beats = jnp.where(col > imp, 1.0, jnp.where(col == imp, jnp.where(m < nblk, 1.0, 0.0), 0.0))
        rank = rank + beats
    return jnp.where(rank < n_sel, jnp.where(imp >= 0.0, 1.0, 0.0), 0.0)


def _attn_tile(carry, k, q, vt_ref, ks, tk, bias):
    m, l, acc = carry
    s = _dot(k, q)
    if bias is not None:
        s = s + bias
    m_new = jnp.maximum(m, jnp.max(s, axis=0, keepdims=True))
    alpha = jnp.exp2(m - m_new)
    pb = jnp.exp2(s - m_new).astype(BF16)
    half = pb.shape[1] // N_KV_HEADS
    ones = jnp.ones((BF16_SUBLANES, tk), BF16)
    pv = [_dot(jnp.concatenate([vt_ref[GD_ROWS + g * HEAD_DIM:GD_ROWS + (g + 1) * HEAD_DIM, pl.ds(ks, tk)], ones],
                               axis=0), pb[:, g * half:(g + 1) * half]) for g in range(N_KV_HEADS)]
    pv = jnp.concatenate(pv, axis=1)
    return m_new, alpha * l + pv[HEAD_DIM:HEAD_DIM + 1], alpha * acc + pv[:HEAD_DIM]


def _maybe(cond, fn, carry):
    return lax.fori_loop(0, jnp.where(cond, 1, 0), lambda _, c: fn(c), carry)


def _attn_kernel(qt_ref, gatet_ref, kc_ref, vct_ref, slc_k_ref, slc_vt_ref, win_k_ref, win_vt_ref, o_ref,
                 *, tq, nb):
    i = pl.program_id(1)
    p0 = i * tq
    tk = tq
    hq = HEADS_PER_GROUP * tq
    n_win_tiles = NSA_WINDOW // tk
    pos = p0 + lax.broadcasted_iota(jnp.int32, (1, tq), 1)
    nblk = lax.broadcasted_iota(jnp.int32, (nb, 1), 0)
    key_off = lax.broadcasted_iota(jnp.int32, (tk, 1), 0)
    qt = qt_ref[...]
    q_all = jnp.concatenate([_group_queries_t(qt, g) for g in range(N_KV_HEADS)], axis=1)
    tile_all = lambda x: jnp.concatenate([x] * N_HEADS, axis=1)

    cvalid = (nblk + 1) * NSA_BLOCK <= tile_all(pos) + 1
    sc = jnp.where(cvalid, _dot(kc_ref[...], q_all), MASK_INIT)
    e = jnp.exp2(sc - jnp.max(sc, axis=0, keepdims=True))
    pc = jnp.where(cvalid, e / jnp.sum(e, axis=0, keepdims=True), 0.0)
    pcb = pc.astype(BF16)
    o_cmp = jnp.concatenate([_dot(vct_ref[g * HEAD_DIM:(g + 1) * HEAD_DIM, :], pcb[:, g * hq:(g + 1) * hq])
                             for g in range(N_KV_HEADS)], axis=1)

    cur = pos // NSA_BLOCK
    forced = HEADS_PER_GROUP + 1.0
    sel_rows = []
    for g in range(N_KV_HEADS):
        imp = functools.reduce(jnp.add, [pc[:, g * hq + j * tq:g * hq + (j + 1) * tq]
                                         for j in range(HEADS_PER_GROUP)])
        imp = jnp.where(nblk == 0, forced, jnp.where(nblk == cur, forced, jnp.where(nblk == cur - 1, forced, imp)))
        imp = jnp.where(nblk * NSA_BLOCK <= pos, imp, -1.0)
        sel = _select_blocks_t(imp, nblk, NSA_TOP_N)
        sel_rows.append(_tile_heads(jnp.where(sel > 0.5, 0.0, MASK_BIAS)))
    sel_rows = jnp.concatenate(sel_rows, axis=1)
    sel_rows = jnp.concatenate([sel_rows, jnp.zeros((LANES - nb, sel_rows.shape[1]), F32)], axis=0)
    q_sel = jnp.concatenate([q_all, sel_rows.astype(BF16)], axis=0)
    blk_lane = lax.broadcasted_iota(jnp.int32, (1, LANES), 1)

    def slc_keys(ks):
        onehot = jnp.where(blk_lane == (ks + key_off) // NSA_BLOCK, 1.0, 0.0).astype(BF16)
        return jnp.concatenate([slc_k_ref[pl.ds(ks, tk), :GD_ROWS], onehot], axis=1)

    cols = q_all.shape[1]
    init = (jnp.full((1, cols), MASK_INIT, F32), jnp.zeros((1, cols), F32), jnp.zeros((HEAD_DIM, cols), F32))
    causal = tile_all(jnp.where(p0 + key_off <= pos, 0.0, MASK_BIAS))
    k_diag = pl.multiple_of(p0, tk)

    def slc_body(j, c):
        ks = pl.multiple_of(j * tk, tk)
        return _attn_tile(c, slc_keys(ks), q_sel, slc_vt_ref, ks, tk, None)

    c = lax.fori_loop(0, i, slc_body, init)
    _, l, acc = _attn_tile(c, slc_keys(k_diag), q_sel, slc_vt_ref, k_diag, tk, causal)
    o_slc = acc / l

    def win_tile(c, ks, bias):
        return _attn_tile(c, win_k_ref[pl.ds(ks, tk), :GD_ROWS], q_all, win_vt_ref, ks, tk, bias)

    def edge_tile(c):
        ks = pl.multiple_of((i - n_win_tiles) * tk, tk)
        bias = tile_all(jnp.where(pos - (ks + key_off) < NSA_WINDOW, 0.0, MASK_BIAS))
        return win_tile(c, ks, bias)

    c = _maybe(i >= n_win_tiles, edge_tile, init)
    c = lax.fori_loop(jnp.maximum(i - n_win_tiles + 1, 0), i,
                      lambda j, c: win_tile(c, pl.multiple_of(j * tk, tk), None), c)
    _, l, acc = win_tile(c, k_diag, causal)
    o_win = acc / l

    gates = gatet_ref[...]
    for g in range(N_KV_HEADS):
        for j in range(HEADS_PER_GROUP):
            head = g * HEADS_PER_GROUP + j
            cs = slice(g * hq + j * tq, g * hq + (j + 1) * tq)
            o = (gates[head:head + 1] * o_cmp[:, cs]
                 + gates[N_HEADS + head:N_HEADS + head + 1] * o_slc[:, cs]
                 + gates[2 * N_HEADS + head:2 * N_HEADS + head + 1] * o_win[:, cs])
            r0 = j * LANES + g * HEAD_DIM
            o_ref[r0:r0 + HEAD_DIM, :] = o.astype(o_ref.dtype)


def _attn_prompt(qt, gatest, kc, vct, kvb, kvtb, n_seq, seq, tq):
    m = qt.shape[1]
    nb = seq // NSA_BLOCK
    nq = seq // tq
    assert seq % tq == 0 and NSA_WINDOW % tq == 0 and tq % NSA_BLOCK == 0 and nb <= LANES
    col = lambda n: pl.BlockSpec((n, tq), lambda b, i: (0, b * nq + i))
    return pl.pallas_call(
        functools.partial(_attn_kernel, tq=tq, nb=nb),
        grid=(n_seq, nq),
        in_specs=[col(Q_COLS), col(4 * N_HEADS),
                  pl.BlockSpec((nb, LANES), lambda b, i: (b, 0)),
                  pl.BlockSpec((None, LANES, nb), lambda b, i: (b, 0, 0)),
                  pl.BlockSpec((seq, KV_BRANCH_COLS), lambda b, i: (b, 1)),
                  pl.BlockSpec((None, None, KV_BRANCH_COLS, seq), lambda b, i: (1, b, 0, 0)),
                  pl.BlockSpec((seq, KV_BRANCH_COLS), lambda b, i: (b, 2)),
                  pl.BlockSpec((None, None, KV_BRANCH_COLS, seq), lambda b, i: (2, b, 0, 0))],
        out_specs=col(Q_COLS),
        out_shape=jax.ShapeDtypeStruct((Q_COLS, m), BF16),
        compiler_params=_params("parallel", "arbitrary"),
        name="attn_prompt",
    )(qt, gatest, kc, vct, kvb, kvtb, kvb, kvtb)


def _softmax_pv(score_tiles, value_tiles):
    m = functools.reduce(jnp.maximum, [jnp.max(s, axis=-1, keepdims=True) for s in score_tiles])
    ps = [jnp.exp2(s - m) for s in score_tiles]
    l = functools.reduce(jnp.add, [jnp.sum(p, axis=-1, keepdims=True) for p in ps])
    o = functools.reduce(jnp.add, [(_dot_nt if fm else _dot)(p.astype(BF16), v)
                                   for p, (v, fm) in zip(ps, value_tiles)])
    return o / l


def _attn_decode_kernel(pt_ref, *refs, n_pages, seqs, pos, nbp):
    del pt_ref
    pages = refs[:seqs * n_pages]
    q_ref, kvb_ref, gate_ref, kvc_ref, win_ref, wnew_ref, o_ref, wnext_ref = refs[seqs * n_pages:]
    for r in range(seqs):
        _attn_decode_one(pages[r * n_pages:(r + 1) * n_pages], q_ref.at[r], kvb_ref.at[r], gate_ref.at[r],
                         kvc_ref.at[:, r], win_ref.at[r], wnew_ref.at[r], o_ref.at[r], wnext_ref.at[r],
                         pos=pos, nbp=nbp)


def _attn_decode_one(pages, q_ref, kvb_ref, gate_ref, kvc_ref, win_ref, wnew_ref, o_ref, wnext_ref, *, pos, nbp):
    n_pages = len(pages)
    tk = PAGE_SIZE
    lane = lax.broadcasted_iota(jnp.int32, (1, LANES), 1)
    head = lax.broadcasted_iota(jnp.int32, (N_HEADS, 1), 0)
    low_group = head < HEADS_PER_GROUP
    low_lanes = lane < HEAD_DIM

    q32 = q_ref[...].astype(F32)
    qm = jnp.zeros((N_HEADS, LANES), F32)
    for j in range(HEADS_PER_GROUP):
        qm = jnp.where(head % HEADS_PER_GROUP == j, q32[:, j * LANES:(j + 1) * LANES], qm)
    qm = jnp.where(low_group, jnp.where(low_lanes, qm, 0.0), jnp.where(low_lanes, 0.0, qm)).astype(BF16)

    gate_row = gate_ref[...]
    gate_col = [jnp.sum(jnp.where(lane == head + br * N_HEADS, gate_row, 0.0), axis=-1, keepdims=True)
                for br in range(3)]
    kv_new = kvb_ref[...].astype(F32)
    first_row = lax.broadcasted_iota(jnp.int32, (tk, 1), 0) == 0

    def new_key_tile(c0):
        return jnp.where(first_row, kv_new[:, c0:c0 + LANES], 0.0).astype(BF16)

    nblk = lax.broadcasted_iota(jnp.int32, (1, nbp), 1)
    cvalid = (nblk + 1) * NSA_BLOCK <= pos + 1
    sc = jnp.where(cvalid, _dot_nt(qm, kvc_ref[0].astype(BF16)), MASK_INIT)
    e = jnp.exp2(sc - jnp.max(sc, axis=-1, keepdims=True))
    pc = jnp.where(cvalid, e / jnp.sum(e, axis=-1, keepdims=True), 0.0)
    o_cmp = _dot(pc.astype(BF16), kvc_ref[1].astype(BF16))

    imp = jnp.where(low_group,
                    jnp.sum(jnp.where(low_group, pc, 0.0), axis=0, keepdims=True),
                    jnp.sum(jnp.where(low_group, 0.0, pc), axis=0, keepdims=True))
    cur = pos // NSA_BLOCK
    forced = HEADS_PER_GROUP + 1.0
    imp = jnp.where(nblk == 0, forced, jnp.where(nblk == cur, forced, jnp.where(nblk == cur - 1, forced, imp)))
    imp = jnp.where(nblk * NSA_BLOCK <= pos, imp, -1.0)
    sel = _select_blocks(imp, nblk, NSA_TOP_N)

    def sel_bias(n):
        return jnp.where(sel[:, n:n + 1] > 0.5, 0.0, MASK_BIAS)

    s_tiles, v_tiles = [], []
    blocks_per_page = tk // NSA_BLOCK
    for k0 in range(0, n_pages, DECODE_PAGES_PER_DOT):
        ks = range(k0, min(k0 + DECODE_PAGES_PER_DOT, n_pages))
        biases = []
        for k in ks:
            bias = sel_bias(k * blocks_per_page + blocks_per_page - 1)
            for b in range(blocks_per_page - 2, -1, -1):
                bias = jnp.where(lane < (b + 1) * NSA_BLOCK, sel_bias(k * blocks_per_page + b), bias)
            biases.append(bias)
        keys = jnp.concatenate([pages[k][:GD_ROWS, :].astype(BF16) for k in ks], axis=1)
        s_tiles.append(_dot(qm, keys) + jnp.concatenate(biases, axis=1))
        v_tiles.append((jnp.concatenate([pages[k][GD_ROWS:, :].astype(BF16) for k in ks], axis=1), True))
    s_new = _dot_nt(qm, new_key_tile(KV_BRANCH_COLS))
    s_tiles.append(s_new + jnp.where(lane == 0, sel_bias(n_pages * blocks_per_page), MASK_BIAS))
    v_tiles.append((new_key_tile(KV_BRANCH_COLS + LANES), False))
    o_slc = _softmax_pv(s_tiles, v_tiles)

    n_win = win_ref.shape[1]
    diff = n_win - lax.broadcasted_iota(jnp.int32, (1, n_win), 1)
    s_tiles = [_dot(qm, win_ref[:GD_ROWS, :].astype(BF16)) + jnp.where(diff < NSA_WINDOW, 0.0, MASK_BIAS)]
    v_tiles = [(win_ref[GD_ROWS:, :].astype(BF16), True)]
    s_tiles.append(_dot_nt(qm, new_key_tile(2 * KV_BRANCH_COLS)) + jnp.where(lane == 0, 0.0, MASK_BIAS))
    v_tiles.append((new_key_tile(2 * KV_BRANCH_COLS + LANES), False))
    o_win = _softmax_pv(s_tiles, v_tiles)

    o = gate_col[0] * o_cmp + gate_col[1] * o_slc + gate_col[2] * o_win
    for j in range(HEADS_PER_GROUP):
        pair = jnp.where(low_lanes, o[j:j + 1], o[j + HEADS_PER_GROUP:j + HEADS_PER_GROUP + 1])
        o_ref[:, j * LANES:(j + 1) * LANES] = pair.astype(o_ref.dtype)

    last = lax.broadcasted_iota(jnp.int32, (1, n_win), 1) == n_win - 1
    wnext_ref[...] = jnp.where(last, wnew_ref[...], pltpu.roll(win_ref[...], n_win - 1, 1))


def _attn_decode(q, kvb, gates, kvc, cache_slc, win_state, win_new, page_table, pos):
    n_dec, n_pages = page_table.shape
    nbp = kvc.shape[2]
    n_win = win_state.shape[2]
    seqs = DECODE_SEQS_PER_STEP
    per_seq = lambda a: pl.BlockSpec((seqs, 1, a.shape[-1]), lambda s, pt: (s, 0, 0))
    win_spec = pl.BlockSpec((seqs, KV_BRANCH_COLS, n_win), lambda s, pt: (s, 0, 0))
    q3, kvb3, g3 = (a.reshape(n_dec, 1, a.shape[-1]) for a in (q, kvb, gates))
    out, win_next = pl.pallas_call(
        functools.partial(_attn_decode_kernel, n_pages=n_pages, seqs=seqs, pos=pos, nbp=nbp),
        grid_spec=pltpu.PrefetchScalarGridSpec(
            num_scalar_prefetch=1,
            grid=(n_dec // seqs,),
            in_specs=_page_specs(n_pages, seqs) + [
                per_seq(q3), per_seq(kvb3), per_seq(g3),
                pl.BlockSpec((2, seqs, nbp, LANES), lambda s, pt: (0, s, 0, 0)),
                win_spec, pl.BlockSpec((seqs, KV_BRANCH_COLS, 1), lambda s, pt: (s, 0, 0)),
            ],
            out_specs=[pl.BlockSpec((seqs, 1, Q_COLS), lambda s, pt: (s, 0, 0)), win_spec],
        ),
        out_shape=[jax.ShapeDtypeStruct((n_dec, 1, Q_COLS), BF16), jax.ShapeDtypeStruct(win_state.shape, F32)],
        compiler_params=_params("arbitrary"),
        name="attn_decode",
    )(page_table, *([cache_slc] * (seqs * n_pages)), q3, kvb3, g3, kvc, win_state, win_new)
    return out.reshape(n_dec, Q_COLS), win_next


def _mix_kernel(*refs, tm, halo, tiles_per_seq):
    if halo:
        (x_ref, xh_ref, o_ref, shift_ref, scale_ref, gate_ref, gpre_ref, gpost_ref, cw_ref,
         wconv_ref, wmerge_ref, wup_ref, wco_ref, wout_ref, y_ref, ulast_ref, u_scr) = refs
        x = x_ref[...]
        xe = jnp.concatenate([xh_ref[...], x], axis=0)
    else:
        (x_ref, um1_ref, um2_ref, o_ref, shift_ref, scale_ref, gate_ref, gpre_ref, gpost_ref, cw_ref,
         wconv_ref, wmerge_ref, wup_ref, wco_ref, wout_ref, y_ref, ulast_ref) = refs
        x = x_ref[...]
        xe = x
    h0 = xe.shape[0] - tm
    a = (_rms(xe, gpre_ref[...]) * (1.0 + scale_ref[...]) + shift_ref[...]).astype(BF16)
    zc = _dot(a, wconv_ref[...])
    u = zc[:, 2 * D_CONV:] * zc[:, :D_CONV]
    cb = zc[h0:, D_CONV:2 * D_CONV]
    if halo:
        keep_halo = jnp.where(pl.program_id(0) % tiles_per_seq == 0, 0.0, 1.0)
        rows = lax.broadcasted_iota(jnp.int32, (xe.shape[0], 1), 0)
        u = jnp.where(rows < h0, u * keep_halo, u)
        u_scr[...] = u
        um1 = u_scr[h0 - 1:h0 - 1 + tm, :]
        um2 = u_scr[h0 - 2:h0 - 2 + tm, :]
        u0 = u[h0:]
        ulast_ref[...] = u[tm:]
    else:
        um1, um2, u0 = um1_ref[...], um2_ref[...], u
        ulast_ref[...] = u
    cw = cw_ref[...]
    y = cw[0:1] * um2 + cw[1:2] * um1 + cw[2:3] * u0
    conv_out = _dot((cb * y).astype(BF16), wco_ref[...])
    attn_out = _dot_tn(o_ref[...], wup_ref[...])
    mg = jax.nn.sigmoid(_dot(a[h0:], wmerge_ref[...]))
    merged = mg[:, :D_MODEL] * attn_out + mg[:, D_MODEL:] * conv_out
    mixed = _dot(merged.astype(BF16), wout_ref[...])
    y_ref[...] = x + gate_ref[...] * _rms(mixed, gpost_ref[...])


def _mix(x, o_att, mod, g3, conv_w, weights, tm, tiles_per_seq=None, prev=None):
    m = x.shape[0]
    halo = prev is None
    h0 = MIX_HALO
    row = lambda n: pl.BlockSpec((tm, n), lambda i: (i, 0))
    full = lambda a: pl.BlockSpec(a.shape, lambda i: (0,) * a.ndim, pipeline_mode=pl.Buffered(1))
    att = pl.BlockSpec((Q_COLS, tm), lambda i: (0, i))
    common = [att, mod.spec(3), mod.spec(4), mod.spec(5), _gspec(2), _gspec(3), full(conv_w)]
    common += [full(w) for w in weights]
    common_args = [o_att, mod.arr, mod.arr, mod.arr, g3, g3, conv_w, *weights]
    if halo:
        hb = tm // h0
        in_specs = [row(D_MODEL), pl.BlockSpec((h0, D_MODEL), lambda i: (jnp.maximum(i * hb - 1, 0), 0))] + common
        args = [x, x] + common_args
        scratch = [pltpu.VMEM((tm + h0, D_CONV), F32)]
        ulast = (jax.ShapeDtypeStruct((m // tm * h0, D_CONV), F32), pl.BlockSpec((h0, D_CONV), lambda i: (i, 0)))
    else:
        in_specs = [row(D_MODEL), row(D_CONV), row(D_CONV)] + common
        args = [x, prev[0], prev[1]] + common_args
        scratch = []
        ulast = (jax.ShapeDtypeStruct((m, D_CONV), F32), row(D_CONV))
    return pl.pallas_call(
        functools.partial(_mix_kernel, tm=tm, halo=halo, tiles_per_seq=tiles_per_seq),
        grid=(m // tm,),
        in_specs=in_specs,
        out_specs=[row(D_MODEL), ulast[1]],
        out_shape=[jax.ShapeDtypeStruct((m, D_MODEL), F32), ulast[0]],
        scratch_shapes=scratch,
        compiler_params=_params("parallel"),
        name="mix",
    )(*args)


def _head_pair_perm():
    order = []
    for j in range(HEADS_PER_GROUP):
        for g in range(N_KV_HEADS):
            head = g * HEADS_PER_GROUP + j
            order.extend(range(head * HEAD_DIM, (head + 1) * HEAD_DIM))
    return jnp.array(order, dtype=jnp.int32)


def _layer_weights(w_in, w_attn_up, w_conv_out, w_out):
    perm = _head_pair_perm()
    o1 = Q_COLS
    o2 = o1 + KV_COLS
    o3 = o2 + NSA_GATE_COLS
    o4 = o3 + CONV_COLS
    w_q = w_in[:, :o1][:, perm]
    w_g = jnp.pad(w_in[:, o2:o3], ((0, 0), (0, GATE_PAD - NSA_GATE_COLS)))
    w_qkvg = jnp.concatenate([w_q, w_in[:, o1:o2], w_g], axis=1).astype(BF16)
    mix_w = (w_in[:, o3:o4].astype(BF16), w_in[:, o4:].astype(BF16), w_attn_up[perm].astype(BF16),
             w_conv_out.astype(BF16), w_out.astype(BF16))
    return w_qkvg, mix_w


def kernel(x_prompt, x_sample, c_prompt, c_sample, cache_cmp_kv, cache_slc_kv, state_win_kv, state_conv,
           page_table, w_ada, b_ada, g_norm, w_ffn_gu, w_ffn_down, w_in, phi_pe, phi_w1, phi_b1, phi_w2,
           w_attn_up, conv_w, w_conv_out, w_out):
    n_seq, seq, _ = x_prompt.shape
    n_dec = x_sample.shape[0]
    assert w_ada.shape[0] == 1
    l = 0

    mod_all = _ada(jnp.concatenate([c_prompt, c_sample], axis=0), w_ada[l], b_ada[l])
    g3 = g_norm[l].reshape(6, 1, D_MODEL)
    w_gu = w_ffn_gu[l].astype(BF16)
    w_down = w_ffn_down[l].astype(BF16)
    w_qkvg, mix_w = _layer_weights(w_in[l], w_attn_up[l], w_conv_out[l], w_out[l])
    cwp = _compress_weights_paged(phi_pe[l], phi_w1[l], phi_b1[l], phi_w2[l])

    tm = 1024
    tq = 256
    m = n_seq * seq
    mod_p = _Mod(mod_all[:n_seq], per_row=False, tiles_per_seq=seq // tm)
    tm_ffn, ffn_sub = 1024, 2
    mod_pf = _Mod(mod_all[:n_seq], per_row=False, tiles_per_seq=seq // tm_ffn)
    x0 = x_prompt.reshape(m, D_MODEL)
    x1 = _ffn(x0, mod_pf, g3, w_gu, w_down, 0, tm_ffn, ffn_sub)
    tables = _rope_tables(jnp.arange(seq, dtype=jnp.int32))
    _, qt, cmp_pages, cmpt, slct, wint, kvtb, kvb, _, gatest = _proj(x1, mod_p, g3, tables, w_qkvg, tm)
    nb = seq // NSA_BLOCK
    kvc = _gather_compress(cmp_pages, None, cwp, pages_per_seq=seq // PAGE_SIZE)
    kc = kvc[0].reshape(n_seq * nb, LANES).astype(BF16)
    vct = kvc[1].transpose(0, 2, 1).astype(BF16)
    o_att = _attn_prompt(qt, gatest, kc, vct, kvb, kvtb, n_seq, seq, tq)
    x2, ulast = _mix(x1, o_att, mod_pf, g3, conv_w[l], mix_w, tm_ffn, tiles_per_seq=seq // tm_ffn)
    y_prompt = _ffn(x2, mod_pf, g3, w_gu, w_down, 2, tm_ffn, ffn_sub).reshape(n_seq, seq, D_MODEL)

    def token_major(t):
        n, _, tt = t.shape
        return t.reshape(n, 2, N_KV_HEADS, HEAD_DIM, tt).transpose(0, 4, 1, 2, 3)[None]

    def feature_major(a):
        n, tt = a.shape[:2]
        return a.transpose(0, 2, 3, 4, 1).reshape(n, KV_BRANCH_COLS, tt)

    cmp_kv_prompt = token_major(cmpt)
    slc_kv_prompt = token_major(slct)
    win_keep = min(NSA_WINDOW, seq)
    win_kv_prompt = token_major(wint[:, :, seq - win_keep:])
    conv_prompt = ulast.reshape(n_seq, seq // tm_ffn, MIX_HALO, D_CONV)[:, -1, MIX_HALO - (CONV_WIDTH - 1):][None]

    n_pages = page_table.shape[1]
    past_len = n_pages * PAGE_SIZE
    pos_s = past_len + jnp.arange(x_sample.shape[1], dtype=jnp.int32)
    assert x_sample.shape[1] == 1 and past_len % NSA_BLOCK == 0 and state_win_kv.shape[2] <= past_len
    mod_s = _Mod(mod_all[n_seq:], per_row=True)
    xs1 = _ffn(x_sample.reshape(n_dec, D_MODEL), mod_s, g3, w_gu, w_down, 0, n_dec)
    tables_s = _rope_tables(jnp.broadcast_to(pos_s, (n_dec,)))
    q_s, _, _, cmpt_s, slct_s, wint_s, _, kvb_s, gates_s, _ = _proj(xs1, mod_s, g3, tables_s, w_qkvg, n_dec)
    nb_past = past_len // NSA_BLOCK
    nb_pad = -(-(nb_past + PAGE_SIZE // NSA_BLOCK) // BF16_SUBLANES) * BF16_SUBLANES
    kvc_s = _gather_compress(feature_major(cache_cmp_kv[l]), page_table, cwp, new_tokens=cmpt_s[0],
                             pad_blocks=nb_pad)
    win_state = feature_major(state_win_kv[l])
    assert win_state.shape[2] == NSA_WINDOW
    o_att_s, win_next = _attn_decode(q_s, kvb_s, gates_s, kvc_s, feature_major(cache_slc_kv[l]), win_state,
                                     wint_s[0].T[:, :, None], page_table, past_len)
    conv_state = state_conv[l]
    xs2, u_s = _mix(xs1, o_att_s.T, mod_s, g3, conv_w[l], mix_w, n_dec,
                    prev=(conv_state[:, CONV_WIDTH - 2], conv_state[:, CONV_WIDTH - 3]))
    y_sample = _ffn(xs2, mod_s, g3, w_gu, w_down, 2, n_dec).reshape(x_sample.shape)

    kvs = (1, n_dec, 1, 2, N_KV_HEADS, HEAD_DIM)
    conv_sample = jnp.concatenate([conv_state[:, 1:], u_s[:, None, :]], axis=1)[None]
    return (y_prompt, y_sample, cmp_kv_prompt, slc_kv_prompt, win_kv_prompt, conv_prompt,
            token_major(cmpt_s).reshape(kvs), token_major(slct_s).reshape(kvs), token_major(win_next), conv_sample)
```
